```python
import math
import jax, jax.numpy as jnp
from jax import lax
import numpy as np

D_MODEL = 1024
BATCH = 16
SEQ = 4096
DEPTH = 1
DEC_BATCH = 8
DEC_SEQ = 8192
PAST_LEN = 128

MIX_WIDTH = D_MODEL
W_ATT = MIX_WIDTH // 2
W_SSM = MIX_WIDTH - W_ATT
ATT_HEAD_DIM = 64
N_ATT_HEADS = W_ATT // (2 * ATT_HEAD_DIM)
SSM_GROUP = 16
N_SSM_GROUPS = W_SSM // SSM_GROUP
SSM_STATE = 64
IN_WIDTH = 3 * W_ATT + W_SSM
N_BUCKETS = 32
MAX_DISTANCE = 128
Q_BLOCK = 128
N_EXPERTS = 32
TOP_K = 4
D_FF = D_MODEL
SWIGLU_ALPHA = 1.702
SWIGLU_LIMIT = 7.0
EXPERT_BLOCK = 128
RMS_EPS = 1e-6

kernel_name = 'hymba_diffattn_s5_moe_encoder'

F32 = jnp.float32


def rmsnorm(x, g):
    xf = x.astype(F32)
    y = xf * lax.rsqrt(jnp.mean(xf * xf, axis=-1, keepdims=True) + RMS_EPS)
    return (y * g.astype(F32)).astype(x.dtype)


def t5_bucket(rel):
    half = N_BUCKETS // 2
    max_exact = half // 2
    ret = jnp.where(rel > 0, half, 0).astype(jnp.int32)
    n = jnp.abs(rel)
    nf = jnp.maximum(n, 1).astype(F32)
    large = max_exact + (jnp.log(nf / max_exact) / math.log(MAX_DISTANCE / max_exact) * (half - max_exact)).astype(jnp.int32)
    large = jnp.minimum(large, half - 1)
    return ret + jnp.where(n < max_exact, n, large)


def diff_attention(q, k, v, rel_bias, lam, subln_g, lambda_init):
    bsz, L = q.shape[0], q.shape[1]
    nq = L // Q_BLOCK
    scale = ATT_HEAD_DIM ** -0.5
    qb = jnp.moveaxis(q.reshape(bsz, nq, Q_BLOCK, N_ATT_HEADS, 2, ATT_HEAD_DIM), 1, 0)
    kpos = jnp.arange(L, dtype=jnp.int32)

    def one_block(args):
        qblk, i = args
        qpos = i * Q_BLOCK + jnp.arange(Q_BLOCK, dtype=jnp.int32)
        bias = jnp.transpose(rel_bias[t5_bucket(kpos[None, :] - qpos[:, None])], (2, 0, 1)).astype(F32)
        s = jnp.einsum('bqhmd,bkhmd->bmhqk', qblk, k, preferred_element_type=F32) * scale + bias
        p = jax.nn.softmax(s, axis=-1)
        a = p[:, 0] - lam * p[:, 1]
        return jnp.einsum('bhqk,bkhe->bqhe', a.astype(v.dtype), v, preferred_element_type=F32)

    o = lax.map(one_block, (qb, jnp.arange(nq, dtype=jnp.int32)))
    o = jnp.moveaxis(o, 0, 1).reshape(bsz, L, N_ATT_HEADS, 2 * ATT_HEAD_DIM)
    o = rmsnorm(o, subln_g) * (1.0 - lambda_init)
    return o.reshape(bsz, L, W_ATT)


def scan_combine(left, right):
    a_l, b_l = left
    a_r, b_r = right
    return a_r * a_l, a_r * b_l + b_r


def s5_direction(uf, A_re, A_im, log_dt, B_re, B_im, C_re, C_im, reverse):
    L = uf.shape[1]
    A = lax.complex(A_re.astype(F32), A_im.astype(F32))
    dt = jnp.exp(log_dt.astype(F32))[:, None]
    A_bar = jnp.exp(A * dt)
    B_bar = ((A_bar - 1.0) / A)[:, :, None] * lax.complex(B_re.astype(F32), B_im.astype(F32))
    Bu = jnp.einsum('blgh,gph->blgp', uf, B_bar)
    a = jnp.broadcast_to(A_bar, (1, L) + A_bar.shape)
    _, states = lax.associative_scan(scan_combine, (a, Bu), reverse=reverse, axis=1)
    C = lax.complex(C_re.astype(F32), C_im.astype(F32))
    return jnp.einsum('blgp,ghp->blgh', states, C).real


def bi_s5(u, A_re, A_im, log_dt, B_re, B_im, C_re, C_im, D_skip, w_glu, b_glu, ssm_g):
    bsz, L, _ = u.shape
    uf = u.astype(F32).reshape(bsz, L, N_SSM_GROUPS, SSM_GROUP)
    y = D_skip.astype(F32).reshape(N_SSM_GROUPS, SSM_GROUP) * uf
    for d in range(2):
        y = y + s5_direction(uf, A_re[d], A_im[d], log_dt[d], B_re[d], B_im[d], C_re[d], C_im[d], reverse=(d == 1))
    y = jax.nn.gelu(y.reshape(bsz, L, W_SSM))
    y = y * jax.nn.sigmoid(y @ w_glu.astype(F32) + b_glu.astype(F32))
    return rmsnorm(y, ssm_g).astype(u.dtype)


def routed_moe(h, w_router, b_router, w1, b1, w2, b2):
    bsz, L, d = h.shape
    hf = h.reshape(-1, d)
    T = hf.shape[0]
    N = T * TOP_K
    logits = jnp.dot(hf, w_router, preferred_element_type=F32) + b_router.astype(F32)
    top_v, top_e = lax.top_k(logits, TOP_K)
    gates = jax.nn.softmax(top_v, axis=-1).reshape(N)
    flat_e = top_e.reshape(N).astype(jnp.int32)
    flat_tok = jnp.arange(N, dtype=jnp.int32) // TOP_K
    order = jnp.argsort(flat_e)
    se, st, sg = flat_e[order], flat_tok[order], gates[order]
    counts = jnp.bincount(flat_e, length=N_EXPERTS).astype(jnp.int32)
    padded = (counts + EXPERT_BLOCK - 1) // EXPERT_BLOCK * EXPERT_BLOCK
    start = jnp.cumsum(counts) - counts
    pend = jnp.cumsum(padded)
    pstart = pend - padded
    dest = pstart[se] + jnp.arange(N, dtype=jnp.int32) - start[se]
    n_blocks = -(-N // EXPERT_BLOCK) + N_EXPERTS
    tok_pad = jnp.full((n_blocks * EXPERT_BLOCK,), T, jnp.int32).at[dest].set(st)
    block_e = jnp.minimum(jnp.searchsorted(pend, jnp.arange(n_blocks, dtype=jnp.int32) * EXPERT_BLOCK, side='right'), N_EXPERTS - 1)
    x_pad = jnp.concatenate([hf, jnp.zeros((1, d), hf.dtype)], axis=0)[tok_pad].reshape(n_blocks, EXPERT_BLOCK, d)

    def expert_fn(args):
        xb, e = args
        hdn = xb @ w1[e] + b1[e]
        gate, lin = jnp.split(hdn, 2, axis=-1)
        gate = jnp.minimum(gate, SWIGLU_LIMIT)
        lin = jnp.clip(lin, -SWIGLU_LIMIT, SWIGLU_LIMIT)
        act = gate * jax.nn.sigmoid(SWIGLU_ALPHA * gate) * (lin + 1.0)
        return act @ w2[e] + b2[e]

    y_pad = lax.map(expert_fn, (x_pad, block_e)).reshape(-1, d)
    y = y_pad[dest] * sg.astype(y_pad.dtype)[:, None]
    return jax.ops.segment_sum(y, st, num_segments=T).reshape(bsz, L, d)


def trunk(x, rel_bias, norm1_g, w_in, lambda_q1, lambda_k1, lambda_q2, lambda_k2, subln_g,
          ssm_A_re, ssm_A_im, ssm_log_dt, ssm_B_re, ssm_B_im, ssm_C_re, ssm_C_im, ssm_D,
          w_glu, b_glu, ssm_norm_g, w_out, norm2_g, w_router, b_router,
          w_moe1, b_moe1, w_moe2, b_moe2, normf_g):
    bsz, L, _ = x.shape
    for l in range(DEPTH):
        lambda_init = 0.8 - 0.6 * math.exp(-0.3 * l)
        h = rmsnorm(x, norm1_g[l])
        proj = h @ w_in[l]
        q, k, v, u = jnp.split(proj, [W_ATT, 2 * W_ATT, 3 * W_ATT], axis=-1)
        q = q.reshape(bsz, L, N_ATT_HEADS, 2, ATT_HEAD_DIM)
        k = k.reshape(bsz, L, N_ATT_HEADS, 2, ATT_HEAD_DIM)
        v = v.reshape(bsz, L, N_ATT_HEADS, 2 * ATT_HEAD_DIM)
        lam = (jnp.exp(jnp.sum(lambda_q1[l].astype(F32) * lambda_k1[l].astype(F32)))
               - jnp.exp(jnp.sum(lambda_q2[l].astype(F32) * lambda_k2[l].astype(F32))) + lambda_init)
        att = diff_attention(q, k, v, rel_bias, lam, subln_g[l], lambda_init).astype(x.dtype)
        ssm = bi_s5(u, ssm_A_re[l], ssm_A_im[l], ssm_log_dt[l], ssm_B_re[l], ssm_B_im[l],
                    ssm_C_re[l], ssm_C_im[l], ssm_D[l], w_glu[l], b_glu[l], ssm_norm_g[l])
        x = x + jnp.concatenate([att, ssm], axis=-1) @ w_out[l]
        x = x + routed_moe(rmsnorm(x, norm2_g[l]), w_router[l], b_router[l],
                           w_moe1[l], b_moe1[l], w_moe2[l], b_moe2[l])
    return rmsnorm(x, normf_g)


def setup_inputs(seed: int = 0) -> dict:
    key = jax.random.key(seed)
    ks = jax.random.split(key, 32)
    G, P, Hc, E = N_SSM_GROUPS, SSM_STATE, SSM_GROUP, N_EXPERTS

    def nrm(k, shape, scale):
        return jax.random.normal(k, shape, F32) * scale

    a_im0 = jnp.pi * jnp.arange(P, dtype=F32)
    return {
        'x_prompt': nrm(ks[0], (BATCH, SEQ, D_MODEL), 1.0),
        'x_sample': nrm(ks[1], (DEC_BATCH, DEC_SEQ, D_MODEL), 1.0),
        'rel_bias': nrm(ks[2], (N_BUCKETS, N_ATT_HEADS), 0.2),
        'norm1_g': 1.0 + nrm(ks[3], (DEPTH, D_MODEL), 0.05),
        'w_in': nrm(ks[4], (DEPTH, D_MODEL, IN_WIDTH), D_MODEL ** -0.5),
        'lambda_q1': nrm(ks[5], (DEPTH, ATT_HEAD_DIM), 0.1),
        'lambda_k1': nrm(ks[6], (DEPTH, ATT_HEAD_DIM), 0.1),
        'lambda_q2': nrm(ks[7], (DEPTH, ATT_HEAD_DIM), 0.1),
        'lambda_k2': nrm(ks[8], (DEPTH, ATT_HEAD_DIM), 0.1),
        'subln_g': 1.0 + nrm(ks[9], (DEPTH, 2 * ATT_HEAD_DIM), 0.05),
        'ssm_A_re': -0.5 + nrm(ks[10], (DEPTH, 2, G, P), 0.01),
        'ssm_A_im': a_im0 + nrm(ks[11], (DEPTH, 2, G, P), 0.01),
        'ssm_log_dt': jax.random.uniform(ks[12], (DEPTH, 2, G), F32, math.log(1e-3), math.log(1e-1)),
        'ssm_B_re': nrm(ks[13], (DEPTH, 2, G, P, Hc), (2.0 * Hc) ** -0.5),
        'ssm_B_im': nrm(ks[14], (DEPTH, 2, G, P, Hc), (2.0 * Hc) ** -0.5),
        'ssm_C_re': nrm(ks[15], (DEPTH, 2, G, Hc, P), (2.0 * P) ** -0.5),
        'ssm_C_im': nrm(ks[16], (DEPTH, 2, G, Hc, P), (2.0 * P) ** -0.5),
        'ssm_D': nrm(ks[17], (DEPTH, W_SSM), 1.0),
        'w_glu': nrm(ks[18], (DEPTH, W_SSM, W_SSM), W_SSM ** -0.5),
        'b_glu': nrm(ks[19], (DEPTH, W_SSM), 0.01),
        'ssm_norm_g': 1.0 + nrm(ks[20], (DEPTH, W_SSM), 0.05),
        'w_out': nrm(ks[21], (DEPTH, MIX_WIDTH, D_MODEL), MIX_WIDTH ** -0.5),
        'norm2_g': 1.0 + nrm(ks[22], (DEPTH, D_MODEL), 0.05),
        'w_router': nrm(ks[23], (DEPTH, D_MODEL, E), D_MODEL ** -0.5),
        'b_router': nrm(ks[24], (DEPTH, E), 0.01),
        'w_moe1': nrm(ks[25], (DEPTH, E, D_MODEL, 2 * D_FF), D_MODEL ** -0.5),
        'b_moe1': nrm(ks[26], (DEPTH, E, 2 * D_FF), 0.01),
        'w_moe2': nrm(ks[27], (DEPTH, E, D_FF, D_MODEL), D_FF ** -0.5),
        'b_moe2': nrm(ks[28], (DEPTH, E, D_MODEL), 0.01),
        'normf_g': 1.0 + nrm(ks[29], (D_MODEL,), 0.05),
    }


def reference(x_prompt, x_sample, rel_bias, norm1_g, w_in, lambda_q1, lambda_k1, lambda_q2, lambda_k2,
              subln_g, ssm_A_re, ssm_A_im, ssm_log_dt, ssm_B_re, ssm_B_im, ssm_C_re, ssm_C_im, ssm_D,
              w_glu, b_glu, ssm_norm_g, w_out, norm2_g, w_router, b_router,
              w_moe1, b_moe1, w_moe2, b_moe2, normf_g):
    params = (rel_bias, norm1_g, w_in, lambda_q1, lambda_k1, lambda_q2, lambda_k2, subln_g,
              ssm_A_re, ssm_A_im, ssm_log_dt, ssm_B_re, ssm_B_im, ssm_C_re, ssm_C_im, ssm_D,
              w_glu, b_glu, ssm_norm_g, w_out, norm2_g, w_router, b_router,
              w_moe1, b_moe1, w_moe2, b_moe2, normf_g)
    y_prompt = trunk(x_prompt, *params)
    y_sample = trunk(x_sample, *params)
    return (y_prompt, y_sample)
```

```python
import functools
import math

import jax
import jax.numpy as jnp
from jax import lax
from jax.experimental import pallas as pl
from jax.experimental.pallas import tpu as pltpu

F32 = jnp.float32
BF16 = jnp.bfloat16

D_MODEL = 1024
W_ATT = 512
W_SSM = 512
HEAD_DIM = 64
N_HEADS = 4
V_DIM = 2 * HEAD_DIM
SSM_GROUP = 16
N_GROUPS = W_SSM // SSM_GROUP
SSM_STATE = 64
IN_WIDTH = 3 * W_ATT + W_SSM
N_BUCKETS = 32
MAX_DISTANCE = 128
N_EXPERTS = 32
TOP_K = 4
D_FF = D_MODEL
SWIGLU_ALPHA = 1.702
SWIGLU_LIMIT = 7.0
RMS_EPS = 1e-6
ATT_SCALE = HEAD_DIM ** -0.5

LANES = 128
SSM_CHUNK = 32
SSM_ROW = SSM_CHUNK * SSM_GROUP
ROW_TILE = 512
EXPERT_ROWS = 256
GATHER_ROWS = 256
COMBINE_ROWS = 256
VMEM_LIMIT = 48 * 1024 * 1024


def _cparams(sem):
    return pltpu.CompilerParams(dimension_semantics=sem, vmem_limit_bytes=VMEM_LIMIT)


def _in_proj_kernel(x_ref, g_ref, w_ref, q_ref, k_ref, v_ref, u_ref):
    x = x_ref[...]
    ms = jnp.mean(x * x, axis=-1, keepdims=True)
    h = (x * lax.rsqrt(ms + RMS_EPS) * g_ref[...]).astype(BF16)
    proj = jnp.dot(h, w_ref[...], preferred_element_type=F32)
    q_ref[...] = (proj[:, 0:W_ATT] * ATT_SCALE).astype(BF16)
    k_ref[...] = proj[:, W_ATT:2 * W_ATT].astype(BF16)
    v_ref[...] = proj[:, 2 * W_ATT:3 * W_ATT].astype(BF16)
    u_ref[...] = proj[:, 3 * W_ATT:].astype(BF16)


def _in_proj(x2d, g, w_bf16):
    t = x2d.shape[0]
    tm = min(ROW_TILE, t)
    out = jax.ShapeDtypeStruct((t, W_ATT), BF16)
    row = lambda i: (i, 0)
    return pl.pallas_call(
        _in_proj_kernel,
        grid=(t // tm,),
        in_specs=[pl.BlockSpec((tm, D_MODEL), row),
                  pl.BlockSpec((1, D_MODEL), lambda i: (0, 0)),
                  pl.BlockSpec((D_MODEL, IN_WIDTH), lambda i: (0, 0))],
        out_specs=[pl.BlockSpec((tm, W_ATT), row)] * 4,
        out_shape=[out] * 4,
        compiler_params=_cparams(("parallel",)),
        name="in_proj",
    )(x2d, g, w_bf16)


def _t5_bucket(rel):
    half = N_BUCKETS // 2
    max_exact = half // 2
    ret = jnp.where(rel > 0, half, 0).astype(jnp.int32)
    n = jnp.abs(rel)
    nf = jnp.maximum(n, 1).astype(F32)
    large = max_exact + (jnp.log(nf / max_exact) / math.log(MAX_DISTANCE / max_exact)
                         * (half - max_exact)).astype(jnp.int32)
    large = jnp.minimum(large, half - 1)
    return ret + jnp.where(n < max_exact, n, large)


def _bias_tiles(rel_bias, t):
    i = jnp.arange(t, dtype=jnp.int32)
    d = jnp.arange(-2, 3, dtype=jnp.int32)
    rel = d[:, None, None] * t + i[None, None, :] - i[None, :, None]
    tiles = rel_bias[_t5_bucket(rel)].astype(F32)
    return jnp.transpose(tiles, (3, 0, 1, 2))


def _attn_kernel(lam_ref, q_ref, k_ref, v_ref, bias_ref, g_ref, o_ref,
                 m_scr, l_scr, acc_scr, *, t, nk):
    qi = pl.program_id(2)
    q = q_ref[0]
    lane = lax.broadcasted_iota(jnp.int32, q.shape, 1)
    zero = jnp.zeros_like(q)
    qs = (jnp.where(lane < HEAD_DIM, q, zero), jnp.where(lane >= HEAD_DIM, q, zero))

    m_scr[...] = jnp.full(m_scr.shape, -jnp.inf, F32)
    l_scr[...] = jnp.zeros(l_scr.shape, F32)
    acc_scr[...] = jnp.zeros(acc_scr.shape, F32)

    def body(j, carry):
        start = pl.multiple_of(j * t, t)
        kj = k_ref[0, pl.ds(start, t), :]
        vj = v_ref[0, pl.ds(start, t), :]
        b = bias_ref[0, jnp.clip(j - qi, -2, 2) + 2]
        for mi in range(2):
            s = lax.dot_general(qs[mi], kj, (((1,), (1,)), ((), ())),
                                preferred_element_type=F32) + b
            m_prev = m_scr[mi][:, 0:1]
            m_next = jnp.maximum(m_prev, jnp.max(s, axis=1, keepdims=True))
            alpha = jnp.exp(m_prev - m_next)
            p = jnp.exp(s - m_next)
            l_next = alpha * l_scr[mi][:, 0:1] + jnp.sum(p, axis=1, keepdims=True)
            acc_scr[mi] = alpha * acc_scr[mi] + jnp.dot(
                p.astype(BF16), vj, preferred_element_type=F32)
            m_scr[mi] = jnp.broadcast_to(m_next, (t, LANES))
            l_scr[mi] = jnp.broadcast_to(l_next, (t, LANES))
        return carry

    lax.fori_loop(0, nk, body, 0)

    o1 = acc_scr[0] / l_scr[0][:, 0:1]
    o2 = acc_scr[1] / l_scr[1][:, 0:1]
    o = o1 - lam_ref[0] * o2
    ms = jnp.mean(o * o, axis=-1, keepdims=True)
    o_ref[0] = (o * lax.rsqrt(ms + RMS_EPS) * g_ref[...]).astype(BF16)


def _attention(q, k, v, bias_tiles, lam, g_scaled, t):
    b, l, _ = q.shape
    nk = l // t
    kern = functools.partial(_attn_kernel, t=t, nk=nk)
    return pl.pallas_call(
        kern,
        grid=(b, N_HEADS, nk),
        in_specs=[pl.BlockSpec(memory_space=pltpu.SMEM),
                  pl.BlockSpec((1, t, V_DIM), lambda bi, h, qi: (bi, qi, h)),
                  pl.BlockSpec((1, l, V_DIM), lambda bi, h, qi: (bi, 0, h)),
                  pl.BlockSpec((1, l, V_DIM), lambda bi, h, qi: (bi, 0, h)),
                  pl.BlockSpec((1, 5, t, t), lambda bi, h, qi: (h, 0, 0, 0)),
                  pl.BlockSpec((1, V_DIM), lambda bi, h, qi: (0, 0))],
        out_specs=pl.BlockSpec((1, t, V_DIM), lambda bi, h, qi: (bi, qi, h)),
        out_shape=jax.ShapeDtypeStruct((b, l, W_ATT), BF16),
        scratch_shapes=[pltpu.VMEM((2, t, LANES), F32),
                        pltpu.VMEM((2, t, LANES), F32),
                        pltpu.VMEM((2, t, V_DIM), F32)],
        compiler_params=_cparams(("parallel", "parallel", "arbitrary")),
        name="diff_attention",
    )(lam, q, k, v, bias_tiles, g_scaled)


def _ssm_matrices(a_re, a_im, log_dt, b_re, b_im, c_re, c_im, d_skip):
    qn, g, p, hc = SSM_CHUNK, N_GROUPS, SSM_STATE, SSM_GROUP
    n = jnp.arange(qn + 1, dtype=F32)
    pw, bbar, cc = [], [], []
    for d in range(2):
        a = lax.complex(a_re[d].astype(F32), a_im[d].astype(F32))
        dt = jnp.exp(log_dt[d].astype(F32))[:, None]
        adt = a * dt
        a_bar = jnp.exp(adt)
        pw.append(jnp.exp(adt[None] * n[:, None, None]))
        bbar.append(((a_bar - 1.0) / a)[:, :, None]
                    * lax.complex(b_re[d].astype(F32), b_im[d].astype(F32)))
        cc.append(lax.complex(c_re[d].astype(F32), c_im[d].astype(F32)))

    kern = [jnp.einsum('gop,tgp,gpi->tgoi', cc[d], pw[d][:qn], bbar[d]).real for d in range(2)]
    s_idx = jnp.arange(qn)[:, None]
    t_idx = jnp.arange(qn)[None, :]
    lag = t_idx - s_idx
    kf = jnp.where((lag >= 0)[:, :, None, None, None], kern[0][jnp.clip(lag, 0, qn - 1)], 0.0)
    kb = jnp.where((lag <= 0)[:, :, None, None, None], kern[1][jnp.clip(-lag, 0, qn - 1)], 0.0)
    skip = (jnp.eye(qn, dtype=F32)[:, :, None, None, None]
            * (jnp.eye(hc, dtype=F32)[None, None, None] * d_skip.astype(F32).reshape(g, hc)[None, None, :, :, None]))
    m_full = kf + kb + skip
    m_mat = jnp.transpose(m_full, (2, 0, 4, 1, 3)).reshape(g, SSM_ROW, SSM_ROW)

    zeros_p = jnp.zeros((g, SSM_ROW, LANES - p), F32)

    def pad_cols(x):
        return jnp.concatenate([x, zeros_p], axis=-1)

    pf = jnp.einsum('sgp,gpi->gsip', pw[0][:qn][::-1], bbar[0]).reshape(g, SSM_ROW, p)
    pb = jnp.einsum('sgp,gpi->gsip', pw[1][:qn], bbar[1]).reshape(g, SSM_ROW, p)
    p_mat = jnp.concatenate([pad_cols(pf.real), pad_cols(pf.imag),
                             pad_cols(pb.real), pad_cols(pb.imag)], axis=-1)

    wf = jnp.einsum('gop,tgp->gpto', cc[0], pw[0][1:qn + 1]).reshape(g, p, SSM_ROW)
    wb = jnp.einsum('gop,tgp->gpto', cc[1], pw[1][1:qn + 1][::-1]).reshape(g, p, SSM_ROW)
    zeros_r = jnp.zeros((g, LANES - p, SSM_ROW), F32)
    r_mat = jnp.concatenate([wf.real, zeros_r, -wf.imag, zeros_r,
                             wb.real, zeros_r, -wb.imag, zeros_r], axis=1)

    zeros_a = jnp.zeros((g, LANES - p), F32)

    def pad_vec(x):
        return jnp.concatenate([x, zeros_a], axis=-1)

    alpha = jnp.stack([pad_vec(pw[0][qn].real), pad_vec(pw[0][qn].imag),
                       pad_vec(pw[1][qn].real), pad_vec(pw[1][qn].imag)], axis=1)
    return p_mat.astype(BF16), m_mat.astype(BF16), r_mat.astype(BF16), alpha


def _s5_kernel(u_ref, p_ref, m_ref, r_ref, a_ref, y_ref, s_scr, x_scr, *, nc, bsz):
    u = u_ref[0]
    s_scr[...] = jnp.dot(u, p_ref[0], preferred_element_type=F32)
    al = a_ref[0]
    afr = jnp.broadcast_to(al[0:1], (bsz, LANES))
    afi = jnp.broadcast_to(al[1:2], (bsz, LANES))
    abr = jnp.broadcast_to(al[2:3], (bsz, LANES))
    abi = jnp.broadcast_to(al[3:4], (bsz, LANES))
    zero = jnp.zeros((bsz, LANES), F32)

    def step(i, carry):
        fr, fi, br, bi = carry
        rf = pl.multiple_of(i * bsz, bsz)
        rb = pl.multiple_of((nc - 1 - i) * bsz, bsz)
        x_scr[pl.ds(rf, bsz), 0:LANES] = fr
        x_scr[pl.ds(rf, bsz), LANES:2 * LANES] = fi
        x_scr[pl.ds(rb, bsz), 2 * LANES:3 * LANES] = br
        x_scr[pl.ds(rb, bsz), 3 * LANES:4 * LANES] = bi
        sfr = s_scr[pl.ds(rf, bsz), 0:LANES]
        sfi = s_scr[pl.ds(rf, bsz), LANES:2 * LANES]
        sbr = s_scr[pl.ds(rb, bsz), 2 * LANES:3 * LANES]
        sbi = s_scr[pl.ds(rb, bsz), 3 * LANES:4 * LANES]
        return (afr * fr - afi * fi + sfr, afr * fi + afi * fr + sfi,
                abr * br - abi * bi + sbr, abr * bi + abi * br + sbi)

    lax.fori_loop(0, nc, step, (zero, zero, zero, zero))
    y = jnp.dot(u, m_ref[0], preferred_element_type=F32)
    y = y + jnp.dot(x_scr[...].astype(BF16), r_ref[0], preferred_element_type=F32)
    y_ref[0] = y


def _s5(u_grp, p_mat, m_mat, r_mat, alpha, nc, bsz):
    g, rows, _ = u_grp.shape
    kern = functools.partial(_s5_kernel, nc=nc, bsz=bsz)
    mat = pl.BlockSpec((1, SSM_ROW, SSM_ROW), lambda i: (i, 0, 0))
    seq = pl.BlockSpec((1, rows, SSM_ROW), lambda i: (i, 0, 0))
    return pl.pallas_call(
        kern,
        grid=(g,),
        in_specs=[seq, mat, mat, mat, pl.BlockSpec((1, 4, LANES), lambda i: (i, 0, 0))],
        out_specs=seq,
        out_shape=jax.ShapeDtypeStruct((g, rows, SSM_ROW), F32),
        scratch_shapes=[pltpu.VMEM((rows, 4 * LANES), F32), pltpu.VMEM((rows, 4 * LANES), F32)],
        compiler_params=_cparams(("parallel",)),
        name="s5_scan",
    )(u_grp, p_mat, m_mat, r_mat, alpha)


def _post_mix_kernel(x_ref, att_ref, y_ref, wglu_ref, bglu_ref, gs_ref, wout_ref, g2_ref,
                     wr_ref, br_ref, x2_ref, h2_ref, ids_ref, gates_ref):
    y = y_ref[...]
    y = 0.5 * y * (1.0 + jnp.tanh(math.sqrt(2.0 / math.pi) * (y + 0.044715 * (y * y * y))))
    z = jnp.dot(y.astype(BF16), wglu_ref[...], preferred_element_type=F32) + bglu_ref[...]
    y = y * (1.0 / (1.0 + jnp.exp(-z)))
    ms = jnp.mean(y * y, axis=-1, keepdims=True)
    ssm = (y * lax.rsqrt(ms + RMS_EPS) * gs_ref[...]).astype(BF16)
    mix = jnp.dot(att_ref[...], wout_ref[0:W_ATT, :], preferred_element_type=F32)
    mix = mix + jnp.dot(ssm, wout_ref[W_ATT:, :], preferred_element_type=F32)
    x2 = x_ref[...] + mix
    x2_ref[...] = x2
    ms2 = jnp.mean(x2 * x2, axis=-1, keepdims=True)
    h2 = x2 * lax.rsqrt(ms2 + RMS_EPS) * g2_ref[...]
    h2_ref[...] = h2
    logits = jnp.dot(h2, wr_ref[...], preferred_element_type=F32,
                     precision=lax.Precision.HIGHEST) + br_ref[...]
    lane = lax.broadcasted_iota(jnp.int32, logits.shape, 1).astype(F32)
    neg = jnp.float32(-jnp.inf)
    cur = jnp.where(lane < N_EXPERTS, logits, neg)
    ids = jnp.zeros(logits.shape, F32)
    vals = jnp.zeros(logits.shape, F32)
    top = None
    den = None
    for kk in range(TOP_K):
        mx = jnp.max(cur, axis=1, keepdims=True)
        idx = jnp.min(jnp.where(cur == mx, lane, float(LANES)), axis=1, keepdims=True)
        if kk == 0:
            top = mx
        e = jnp.exp(mx - top)
        den = e if den is None else den + e
        ids = jnp.where(lane == kk, idx, ids)
        vals = jnp.where(lane == kk, e, vals)
        cur = jnp.where(lane == idx, neg, cur)
    ids_ref[...] = ids.astype(jnp.int32)
    gates_ref[...] = vals / den


def _post_mix(x2d, att, yssm, wglu, bglu, gs, wout, g2, wr, br):
    t = x2d.shape[0]
    tm = min(ROW_TILE, t)
    row = lambda i: (i, 0)
    const = lambda i: (0, 0)
    return pl.pallas_call(
        _post_mix_kernel,
        grid=(t // tm,),
        in_specs=[pl.BlockSpec((tm, D_MODEL), row),
                  pl.BlockSpec((tm, W_ATT), row),
                  pl.BlockSpec((tm, W_SSM), row),
                  pl.BlockSpec((W_SSM, W_SSM), const),
                  pl.BlockSpec((1, W_SSM), const),
                  pl.BlockSpec((1, W_SSM), const),
                  pl.BlockSpec((D_MODEL, D_MODEL), const),
                  pl.BlockSpec((1, D_MODEL), const),
                  pl.BlockSpec((D_MODEL, LANES), const),
                  pl.BlockSpec((1, LANES), const)],
        out_specs=[pl.BlockSpec((tm, D_MODEL), row),
                   pl.BlockSpec((tm, D_MODEL), row),
                   pl.BlockSpec((tm, LANES), row),
                   pl.BlockSpec((tm, LANES), row)],
        out_shape=[jax.ShapeDtypeStruct((t, D_MODEL), F32),
                   jax.ShapeDtypeStruct((t, D_MODEL), F32),
                   jax.ShapeDtypeStruct((t, LANES), jnp.int32),
                   jax.ShapeDtypeStruct((t, LANES), F32)],
        compiler_params=_cparams(("parallel",)),
        name="post_mix",
    )(x2d, att, yssm, wglu, bglu, gs, wout, g2, wr, br)


def _row_copy(src_hbm, row, dst, dst_row, sem):
    return pltpu.make_async_copy(src_hbm.at[pl.ds(row, 1), :], dst.at[pl.ds(dst_row, 1), :], sem)


def _gather_kernel(idx_ref, h_hbm, o_ref, sem, *, rows):
    def issue(r, c):
        _row_copy(h_hbm, idx_ref[0, 0, r], o_ref, r, sem).start()
        return c

    lax.fori_loop(0, rows, issue, 0)

    def drain(r, c):
        _row_copy(h_hbm, 0, o_ref, r, sem).wait()
        return c

    lax.fori_loop(0, rows, drain, 0)


def _gather_rows(h2, tok_pad):
    n_pad = tok_pad.shape[0]
    rows = GATHER_ROWS
    steps = n_pad // rows
    kern = functools.partial(_gather_kernel, rows=rows)
    return pl.pallas_call(
        kern,
        grid=(steps,),
        in_specs=[pl.BlockSpec((1, 1, rows), lambda i: (i, 0, 0), memory_space=pltpu.SMEM),
                  pl.BlockSpec(memory_space=pl.ANY)],
        out_specs=pl.BlockSpec((rows, D_MODEL), lambda i: (i, 0)),
        out_shape=jax.ShapeDtypeStruct((n_pad, D_MODEL), F32),
        scratch_shapes=[pltpu.SemaphoreType.DMA(())],
        compiler_params=_cparams(("arbitrary",)),
        name="moe_gather",
    )(tok_pad.reshape(steps, 1, rows), h2)


def _expert_kernel(be_ref, nu_ref, x_ref, w1_ref, b1_ref, w2_ref, b2_ref, y_ref):
    @pl.when(pl.program_id(0) < nu_ref[0])
    def _():
        x = x_ref[...].astype(BF16)
        hdn = jnp.dot(x, w1_ref[0], preferred_element_type=F32) + b1_ref[0]
        gate = jnp.minimum(hdn[:, :D_FF], SWIGLU_LIMIT)
        lin = jnp.clip(hdn[:, D_FF:], -SWIGLU_LIMIT, SWIGLU_LIMIT)
        act = gate * (1.0 / (1.0 + jnp.exp(-SWIGLU_ALPHA * gate))) * (lin + 1.0)
        y_ref[...] = jnp.dot(act.astype(BF16), w2_ref[0], preferred_element_type=F32) + b2_ref[0]

    @pl.when(pl.program_id(0) >= nu_ref[0])
    def _():
        y_ref[...] = jnp.zeros(y_ref.shape, F32)


def _experts(block_e, n_used, x_pad, w1, b1, w2, b2):
    n_pad = x_pad.shape[0]
    rows = EXPERT_ROWS
    grid_spec = pltpu.PrefetchScalarGridSpec(
        num_scalar_prefetch=2,
        grid=(n_pad // rows,),
        in_specs=[pl.BlockSpec((rows, D_MODEL), lambda i, be, nu: (i, 0)),
                  pl.BlockSpec((1, D_MODEL, 2 * D_FF), lambda i, be, nu: (be[i], 0, 0)),
                  pl.BlockSpec((1, 1, 2 * D_FF), lambda i, be, nu: (be[i], 0, 0)),
                  pl.BlockSpec((1, D_FF, D_MODEL), lambda i, be, nu: (be[i], 0, 0)),
                  pl.BlockSpec((1, 1, D_MODEL), lambda i, be, nu: (be[i], 0, 0))],
        out_specs=pl.BlockSpec((rows, D_MODEL), lambda i, be, nu: (i, 0)),
    )
    return pl.pallas_call(
        _expert_kernel,
        grid_spec=grid_spec,
        out_shape=jax.ShapeDtypeStruct((n_pad, D_MODEL), F32),
        compiler_params=_cparams(("arbitrary",)),
        name="moe_experts",
    )(block_e, n_used, x_pad, w1, b1, w2, b2)


def _combine_kernel(pos_ref, x2_ref, gates_ref, gf_ref, y_hbm, o_ref, buf, sem, *, rows):
    def issue(r, c):
        for kk in range(TOP_K):
            _row_copy(y_hbm, pos_ref[0, 0, r * TOP_K + kk], buf.at[kk], r, sem).start()
        return c

    lax.fori_loop(0, rows, issue, 0)

    def drain(r, c):
        for kk in range(TOP_K):
            _row_copy(y_hbm, 0, buf.at[kk], r, sem).wait()
        return c

    lax.fori_loop(0, rows, drain, 0)

    gates = gates_ref[...]
    y = x2_ref[...]
    for kk in range(TOP_K):
        y = y + gates[:, kk:kk + 1] * buf[kk]
    ms = jnp.mean(y * y, axis=-1, keepdims=True)
    o_ref[...] = y * lax.rsqrt(ms + RMS_EPS) * gf_ref[...]


def _combine(pos, x2, gates, gf, y_pad):
    t = x2.shape[0]
    rows = min(COMBINE_ROWS, t)
    steps = t // rows
    kern = functools.partial(_combine_kernel, rows=rows)
    row = lambda i: (i, 0)
    return pl.pallas_call(
        kern,
        grid=(steps,),
        in_specs=[pl.BlockSpec((1, 1, rows * TOP_K), lambda i: (i, 0, 0), memory_space=pltpu.SMEM),
                  pl.BlockSpec((rows, D_MODEL), row),
                  pl.BlockSpec((rows, LANES), row),
                  pl.BlockSpec((1, D_MODEL), lambda i: (0, 0)),
                  pl.BlockSpec(memory_space=pl.ANY)],
        out_specs=pl.BlockSpec((rows, D_MODEL), row),
        out_shape=jax.ShapeDtypeStruct((t, D_MODEL), F32),
        scratch_shapes=[pltpu.VMEM((TOP_K, rows, D_MODEL), F32), pltpu.SemaphoreType.DMA(())],
        compiler_params=_cparams(("arbitrary",)),
        name="moe_combine",
    )(pos.reshape(steps, 1, rows * TOP_K), x2, gates, gf, y_pad)


def _dispatch_plan(top_e, n_tok):
    n = n_tok * TOP_K
    blk = EXPERT_ROWS
    flat_e = top_e.reshape(n)
    order = jnp.argsort(flat_e)
    se = flat_e[order]
    st = (order // TOP_K).astype(jnp.int32)
    counts = jnp.bincount(flat_e, length=N_EXPERTS).astype(jnp.int32)
    padded = (counts + blk - 1) // blk * blk
    start = jnp.cumsum(counts) - counts
    pend = jnp.cumsum(padded)
    pstart = pend - padded
    dest = (pstart[se] + jnp.arange(n, dtype=jnp.int32) - start[se]).astype(jnp.int32)
    n_blocks = n // blk + N_EXPERTS
    tok_pad = jnp.zeros((n_blocks * blk,), jnp.int32).at[dest].set(st)
    pos = jnp.zeros((n,), jnp.int32).at[order].set(dest)
    block_e = jnp.minimum(
        jnp.searchsorted(pend, jnp.arange(n_blocks, dtype=jnp.int32) * blk, side='right'),
        N_EXPERTS - 1).astype(jnp.int32)
    n_used = (pend[-1:] // blk).astype(jnp.int32)
    return tok_pad, pos, block_e, n_used


def _trunk(x, prm):
    bsz, l, _ = x.shape
    t = bsz * l
    x2d = x.reshape(t, D_MODEL)
    q, k, v, u = _in_proj(x2d, prm['norm1_g'], prm['w_in'])

    t_att = min(512, l)
    att = _attention(q.reshape(bsz, l, W_ATT), k.reshape(bsz, l, W_ATT), v.reshape(bsz, l, W_ATT),
                     prm['bias_tiles'][t_att], prm['lam'], prm['subln_g'], t_att)

    nc = l // SSM_CHUNK
    u_grp = jnp.transpose(u.reshape(bsz, nc, SSM_CHUNK, N_GROUPS, SSM_GROUP), (3, 1, 0, 2, 4))
    u_grp = u_grp.reshape(N_GROUPS, nc * bsz, SSM_ROW)
    y_grp = _s5(u_grp, prm['ssm_p'], prm['ssm_m'], prm['ssm_r'], prm['ssm_alpha'], nc, bsz)
    yssm = jnp.transpose(y_grp.reshape(N_GROUPS, nc, bsz, SSM_CHUNK, SSM_GROUP), (2, 1, 3, 0, 4))
    yssm = yssm.reshape(t, W_SSM)

    x2, h2, ids, gates = _post_mix(x2d, att.reshape(t, W_ATT), yssm, prm['w_glu'], prm['b_glu'],
                                   prm['ssm_norm_g'], prm['w_out'], prm['norm2_g'],
                                   prm['w_router'], prm['b_router'])

    tok_pad, pos, block_e, n_used = _dispatch_plan(ids[:, :TOP_K], t)
    x_pad = _gather_rows(h2, tok_pad)
    y_pad = _experts(block_e, n_used, x_pad, prm['w_moe1'], prm['b_moe1'], prm['w_moe2'], prm['b_moe2'])
    out = _combine(pos, x2, gates, prm['normf_g'], y_pad)
    return out.reshape(bsz, l, D_MODEL)


def _prepare(seq_lens, rel_bias, norm1_g, w_in, lambda_q1, lambda_k1, lambda_q2, lambda_k2, subln_g,
             ssm_A_re, ssm_A_im, ssm_log_dt, ssm_B_re, ssm_B_im, ssm_C_re, ssm_C_im, ssm_D,
             w_glu, b_glu, ssm_norm_g, w_out, norm2_g, w_router, b_router,
             w_moe1, b_moe1, w_moe2, b_moe2, normf_g):
    layer = 0
    lambda_init = 0.8 - 0.6 * math.exp(-0.3 * layer)
    lam = (jnp.exp(jnp.sum(lambda_q1[layer].astype(F32) * lambda_k1[layer].astype(F32)))
           - jnp.exp(jnp.sum(lambda_q2[layer].astype(F32) * lambda_k2[layer].astype(F32))) + lambda_init)
    p_mat, m_mat, r_mat, alpha = _ssm_matrices(
        ssm_A_re[layer], ssm_A_im[layer], ssm_log_dt[layer], ssm_B_re[layer], ssm_B_im[layer],
        ssm_C_re[layer], ssm_C_im[layer], ssm_D[layer])
    pad_e = LANES - N_EXPERTS
    return {
        'norm1_g': norm1_g[layer].reshape(1, D_MODEL).astype(F32),
        'w_in': w_in[layer].astype(BF16),
        'lam': lam.reshape(1).astype(F32),
        'subln_g': (subln_g[layer].astype(F32) * (1.0 - lambda_init)).reshape(1, V_DIM),
        'bias_tiles': {t: _bias_tiles(rel_bias, t) for t in sorted({min(512, l) for l in seq_lens})},
        'ssm_p': p_mat, 'ssm_m': m_mat, 'ssm_r': r_mat, 'ssm_alpha': alpha,
        'w_glu': w_glu[layer].astype(BF16),
        'b_glu': b_glu[layer].reshape(1, W_SSM).astype(F32),
        'ssm_norm_g': ssm_norm_g[layer].reshape(1, W_SSM).astype(F32),
        'w_out': w_out[layer].astype(BF16),
        'norm2_g': norm2_g[layer].reshape(1, D_MODEL).astype(F32),
        'w_router': jnp.pad(w_router[layer].astype(F32), ((0, 0), (0, pad_e))),
        'b_router': jnp.pad(b_router[layer].astype(F32), (0, pad_e)).reshape(1, LANES),
        'w_moe1': w_moe1[layer].astype(BF16),
        'b_moe1': b_moe1[layer].reshape(N_EXPERTS, 1, 2 * D_FF).astype(F32),
        'w_moe2': w_moe2[layer].astype(BF16),
        'b_moe2': b_moe2[layer].reshape(N_EXPERTS, 1, D_MODEL).astype(F32),
        'normf_g': normf_g.reshape(1, D_MODEL).astype(F32),
    }


def kernel(x_prompt, x_sample, rel_bias, norm1_g, w_in, lambda_q1, lambda_k1, lambda_q2, lambda_k2, subln_g, ssm_A_re, ssm_A_im, ssm_log_dt, ssm_B_re, ssm_B_im, ssm_C_re, ssm_C_im, ssm_D, w_glu, b_glu, ssm_norm_g, w_out, norm2_g, w_router, b_router, w_moe1, b_moe1, w_moe2, b_moe2, normf_g):
    prm = _prepare((x_prompt.shape[1], x_sample.shape[1]), rel_bias, norm1_g, w_in, lambda_q1,
                   lambda_k1, lambda_q2, lambda_k2, subln_g, ssm_A_re, ssm_A_im, ssm_log_dt,
                   ssm_B_re, ssm_B_im, ssm_C_re, ssm_C_im, ssm_D, w_glu, b_glu, ssm_norm_g, w_out,
                   norm2_g, w_router, b_router, w_moe1, b_moe1, w_moe2, b_moe2, normf_g)
    return (_trunk(x_prompt, prm), _trunk(x_sample, prm))
```

```python
import functools
import math

import jax
import jax.numpy as jnp
from jax import lax
from jax.experimental import pallas as pl
from jax.experimental.pallas import tpu as pltpu

F32 = jnp.float32
BF16 = jnp.bfloat16

D_MODEL = 1024
W_ATT = 512
W_SSM = 512
HEAD_DIM = 64
N_HEADS = 4
V_DIM = 2 * HEAD_DIM
SSM_GROUP = 16
N_GROUPS = W_SSM // SSM_GROUP
SSM_STATE = 64
IN_WIDTH = 3 * W_ATT + W_SSM
N_BUCKETS = 32
MAX_DISTANCE = 128
N_EXPERTS = 32
TOP_K = 4
D_FF = D_MODEL
SWIGLU_ALPHA = 1.702
SWIGLU_LIMIT = 7.0
RMS_EPS = 1e-6
ATT_SCALE = HEAD_DIM ** -0.5
LOG2E = math.log2(math.e)

LANES = 128
SSM_CHUNK = 32
SSM_ROW = SSM_CHUNK * SSM_GROUP
ROW_TILE = 512
EXPERT_ROWS = 256
GATHER_ROWS = 256
COMBINE_ROWS = 256
VMEM_LIMIT = 48 * 1024 * 1024


def _cparams(sem):
    return pltpu.CompilerParams(dimension_semantics=sem, vmem_limit_bytes=VMEM_LIMIT)


def _in_proj_kernel(x_ref, g_ref, w_ref, q_ref, k_ref, v_ref, u_ref):
    x = x_ref[...]
    ms = jnp.mean(x * x, axis=-1, keepdims=True)
    h = (x * lax.rsqrt(ms + RMS_EPS) * g_ref[...]).astype(BF16)
    proj = jnp.dot(h, w_ref[...], preferred_element_type=F32)
    q_ref[...] = (proj[:, 0:W_ATT] * (ATT_SCALE * LOG2E)).astype(BF16)
    k_ref[...] = proj[:, W_ATT:2 * W_ATT].astype(BF16)
    v_ref[...] = proj[:, 2 * W_ATT:3 * W_ATT].astype(BF16)
    u_ref[...] = proj[:, 3 * W_ATT:].astype(BF16)


def _in_proj(x2d, g, w_bf16):
    t = x2d.shape[0]
    tm = min(ROW_TILE, t)
    out = jax.ShapeDtypeStruct((t, W_ATT), BF16)
    row = lambda i: (i, 0)
    return pl.pallas_call(
        _in_proj_kernel,
        grid=(t // tm,),
        in_specs=[pl.BlockSpec((tm, D_MODEL), row),
                  pl.BlockSpec((1, D_MODEL), lambda i: (0, 0)),
                  pl.BlockSpec((D_MODEL, IN_WIDTH), lambda i: (0, 0))],
        out_specs=[pl.BlockSpec((tm, W_ATT), row)] * 4,
        out_shape=[out] * 4,
        compiler_params=_cparams(("parallel",)),
        name="in_proj",
    )(x2d, g, w_bf16)


def _t5_bucket(rel):
    half = N_BUCKETS // 2
    max_exact = half // 2
    ret = jnp.where(rel > 0, half, 0).astype(jnp.int32)
    n = jnp.abs(rel)
    nf = jnp.maximum(n, 1).astype(F32)
    large = max_exact + (jnp.log(nf / max_exact) / math.log(MAX_DISTANCE / max_exact)
                         * (half - max_exact)).astype(jnp.int32)
    large = jnp.minimum(large, half - 1)
    return ret + jnp.where(n < max_exact, n, large)


def _bias_tiles(rel_bias, t):
    i = jnp.arange(t, dtype=jnp.int32)
    d = jnp.arange(-2, 3, dtype=jnp.int32)
    rel = d[:, None, None] * t + i[None, None, :] - i[None, :, None]
    onehot = (_t5_bucket(rel)[..., None] == jnp.arange(N_BUCKETS, dtype=jnp.int32)).astype(F32)
    tiles = jnp.einsum('dqkn,nh->hdqk', onehot, rel_bias.astype(F32), precision=lax.Precision.HIGHEST)
    return tiles * LOG2E


def _attn_kernel(lam_ref, q_ref, k_ref, v_ref, bias_ref, g_ref, o_ref,
                 m_scr, l_scr, acc_scr, s_scr, p_scr, *, t, sub, n_iter):
    qi = pl.program_id(2)
    q = q_ref[0]
    lane = lax.broadcasted_iota(jnp.int32, q.shape, 1)
    zero = jnp.zeros_like(q)
    qs = (jnp.where(lane < HEAD_DIM, q, zero), jnp.where(lane >= HEAD_DIM, q, zero))
    nb = t // LANES

    m_scr[...] = jnp.full(m_scr.shape, -jnp.inf, F32)
    l_scr[...] = jnp.zeros(l_scr.shape, F32)
    acc_scr[...] = jnp.zeros(acc_scr.shape, F32)

    def body(j, carry):
        vj = v_ref[0, pl.ds(pl.multiple_of(j * (sub * t), sub * t), sub * t), :]
        for mi in range(2):
            mx = None
            for c in range(sub):
                blk = j * sub + c
                kc = k_ref[0, pl.ds(pl.multiple_of(blk * t, t), t), :]
                s = lax.dot_general(qs[mi], kc, (((1,), (1,)), ((), ())), preferred_element_type=F32)
                s = s + bias_ref[0, jnp.clip(blk - qi, -2, 2) + 2]
                s_scr[mi, :, c * t:(c + 1) * t] = s
                for i in range(nb):
                    piece = s[:, i * LANES:(i + 1) * LANES]
                    mx = piece if mx is None else jnp.maximum(mx, piece)
            m_prev = m_scr[mi]
            m_next = jnp.maximum(m_prev, jnp.max(mx, axis=1, keepdims=True))
            alpha = jnp.exp2(m_prev - m_next)
            m_scr[mi] = m_next
            lsum = None
            for i in range(sub * nb):
                p = jnp.exp2(s_scr[mi, :, i * LANES:(i + 1) * LANES] - m_next)
                lsum = p if lsum is None else lsum + p
                p_scr[mi, :, i * LANES:(i + 1) * LANES] = p.astype(BF16)
            l_scr[mi] = alpha * l_scr[mi] + lsum
            acc_scr[mi] = alpha * acc_scr[mi] + jnp.dot(p_scr[mi], vj, preferred_element_type=F32)
        return carry

    lax.fori_loop(0, n_iter, body, 0)

    o1 = acc_scr[0] / jnp.sum(l_scr[0], axis=1, keepdims=True)
    o2 = acc_scr[1] / jnp.sum(l_scr[1], axis=1, keepdims=True)
    o = o1 - lam_ref[0] * o2
    ms = jnp.mean(o * o, axis=-1, keepdims=True)
    o_ref[0] = (o * lax.rsqrt(ms + RMS_EPS) * g_ref[...]).astype(BF16)


def _attention(q, k, v, bias_tiles, lam, g_scaled, t):
    b, l, _ = q.shape
    nq = l // t
    sub = 2 if nq % 2 == 0 else 1
    kern = functools.partial(_attn_kernel, t=t, sub=sub, n_iter=nq // sub)
    return pl.pallas_call(
        kern,
        grid=(b, N_HEADS, nq),
        in_specs=[pl.BlockSpec(memory_space=pltpu.SMEM),
                  pl.BlockSpec((1, t, V_DIM), lambda bi, h, qi: (bi, qi, h)),
                  pl.BlockSpec((1, l, V_DIM), lambda bi, h, qi: (bi, 0, h)),
                  pl.BlockSpec((1, l, V_DIM), lambda bi, h, qi: (bi, 0, h)),
                  pl.BlockSpec((1, 5, t, t), lambda bi, h, qi: (h, 0, 0, 0)),
                  pl.BlockSpec((1, V_DIM), lambda bi, h, qi: (0, 0))],
        out_specs=pl.BlockSpec((1, t, V_DIM), lambda bi, h, qi: (bi, qi, h)),
        out_shape=jax.ShapeDtypeStruct((b, l, W_ATT), BF16),
        scratch_shapes=[pltpu.VMEM((2, t, LANES), F32),
                        pltpu.VMEM((2, t, LANES), F32),
                        pltpu.VMEM((2, t, V_DIM), F32),
                        pltpu.VMEM((2, t, sub * t), F32),
                        pltpu.VMEM((2, t, sub * t), BF16)],
        compiler_params=_cparams(("parallel", "parallel", "arbitrary")),
        name="diff_attention",
    )(lam, q, k, v, bias_tiles, g_scaled)


def _ssm_matrices(a_re, a_im, log_dt, b_re, b_im, c_re, c_im, d_skip):
    qn, g, p, hc = SSM_CHUNK, N_GROUPS, SSM_STATE, SSM_GROUP
    n = jnp.arange(qn + 1, dtype=F32)
    pw, bbar, cc = [], [], []
    for d in range(2):
        a = lax.complex(a_re[d].astype(F32), a_im[d].astype(F32))
        dt = jnp.exp(log_dt[d].astype(F32))[:, None]
        adt = a * dt
        a_bar = jnp.exp(adt)
        pw.append(jnp.exp(adt[None] * n[:, None, None]))
        bbar.append(((a_bar - 1.0) / a)[:, :, None]
                    * lax.complex(b_re[d].astype(F32), b_im[d].astype(F32)))
        cc.append(lax.complex(c_re[d].astype(F32), c_im[d].astype(F32)))

    hi = lax.Precision.HIGHEST
    kern = [jnp.einsum('gop,tgp,gpi->tgoi', cc[d], pw[d][:qn], bbar[d], precision=hi).real
            for d in range(2)]
    s_idx = jnp.arange(qn)[:, None]
    t_idx = jnp.arange(qn)[None, :]
    lag = t_idx - s_idx
    taus = jnp.arange(qn)
    sel_f = (lag[:, :, None] == taus).astype(F32)
    sel_b = (-lag[:, :, None] == taus).astype(F32)
    kf = jnp.einsum('stu,ugoi->stgoi', sel_f, kern[0], precision=hi)
    kb = jnp.einsum('stu,ugoi->stgoi', sel_b, kern[1], precision=hi)
    skip = (jnp.eye(qn, dtype=F32)[:, :, None, None, None]
            * (jnp.eye(hc, dtype=F32)[None, None, None] * d_skip.astype(F32).reshape(g, hc)[None, None, :, :, None]))
    m_full = kf + kb + skip
    m_mat = jnp.transpose(m_full, (2, 0, 4, 1, 3)).reshape(g, SSM_ROW, SSM_ROW)

    zeros_p = jnp.zeros((g, SSM_ROW, LANES - p), F32)

    def pad_cols(x):
        return jnp.concatenate([x, zeros_p], axis=-1)

    pf = jnp.einsum('sgp,gpi->gsip', pw[0][:qn][::-1], bbar[0]).reshape(g, SSM_ROW, p)
    pb = jnp.einsum('sgp,gpi->gsip', pw[1][:qn], bbar[1]).reshape(g, SSM_ROW, p)
    p_mat = jnp.concatenate([pad_cols(pf.real), pad_cols(pf.imag),
                             pad_cols(pb.real), pad_cols(pb.imag)], axis=-1)

    wf = jnp.einsum('gop,tgp->gpto', cc[0], pw[0][1:qn + 1]).reshape(g, p, SSM_ROW)
    wb = jnp.einsum('gop,tgp->gpto', cc[1], pw[1][1:qn + 1][::-1]).reshape(g, p, SSM_ROW)
    zeros_r = jnp.zeros((g, LANES - p, SSM_ROW), F32)
    r_mat = jnp.concatenate([wf.real, zeros_r, -wf.imag, zeros_r,
                             wb.real, zeros_r, -wb.imag, zeros_r], axis=1)

    zeros_a = jnp.zeros((g, LANES - p), F32)

    def pad_vec(x):
        return jnp.concatenate([x, zeros_a], axis=-1)

    alpha = jnp.stack([pad_vec(pw[0][qn].real), pad_vec(pw[0][qn].imag),
                       pad_vec(pw[1][qn].real), pad_vec(pw[1][qn].imag)], axis=1)
    return p_mat.astype(BF16), m_mat.astype(BF16), r_mat.astype(BF16), alpha


def _s5_kernel(u_ref, p_ref, m_ref, r_ref, a_ref, y_ref, s_scr, x_scr, *, nc, bsz):
    u = u_ref[0]
    s_scr[...] = jnp.dot(u, p_ref[0], preferred_element_type=F32)
    al = a_ref[0]
    afr = jnp.broadcast_to(al[0:1], (bsz, LANES))
    afi = jnp.broadcast_to(al[1:2], (bsz, LANES))
    abr = jnp.broadcast_to(al[2:3], (bsz, LANES))
    abi = jnp.broadcast_to(al[3:4], (bsz, LANES))
    zero = jnp.zeros((bsz, LANES), F32)

    def step(i, carry):
        fr, fi, br, bi = carry
        rf = pl.multiple_of(i * bsz, bsz)
        rb = pl.multiple_of((nc - 1 - i) * bsz, bsz)
        x_scr[pl.ds(rf, bsz), 0:LANES] = fr
        x_scr[pl.ds(rf, bsz), LANES:2 * LANES] = fi
        x_scr[pl.ds(rb, bsz), 2 * LANES:3 * LANES] = br
        x_scr[pl.ds(rb, bsz), 3 * LANES:4 * LANES] = bi
        sfr = s_scr[pl.ds(rf, bsz), 0:LANES]
        sfi = s_scr[pl.ds(rf, bsz), LANES:2 * LANES]
        sbr = s_scr[pl.ds(rb, bsz), 2 * LANES:3 * LANES]
        sbi = s_scr[pl.ds(rb, bsz), 3 * LANES:4 * LANES]
        return (afr * fr - afi * fi + sfr, afr * fi + afi * fr + sfi,
                abr * br - abi * bi + sbr, abr * bi + abi * br + sbi)

    lax.fori_loop(0, nc, step, (zero, zero, zero, zero))
    y = jnp.dot(u, m_ref[0], preferred_element_type=F32)
    y = y + jnp.dot(x_scr[...].astype(BF16), r_ref[0], preferred_element_type=F32)
    y_ref[0] = y


def _s5(u_grp, p_mat, m_mat, r_mat, alpha, nc, bsz):
    g, rows, _ = u_grp.shape
    kern = functools.partial(_s5_kernel, nc=nc, bsz=bsz)
    mat = pl.BlockSpec((1, SSM_ROW, SSM_ROW), lambda i: (i, 0, 0))
    seq = pl.BlockSpec((1, rows, SSM_ROW), lambda i: (i, 0, 0))
    return pl.pallas_call(
        kern,
        grid=(g,),
        in_specs=[seq, mat, mat, mat, pl.BlockSpec((1, 4, LANES), lambda i: (i, 0, 0))],
        out_specs=seq,
        out_shape=jax.ShapeDtypeStruct((g, rows, SSM_ROW), F32),
        scratch_shapes=[pltpu.VMEM((rows, 4 * LANES), F32), pltpu.VMEM((rows, 4 * LANES), F32)],
        compiler_params=_cparams(("parallel",)),
        name="s5_scan",
    )(u_grp, p_mat, m_mat, r_mat, alpha)


def _post_mix_kernel(x_ref, att_ref, y_ref, wglu_ref, bglu_ref, gs_ref, wout_ref, g2_ref,
                     wr_ref, br_ref, x2_ref, h2_ref, ids_ref, gates_ref):
    y = y_ref[...]
    y = 0.5 * y * (1.0 + jnp.tanh(math.sqrt(2.0 / math.pi) * (y + 0.044715 * (y * y * y))))
    z = jnp.dot(y.astype(BF16), wglu_ref[...], preferred_element_type=F32) + bglu_ref[...]
    y = y * (1.0 / (1.0 + jnp.exp(-z)))
    ms = jnp.mean(y * y, axis=-1, keepdims=True)
    ssm = (y * lax.rsqrt(ms + RMS_EPS) * gs_ref[...]).astype(BF16)
    mix = jnp.dot(att_ref[...], wout_ref[0:W_ATT, :], preferred_element_type=F32)
    mix = mix + jnp.dot(ssm, wout_ref[W_ATT:, :], preferred_element_type=F32)
    x2 = x_ref[...] + mix
    x2_ref[...] = x2
    ms2 = jnp.mean(x2 * x2, axis=-1, keepdims=True)
    h2 = x2 * lax.rsqrt(ms2 + RMS_EPS) * g2_ref[...]
    h2_ref[...] = h2
    logits = jnp.dot(h2, wr_ref[...], preferred_element_type=F32,
                     precision=lax.Precision.HIGHEST) + br_ref[...]
    lane = lax.broadcasted_iota(jnp.int32, logits.shape, 1).astype(F32)
    neg = jnp.float32(-jnp.inf)
    cur = jnp.where(lane < N_EXPERTS, logits, neg)
    ids = jnp.zeros(logits.shape, F32)
    vals = jnp.zeros(logits.shape, F32)
    top = None
    den = None
    for kk in range(TOP_K):
        mx = jnp.max(cur, axis=1, keepdims=True)
        idx = jnp.min(jnp.where(cur == mx, lane, float(LANES)), axis=1, keepdims=True)
        if kk == 0:
            top = mx
        e = jnp.exp(mx - top)
        den = e if den is None else den + e
        ids = jnp.where(lane == kk, idx, ids)
        vals = jnp.where(lane == kk, e, vals)
        cur = jnp.where(lane == idx, neg, cur)
    ids_ref[...] = ids.astype(jnp.int32)
    gates_ref[...] = vals / den


def _post_mix(x2d, att, yssm, wglu, bglu, gs, wout, g2, wr, br):
    t = x2d.shape[0]
    tm = min(ROW_TILE, t)
    row = lambda i: (i, 0)
    const = lambda i: (0, 0)
    return pl.pallas_call(
        _post_mix_kernel,
        grid=(t // tm,),
        in_specs=[pl.BlockSpec((tm, D_MODEL), row),
                  pl.BlockSpec((tm, W_ATT), row),
                  pl.BlockSpec((tm, W_SSM), row),
                  pl.BlockSpec((W_SSM, W_SSM), const),
                  pl.BlockSpec((1, W_SSM), const),
                  pl.BlockSpec((1, W_SSM), const),
                  pl.BlockSpec((D_MODEL, D_MODEL), const),
                  pl.BlockSpec((1, D_MODEL), const),
                  pl.BlockSpec((D_MODEL, LANES), const),
                  pl.BlockSpec((1, LANES), const)],
        out_specs=[pl.BlockSpec((tm, D_MODEL), row),
                   pl.BlockSpec((tm, D_MODEL), row),
                   pl.BlockSpec((tm, LANES), row),
                   pl.BlockSpec((tm, LANES), row)],
        out_shape=[jax.ShapeDtypeStruct((t, D_MODEL), F32),
                   jax.ShapeDtypeStruct((t, D_MODEL), F32),
                   jax.ShapeDtypeStruct((t, LANES), jnp.int32),
                   jax.ShapeDtypeStruct((t, LANES), F32)],
        compiler_params=_cparams(("parallel",)),
        name="post_mix",
    )(x2d, att, yssm, wglu, bglu, gs, wout, g2, wr, br)


def _row_copy(src_hbm, row, dst, dst_row, sem):
    return pltpu.make_async_copy(src_hbm.at[pl.ds(row, 1), :], dst.at[pl.ds(dst_row, 1), :], sem)


def _gather_kernel(idx_ref, h_hbm, o_ref, sem, *, rows):
    def issue(r, c):
        _row_copy(h_hbm, idx_ref[0, 0, r], o_ref, r, sem).start()
        return c

    lax.fori_loop(0, rows, issue, 0)

    def drain(r, c):
        _row_copy(h_hbm, 0, o_ref, r, sem).wait()
        return c

    lax.fori_loop(0, rows, drain, 0)


def _gather_rows(h2, tok_pad):
    n_pad = tok_pad.shape[0]
    rows = GATHER_ROWS
    steps = n_pad // rows
    kern = functools.partial(_gather_kernel, rows=rows)
    return pl.pallas_call(
        kern,
        grid=(steps,),
        in_specs=[pl.BlockSpec((1, 1, rows), lambda i: (i, 0, 0), memory_space=pltpu.SMEM),
                  pl.BlockSpec(memory_space=pl.ANY)],
        out_specs=pl.BlockSpec((rows, D_MODEL), lambda i: (i, 0)),
        out_shape=jax.ShapeDtypeStruct((n_pad, D_MODEL), F32),
        scratch_shapes=[pltpu.SemaphoreType.DMA(())],
        compiler_params=_cparams(("arbitrary",)),
        name="moe_gather",
    )(tok_pad.reshape(steps, 1, rows), h2)


def _expert_kernel(be_ref, nu_ref, x_ref, w1_ref, b1_ref, w2_ref, b2_ref, y_ref):
    @pl.when(pl.program_id(0) < nu_ref[0])
    def _():
        x = x_ref[...].astype(BF16)
        hdn = jnp.dot(x, w1_ref[0], preferred_element_type=F32) + b1_ref[0]
        gate = jnp.minimum(hdn[:, :D_FF], SWIGLU_LIMIT)
        lin = jnp.clip(hdn[:, D_FF:], -SWIGLU_LIMIT, SWIGLU_LIMIT)
        act = gate * (1.0 / (1.0 + jnp.exp(-SWIGLU_ALPHA * gate))) * (lin + 1.0)
        y_ref[...] = jnp.dot(act.astype(BF16), w2_ref[0], preferred_element_type=F32) + b2_ref[0]

    @pl.when(pl.program_id(0) >= nu_ref[0])
    def _():
        y_ref[...] = jnp.zeros(y_ref.shape, F32)


def _experts(block_e, n_used, x_pad, w1, b1, w2, b2):
    n_pad = x_pad.shape[0]
    rows = EXPERT_ROWS
    grid_spec = pltpu.PrefetchScalarGridSpec(
        num_scalar_prefetch=2,
        grid=(n_pad // rows,),
        in_specs=[pl.BlockSpec((rows, D_MODEL), lambda i, be, nu: (i, 0)),
                  pl.BlockSpec((1, D_MODEL, 2 * D_FF), lambda i, be, nu: (be[i], 0, 0)),
                  pl.BlockSpec((1, 1, 2 * D_FF), lambda i, be, nu: (be[i], 0, 0)),
                  pl.BlockSpec((1, D_FF, D_MODEL), lambda i, be, nu: (be[i], 0, 0)),
                  pl.BlockSpec((1, 1, D_MODEL), lambda i, be, nu: (be[i], 0, 0))],
        out_specs=pl.BlockSpec((rows, D_MODEL), lambda i, be, nu: (i, 0)),
    )
    return pl.pallas_call(
        _expert_kernel,
        grid_spec=grid_spec,
        out_shape=jax.ShapeDtypeStruct((n_pad, D_MODEL), F32),
        compiler_params=_cparams(("arbitrary",)),
        name="moe_experts",
    )(block_e, n_used, x_pad, w1, b1, w2, b2)


def _combine_kernel(pos_ref, x2_ref, gates_ref, gf_ref, y_hbm, o_ref, buf, sem, *, rows):
    def issue(r, c):
        for kk in range(TOP_K):
            _row_copy(y_hbm, pos_ref[0, 0, r * TOP_K + kk], buf.at[kk], r, sem).start()
        return c

    lax.fori_loop(0, rows, issue, 0)

    def drain(r, c):
        for kk in range(TOP_K):
            _row_copy(y_hbm, 0, buf.at[kk], r, sem).wait()
        return c

    lax.fori_loop(0, rows, drain, 0)

    gates = gates_ref[...]
    y = x2_ref[...]
    for kk in range(TOP_K):
        y = y + gates[:, kk:kk + 1] * buf[kk]
    ms = jnp.mean(y * y, axis=-1, keepdims=True)
    o_ref[...] = y * lax.rsqrt(ms + RMS_EPS) * gf_ref[...]


def _combine(pos, x2, gates, gf, y_pad):
    t = x2.shape[0]
    rows = min(COMBINE_ROWS, t)
    steps = t // rows
    kern = functools.partial(_combine_kernel, rows=rows)
    row = lambda i: (i, 0)
    return pl.pallas_call(
        kern,
        grid=(steps,),
        in_specs=[pl.BlockSpec((1, 1, rows * TOP_K), lambda i: (i, 0, 0), memory_space=pltpu.SMEM),
                  pl.BlockSpec((rows, D_MODEL), row),
                  pl.BlockSpec((rows, LANES), row),
                  pl.BlockSpec((1, D_MODEL), lambda i: (0, 0)),
                  pl.BlockSpec(memory_space=pl.ANY)],
        out_specs=pl.BlockSpec((rows, D_MODEL), row),
        out_shape=jax.ShapeDtypeStruct((t, D_MODEL), F32),
        scratch_shapes=[pltpu.VMEM((TOP_K, rows, D_MODEL), F32), pltpu.SemaphoreType.DMA(())],
        compiler_params=_cparams(("arbitrary",)),
        name="moe_combine",
    )(pos.reshape(steps, 1, rows * TOP_K), x2, gates, gf, y_pad)


def _dispatch_plan(top_e, n_tok):
    n = n_tok * TOP_K
    blk = EXPERT_ROWS
    flat_e = top_e.reshape(n)
    order = jnp.argsort(flat_e)
    se = flat_e[order]
    st = (order // TOP_K).astype(jnp.int32)
    counts = jnp.bincount(flat_e, length=N_EXPERTS).astype(jnp.int32)
    padded = (counts + blk - 1) // blk * blk
    start = jnp.cumsum(counts) - counts
    pend = jnp.cumsum(padded)
    pstart = pend - padded
    dest = (pstart[se] + jnp.arange(n, dtype=jnp.int32) - start[se]).astype(jnp.int32)
    n_blocks = n // blk + N_EXPERTS
    tok_pad = jnp.zeros((n_blocks * blk,), jnp.int32).at[dest].set(st)
    pos = jnp.zeros((n,), jnp.int32).at[order].set(dest)
    block_e = jnp.minimum(
        jnp.searchsorted(pend, jnp.arange(n_blocks, dtype=jnp.int32) * blk, side='right'),
        N_EXPERTS - 1).astype(jnp.int32)
    n_used = (pend[-1:] // blk).astype(jnp.int32)
    return tok_pad, pos, block_e, n_used


def _trunk(x, prm):
    bsz, l, _ = x.shape
    t = bsz * l
    x2d = x.reshape(t, D_MODEL)
    q, k, v, u = _in_proj(x2d, prm['norm1_g'], prm['w_in'])

    t_att = min(512, l)
    att = _attention(q.reshape(bsz, l, W_ATT), k.reshape(bsz, l, W_ATT), v.reshape(bsz, l, W_ATT),
                     prm['bias_tiles'][t_att], prm['lam'], prm['subln_g'], t_att)

    nc = l // SSM_CHUNK
    u_grp = jnp.transpose(u.reshape(bsz, nc, SSM_CHUNK, N_GROUPS, SSM_GROUP), (3, 1, 0, 2, 4))
    u_grp = u_grp.reshape(N_GROUPS, nc * bsz, SSM_ROW)
    y_grp = _s5(u_grp, prm['ssm_p'], prm['ssm_m'], prm['ssm_r'], prm['ssm_alpha'], nc, bsz)
    yssm = jnp.transpose(y_grp.reshape(N_GROUPS, nc, bsz, SSM_CHUNK, SSM_GROUP), (2, 1, 3, 0, 4))
    yssm = yssm.reshape(t, W_SSM)

    x2, h2, ids, gates = _post_mix(x2d, att.reshape(t, W_ATT), yssm, prm['w_glu'], prm['b_glu'],
                                   prm['ssm_norm_g'], prm['w_out'], prm['norm2_g'],
                                   prm['w_router'], prm['b_router'])

    tok_pad, pos, block_e, n_used = _dispatch_plan(ids[:, :TOP_K], t)
    x_pad = _gather_rows(h2, tok_pad)
    y_pad = _experts(block_e, n_used, x_pad, prm['w_moe1'], prm['b_moe1'], prm['w_moe2'], prm['b_moe2'])
    out = _combine(pos, x2, gates, prm['normf_g'], y_pad)
    return out.reshape(bsz, l, D_MODEL)


def _prepare(seq_lens, rel_bias, norm1_g, w_in, lambda_q1, lambda_k1, lambda_q2, lambda_k2, subln_g,
             ssm_A_re, ssm_A_im, ssm_log_dt, ssm_B_re, ssm_B_im, ssm_C_re, ssm_C_im, ssm_D,
             w_glu, b_glu, ssm_norm_g, w_out, norm2_g, w_router, b_router,
             w_moe1, b_moe1, w_moe2, b_moe2, normf_g):
    layer = 0
    lambda_init = 0.8 - 0.6 * math.exp(-0.3 * layer)
    lam = (jnp.exp(jnp.sum(lambda_q1[layer].astype(F32) * lambda_k1[layer].astype(F32)))
           - jnp.exp(jnp.sum(lambda_q2[layer].astype(F32) * lambda_k2[layer].astype(F32))) + lambda_init)
    p_mat, m_mat, r_mat, alpha = _ssm_matrices(
        ssm_A_re[layer], ssm_A_im[layer], ssm_log_dt[layer], ssm_B_re[layer], ssm_B_im[layer],
        ssm_C_re[layer], ssm_C_im[layer], ssm_D[layer])
    pad_e = LANES - N_EXPERTS
    return {
        'norm1_g': norm1_g[layer].reshape(1, D_MODEL).astype(F32),
        'w_in': w_in[layer].astype(BF16),
        'lam': lam.reshape(1).astype(F32),
        'subln_g': (subln_g[layer].astype(F32) * (1.0 - lambda_init)).reshape(1, V_DIM),
        'bias_tiles': {t: _bias_tiles(rel_bias, t) for t in sorted({min(512, l) for l in seq_lens})},
        'ssm_p': p_mat, 'ssm_m': m_mat, 'ssm_r': r_mat, 'ssm_alpha': alpha,
        'w_glu': w_glu[layer].astype(BF16),
        'b_glu': b_glu[layer].reshape(1, W_SSM).astype(F32),
        'ssm_norm_g': ssm_norm_g[layer].reshape(1, W_SSM).astype(F32),
        'w_out': w_out[layer].astype(BF16),
        'norm2_g': norm2_g[layer].reshape(1, D_MODEL).astype(F32),
        'w_router': jnp.pad(w_router[layer].astype(F32), ((0, 0), (0, pad_e))),
        'b_router': jnp.pad(b_router[layer].astype(F32), (0, pad_e)).reshape(1, LANES),
        'w_moe1': w_moe1[layer].astype(BF16),
        'b_moe1': b_moe1[layer].reshape(N_EXPERTS, 1, 2 * D_FF).astype(F32),
        'w_moe2': w_moe2[layer].astype(BF16),
        'b_moe2': b_moe2[layer].reshape(N_EXPERTS, 1, D_MODEL).astype(F32),
        'normf_g': normf_g.reshape(1, D_MODEL).astype(F32),
    }


def kernel(x_prompt, x_sample, rel_bias, norm1_g, w_in, lambda_q1, lambda_k1, lambda_q2, lambda_k2, subln_g, ssm_A_re, ssm_A_im, ssm_log_dt, ssm_B_re, ssm_B_im, ssm_C_re, ssm_C_im, ssm_D, w_glu, b_glu, ssm_norm_g, w_out, norm2_g, w_router, b_router, w_moe1, b_moe1, w_moe2, b_moe2, normf_g):
    prm = _prepare((x_prompt.shape[1], x_sample.shape[1]), rel_bias, norm1_g, w_in, lambda_q1,
                   lambda_k1, lambda_q2, lambda_k2, subln_g, ssm_A_re, ssm_A_im, ssm_log_dt,
                   ssm_B_re, ssm_B_im, ssm_C_re, ssm_C_im, ssm_D, w_glu, b_glu, ssm_norm_g, w_out,
                   norm2_g, w_router, b_router, w_moe1, b_moe1, w_moe2, b_moe2, normf_g)
    return (_trunk(x_prompt, prm), _trunk(x_sample, prm))
```

```python
import functools
import math

import jax
import jax.numpy as jnp
from jax import lax
from jax.experimental import pallas as pl
from jax.experimental.pallas import tpu as pltpu

F32 = jnp.float32
BF16 = jnp.bfloat16

D_MODEL = 1024
W_ATT = 512
W_SSM = 512
HEAD_DIM = 64
N_HEADS = 4
V_DIM = 2 * HEAD_DIM
SSM_GROUP = 16
N_GROUPS = W_SSM // SSM_GROUP
SSM_STATE = 64
IN_WIDTH = 3 * W_ATT + W_SSM
N_BUCKETS = 32
MAX_DISTANCE = 128
N_EXPERTS = 32
TOP_K = 4
D_FF = D_MODEL
SWIGLU_ALPHA = 1.702
SWIGLU_LIMIT = 7.0
RMS_EPS = 1e-6
ATT_SCALE = HEAD_DIM ** -0.5
LOG2E = math.log2(math.e)

LANES = 128
SSM_CHUNK = 32
SSM_ROW = SSM_CHUNK * SSM_GROUP
ROW_TILE = 512
EXPERT_ROWS = 256
GATHER_ROWS = 512
DMA_UNROLL = 8
COMBINE_ROWS = 256
VMEM_LIMIT = 48 * 1024 * 1024


def _cparams(sem):
    return pltpu.CompilerParams(dimension_semantics=sem, vmem_limit_bytes=VMEM_LIMIT)


def _in_proj_kernel(x_ref, g_ref, w_ref, q_ref, k_ref, v_ref, u_ref):
    x = x_ref[...]
    ms = jnp.mean(x * x, axis=-1, keepdims=True)
    h = (x * lax.rsqrt(ms + RMS_EPS) * g_ref[...]).astype(BF16)
    proj = jnp.dot(h, w_ref[...], preferred_element_type=F32)
    q_ref[...] = (proj[:, 0:W_ATT] * (ATT_SCALE * LOG2E)).astype(BF16)
    k_ref[...] = proj[:, W_ATT:2 * W_ATT].astype(BF16)
    v_ref[...] = proj[:, 2 * W_ATT:3 * W_ATT].astype(BF16)
    u_ref[...] = proj[:, 3 * W_ATT:].astype(BF16)


def _in_proj(x2d, g, w_bf16):
    t = x2d.shape[0]
    tm = min(ROW_TILE, t)
    out = jax.ShapeDtypeStruct((t, W_ATT), BF16)
    row = lambda i: (i, 0)
    return pl.pallas_call(
        _in_proj_kernel,
        grid=(t // tm,),
        in_specs=[pl.BlockSpec((tm, D_MODEL), row),
                  pl.BlockSpec((1, D_MODEL), lambda i: (0, 0)),
                  pl.BlockSpec((D_MODEL, IN_WIDTH), lambda i: (0, 0))],
        out_specs=[pl.BlockSpec((tm, W_ATT), row)] * 4,
        out_shape=[out] * 4,
        compiler_params=_cparams(("parallel",)),
        name="in_proj",
    )(x2d, g, w_bf16)


def _t5_bucket(rel):
    half = N_BUCKETS // 2
    max_exact = half // 2
    ret = jnp.where(rel > 0, half, 0).astype(jnp.int32)
    n = jnp.abs(rel)
    nf = jnp.maximum(n, 1).astype(F32)
    large = max_exact + (jnp.log(nf / max_exact) / math.log(MAX_DISTANCE / max_exact)
                         * (half - max_exact)).astype(jnp.int32)
    large = jnp.minimum(large, half - 1)
    return ret + jnp.where(n < max_exact, n, large)


def _bias_tiles(rel_bias, t):
    i = jnp.arange(t, dtype=jnp.int32)
    d = jnp.arange(-2, 3, dtype=jnp.int32)
    rel = d[:, None, None] * t + i[None, None, :] - i[None, :, None]
    onehot = (_t5_bucket(rel)[..., None] == jnp.arange(N_BUCKETS, dtype=jnp.int32)).astype(F32)
    tiles = jnp.einsum('dqkn,nh->hdqk', onehot, rel_bias.astype(F32), precision=lax.Precision.HIGHEST)
    return tiles * LOG2E


def _attn_kernel(lam_ref, q_ref, k_ref, v_ref, bias_ref, g_ref, o_ref,
                 m_scr, l_scr, acc_scr, s_scr, p_scr, *, t, sub, n_iter):
    qi = pl.program_id(2)
    q = q_ref[0]
    lane = lax.broadcasted_iota(jnp.int32, q.shape, 1)
    zero = jnp.zeros_like(q)
    qs = (jnp.where(lane < HEAD_DIM, q, zero), jnp.where(lane >= HEAD_DIM, q, zero))
    nb = t // LANES

    m_scr[...] = jnp.full(m_scr.shape, -jnp.inf, F32)
    l_scr[...] = jnp.zeros(l_scr.shape, F32)
    acc_scr[...] = jnp.zeros(acc_scr.shape, F32)

    def body(j, carry):
        vj = v_ref[0, pl.ds(pl.multiple_of(j * (sub * t), sub * t), sub * t), :]
        for mi in range(2):
            mx = None
            for c in range(sub):
                blk = j * sub + c
                kc = k_ref[0, pl.ds(pl.multiple_of(blk * t, t), t), :]
                s = lax.dot_general(qs[mi], kc, (((1,), (1,)), ((), ())), preferred_element_type=F32)
                s = s + bias_ref[0, jnp.clip(blk - qi, -2, 2) + 2]
                s_scr[mi, :, c * t:(c + 1) * t] = s
                for i in range(nb):
                    piece = s[:, i * LANES:(i + 1) * LANES]
                    mx = piece if mx is None else jnp.maximum(mx, piece)
            m_prev = m_scr[mi]
            m_next = jnp.maximum(m_prev, jnp.max(mx, axis=1, keepdims=True))
            alpha = jnp.exp2(m_prev - m_next)
            m_scr[mi] = m_next
            lsum = None
            for i in range(sub * nb):
                p = jnp.exp2(s_scr[mi, :, i * LANES:(i + 1) * LANES] - m_next)
                lsum = p if lsum is None else lsum + p
                p_scr[mi, :, i * LANES:(i + 1) * LANES] = p.astype(BF16)
            l_scr[mi] = alpha * l_scr[mi] + lsum
            acc_scr[mi] = alpha * acc_scr[mi] + jnp.dot(p_scr[mi], vj, preferred_element_type=F32)
        return carry

    lax.fori_loop(0, n_iter, body, 0)

    o1 = acc_scr[0] / jnp.sum(l_scr[0], axis=1, keepdims=True)
    o2 = acc_scr[1] / jnp.sum(l_scr[1], axis=1, keepdims=True)
    o = o1 - lam_ref[0] * o2
    ms = jnp.mean(o * o, axis=-1, keepdims=True)
    o_ref[0] = (o * lax.rsqrt(ms + RMS_EPS) * g_ref[...]).astype(BF16)


def _attention(q, k, v, bias_tiles, lam, g_scaled, t):
    b, l, _ = q.shape
    nq = l // t
    sub = 2 if nq % 2 == 0 else 1
    kern = functools.partial(_attn_kernel, t=t, sub=sub, n_iter=nq // sub)
    return pl.pallas_call(
        kern,
        grid=(b, N_HEADS, nq),
        in_specs=[pl.BlockSpec(memory_space=pltpu.SMEM),
                  pl.BlockSpec((1, t, V_DIM), lambda bi, h, qi: (bi, qi, h)),
                  pl.BlockSpec((1, l, V_DIM), lambda bi, h, qi: (bi, 0, h)),
                  pl.BlockSpec((1, l, V_DIM), lambda bi, h, qi: (bi, 0, h)),
                  pl.BlockSpec((1, 5, t, t), lambda bi, h, qi: (h, 0, 0, 0)),
                  pl.BlockSpec((1, V_DIM), lambda bi, h, qi: (0, 0))],
        out_specs=pl.BlockSpec((1, t, V_DIM), lambda bi, h, qi: (bi, qi, h)),
        out_shape=jax.ShapeDtypeStruct((b, l, W_ATT), BF16),
        scratch_shapes=[pltpu.VMEM((2, t, LANES), F32),
                        pltpu.VMEM((2, t, LANES), F32),
                        pltpu.VMEM((2, t, V_DIM), F32),
                        pltpu.VMEM((2, t, sub * t), F32),
                        pltpu.VMEM((2, t, sub * t), BF16)],
        compiler_params=_cparams(("parallel", "parallel", "arbitrary")),
        name="diff_attention",
    )(lam, q, k, v, bias_tiles, g_scaled)


def _ssm_matrices(a_re, a_im, log_dt, b_re, b_im, c_re, c_im, d_skip):
    qn, g, p, hc = SSM_CHUNK, N_GROUPS, SSM_STATE, SSM_GROUP
    n = jnp.arange(qn + 1, dtype=F32)
    pw, bbar, cc = [], [], []
    for d in range(2):
        a = lax.complex(a_re[d].astype(F32), a_im[d].astype(F32))
        dt = jnp.exp(log_dt[d].astype(F32))[:, None]
        adt = a * dt
        a_bar = jnp.exp(adt)
        pw.append(jnp.exp(adt[None] * n[:, None, None]))
        bbar.append(((a_bar - 1.0) / a)[:, :, None]
                    * lax.complex(b_re[d].astype(F32), b_im[d].astype(F32)))
        cc.append(lax.complex(c_re[d].astype(F32), c_im[d].astype(F32)))

    hi = lax.Precision.HIGHEST
    kern = [jnp.einsum('gop,tgp,gpi->tgoi', cc[d], pw[d][:qn], bbar[d], precision=hi).real
            for d in range(2)]
    s_idx = jnp.arange(qn)[:, None]
    t_idx = jnp.arange(qn)[None, :]
    lag = t_idx - s_idx
    taus = jnp.arange(qn)
    sel_f = (lag[:, :, None] == taus).astype(F32)
    sel_b = (-lag[:, :, None] == taus).astype(F32)
    kf = jnp.einsum('stu,ugoi->stgoi', sel_f, kern[0], precision=hi)
    kb = jnp.einsum('stu,ugoi->stgoi', sel_b, kern[1], precision=hi)
    skip = (jnp.eye(qn, dtype=F32)[:, :, None, None, None]
            * (jnp.eye(hc, dtype=F32)[None, None, None] * d_skip.astype(F32).reshape(g, hc)[None, None, :, :, None]))
    m_full = kf + kb + skip
    m_mat = jnp.transpose(m_full, (2, 0, 4, 1, 3)).reshape(g, SSM_ROW, SSM_ROW)

    zeros_p = jnp.zeros((g, SSM_ROW, LANES - p), F32)

    def pad_cols(x):
        return jnp.concatenate([x, zeros_p], axis=-1)

    pf = jnp.einsum('sgp,gpi->gsip', pw[0][:qn][::-1], bbar[0]).reshape(g, SSM_ROW, p)
    pb = jnp.einsum('sgp,gpi->gsip', pw[1][:qn], bbar[1]).reshape(g, SSM_ROW, p)
    p_mat = jnp.concatenate([pad_cols(pf.real), pad_cols(pf.imag),
                             pad_cols(pb.real), pad_cols(pb.imag)], axis=-1)

    wf = jnp.einsum('gop,tgp->gpto', cc[0], pw[0][1:qn + 1]).reshape(g, p, SSM_ROW)
    wb = jnp.einsum('gop,tgp->gpto', cc[1], pw[1][1:qn + 1][::-1]).reshape(g, p, SSM_ROW)
    zeros_r = jnp.zeros((g, LANES - p, SSM_ROW), F32)
    r_mat = jnp.concatenate([wf.real, zeros_r, -wf.imag, zeros_r,
                             wb.real, zeros_r, -wb.imag, zeros_r], axis=1)

    zeros_a = jnp.zeros((g, LANES - p), F32)

    def pad_vec(x):
        return jnp.concatenate([x, zeros_a], axis=-1)

    alpha = jnp.stack([pad_vec(pw[0][qn].real), pad_vec(pw[0][qn].imag),
                       pad_vec(pw[1][qn].real), pad_vec(pw[1][qn].imag)], axis=1)
    return p_mat.astype(BF16), m_mat.astype(BF16), r_mat.astype(BF16), alpha


def _s5_kernel(u_ref, p_ref, m_ref, r_ref, a_ref, y_ref, s_scr, x_scr, *, nc, bsz):
    u = u_ref[0]
    s_scr[...] = jnp.dot(u, p_ref[0], preferred_element_type=F32)
    al = a_ref[0]
    afr = jnp.broadcast_to(al[0:1], (bsz, LANES))
    afi = jnp.broadcast_to(al[1:2], (bsz, LANES))
    abr = jnp.broadcast_to(al[2:3], (bsz, LANES))
    abi = jnp.broadcast_to(al[3:4], (bsz, LANES))
    zero = jnp.zeros((bsz, LANES), F32)

    def step(i, carry):
        fr, fi, br, bi = carry
        rf = pl.multiple_of(i * bsz, bsz)
        rb = pl.multiple_of((nc - 1 - i) * bsz, bsz)
        x_scr[pl.ds(rf, bsz), 0:LANES] = fr
        x_scr[pl.ds(rf, bsz), LANES:2 * LANES] = fi
        x_scr[pl.ds(rb, bsz), 2 * LANES:3 * LANES] = br
        x_scr[pl.ds(rb, bsz), 3 * LANES:4 * LANES] = bi
        sfr = s_scr[pl.ds(rf, bsz), 0:LANES]
        sfi = s_scr[pl.ds(rf, bsz), LANES:2 * LANES]
        sbr = s_scr[pl.ds(rb, bsz), 2 * LANES:3 * LANES]
        sbi = s_scr[pl.ds(rb, bsz), 3 * LANES:4 * LANES]
        return (afr * fr - afi * fi + sfr, afr * fi + afi * fr + sfi,
                abr * br - abi * bi + sbr, abr * bi + abi * br + sbi)

    lax.fori_loop(0, nc, step, (zero, zero, zero, zero))
    y = jnp.dot(u, m_ref[0], preferred_element_type=F32)
    y = y + jnp.dot(x_scr[...].astype(BF16), r_ref[0], preferred_element_type=F32)
    y_ref[0] = y


def _s5(u_grp, p_mat, m_mat, r_mat, alpha, nc, bsz):
    g, rows, _ = u_grp.shape
    kern = functools.partial(_s5_kernel, nc=nc, bsz=bsz)
    mat = pl.BlockSpec((1, SSM_ROW, SSM_ROW), lambda i: (i, 0, 0))
    seq = pl.BlockSpec((1, rows, SSM_ROW), lambda i: (i, 0, 0))
    return pl.pallas_call(
        kern,
        grid=(g,),
        in_specs=[seq, mat, mat, mat, pl.BlockSpec((1, 4, LANES), lambda i: (i, 0, 0))],
        out_specs=seq,
        out_shape=jax.ShapeDtypeStruct((g, rows, SSM_ROW), F32),
        scratch_shapes=[pltpu.VMEM((rows, 4 * LANES), F32), pltpu.VMEM((rows, 4 * LANES), F32)],
        compiler_params=_cparams(("parallel",)),
        name="s5_scan",
    )(u_grp, p_mat, m_mat, r_mat, alpha)


def _post_mix_kernel(x_ref, att_ref, y_ref, wglu_ref, bglu_ref, gs_ref, wout_ref, g2_ref,
                     wr_ref, br_ref, x2_ref, h2_ref, ids_ref, gates_ref):
    y = y_ref[...]
    y = 0.5 * y * (1.0 + jnp.tanh(math.sqrt(2.0 / math.pi) * (y + 0.044715 * (y * y * y))))
    z = jnp.dot(y.astype(BF16), wglu_ref[...], preferred_element_type=F32) + bglu_ref[...]
    y = y * (1.0 / (1.0 + jnp.exp(-z)))
    ms = jnp.mean(y * y, axis=-1, keepdims=True)
    ssm = (y * lax.rsqrt(ms + RMS_EPS) * gs_ref[...]).astype(BF16)
    mix = jnp.dot(att_ref[...], wout_ref[0:W_ATT, :], preferred_element_type=F32)
    mix = mix + jnp.dot(ssm, wout_ref[W_ATT:, :], preferred_element_type=F32)
    x2 = x_ref[...] + mix
    x2_ref[...] = x2
    ms2 = jnp.mean(x2 * x2, axis=-1, keepdims=True)
    h2 = x2 * lax.rsqrt(ms2 + RMS_EPS) * g2_ref[...]
    h2_ref[...] = h2
    logits = jnp.dot(h2, wr_ref[...], preferred_element_type=F32,
                     precision=lax.Precision.HIGHEST) + br_ref[...]
    lane = lax.broadcasted_iota(jnp.int32, logits.shape, 1).astype(F32)
    neg = jnp.float32(-jnp.inf)
    cur = jnp.where(lane < N_EXPERTS, logits, neg)
    ids = jnp.zeros(logits.shape, F32)
    vals = jnp.zeros(logits.shape, F32)
    top = None
    den = None
    for kk in range(TOP_K):
        mx = jnp.max(cur, axis=1, keepdims=True)
        idx = jnp.min(jnp.where(cur == mx, lane, float(LANES)), axis=1, keepdims=True)
        if kk == 0:
            top = mx
        e = jnp.exp(mx - top)
        den = e if den is None else den + e
        ids = jnp.where(lane == kk, idx, ids)
        vals = jnp.where(lane == kk, e, vals)
        cur = jnp.where(lane == idx, neg, cur)
    ids_ref[...] = ids.astype(jnp.int32)
    gates_ref[...] = vals / den


def _post_mix(x2d, att, yssm, wglu, bglu, gs, wout, g2, wr, br):
    t = x2d.shape[0]
    tm = min(ROW_TILE, t)
    row = lambda i: (i, 0)
    const = lambda i: (0, 0)
    return pl.pallas_call(
        _post_mix_kernel,
        grid=(t // tm,),
        in_specs=[pl.BlockSpec((tm, D_MODEL), row),
                  pl.BlockSpec((tm, W_ATT), row),
                  pl.BlockSpec((tm, W_SSM), row),
                  pl.BlockSpec((W_SSM, W_SSM), const),
                  pl.BlockSpec((1, W_SSM), const),
                  pl.BlockSpec((1, W_SSM), const),
                  pl.BlockSpec((D_MODEL, D_MODEL), const),
                  pl.BlockSpec((1, D_MODEL), const),
                  pl.BlockSpec((D_MODEL, LANES), const),
                  pl.BlockSpec((1, LANES), const)],
        out_specs=[pl.BlockSpec((tm, D_MODEL), row),
                   pl.BlockSpec((tm, D_MODEL), row),
                   pl.BlockSpec((tm, LANES), row),
                   pl.BlockSpec((tm, LANES), row)],
        out_shape=[jax.ShapeDtypeStruct((t, D_MODEL), F32),
                   jax.ShapeDtypeStruct((t, D_MODEL), F32),
                   jax.ShapeDtypeStruct((t, LANES), jnp.int32),
                   jax.ShapeDtypeStruct((t, LANES), F32)],
        compiler_params=_cparams(("parallel",)),
        name="post_mix",
    )(x2d, att, yssm, wglu, bglu, gs, wout, g2, wr, br)


def _row_copy(src_hbm, row, dst, dst_row, sem):
    return pltpu.make_async_copy(src_hbm.at[pl.ds(row, 1), :], dst.at[pl.ds(dst_row, 1), :], sem)


def _gather_kernel(idx_ref, h_hbm, o_hbm, sem, *, rows):
    i = pl.program_id(0)
    base = i * rows

    def issue(g, c):
        for u in range(DMA_UNROLL):
            r = g * DMA_UNROLL + u
            _row_copy(h_hbm, idx_ref[0, 0, r], o_hbm, base + r, sem).start()
        return c

    lax.fori_loop(0, rows // DMA_UNROLL, issue, 0)

    def wait_block():
        pltpu.make_async_copy(h_hbm.at[pl.ds(0, rows), :], o_hbm.at[pl.ds(0, rows), :], sem).wait()

    @pl.when(i > 0)
    def _():
        wait_block()

    @pl.when(i == pl.num_programs(0) - 1)
    def _():
        wait_block()


def _gather_rows(h2, tok_pad):
    n_pad = tok_pad.shape[0]
    rows = GATHER_ROWS
    steps = n_pad // rows
    kern = functools.partial(_gather_kernel, rows=rows)
    return pl.pallas_call(
        kern,
        grid=(steps,),
        in_specs=[pl.BlockSpec((1, 1, rows), lambda i: (i, 0, 0), memory_space=pltpu.SMEM),
                  pl.BlockSpec(memory_space=pl.ANY)],
        out_specs=pl.BlockSpec(memory_space=pl.ANY),
        out_shape=jax.ShapeDtypeStruct((n_pad, D_MODEL), F32),
        scratch_shapes=[pltpu.SemaphoreType.DMA(())],
        compiler_params=_cparams(("arbitrary",)),
        name="moe_gather",
    )(tok_pad.reshape(steps, 1, rows), h2)


def _expert_kernel(be_ref, nu_ref, x_ref, w1_ref, b1_ref, w2_ref, b2_ref, y_ref):
    @pl.when(pl.program_id(0) < nu_ref[0])
    def _():
        x = x_ref[...].astype(BF16)
        hdn = jnp.dot(x, w1_ref[0], preferred_element_type=F32) + b1_ref[0]
        gate = jnp.minimum(hdn[:, :D_FF], SWIGLU_LIMIT)
        lin = jnp.clip(hdn[:, D_FF:], -SWIGLU_LIMIT, SWIGLU_LIMIT)
        act = gate * (1.0 / (1.0 + jnp.exp(-SWIGLU_ALPHA * gate))) * (lin + 1.0)
        y_ref[...] = jnp.dot(act.astype(BF16), w2_ref[0], preferred_element_type=F32) + b2_ref[0]

    @pl.when(pl.program_id(0) >= nu_ref[0])
    def _():
        y_ref[...] = jnp.zeros(y_ref.shape, F32)


def _experts(block_e, n_used, x_pad, w1, b1, w2, b2):
    n_pad = x_pad.shape[0]
    rows = EXPERT_ROWS
    grid_spec = pltpu.PrefetchScalarGridSpec(
        num_scalar_prefetch=2,
        grid=(n_pad // rows,),
        in_specs=[pl.BlockSpec((rows, D_MODEL), lambda i, be, nu: (i, 0)),
                  pl.BlockSpec((1, D_MODEL, 2 * D_FF), lambda i, be, nu: (be[i], 0, 0)),
                  pl.BlockSpec((1, 1, 2 * D_FF), lambda i, be, nu: (be[i], 0, 0)),
                  pl.BlockSpec((1, D_FF, D_MODEL), lambda i, be, nu: (be[i], 0, 0)),
                  pl.BlockSpec((1, 1, D_MODEL), lambda i, be, nu: (be[i], 0, 0))],
        out_specs=pl.BlockSpec((rows, D_MODEL), lambda i, be, nu: (i, 0)),
    )
    return pl.pallas_call(
        _expert_kernel,
        grid_spec=grid_spec,
        out_shape=jax.ShapeDtypeStruct((n_pad, D_MODEL), F32),
        compiler_params=_cparams(("arbitrary",)),
        name="moe_experts",
    )(block_e, n_used, x_pad, w1, b1, w2, b2)


def _combine_kernel(pos_ref, nxt_ref, x2_ref, gates_ref, gf_ref, y_hbm, o_ref, buf, sem, *, rows):
    i = pl.program_id(0)
    n = pl.num_programs(0)
    slot = i % 2

    def issue(idx_ref, dst_slot):
        def body(r, c):
            for u in range(2):
                for kk in range(TOP_K):
                    row = r * 2 + u
                    _row_copy(y_hbm, idx_ref[0, 0, row * TOP_K + kk], buf.at[dst_slot, kk], row,
                              sem.at[dst_slot]).start()
            return c

        lax.fori_loop(0, rows // 2, body, 0)

    @pl.when(i == 0)
    def _():
        issue(pos_ref, 0)

    @pl.when(i + 1 < n)
    def _():
        issue(nxt_ref, 1 - slot)

    for kk in range(TOP_K):
        pltpu.make_async_copy(y_hbm.at[pl.ds(0, rows), :], buf.at[slot, kk], sem.at[slot]).wait()

    gates = gates_ref[...]
    y = x2_ref[...]
    for kk in range(TOP_K):
        y = y + gates[:, kk:kk + 1] * buf[slot, kk]
    ms = jnp.mean(y * y, axis=-1, keepdims=True)
    o_ref[...] = y * lax.rsqrt(ms + RMS_EPS) * gf_ref[...]


def _combine(pos, x2, gates, gf, y_pad):
    t = x2.shape[0]
    rows = min(COMBINE_ROWS, t)
    steps = t // rows
    kern = functools.partial(_combine_kernel, rows=rows)
    row = lambda i: (i, 0)
    pos3 = pos.reshape(steps, 1, rows * TOP_K)
    return pl.pallas_call(
        kern,
        grid=(steps,),
        in_specs=[pl.BlockSpec((1, 1, rows * TOP_K), lambda i: (i, 0, 0), memory_space=pltpu.SMEM),
                  pl.BlockSpec((1, 1, rows * TOP_K), lambda i: (jnp.minimum(i + 1, steps - 1), 0, 0),
                               memory_space=pltpu.SMEM),
                  pl.BlockSpec((rows, D_MODEL), row),
                  pl.BlockSpec((rows, LANES), row),
                  pl.BlockSpec((1, D_MODEL), lambda i: (0, 0)),
                  pl.BlockSpec(memory_space=pl.ANY)],
        out_specs=pl.BlockSpec((rows, D_MODEL), row),
        out_shape=jax.ShapeDtypeStruct((t, D_MODEL), F32),
        scratch_shapes=[pltpu.VMEM((2, TOP_K, rows, D_MODEL), F32), pltpu.SemaphoreType.DMA((2,))],
        compiler_params=_cparams(("arbitrary",)),
        name="moe_combine",
    )(pos3, pos3, x2, gates, gf, y_pad)


def _dispatch_plan(top_e, n_tok):
    n = n_tok * TOP_K
    blk = EXPERT_ROWS
    flat_e = top_e.reshape(n)
    order = jnp.argsort(flat_e).astype(jnp.int32)
    rank = jnp.argsort(order).astype(jnp.int32)
    st = order // TOP_K
    experts = jnp.arange(N_EXPERTS, dtype=jnp.int32)
    counts = jnp.sum((flat_e[:, None] == experts).astype(jnp.int32), axis=0)
    padded = (counts + blk - 1) // blk * blk
    start = jnp.cumsum(counts) - counts
    pend = jnp.cumsum(padded)
    pstart = pend - padded
    pos = (pstart - start)[flat_e] + rank
    n_blocks = n // blk + N_EXPERTS
    block_e = jnp.minimum(
        jnp.searchsorted(pend, jnp.arange(n_blocks, dtype=jnp.int32) * blk, side='right'),
        N_EXPERTS - 1).astype(jnp.int32)
    slot = jnp.arange(n_blocks * blk, dtype=jnp.int32)
    slot_e = jnp.repeat(block_e, blk)
    off = slot - pstart[slot_e]
    src = jnp.clip(start[slot_e] + off, 0, n - 1)
    tok_pad = jnp.where(off < counts[slot_e], st[src], 0)
    n_used = (pend[-1:] // blk).astype(jnp.int32)
    return tok_pad, pos, block_e, n_used


def _trunk(x, prm):
    bsz, l, _ = x.shape
    t = bsz * l
    x2d = x.reshape(t, D_MODEL)
    q, k, v, u = _in_proj(x2d, prm['norm1_g'], prm['w_in'])

    t_att = min(512, l)
    att = _attention(q.reshape(bsz, l, W_ATT), k.reshape(bsz, l, W_ATT), v.reshape(bsz, l, W_ATT),
                     prm['bias_tiles'][t_att], prm['lam'], prm['subln_g'], t_att)

    nc = l // SSM_CHUNK
    u_grp = jnp.transpose(u.reshape(bsz, nc, SSM_CHUNK, N_GROUPS, SSM_GROUP), (3, 1, 0, 2, 4))
    u_grp = u_grp.reshape(N_GROUPS, nc * bsz, SSM_ROW)
    y_grp = _s5(u_grp, prm['ssm_p'], prm['ssm_m'], prm['ssm_r'], prm['ssm_alpha'], nc, bsz)
    yssm = jnp.transpose(y_grp.reshape(N_GROUPS, nc, bsz, SSM_CHUNK, SSM_GROUP), (2, 1, 3, 0, 4))
    yssm = yssm.reshape(t, W_SSM)

    x2, h2, ids, gates = _post_mix(x2d, att.reshape(t, W_ATT), yssm, prm['w_glu'], prm['b_glu'],
                                   prm['ssm_norm_g'], prm['w_out'], prm['norm2_g'],
                                   prm['w_router'], prm['b_router'])

    tok_pad, pos, block_e, n_used = _dispatch_plan(ids[:, :TOP_K], t)
    x_pad = _gather_rows(h2, tok_pad)
    y_pad = _experts(block_e, n_used, x_pad, prm['w_moe1'], prm['b_moe1'], prm['w_moe2'], prm['b_moe2'])
    out = _combine(pos, x2, gates, prm['normf_g'], y_pad)
    return out.reshape(bsz, l, D_MODEL)


def _prepare(seq_lens, rel_bias, norm1_g, w_in, lambda_q1, lambda_k1, lambda_q2, lambda_k2, subln_g,
             ssm_A_re, ssm_A_im, ssm_log_dt, ssm_B_re, ssm_B_im, ssm_C_re, ssm_C_im, ssm_D,
             w_glu, b_glu, ssm_norm_g, w_out, norm2_g, w_router, b_router,
             w_moe1, b_moe1, w_moe2, b_moe2, normf_g):
    layer = 0
    lambda_init = 0.8 - 0.6 * math.exp(-0.3 * layer)
    lam = (jnp.exp(jnp.sum(lambda_q1[layer].astype(F32) * lambda_k1[layer].astype(F32)))
           - jnp.exp(jnp.sum(lambda_q2[layer].astype(F32) * lambda_k2[layer].astype(F32))) + lambda_init)
    p_mat, m_mat, r_mat, alpha = _ssm_matrices(
        ssm_A_re[layer], ssm_A_im[layer], ssm_log_dt[layer], ssm_B_re[layer], ssm_B_im[layer],
        ssm_C_re[layer], ssm_C_im[layer], ssm_D[layer])
    pad_e = LANES - N_EXPERTS
    return {
        'norm1_g': norm1_g[layer].reshape(1, D_MODEL).astype(F32),
        'w_in': w_in[layer].astype(BF16),
        'lam': lam.reshape(1).astype(F32),
        'subln_g': (subln_g[layer].astype(F32) * (1.0 - lambda_init)).reshape(1, V_DIM),
        'bias_tiles': {t: _bias_tiles(rel_bias, t) for t in sorted({min(512, l) for l in seq_lens})},
        'ssm_p': p_mat, 'ssm_m': m_mat, 'ssm_r': r_mat, 'ssm_alpha': alpha,
        'w_glu': w_glu[layer].astype(BF16),
        'b_glu': b_glu[layer].reshape(1, W_SSM).astype(F32),
        'ssm_norm_g': ssm_norm_g[layer].reshape(1, W_SSM).astype(F32),
        'w_out': w_out[layer].astype(BF16),
        'norm2_g': norm2_g[layer].reshape(1, D_MODEL).astype(F32),
        'w_router': jnp.pad(w_router[layer].astype(F32), ((0, 0), (0, pad_e))),
        'b_router': jnp.pad(b_router[layer].astype(F32), (0, pad_e)).reshape(1, LANES),
        'w_moe1': w_moe1[layer].astype(BF16),
        'b_moe1': b_moe1[layer].reshape(N_EXPERTS, 1, 2 * D_FF).astype(F32),
        'w_moe2': w_moe2[layer].astype(BF16),
        'b_moe2': b_moe2[layer].reshape(N_EXPERTS, 1, D_MODEL).astype(F32),
        'normf_g': normf_g.reshape(1, D_MODEL).astype(F32),
    }


def kernel(x_prompt, x_sample, rel_bias, norm1_g, w_in, lambda_q1, lambda_k1, lambda_q2, lambda_k2, subln_g, ssm_A_re, ssm_A_im, ssm_log_dt, ssm_B_re, ssm_B_im, ssm_C_re, ssm_C_im, ssm_D, w_glu, b_glu, ssm_norm_g, w_out, norm2_g, w_router, b_router, w_moe1, b_moe1, w_moe2, b_moe2, normf_g):
    prm = _prepare((x_prompt.shape[1], x_sample.shape[1]), rel_bias, norm1_g, w_in, lambda_q1,
                   lambda_k1, lambda_q2, lambda_k2, subln_g, ssm_A_re, ssm_A_im, ssm_log_dt,
                   ssm_B_re, ssm_B_im, ssm_C_re, ssm_C_im, ssm_D, w_glu, b_glu, ssm_norm_g, w_out,
                   norm2_g, w_router, b_router, w_moe1, b_moe1, w_moe2, b_moe2, normf_g)
    return (_trunk(x_prompt, prm), _trunk(x_sample, prm))
```

```python
import functools
import math

import jax
import jax.numpy as jnp
from jax import lax
from jax.experimental import pallas as pl
from jax.experimental.pallas import tpu as pltpu

F32 = jnp.float32
BF16 = jnp.bfloat16

D_MODEL = 1024
W_ATT = 512
W_SSM = 512
HEAD_DIM = 64
N_HEADS = 4
V_DIM = 2 * HEAD_DIM
SSM_GROUP = 16
N_GROUPS = W_SSM // SSM_GROUP
SSM_STATE = 64
IN_WIDTH = 3 * W_ATT + W_SSM
N_BUCKETS = 32
MAX_DISTANCE = 128
N_EXPERTS = 32
TOP_K = 4
D_FF = D_MODEL
SWIGLU_ALPHA = 1.702
SWIGLU_LIMIT = 7.0
RMS_EPS = 1e-6
ATT_SCALE = HEAD_DIM ** -0.5
LOG2E = math.log2(math.e)

LANES = 128
SSM_CHUNK = 32
SSM_ROW = SSM_CHUNK * SSM_GROUP
ROW_TILE = 512
EXPERT_ROWS = 256
GATHER_ROWS = 512
DMA_UNROLL = 8
COMBINE_ROWS = 256
VMEM_LIMIT = 48 * 1024 * 1024


def _cparams(sem):
    return pltpu.CompilerParams(dimension_semantics=sem, vmem_limit_bytes=VMEM_LIMIT)


def _in_proj_kernel(x_ref, g_ref, w_ref, q_ref, k_ref, v_ref, u_ref):
    x = x_ref[...]
    ms = jnp.mean(x * x, axis=-1, keepdims=True)
    h = (x * lax.rsqrt(ms + RMS_EPS) * g_ref[...]).astype(BF16)
    proj = jnp.dot(h, w_ref[...], preferred_element_type=F32)
    q_ref[...] = (proj[:, 0:W_ATT] * (ATT_SCALE * LOG2E)).astype(BF16)
    k_ref[...] = proj[:, W_ATT:2 * W_ATT].astype(BF16)
    v_ref[...] = proj[:, 2 * W_ATT:3 * W_ATT].astype(BF16)
    u_ref[...] = proj[:, 3 * W_ATT:].astype(BF16)


def _in_proj(x2d, g, w_bf16):
    t = x2d.shape[0]
    tm = min(ROW_TILE, t)
    out = jax.ShapeDtypeStruct((t, W_ATT), BF16)
    row = lambda i: (i, 0)
    return pl.pallas_call(
        _in_proj_kernel,
        grid=(t // tm,),
        in_specs=[pl.BlockSpec((tm, D_MODEL), row),
                  pl.BlockSpec((1, D_MODEL), lambda i: (0, 0)),
                  pl.BlockSpec((D_MODEL, IN_WIDTH), lambda i: (0, 0))],
        out_specs=[pl.BlockSpec((tm, W_ATT), row)] * 4,
        out_shape=[out] * 4,
        compiler_params=_cparams(("parallel",)),
        name="in_proj",
    )(x2d, g, w_bf16)


def _t5_bucket(rel):
    half = N_BUCKETS // 2
    max_exact = half // 2
    ret = jnp.where(rel > 0, half, 0).astype(jnp.int32)
    n = jnp.abs(rel)
    nf = jnp.maximum(n, 1).astype(F32)
    large = max_exact + (jnp.log(nf / max_exact) / math.log(MAX_DISTANCE / max_exact)
                         * (half - max_exact)).astype(jnp.int32)
    large = jnp.minimum(large, half - 1)
    return ret + jnp.where(n < max_exact, n, large)


def _bias_tiles(rel_bias, t):
    i = jnp.arange(t, dtype=jnp.int32)
    d = jnp.arange(-2, 3, dtype=jnp.int32)
    rel = d[:, None, None] * t + i[None, None, :] - i[None, :, None]
    onehot = (_t5_bucket(rel)[..., None] == jnp.arange(N_BUCKETS, dtype=jnp.int32)).astype(F32)
    tiles = jnp.einsum('dqkn,nh->hdqk', onehot, rel_bias.astype(F32), precision=lax.Precision.HIGHEST)
    return tiles * LOG2E


def _attn_kernel(lam_ref, q_ref, k_ref, v_ref, bias_ref, g_ref, o_ref,
                 m_scr, l_scr, acc_scr, s_scr, p_scr, *, t, sub, n_iter):
    qi = pl.program_id(2)
    q = q_ref[0]
    lane = lax.broadcasted_iota(jnp.int32, q.shape, 1)
    zero = jnp.zeros_like(q)
    qs = (jnp.where(lane < HEAD_DIM, q, zero), jnp.where(lane >= HEAD_DIM, q, zero))
    nb = t // LANES

    m_scr[...] = jnp.full(m_scr.shape, -jnp.inf, F32)
    l_scr[...] = jnp.zeros(l_scr.shape, F32)
    acc_scr[...] = jnp.zeros(acc_scr.shape, F32)

    def body(j, carry):
        vj = v_ref[0, pl.ds(pl.multiple_of(j * (sub * t), sub * t), sub * t), :]
        for mi in range(2):
            mx = None
            for c in range(sub):
                blk = j * sub + c
                kc = k_ref[0, pl.ds(pl.multiple_of(blk * t, t), t), :]
                s = lax.dot_general(qs[mi], kc, (((1,), (1,)), ((), ())), preferred_element_type=F32)
                s = s + bias_ref[0, jnp.clip(blk - qi, -2, 2) + 2]
                s_scr[mi, :, c * t:(c + 1) * t] = s
                for i in range(nb):
                    piece = s[:, i * LANES:(i + 1) * LANES]
                    mx = piece if mx is None else jnp.maximum(mx, piece)
            m_prev = m_scr[mi]
            m_next = jnp.maximum(m_prev, jnp.max(mx, axis=1, keepdims=True))
            alpha = jnp.exp2(m_prev - m_next)
            m_scr[mi] = m_next
            lsum = None
            for i in range(sub * nb):
                p = jnp.exp2(s_scr[mi, :, i * LANES:(i + 1) * LANES] - m_next)
                lsum = p if lsum is None else lsum + p
                p_scr[mi, :, i * LANES:(i + 1) * LANES] = p.astype(BF16)
            l_scr[mi] = alpha * l_scr[mi] + lsum
            acc_scr[mi] = alpha * acc_scr[mi] + jnp.dot(p_scr[mi], vj, preferred_element_type=F32)
        return carry

    lax.fori_loop(0, n_iter, body, 0)

    o1 = acc_scr[0] / jnp.sum(l_scr[0], axis=1, keepdims=True)
    o2 = acc_scr[1] / jnp.sum(l_scr[1], axis=1, keepdims=True)
    o = o1 - lam_ref[0] * o2
    ms = jnp.mean(o * o, axis=-1, keepdims=True)
    o_ref[0] = (o * lax.rsqrt(ms + RMS_EPS) * g_ref[...]).astype(BF16)


def _attention(q, k, v, bias_tiles, lam, g_scaled, t):
    b, l, _ = q.shape
    nq = l // t
    sub = 2 if nq % 2 == 0 else 1
    kern = functools.partial(_attn_kernel, t=t, sub=sub, n_iter=nq // sub)
    return pl.pallas_call(
        kern,
        grid=(b, N_HEADS, nq),
        in_specs=[pl.BlockSpec(memory_space=pltpu.SMEM),
                  pl.BlockSpec((1, t, V_DIM), lambda bi, h, qi: (bi, qi, h)),
                  pl.BlockSpec((1, l, V_DIM), lambda bi, h, qi: (bi, 0, h)),
                  pl.BlockSpec((1, l, V_DIM), lambda bi, h, qi: (bi, 0, h)),
                  pl.BlockSpec((1, 5, t, t), lambda bi, h, qi: (h, 0, 0, 0)),
                  pl.BlockSpec((1, V_DIM), lambda bi, h, qi: (0, 0))],
        out_specs=pl.BlockSpec((1, t, V_DIM), lambda bi, h, qi: (bi, qi, h)),
        out_shape=jax.ShapeDtypeStruct((b, l, W_ATT), BF16),
        scratch_shapes=[pltpu.VMEM((2, t, LANES), F32),
                        pltpu.VMEM((2, t, LANES), F32),
                        pltpu.VMEM((2, t, V_DIM), F32),
                        pltpu.VMEM((2, t, sub * t), F32),
                        pltpu.VMEM((2, t, sub * t), BF16)],
        compiler_params=_cparams(("parallel", "parallel", "arbitrary")),
        name="diff_attention",
    )(lam, q, k, v, bias_tiles, g_scaled)


def _ssm_matrices(a_re, a_im, log_dt, b_re, b_im, c_re, c_im, d_skip):
    qn, g, p, hc = SSM_CHUNK, N_GROUPS, SSM_STATE, SSM_GROUP
    n = jnp.arange(qn + 1, dtype=F32)
    pw, bbar, cc = [], [], []
    for d in range(2):
        a = lax.complex(a_re[d].astype(F32), a_im[d].astype(F32))
        dt = jnp.exp(log_dt[d].astype(F32))[:, None]
        adt = a * dt
        a_bar = jnp.exp(adt)
        pw.append(jnp.exp(adt[None] * n[:, None, None]))
        bbar.append(((a_bar - 1.0) / a)[:, :, None]
                    * lax.complex(b_re[d].astype(F32), b_im[d].astype(F32)))
        cc.append(lax.complex(c_re[d].astype(F32), c_im[d].astype(F32)))

    hi = lax.Precision.HIGHEST
    kern = [jnp.einsum('gop,tgp,gpi->tgoi', cc[d], pw[d][:qn], bbar[d], precision=hi).real
            for d in range(2)]
    s_idx = jnp.arange(qn)[:, None]
    t_idx = jnp.arange(qn)[None, :]
    lag = t_idx - s_idx
    taus = jnp.arange(qn)
    sel_f = (lag[:, :, None] == taus).astype(F32)
    sel_b = (-lag[:, :, None] == taus).astype(F32)
    kf = jnp.einsum('stu,ugoi->stgoi', sel_f, kern[0], precision=hi)
    kb = jnp.einsum('stu,ugoi->stgoi', sel_b, kern[1], precision=hi)
    skip = (jnp.eye(qn, dtype=F32)[:, :, None, None, None]
            * (jnp.eye(hc, dtype=F32)[None, None, None] * d_skip.astype(F32).reshape(g, hc)[None, None, :, :, None]))
    m_full = kf + kb + skip
    m_mat = jnp.transpose(m_full, (2, 0, 4, 1, 3)).reshape(g, SSM_ROW, SSM_ROW)

    zeros_p = jnp.zeros((g, SSM_ROW, LANES - p), F32)

    def pad_cols(x):
        return jnp.concatenate([x, zeros_p], axis=-1)

    pf = jnp.einsum('sgp,gpi->gsip', pw[0][:qn][::-1], bbar[0]).reshape(g, SSM_ROW, p)
    pb = jnp.einsum('sgp,gpi->gsip', pw[1][:qn], bbar[1]).reshape(g, SSM_ROW, p)
    p_mat = jnp.concatenate([pad_cols(pf.real), pad_cols(pf.imag),
                             pad_cols(pb.real), pad_cols(pb.imag)], axis=-1)

    wf = jnp.einsum('gop,tgp->gpto', cc[0], pw[0][1:qn + 1]).reshape(g, p, SSM_ROW)
    wb = jnp.einsum('gop,tgp->gpto', cc[1], pw[1][1:qn + 1][::-1]).reshape(g, p, SSM_ROW)
    zeros_r = jnp.zeros((g, LANES - p, SSM_ROW), F32)
    r_mat = jnp.concatenate([wf.real, zeros_r, -wf.imag, zeros_r,
                             wb.real, zeros_r, -wb.imag, zeros_r], axis=1)

    zeros_a = jnp.zeros((g, LANES - p), F32)

    def pad_vec(x):
        return jnp.concatenate([x, zeros_a], axis=-1)

    alpha = jnp.stack([pad_vec(pw[0][qn].real), pad_vec(pw[0][qn].imag),
                       pad_vec(pw[1][qn].real), pad_vec(pw[1][qn].imag)], axis=1)
    return p_mat.astype(BF16), m_mat.astype(BF16), r_mat.astype(BF16), alpha


def _s5_kernel(u_ref, p_ref, m_ref, r_ref, a_ref, y_ref, s_scr, x_scr, *, nc, bsz):
    u = u_ref[0]
    s_scr[...] = jnp.dot(u, p_ref[0], preferred_element_type=F32)
    al = a_ref[0]
    afr = jnp.broadcast_to(al[0:1], (bsz, LANES))
    afi = jnp.broadcast_to(al[1:2], (bsz, LANES))
    abr = jnp.broadcast_to(al[2:3], (bsz, LANES))
    abi = jnp.broadcast_to(al[3:4], (bsz, LANES))
    zero = jnp.zeros((bsz, LANES), F32)

    def step(i, carry):
        fr, fi, br, bi = carry
        rf = pl.multiple_of(i * bsz, bsz)
        rb = pl.multiple_of((nc - 1 - i) * bsz, bsz)
        x_scr[pl.ds(rf, bsz), 0:LANES] = fr
        x_scr[pl.ds(rf, bsz), LANES:2 * LANES] = fi
        x_scr[pl.ds(rb, bsz), 2 * LANES:3 * LANES] = br
        x_scr[pl.ds(rb, bsz), 3 * LANES:4 * LANES] = bi
        sfr = s_scr[pl.ds(rf, bsz), 0:LANES]
        sfi = s_scr[pl.ds(rf, bsz), LANES:2 * LANES]
        sbr = s_scr[pl.ds(rb, bsz), 2 * LANES:3 * LANES]
        sbi = s_scr[pl.ds(rb, bsz), 3 * LANES:4 * LANES]
        return (afr * fr - afi * fi + sfr, afr * fi + afi * fr + sfi,
                abr * br - abi * bi + sbr, abr * bi + abi * br + sbi)

    lax.fori_loop(0, nc, step, (zero, zero, zero, zero))
    y = jnp.dot(u, m_ref[0], preferred_element_type=F32)
    y = y + jnp.dot(x_scr[...].astype(BF16), r_ref[0], preferred_element_type=F32)
    y_ref[0] = y


def _s5(u_grp, p_mat, m_mat, r_mat, alpha, nc, bsz):
    g, rows, _ = u_grp.shape
    kern = functools.partial(_s5_kernel, nc=nc, bsz=bsz)
    mat = pl.BlockSpec((1, SSM_ROW, SSM_ROW), lambda i: (i, 0, 0))
    seq = pl.BlockSpec((1, rows, SSM_ROW), lambda i: (i, 0, 0))
    return pl.pallas_call(
        kern,
        grid=(g,),
        in_specs=[seq, mat, mat, mat, pl.BlockSpec((1, 4, LANES), lambda i: (i, 0, 0))],
        out_specs=seq,
        out_shape=jax.ShapeDtypeStruct((g, rows, SSM_ROW), F32),
        scratch_shapes=[pltpu.VMEM((rows, 4 * LANES), F32), pltpu.VMEM((rows, 4 * LANES), F32)],
        compiler_params=_cparams(("parallel",)),
        name="s5_scan",
    )(u_grp, p_mat, m_mat, r_mat, alpha)


def _post_mix_kernel(x_ref, att_ref, y_ref, wglu_ref, bglu_ref, gs_ref, wout_ref, g2_ref,
                     wr_ref, br_ref, x2_ref, h2_ref, ids_ref, gates_ref):
    y = y_ref[...]
    y = 0.5 * y * (1.0 + jnp.tanh(math.sqrt(2.0 / math.pi) * (y + 0.044715 * (y * y * y))))
    z = jnp.dot(y.astype(BF16), wglu_ref[...], preferred_element_type=F32) + bglu_ref[...]
    y = y * (1.0 / (1.0 + jnp.exp(-z)))
    ms = jnp.mean(y * y, axis=-1, keepdims=True)
    ssm = (y * lax.rsqrt(ms + RMS_EPS) * gs_ref[...]).astype(BF16)
    mix = jnp.dot(att_ref[...], wout_ref[0:W_ATT, :], preferred_element_type=F32)
    mix = mix + jnp.dot(ssm, wout_ref[W_ATT:, :], preferred_element_type=F32)
    x2 = x_ref[...] + mix
    x2_ref[...] = x2
    ms2 = jnp.mean(x2 * x2, axis=-1, keepdims=True)
    h2 = x2 * lax.rsqrt(ms2 + RMS_EPS) * g2_ref[...]
    h2_ref[...] = h2
    logits = jnp.dot(h2, wr_ref[...], preferred_element_type=F32,
                     precision=lax.Precision.HIGHEST) + br_ref[...]
    lane = lax.broadcasted_iota(jnp.int32, logits.shape, 1).astype(F32)
    neg = jnp.float32(-jnp.inf)
    cur = jnp.where(lane < N_EXPERTS, logits, neg)
    ids = jnp.zeros(logits.shape, F32)
    vals = jnp.zeros(logits.shape, F32)
    top = None
    den = None
    for kk in range(TOP_K):
        mx = jnp.max(cur, axis=1, keepdims=True)
        idx = jnp.min(jnp.where(cur == mx, lane, float(LANES)), axis=1, keepdims=True)
        if kk == 0:
            top = mx
        e = jnp.exp(mx - top)
        den = e if den is None else den + e
        ids = jnp.where(lane == kk, idx, ids)
        vals = jnp.where(lane == kk, e, vals)
        cur = jnp.where(lane == idx, neg, cur)
    ids_ref[...] = ids.astype(jnp.int32)
    gates_ref[...] = vals / den


def _post_mix(x2d, att, yssm, wglu, bglu, gs, wout, g2, wr, br):
    t = x2d.shape[0]
    tm = min(ROW_TILE, t)
    row = lambda i: (i, 0)
    const = lambda i: (0, 0)
    return pl.pallas_call(
        _post_mix_kernel,
        grid=(t // tm,),
        in_specs=[pl.BlockSpec((tm, D_MODEL), row),
                  pl.BlockSpec((tm, W_ATT), row),
                  pl.BlockSpec((tm, W_SSM), row),
                  pl.BlockSpec((W_SSM, W_SSM), const),
                  pl.BlockSpec((1, W_SSM), const),
                  pl.BlockSpec((1, W_SSM), const),
                  pl.BlockSpec((D_MODEL, D_MODEL), const),
                  pl.BlockSpec((1, D_MODEL), const),
                  pl.BlockSpec((D_MODEL, LANES), const),
                  pl.BlockSpec((1, LANES), const)],
        out_specs=[pl.BlockSpec((tm, D_MODEL), row),
                   pl.BlockSpec((tm, D_MODEL), row),
                   pl.BlockSpec((tm, LANES), row),
                   pl.BlockSpec((tm, LANES), row)],
        out_shape=[jax.ShapeDtypeStruct((t, D_MODEL), F32),
                   jax.ShapeDtypeStruct((t, D_MODEL), F32),
                   jax.ShapeDtypeStruct((t, LANES), jnp.int32),
                   jax.ShapeDtypeStruct((t, LANES), F32)],
        compiler_params=_cparams(("parallel",)),
        name="post_mix",
    )(x2d, att, yssm, wglu, bglu, gs, wout, g2, wr, br)


def _row_copy(src_hbm, row, dst, dst_row, sem):
    return pltpu.make_async_copy(src_hbm.at[pl.ds(row, 1), :], dst.at[pl.ds(dst_row, 1), :], sem)


def _gather_kernel(idx_ref, h_hbm, o_ref, sem, *, rows):
    def issue(g, c):
        for u in range(DMA_UNROLL):
            r = g * DMA_UNROLL + u
            _row_copy(h_hbm, idx_ref[0, 0, r], o_ref, r, sem).start()
        return c

    lax.fori_loop(0, rows // DMA_UNROLL, issue, 0)
    pltpu.make_async_copy(h_hbm.at[pl.ds(0, rows), :], o_ref, sem).wait()


def _gather_rows(h2, tok_pad):
    n_pad = tok_pad.shape[0]
    rows = GATHER_ROWS
    steps = n_pad // rows
    kern = functools.partial(_gather_kernel, rows=rows)
    return pl.pallas_call(
        kern,
        grid=(steps,),
        in_specs=[pl.BlockSpec((1, 1, rows), lambda i: (i, 0, 0), memory_space=pltpu.SMEM),
                  pl.BlockSpec(memory_space=pl.ANY)],
        out_specs=pl.BlockSpec((rows, D_MODEL), lambda i: (i, 0)),
        out_shape=jax.ShapeDtypeStruct((n_pad, D_MODEL), F32),
        scratch_shapes=[pltpu.SemaphoreType.DMA(())],
        compiler_params=_cparams(("arbitrary",)),
        name="moe_gather",
    )(tok_pad.reshape(steps, 1, rows), h2)


def _expert_kernel(be_ref, nu_ref, x_ref, w1_ref, b1_ref, w2_ref, b2_ref, y_ref):
    @pl.when(pl.program_id(0) < nu_ref[0])
    def _():
        x = x_ref[...].astype(BF16)
        hdn = jnp.dot(x, w1_ref[0], preferred_element_type=F32) + b1_ref[0]
        gate = jnp.minimum(hdn[:, :D_FF], SWIGLU_LIMIT)
        lin = jnp.clip(hdn[:, D_FF:], -SWIGLU_LIMIT, SWIGLU_LIMIT)
        act = gate * (1.0 / (1.0 + jnp.exp(-SWIGLU_ALPHA * gate))) * (lin + 1.0)
        y_ref[...] = jnp.dot(act.astype(BF16), w2_ref[0], preferred_element_type=F32) + b2_ref[0]

    @pl.when(pl.program_id(0) >= nu_ref[0])
    def _():
        y_ref[...] = jnp.zeros(y_ref.shape, F32)


def _experts(block_e, n_used, x_pad, w1, b1, w2, b2):
    n_pad = x_pad.shape[0]
    rows = EXPERT_ROWS
    grid_spec = pltpu.PrefetchScalarGridSpec(
        num_scalar_prefetch=2,
        grid=(n_pad // rows,),
        in_specs=[pl.BlockSpec((rows, D_MODEL), lambda i, be, nu: (i, 0)),
                  pl.BlockSpec((1, D_MODEL, 2 * D_FF), lambda i, be, nu: (be[i], 0, 0)),
                  pl.BlockSpec((1, 1, 2 * D_FF), lambda i, be, nu: (be[i], 0, 0)),
                  pl.BlockSpec((1, D_FF, D_MODEL), lambda i, be, nu: (be[i], 0, 0)),
                  pl.BlockSpec((1, 1, D_MODEL), lambda i, be, nu: (be[i], 0, 0))],
        out_specs=pl.BlockSpec((rows, D_MODEL), lambda i, be, nu: (i, 0)),
    )
    return pl.pallas_call(
        _expert_kernel,
        grid_spec=grid_spec,
        out_shape=jax.ShapeDtypeStruct((n_pad, D_MODEL), F32),
        compiler_params=_cparams(("arbitrary",)),
        name="moe_experts",
    )(block_e, n_used, x_pad, w1, b1, w2, b2)


def _combine_kernel(pos_ref, nxt_ref, x2_ref, gates_ref, gf_ref, y_hbm, o_ref, buf, sem, *, rows):
    i = pl.program_id(0)
    n = pl.num_programs(0)
    slot = i % 2

    def issue(idx_ref, dst_slot):
        def body(r, c):
            for u in range(2):
                for kk in range(TOP_K):
                    row = r * 2 + u
                    _row_copy(y_hbm, idx_ref[0, 0, row * TOP_K + kk], buf.at[dst_slot, kk], row,
                              sem.at[dst_slot]).start()
            return c

        lax.fori_loop(0, rows // 2, body, 0)

    @pl.when(i == 0)
    def _():
        issue(pos_ref, 0)

    @pl.when(i + 1 < n)
    def _():
        issue(nxt_ref, 1 - slot)

    for kk in range(TOP_K):
        pltpu.make_async_copy(y_hbm.at[pl.ds(0, rows), :], buf.at[slot, kk], sem.at[slot]).wait()

    gates = gates_ref[...]
    y = x2_ref[...]
    for kk in range(TOP_K):
        y = y + gates[:, kk:kk + 1] * buf[slot, kk]
    ms = jnp.mean(y * y, axis=-1, keepdims=True)
    o_ref[...] = y * lax.rsqrt(ms + RMS_EPS) * gf_ref[...]


def _combine(pos, x2, gates, gf, y_pad):
    t = x2.shape[0]
    rows = min(COMBINE_ROWS, t)
    steps = t // rows
    kern = functools.partial(_combine_kernel, rows=rows)
    row = lambda i: (i, 0)
    pos3 = pos.reshape(steps, 1, rows * TOP_K)
    return pl.pallas_call(
        kern,
        grid=(steps,),
        in_specs=[pl.BlockSpec((1, 1, rows * TOP_K), lambda i: (i, 0, 0), memory_space=pltpu.SMEM),
                  pl.BlockSpec((1, 1, rows * TOP_K), lambda i: (jnp.minimum(i + 1, steps - 1), 0, 0),
                               memory_space=pltpu.SMEM),
                  pl.BlockSpec((rows, D_MODEL), row),
                  pl.BlockSpec((rows, LANES), row),
                  pl.BlockSpec((1, D_MODEL), lambda i: (0, 0)),
                  pl.BlockSpec(memory_space=pl.ANY)],
        out_specs=pl.BlockSpec((rows, D_MODEL), row),
        out_shape=jax.ShapeDtypeStruct((t, D_MODEL), F32),
        scratch_shapes=[pltpu.VMEM((2, TOP_K, rows, D_MODEL), F32), pltpu.SemaphoreType.DMA((2,))],
        compiler_params=_cparams(("arbitrary",)),
        name="moe_combine",
    )(pos3, pos3, x2, gates, gf, y_pad)


def _dispatch_plan(top_e, n_tok):
    n = n_tok * TOP_K
    blk = EXPERT_ROWS
    flat_e = top_e.reshape(n)
    order = jnp.argsort(flat_e).astype(jnp.int32)
    rank = jnp.argsort(order).astype(jnp.int32)
    st = order // TOP_K
    experts = jnp.arange(N_EXPERTS, dtype=jnp.int32)
    counts = jnp.sum((flat_e[:, None] == experts).astype(jnp.int32), axis=0)
    padded = (counts + blk - 1) // blk * blk
    start = jnp.cumsum(counts) - counts
    pend = jnp.cumsum(padded)
    pstart = pend - padded
    pos = (pstart - start)[flat_e] + rank
    n_blocks = n // blk + N_EXPERTS
    block_e = jnp.minimum(
        jnp.searchsorted(pend, jnp.arange(n_blocks, dtype=jnp.int32) * blk, side='right'),
        N_EXPERTS - 1).astype(jnp.int32)
    slot = jnp.arange(n_blocks * blk, dtype=jnp.int32)
    slot_e = jnp.repeat(block_e, blk)
    off = slot - pstart[slot_e]
    src = jnp.clip(start[slot_e] + off, 0, n - 1)
    tok_pad = jnp.where(off < counts[slot_e], st[src], 0)
    n_used = (pend[-1:] // blk).astype(jnp.int32)
    return tok_pad, pos, block_e, n_used


def _trunk(x, prm):
    bsz, l, _ = x.shape
    t = bsz * l
    x2d = x.reshape(t, D_MODEL)
    q, k, v, u = _in_proj(x2d, prm['norm1_g'], prm['w_in'])

    t_att = min(512, l)
    att = _attention(q.reshape(bsz, l, W_ATT), k.reshape(bsz, l, W_ATT), v.reshape(bsz, l, W_ATT),
                     prm['bias_tiles'][t_att], prm['lam'], prm['subln_g'], t_att)

    nc = l // SSM_CHUNK
    u_grp = jnp.transpose(u.reshape(bsz, nc, SSM_CHUNK, N_GROUPS, SSM_GROUP), (3, 1, 0, 2, 4))
    u_grp = u_grp.reshape(N_GROUPS, nc * bsz, SSM_ROW)
    y_grp = _s5(u_grp, prm['ssm_p'], prm['ssm_m'], prm['ssm_r'], prm['ssm_alpha'], nc, bsz)
    yssm = jnp.transpose(y_grp.reshape(N_GROUPS, nc, bsz, SSM_CHUNK, SSM_GROUP), (2, 1, 3, 0, 4))
    yssm = yssm.reshape(t, W_SSM)

    x2, h2, ids, gates = _post_mix(x2d, att.reshape(t, W_ATT), yssm, prm['w_glu'], prm['b_glu'],
                                   prm['ssm_norm_g'], prm['w_out'], prm['norm2_g'],
                                   prm['w_router'], prm['b_router'])

    tok_pad, pos, block_e, n_used = _dispatch_plan(ids[:, :TOP_K], t)
    x_pad = _gather_rows(h2, tok_pad)
    y_pad = _experts(block_e, n_used, x_pad, prm['w_moe1'], prm['b_moe1'], prm['w_moe2'], prm['b_moe2'])
    out = _combine(pos, x2, gates, prm['normf_g'], y_pad)
    return out.reshape(bsz, l, D_MODEL)


def _prepare(seq_lens, rel_bias, norm1_g, w_in, lambda_q1, lambda_k1, lambda_q2, lambda_k2, subln_g,
             ssm_A_re, ssm_A_im, ssm_log_dt, ssm_B_re, ssm_B_im, ssm_C_re, ssm_C_im, ssm_D,
             w_glu, b_glu, ssm_norm_g, w_out, norm2_g, w_router, b_router,
             w_moe1, b_moe1, w_moe2, b_moe2, normf_g):
    layer = 0
    lambda_init = 0.8 - 0.6 * math.exp(-0.3 * layer)
    lam = (jnp.exp(jnp.sum(lambda_q1[layer].astype(F32) * lambda_k1[layer].astype(F32)))
           - jnp.exp(jnp.sum(lambda_q2[layer].astype(F32) * lambda_k2[layer].astype(F32))) + lambda_init)
    p_mat, m_mat, r_mat, alpha = _ssm_matrices(
        ssm_A_re[layer], ssm_A_im[layer], ssm_log_dt[layer], ssm_B_re[layer], ssm_B_im[layer],
        ssm_C_re[layer], ssm_C_im[layer], ssm_D[layer])
    pad_e = LANES - N_EXPERTS
    return {
        'norm1_g': norm1_g[layer].reshape(1, D_MODEL).astype(F32),
        'w_in': w_in[layer].astype(BF16),
        'lam': lam.reshape(1).astype(F32),
        'subln_g': (subln_g[layer].astype(F32) * (1.0 - lambda_init)).reshape(1, V_DIM),
        'bias_tiles': {t: _bias_tiles(rel_bias, t) for t in sorted({min(512, l) for l in seq_lens})},
        'ssm_p': p_mat, 'ssm_m': m_mat, 'ssm_r': r_mat, 'ssm_alpha': alpha,
        'w_glu': w_glu[layer].astype(BF16),
        'b_glu': b_glu[layer].reshape(1, W_SSM).astype(F32),
        'ssm_norm_g': ssm_norm_g[layer].reshape(1, W_SSM).astype(F32),
        'w_out': w_out[layer].astype(BF16),
        'norm2_g': norm2_g[layer].reshape(1, D_MODEL).astype(F32),
        'w_router': jnp.pad(w_router[layer].astype(F32), ((0, 0), (0, pad_e))),
        'b_router': jnp.pad(b_router[layer].astype(F32), (0, pad_e)).reshape(1, LANES),
        'w_moe1': w_moe1[layer].astype(BF16),
        'b_moe1': b_moe1[layer].reshape(N_EXPERTS, 1, 2 * D_FF).astype(F32),
        'w_moe2': w_moe2[layer].astype(BF16),
        'b_moe2': b_moe2[layer].reshape(N_EXPERTS, 1, D_MODEL).astype(F32),
        'normf_g': normf_g.reshape(1, D_MODEL).astype(F32),
    }


def kernel(x_prompt, x_sample, rel_bias, norm1_g, w_in, lambda_q1, lambda_k1, lambda_q2, lambda_k2, subln_g, ssm_A_re, ssm_A_im, ssm_log_dt, ssm_B_re, ssm_B_im, ssm_C_re, ssm_C_im, ssm_D, w_glu, b_glu, ssm_norm_g, w_out, norm2_g, w_router, b_router, w_moe1, b_moe1, w_moe2, b_moe2, normf_g):
    prm = _prepare((x_prompt.shape[1], x_sample.shape[1]), rel_bias, norm1_g, w_in, lambda_q1,
                   lambda_k1, lambda_q2, lambda_k2, subln_g, ssm_A_re, ssm_A_im, ssm_log_dt,
                   ssm_B_re, ssm_B_im, ssm_C_re, ssm_C_im, ssm_D, w_glu, b_glu, ssm_norm_g, w_out,
                   norm2_g, w_router, b_router, w_moe1, b_moe1, w_moe2, b_moe2, normf_g)
    return (_trunk(x_prompt, prm), _trunk(x_sample, prm))
```

```python
import functools
import math

import jax
import jax.numpy as jnp
from jax import lax
from jax.experimental import pallas as pl
from jax.experimental.pallas import tpu as pltpu

F32 = jnp.float32
BF16 = jnp.bfloat16

D_MODEL = 1024
W_ATT = 512
W_SSM = 512
HEAD_DIM = 64
N_HEADS = 4
V_DIM = 2 * HEAD_DIM
SSM_GROUP = 16
N_GROUPS = W_SSM // SSM_GROUP
SSM_STATE = 64
IN_WIDTH = 3 * W_ATT + W_SSM
N_BUCKETS = 32
MAX_DISTANCE = 128
N_EXPERTS = 32
TOP_K = 4
D_FF = D_MODEL
SWIGLU_ALPHA = 1.702
SWIGLU_LIMIT = 7.0
RMS_EPS = 1e-6
ATT_SCALE = HEAD_DIM ** -0.5
LOG2E = math.log2(math.e)

LANES = 128
SSM_CHUNK = 32
SSM_ROW = SSM_CHUNK * SSM_GROUP
ROW_TILE = 512
EXPERT_ROWS = 256
FFN_CHUNK = 256
DMA_UNROLL = 8
COMBINE_ROWS = 256
VMEM_LIMIT = 48 * 1024 * 1024


def _cparams(sem):
    return pltpu.CompilerParams(dimension_semantics=sem, vmem_limit_bytes=VMEM_LIMIT)


def _in_proj_kernel(x_ref, g_ref, w_ref, q_ref, k_ref, v_ref, u_ref):
    x = x_ref[...]
    ms = jnp.mean(x * x, axis=-1, keepdims=True)
    h = (x * lax.rsqrt(ms + RMS_EPS) * g_ref[...]).astype(BF16)
    proj = jnp.dot(h, w_ref[...], preferred_element_type=F32)
    q_ref[...] = (proj[:, 0:W_ATT] * (ATT_SCALE * LOG2E)).astype(BF16)
    k_ref[...] = proj[:, W_ATT:2 * W_ATT].astype(BF16)
    v_ref[...] = proj[:, 2 * W_ATT:3 * W_ATT].astype(BF16)
    u_ref[...] = proj[:, 3 * W_ATT:].astype(BF16)


def _in_proj(x2d, g, w_bf16):
    t = x2d.shape[0]
    tm = min(ROW_TILE, t)
    out = jax.ShapeDtypeStruct((t, W_ATT), BF16)
    row = lambda i: (i, 0)
    return pl.pallas_call(
        _in_proj_kernel,
        grid=(t // tm,),
        in_specs=[pl.BlockSpec((tm, D_MODEL), row),
                  pl.BlockSpec((1, D_MODEL), lambda i: (0, 0)),
                  pl.BlockSpec((D_MODEL, IN_WIDTH), lambda i: (0, 0))],
        out_specs=[pl.BlockSpec((tm, W_ATT), row)] * 4,
        out_shape=[out] * 4,
        compiler_params=_cparams(("parallel",)),
        name="in_proj",
    )(x2d, g, w_bf16)


def _t5_bucket(rel):
    half = N_BUCKETS // 2
    max_exact = half // 2
    ret = jnp.where(rel > 0, half, 0).astype(jnp.int32)
    n = jnp.abs(rel)
    nf = jnp.maximum(n, 1).astype(F32)
    large = max_exact + (jnp.log(nf / max_exact) / math.log(MAX_DISTANCE / max_exact)
                         * (half - max_exact)).astype(jnp.int32)
    large = jnp.minimum(large, half - 1)
    return ret + jnp.where(n < max_exact, n, large)


def _bias_tiles(rel_bias, t):
    i = jnp.arange(t, dtype=jnp.int32)
    d = jnp.arange(-2, 3, dtype=jnp.int32)
    rel = d[:, None, None] * t + i[None, None, :] - i[None, :, None]
    onehot = (_t5_bucket(rel)[..., None] == jnp.arange(N_BUCKETS, dtype=jnp.int32)).astype(F32)
    tiles = jnp.einsum('dqkn,nh->hdqk', onehot, rel_bias.astype(F32), precision=lax.Precision.HIGHEST)
    return tiles * LOG2E


def _attn_kernel(lam_ref, q_ref, k_ref, v_ref, bias_ref, g_ref, o_ref,
                 m_scr, l_scr, acc_scr, s_scr, p_scr, *, t, sub, n_iter):
    qi = pl.program_id(2)
    q = q_ref[0]
    lane = lax.broadcasted_iota(jnp.int32, q.shape, 1)
    zero = jnp.zeros_like(q)
    qs = (jnp.where(lane < HEAD_DIM, q, zero), jnp.where(lane >= HEAD_DIM, q, zero))
    nb = t // LANES

    m_scr[...] = jnp.full(m_scr.shape, -jnp.inf, F32)
    l_scr[...] = jnp.zeros(l_scr.shape, F32)
    acc_scr[...] = jnp.zeros(acc_scr.shape, F32)

    def body(j, carry):
        vj = v_ref[0, pl.ds(pl.multiple_of(j * (sub * t), sub * t), sub * t), :]
        for mi in range(2):
            mx = None
            for c in range(sub):
                blk = j * sub + c
                kc = k_ref[0, pl.ds(pl.multiple_of(blk * t, t), t), :]
                s = lax.dot_general(qs[mi], kc, (((1,), (1,)), ((), ())), preferred_element_type=F32)
                s = s + bias_ref[0, jnp.clip(blk - qi, -2, 2) + 2]
                s_scr[mi, :, c * t:(c + 1) * t] = s
                for i in range(nb):
                    piece = s[:, i * LANES:(i + 1) * LANES]
                    mx = piece if mx is None else jnp.maximum(mx, piece)
            m_prev = m_scr[mi]
            m_next = jnp.maximum(m_prev, jnp.max(mx, axis=1, keepdims=True))
            alpha = jnp.exp2(m_prev - m_next)
            m_scr[mi] = m_next
            lsum = None
            for i in range(sub * nb):
                p = jnp.exp2(s_scr[mi, :, i * LANES:(i + 1) * LANES] - m_next)
                lsum = p if lsum is None else lsum + p
                p_scr[mi, :, i * LANES:(i + 1) * LANES] = p.astype(BF16)
            l_scr[mi] = alpha * l_scr[mi] + lsum
            acc_scr[mi] = alpha * acc_scr[mi] + jnp.dot(p_scr[mi], vj, preferred_element_type=F32)
        return carry

    lax.fori_loop(0, n_iter, body, 0)

    o1 = acc_scr[0] / jnp.sum(l_scr[0], axis=1, keepdims=True)
    o2 = acc_scr[1] / jnp.sum(l_scr[1], axis=1, keepdims=True)
    o = o1 - lam_ref[0] * o2
    ms = jnp.mean(o * o, axis=-1, keepdims=True)
    o_ref[0] = (o * lax.rsqrt(ms + RMS_EPS) * g_ref[...]).astype(BF16)


def _attention(q, k, v, bias_tiles, lam, g_scaled, t):
    b, l, _ = q.shape
    nq = l // t
    sub = 2 if nq % 2 == 0 else 1
    kern = functools.partial(_attn_kernel, t=t, sub=sub, n_iter=nq // sub)
    return pl.pallas_call(
        kern,
        grid=(b, N_HEADS, nq),
        in_specs=[pl.BlockSpec(memory_space=pltpu.SMEM),
                  pl.BlockSpec((1, t, V_DIM), lambda bi, h, qi: (bi, qi, h)),
                  pl.BlockSpec((1, l, V_DIM), lambda bi, h, qi: (bi, 0, h)),
                  pl.BlockSpec((1, l, V_DIM), lambda bi, h, qi: (bi, 0, h)),
                  pl.BlockSpec((1, 5, t, t), lambda bi, h, qi: (h, 0, 0, 0)),
                  pl.BlockSpec((1, V_DIM), lambda bi, h, qi: (0, 0))],
        out_specs=pl.BlockSpec((1, t, V_DIM), lambda bi, h, qi: (bi, qi, h)),
        out_shape=jax.ShapeDtypeStruct((b, l, W_ATT), BF16),
        scratch_shapes=[pltpu.VMEM((2, t, LANES), F32),
                        pltpu.VMEM((2, t, LANES), F32),
                        pltpu.VMEM((2, t, V_DIM), F32),
                        pltpu.VMEM((2, t, sub * t), F32),
                        pltpu.VMEM((2, t, sub * t), BF16)],
        compiler_params=_cparams(("parallel", "parallel", "arbitrary")),
        name="diff_attention",
    )(lam, q, k, v, bias_tiles, g_scaled)


def _ssm_matrices(a_re, a_im, log_dt, b_re, b_im, c_re, c_im, d_skip):
    qn, g, p, hc = SSM_CHUNK, N_GROUPS, SSM_STATE, SSM_GROUP
    n = jnp.arange(qn + 1, dtype=F32)
    pw, bbar, cc = [], [], []
    for d in range(2):
        a = lax.complex(a_re[d].astype(F32), a_im[d].astype(F32))
        dt = jnp.exp(log_dt[d].astype(F32))[:, None]
        adt = a * dt
        a_bar = jnp.exp(adt)
        pw.append(jnp.exp(adt[None] * n[:, None, None]))
        bbar.append(((a_bar - 1.0) / a)[:, :, None]
                    * lax.complex(b_re[d].astype(F32), b_im[d].astype(F32)))
        cc.append(lax.complex(c_re[d].astype(F32), c_im[d].astype(F32)))

    hi = lax.Precision.HIGHEST
    kern = [jnp.einsum('gop,tgp,gpi->tgoi', cc[d], pw[d][:qn], bbar[d], precision=hi).real
            for d in range(2)]
    s_idx = jnp.arange(qn)[:, None]
    t_idx = jnp.arange(qn)[None, :]
    lag = t_idx - s_idx
    taus = jnp.arange(qn)
    sel_f = (lag[:, :, None] == taus).astype(F32)
    sel_b = (-lag[:, :, None] == taus).astype(F32)
    kf = jnp.einsum('stu,ugoi->stgoi', sel_f, kern[0], precision=hi)
    kb = jnp.einsum('stu,ugoi->stgoi', sel_b, kern[1], precision=hi)
    skip = (jnp.eye(qn, dtype=F32)[:, :, None, None, None]
            * (jnp.eye(hc, dtype=F32)[None, None, None] * d_skip.astype(F32).reshape(g, hc)[None, None, :, :, None]))
    m_full = kf + kb + skip
    m_mat = jnp.transpose(m_full, (2, 0, 4, 1, 3)).reshape(g, SSM_ROW, SSM_ROW)

    zeros_p = jnp.zeros((g, SSM_ROW, LANES - p), F32)

    def pad_cols(x):
        return jnp.concatenate([x, zeros_p], axis=-1)

    pf = jnp.einsum('sgp,gpi->gsip', pw[0][:qn][::-1], bbar[0]).reshape(g, SSM_ROW, p)
    pb = jnp.einsum('sgp,gpi->gsip', pw[1][:qn], bbar[1]).reshape(g, SSM_ROW, p)
    p_mat = jnp.concatenate([pad_cols(pf.real), pad_cols(pf.imag),
                             pad_cols(pb.real), pad_cols(pb.imag)], axis=-1)

    wf = jnp.einsum('gop,tgp->gpto', cc[0], pw[0][1:qn + 1]).reshape(g, p, SSM_ROW)
    wb = jnp.einsum('gop,tgp->gpto', cc[1], pw[1][1:qn + 1][::-1]).reshape(g, p, SSM_ROW)
    zeros_r = jnp.zeros((g, LANES - p, SSM_ROW), F32)
    r_mat = jnp.concatenate([wf.real, zeros_r, -wf.imag, zeros_r,
                             wb.real, zeros_r, -wb.imag, zeros_r], axis=1)

    zeros_a = jnp.zeros((g, LANES - p), F32)

    def pad_vec(x):
        return jnp.concatenate([x, zeros_a], axis=-1)

    alpha = jnp.stack([pad_vec(pw[0][qn].real), pad_vec(pw[0][qn].imag),
                       pad_vec(pw[1][qn].real), pad_vec(pw[1][qn].imag)], axis=1)
    return p_mat.astype(BF16), m_mat.astype(BF16), r_mat.astype(BF16), alpha


def _s5_kernel(u_ref, p_ref, m_ref, r_ref, a_ref, y_ref, s_scr, x_scr, *, nc, bsz):
    u = u_ref[0]
    s_scr[...] = jnp.dot(u, p_ref[0], preferred_element_type=F32)
    al = a_ref[0]
    afr = jnp.broadcast_to(al[0:1], (bsz, LANES))
    afi = jnp.broadcast_to(al[1:2], (bsz, LANES))
    abr = jnp.broadcast_to(al[2:3], (bsz, LANES))
    abi = jnp.broadcast_to(al[3:4], (bsz, LANES))
    zero = jnp.zeros((bsz, LANES), F32)

    def step(i, carry):
        fr, fi, br, bi = carry
        rf = pl.multiple_of(i * bsz, bsz)
        rb = pl.multiple_of((nc - 1 - i) * bsz, bsz)
        x_scr[pl.ds(rf, bsz), 0:LANES] = fr
        x_scr[pl.ds(rf, bsz), LANES:2 * LANES] = fi
        x_scr[pl.ds(rb, bsz), 2 * LANES:3 * LANES] = br
        x_scr[pl.ds(rb, bsz), 3 * LANES:4 * LANES] = bi
        sfr = s_scr[pl.ds(rf, bsz), 0:LANES]
        sfi = s_scr[pl.ds(rf, bsz), LANES:2 * LANES]
        sbr = s_scr[pl.ds(rb, bsz), 2 * LANES:3 * LANES]
        sbi = s_scr[pl.ds(rb, bsz), 3 * LANES:4 * LANES]
        return (afr * fr - afi * fi + sfr, afr * fi + afi * fr + sfi,
                abr * br - abi * bi + sbr, abr * bi + abi * br + sbi)

    lax.fori_loop(0, nc, step, (zero, zero, zero, zero))
    y = jnp.dot(u, m_ref[0], preferred_element_type=F32)
    y = y + jnp.dot(x_scr[...].astype(BF16), r_ref[0], preferred_element_type=F32)
    y_ref[0] = y


def _s5(u_grp, p_mat, m_mat, r_mat, alpha, nc, bsz):
    g, rows, _ = u_grp.shape
    kern = functools.partial(_s5_kernel, nc=nc, bsz=bsz)
    mat = pl.BlockSpec((1, SSM_ROW, SSM_ROW), lambda i: (i, 0, 0))
    seq = pl.BlockSpec((1, rows, SSM_ROW), lambda i: (i, 0, 0))
    return pl.pallas_call(
        kern,
        grid=(g,),
        in_specs=[seq, mat, mat, mat, pl.BlockSpec((1, 4, LANES), lambda i: (i, 0, 0))],
        out_specs=seq,
        out_shape=jax.ShapeDtypeStruct((g, rows, SSM_ROW), F32),
        scratch_shapes=[pltpu.VMEM((rows, 4 * LANES), F32), pltpu.VMEM((rows, 4 * LANES), F32)],
        compiler_params=_cparams(("parallel",)),
        name="s5_scan",
    )(u_grp, p_mat, m_mat, r_mat, alpha)


def _post_mix_kernel(x_ref, att_ref, y_ref, wglu_ref, bglu_ref, gs_ref, wout_ref, g2_ref,
                     wrh_ref, wrl_ref, br_ref, x2_ref, h2_ref, ids_ref, gates_ref):
    y = y_ref[...]
    y = 0.5 * y * (1.0 + jnp.tanh(math.sqrt(2.0 / math.pi) * (y + 0.044715 * (y * y * y))))
    z = jnp.dot(y.astype(BF16), wglu_ref[...], preferred_element_type=F32) + bglu_ref[...]
    y = y * (1.0 / (1.0 + jnp.exp(-z)))
    ms = jnp.mean(y * y, axis=-1, keepdims=True)
    ssm = (y * lax.rsqrt(ms + RMS_EPS) * gs_ref[...]).astype(BF16)
    mix = jnp.dot(att_ref[...], wout_ref[0:W_ATT, :], preferred_element_type=F32)
    mix = mix + jnp.dot(ssm, wout_ref[W_ATT:, :], preferred_element_type=F32)
    x2 = x_ref[...] + mix
    x2_ref[...] = x2
    ms2 = jnp.mean(x2 * x2, axis=-1, keepdims=True)
    h2 = x2 * lax.rsqrt(ms2 + RMS_EPS) * g2_ref[...]
    h2_ref[...] = h2
    h_hi = h2.astype(BF16)
    h_lo = (h2 - h_hi.astype(F32)).astype(BF16)
    logits = (jnp.dot(h_hi, wrh_ref[...], preferred_element_type=F32)
              + jnp.dot(h_lo, wrh_ref[...], preferred_element_type=F32)
              + jnp.dot(h_hi, wrl_ref[...], preferred_element_type=F32)) + br_ref[...]
    lane = lax.broadcasted_iota(jnp.int32, logits.shape, 1).astype(F32)
    neg = jnp.float32(-jnp.inf)
    cur = jnp.where(lane < N_EXPERTS, logits, neg)
    ids = jnp.zeros(logits.shape, F32)
    vals = jnp.zeros(logits.shape, F32)
    top = None
    den = None
    for kk in range(TOP_K):
        mx = jnp.max(cur, axis=1, keepdims=True)
        idx = jnp.min(jnp.where(cur == mx, lane, float(LANES)), axis=1, keepdims=True)
        if kk == 0:
            top = mx
        e = jnp.exp(mx - top)
        den = e if den is None else den + e
        ids = jnp.where(lane == kk, idx, ids)
        vals = jnp.where(lane == kk, e, vals)
        cur = jnp.where(lane == idx, neg, cur)
    ids_ref[...] = ids.astype(jnp.int32)
    gates_ref[...] = vals / den


def _post_mix(x2d, att, yssm, wglu, bglu, gs, wout, g2, wr_hi, wr_lo, br):
    t = x2d.shape[0]
    tm = min(ROW_TILE, t)
    row = lambda i: (i, 0)
    const = lambda i: (0, 0)
    return pl.pallas_call(
        _post_mix_kernel,
        grid=(t // tm,),
        in_specs=[pl.BlockSpec((tm, D_MODEL), row),
                  pl.BlockSpec((tm, W_ATT), row),
                  pl.BlockSpec((tm, W_SSM), row),
                  pl.BlockSpec((W_SSM, W_SSM), const),
                  pl.BlockSpec((1, W_SSM), const),
                  pl.BlockSpec((1, W_SSM), const),
                  pl.BlockSpec((D_MODEL, D_MODEL), const),
                  pl.BlockSpec((1, D_MODEL), const),
                  pl.BlockSpec((D_MODEL, LANES), const),
                  pl.BlockSpec((D_MODEL, LANES), const),
                  pl.BlockSpec((1, LANES), const)],
        out_specs=[pl.BlockSpec((tm, D_MODEL), row),
                   pl.BlockSpec((tm, D_MODEL), row),
                   pl.BlockSpec((tm, LANES), row),
                   pl.BlockSpec((tm, LANES), row)],
        out_shape=[jax.ShapeDtypeStruct((t, D_MODEL), F32),
                   jax.ShapeDtypeStruct((t, D_MODEL), F32),
                   jax.ShapeDtypeStruct((t, LANES), jnp.int32),
                   jax.ShapeDtypeStruct((t, LANES), F32)],
        compiler_params=_cparams(("parallel",)),
        name="post_mix",
    )(x2d, att, yssm, wglu, bglu, gs, wout, g2, wr_hi, wr_lo, br)


def _row_copy(src_hbm, row, dst, dst_row, sem):
    return pltpu.make_async_copy(src_hbm.at[pl.ds(row, 1), :], dst.at[pl.ds(dst_row, 1), :], sem)


def _moe_ffn_kernel(be_ref, cur_ref, nxt_ref, h_hbm, w1_ref, b1_ref, w2_ref, b2_ref, y_ref,
                    xbuf, hdn_scr, sem, *, rows):
    i = pl.program_id(0)
    n = pl.num_programs(0)
    slot = i % 2

    def block_copy(dst_slot):
        return pltpu.make_async_copy(h_hbm.at[pl.ds(0, rows), :], xbuf.at[dst_slot], sem.at[dst_slot])

    @pl.when(i == 0)
    def _():
        def issue(g, c):
            for u in range(DMA_UNROLL):
                r = g * DMA_UNROLL + u
                _row_copy(h_hbm, cur_ref[0, 0, r], xbuf.at[0], r, sem.at[0]).start()
            return c

        lax.fori_loop(0, rows // DMA_UNROLL, issue, 0)

    block_copy(slot).wait()
    x = xbuf[slot].astype(BF16)

    n_chunks = (2 * D_FF + D_MODEL) // FFN_CHUNK
    per_chunk = rows // n_chunks

    def issue_next(chunk):
        for r in range(chunk * per_chunk, (chunk + 1) * per_chunk):
            _row_copy(h_hbm, nxt_ref[0, 0, r], xbuf.at[1 - slot], r, sem.at[1 - slot]).start()

    chunk = 0
    for c in range(2 * D_FF // FFN_CHUNK):
        issue_next(chunk)
        chunk += 1
        cols = slice(c * FFN_CHUNK, (c + 1) * FFN_CHUNK)
        hdn_scr[:, cols] = jnp.dot(x, w1_ref[0, :, cols], preferred_element_type=F32) + b1_ref[0, :, cols]
    gate = jnp.minimum(hdn_scr[:, :D_FF], SWIGLU_LIMIT)
    lin = jnp.clip(hdn_scr[:, D_FF:], -SWIGLU_LIMIT, SWIGLU_LIMIT)
    act = (gate * (1.0 / (1.0 + jnp.exp(-SWIGLU_ALPHA * gate))) * (lin + 1.0)).astype(BF16)
    for c in range(D_MODEL // FFN_CHUNK):
        issue_next(chunk)
        chunk += 1
        cols = slice(c * FFN_CHUNK, (c + 1) * FFN_CHUNK)
        y_ref[:, cols] = jnp.dot(act, w2_ref[0, :, cols], preferred_element_type=F32) + b2_ref[0, :, cols]
    for r in range(n_chunks * per_chunk, rows):
        _row_copy(h_hbm, nxt_ref[0, 0, r], xbuf.at[1 - slot], r, sem.at[1 - slot]).start()

    @pl.when(i == n - 1)
    def _():
        block_copy(1 - slot).wait()


def _moe_ffn(block_e, tok_pad, h2, w1, b1, w2, b2):
    n_pad = tok_pad.shape[0]
    rows = EXPERT_ROWS
    steps = n_pad // rows
    tok3 = tok_pad.reshape(steps, 1, rows)
    kern = functools.partial(_moe_ffn_kernel, rows=rows)
    grid_spec = pltpu.PrefetchScalarGridSpec(
        num_scalar_prefetch=1,
        grid=(steps,),
        in_specs=[pl.BlockSpec((1, 1, rows), lambda i, be: (i, 0, 0), memory_space=pltpu.SMEM),
                  pl.BlockSpec((1, 1, rows), lambda i, be: (jnp.minimum(i + 1, steps - 1), 0, 0),
                               memory_space=pltpu.SMEM),
                  pl.BlockSpec(memory_space=pl.ANY),
                  pl.BlockSpec((1, D_MODEL, 2 * D_FF), lambda i, be: (be[i], 0, 0)),
                  pl.BlockSpec((1, 1, 2 * D_FF), lambda i, be: (be[i], 0, 0)),
                  pl.BlockSpec((1, D_FF, D_MODEL), lambda i, be: (be[i], 0, 0)),
                  pl.BlockSpec((1, 1, D_MODEL), lambda i, be: (be[i], 0, 0))],
        out_specs=pl.BlockSpec((rows, D_MODEL), lambda i, be: (i, 0)),
        scratch_shapes=[pltpu.VMEM((2, rows, D_MODEL), F32),
                        pltpu.VMEM((rows, 2 * D_FF), F32),
                        pltpu.SemaphoreType.DMA((2,))],
    )
    return pl.pallas_call(
        kern,
        grid_spec=grid_spec,
        out_shape=jax.ShapeDtypeStruct((n_pad, D_MODEL), F32),
        compiler_params=_cparams(("arbitrary",)),
        name="moe_ffn",
    )(block_e, tok3, tok3, h2, w1, b1, w2, b2)


def _combine_kernel(pos_ref, nxt_ref, x2_ref, gates_ref, gf_ref, y_hbm, o_ref, buf, sem, *, rows):
    i = pl.program_id(0)
    n = pl.num_programs(0)
    slot = i % 2

    def issue(idx_ref, dst_slot):
        def body(r, c):
            for u in range(2):
                for kk in range(TOP_K):
                    row = r * 2 + u
                    _row_copy(y_hbm, idx_ref[0, 0, row * TOP_K + kk], buf.at[dst_slot, kk], row,
                              sem.at[dst_slot]).start()
            return c

        lax.fori_loop(0, rows // 2, body, 0)

    @pl.when(i == 0)
    def _():
        issue(pos_ref, 0)

    @pl.when(i + 1 < n)
    def _():
        issue(nxt_ref, 1 - slot)

    for kk in range(TOP_K):
        pltpu.make_async_copy(y_hbm.at[pl.ds(0, rows), :], buf.at[slot, kk], sem.at[slot]).wait()

    gates = gates_ref[...]
    y = x2_ref[...]
    for kk in range(TOP_K):
        y = y + gates[:, kk:kk + 1] * buf[slot, kk]
    ms = jnp.mean(y * y, axis=-1, keepdims=True)
    o_ref[...] = y * lax.rsqrt(ms + RMS_EPS) * gf_ref[...]


def _combine(pos, x2, gates, gf, y_pad):
    t = x2.shape[0]
    rows = min(COMBINE_ROWS, t)
    steps = t // rows
    kern = functools.partial(_combine_kernel, rows=rows)
    row = lambda i: (i, 0)
    pos3 = pos.reshape(steps, 1, rows * TOP_K)
    return pl.pallas_call(
        kern,
        grid=(steps,),
        in_specs=[pl.BlockSpec((1, 1, rows * TOP_K), lambda i: (i, 0, 0), memory_space=pltpu.SMEM),
                  pl.BlockSpec((1, 1, rows * TOP_K), lambda i: (jnp.minimum(i + 1, steps - 1), 0, 0),
                               memory_space=pltpu.SMEM),
                  pl.BlockSpec((rows, D_MODEL), row),
                  pl.BlockSpec((rows, LANES), row),
                  pl.BlockSpec((1, D_MODEL), lambda i: (0, 0)),
                  pl.BlockSpec(memory_space=pl.ANY)],
        out_specs=pl.BlockSpec((rows, D_MODEL), row),
        out_shape=jax.ShapeDtypeStruct((t, D_MODEL), F32),
        scratch_shapes=[pltpu.VMEM((2, TOP_K, rows, D_MODEL), F32), pltpu.SemaphoreType.DMA((2,))],
        compiler_params=_cparams(("arbitrary",)),
        name="moe_combine",
    )(pos3, pos3, x2, gates, gf, y_pad)


def _dispatch_plan(top_e, n_tok):
    n = n_tok * TOP_K
    blk = EXPERT_ROWS
    flat_e = top_e.reshape(n)
    order = jnp.argsort(flat_e).astype(jnp.int32)
    rank = jnp.argsort(order).astype(jnp.int32)
    st = order // TOP_K
    experts = jnp.arange(N_EXPERTS, dtype=jnp.int32)
    counts = jnp.sum((flat_e[:, None] == experts).astype(jnp.int32), axis=0)
    padded = (counts + blk - 1) // blk * blk
    start = jnp.cumsum(counts) - counts
    pend = jnp.cumsum(padded)
    pstart = pend - padded
    pos = (pstart - start)[flat_e] + rank
    n_blocks = n // blk + N_EXPERTS
    block_e = jnp.minimum(
        jnp.searchsorted(pend, jnp.arange(n_blocks, dtype=jnp.int32) * blk, side='right'),
        N_EXPERTS - 1).astype(jnp.int32)
    slot = jnp.arange(n_blocks * blk, dtype=jnp.int32)
    slot_e = jnp.repeat(block_e, blk)
    off = slot - pstart[slot_e]
    src = jnp.clip(start[slot_e] + off, 0, n - 1)
    tok_pad = jnp.where(off < counts[slot_e], st[src], 0)
    return tok_pad, pos, block_e


def _trunk(x, prm):
    bsz, l, _ = x.shape
    t = bsz * l
    x2d = x.reshape(t, D_MODEL)
    q, k, v, u = _in_proj(x2d, prm['norm1_g'], prm['w_in'])

    t_att = min(512, l)
    att = _attention(q.reshape(bsz, l, W_ATT), k.reshape(bsz, l, W_ATT), v.reshape(bsz, l, W_ATT),
                     prm['bias_tiles'][t_att], prm['lam'], prm['subln_g'], t_att)

    nc = l // SSM_CHUNK
    u_grp = jnp.transpose(u.reshape(bsz, nc, SSM_CHUNK, N_GROUPS, SSM_GROUP), (3, 1, 0, 2, 4))
    u_grp = u_grp.reshape(N_GROUPS, nc * bsz, SSM_ROW)
    y_grp = _s5(u_grp, prm['ssm_p'], prm['ssm_m'], prm['ssm_r'], prm['ssm_alpha'], nc, bsz)
    yssm = jnp.transpose(y_grp.reshape(N_GROUPS, nc, bsz, SSM_CHUNK, SSM_GROUP), (2, 1, 3, 0, 4))
    yssm = yssm.reshape(t, W_SSM)

    x2, h2, ids, gates = _post_mix(x2d, att.reshape(t, W_ATT), yssm, prm['w_glu'], prm['b_glu'],
                                   prm['ssm_norm_g'], prm['w_out'], prm['norm2_g'],
                                   prm['w_router_hi'], prm['w_router_lo'], prm['b_router'])

    tok_pad, pos, block_e = _dispatch_plan(ids[:, :TOP_K], t)
    y_pad = _moe_ffn(block_e, tok_pad, h2, prm['w_moe1'], prm['b_moe1'], prm['w_moe2'], prm['b_moe2'])
    out = _combine(pos, x2, gates, prm['normf_g'], y_pad)
    return out.reshape(bsz, l, D_MODEL)


def _prepare(seq_lens, rel_bias, norm1_g, w_in, lambda_q1, lambda_k1, lambda_q2, lambda_k2, subln_g,
             ssm_A_re, ssm_A_im, ssm_log_dt, ssm_B_re, ssm_B_im, ssm_C_re, ssm_C_im, ssm_D,
             w_glu, b_glu, ssm_norm_g, w_out, norm2_g, w_router, b_router,
             w_moe1, b_moe1, w_moe2, b_moe2, normf_g):
    layer = 0
    lambda_init = 0.8 - 0.6 * math.exp(-0.3 * layer)
    lam = (jnp.exp(jnp.sum(lambda_q1[layer].astype(F32) * lambda_k1[layer].astype(F32)))
           - jnp.exp(jnp.sum(lambda_q2[layer].astype(F32) * lambda_k2[layer].astype(F32))) + lambda_init)
    p_mat, m_mat, r_mat, alpha = _ssm_matrices(
        ssm_A_re[layer], ssm_A_im[layer], ssm_log_dt[layer], ssm_B_re[layer], ssm_B_im[layer],
        ssm_C_re[layer], ssm_C_im[layer], ssm_D[layer])
    pad_e = LANES - N_EXPERTS
    w_r = jnp.pad(w_router[layer].astype(F32), ((0, 0), (0, pad_e)))
    w_r_hi = w_r.astype(BF16)
    return {
        'norm1_g': norm1_g[layer].reshape(1, D_MODEL).astype(F32),
        'w_in': w_in[layer].astype(BF16),
        'lam': lam.reshape(1).astype(F32),
        'subln_g': (subln_g[layer].astype(F32) * (1.0 - lambda_init)).reshape(1, V_DIM),
        'bias_tiles': {t: _bias_tiles(rel_bias, t) for t in sorted({min(512, l) for l in seq_lens})},
        'ssm_p': p_mat, 'ssm_m': m_mat, 'ssm_r': r_mat, 'ssm_alpha': alpha,
        'w_glu': w_glu[layer].astype(BF16),
        'b_glu': b_glu[layer].reshape(1, W_SSM).astype(F32),
        'ssm_norm_g': ssm_norm_g[layer].reshape(1, W_SSM).astype(F32),
        'w_out': w_out[layer].astype(BF16),
        'norm2_g': norm2_g[layer].reshape(1, D_MODEL).astype(F32),
        'w_router_hi': w_r_hi,
        'w_router_lo': (w_r - w_r_hi.astype(F32)).astype(BF16),
        'b_router': jnp.pad(b_router[layer].astype(F32), (0, pad_e)).reshape(1, LANES),
        'w_moe1': w_moe1[layer].astype(BF16),
        'b_moe1': b_moe1[layer].reshape(N_EXPERTS, 1, 2 * D_FF).astype(F32),
        'w_moe2': w_moe2[layer].astype(BF16),
        'b_moe2': b_moe2[layer].reshape(N_EXPERTS, 1, D_MODEL).astype(F32),
        'normf_g': normf_g.reshape(1, D_MODEL).astype(F32),
    }


def kernel(x_prompt, x_sample, rel_bias, norm1_g, w_in, lambda_q1, lambda_k1, lambda_q2, lambda_k2, subln_g, ssm_A_re, ssm_A_im, ssm_log_dt, ssm_B_re, ssm_B_im, ssm_C_re, ssm_C_im, ssm_D, w_glu, b_glu, ssm_norm_g, w_out, norm2_g, w_router, b_router, w_moe1, b_moe1, w_moe2, b_moe2, normf_g):
    prm = _prepare((x_prompt.shape[1], x_sample.shape[1]), rel_bias, norm1_g, w_in, lambda_q1,
                   lambda_k1, lambda_q2, lambda_k2, subln_g, ssm_A_re, ssm_A_im, ssm_log_dt,
                   ssm_B_re, ssm_B_im, ssm_C_re, ssm_C_im, ssm_D, w_glu, b_glu, ssm_norm_g, w_out,
                   norm2_g, w_router, b_router, w_moe1, b_moe1, w_moe2, b_moe2, normf_g)
    return (_trunk(x_prompt, prm), _trunk(x_sample, prm))
```

```python
import functools
import math

import jax
import jax.numpy as jnp
from jax import lax
from jax.experimental import pallas as pl
from jax.experimental.pallas import tpu as pltpu

F32 = jnp.float32
BF16 = jnp.bfloat16

D_MODEL = 1024
W_ATT = 512
W_SSM = 512
HEAD_DIM = 64
N_HEADS = 4
V_DIM = 2 * HEAD_DIM
SSM_GROUP = 16
N_GROUPS = W_SSM // SSM_GROUP
SSM_STATE = 64
IN_WIDTH = 3 * W_ATT + W_SSM
N_BUCKETS = 32
MAX_DISTANCE = 128
N_EXPERTS = 32
TOP_K = 4
D_FF = D_MODEL
SWIGLU_ALPHA = 1.702
SWIGLU_LIMIT = 7.0
RMS_EPS = 1e-6
ATT_SCALE = HEAD_DIM ** -0.5
LOG2E = math.log2(math.e)

LANES = 128
SSM_CHUNK = 32
SSM_ROW = SSM_CHUNK * SSM_GROUP
ROW_TILE = 512
EXPERT_ROWS = 256
FFN_CHUNK = 256
DMA_UNROLL = 8
COMBINE_ROWS = 256
VMEM_LIMIT = 48 * 1024 * 1024


def _cparams(sem):
    return pltpu.CompilerParams(dimension_semantics=sem, vmem_limit_bytes=VMEM_LIMIT)


def _in_proj_kernel(x_ref, g_ref, w_ref, q_ref, k_ref, v_ref, u_ref):
    x = x_ref[...]
    ms = jnp.mean(x * x, axis=-1, keepdims=True)
    h = (x * lax.rsqrt(ms + RMS_EPS) * g_ref[...]).astype(BF16)
    proj = jnp.dot(h, w_ref[...], preferred_element_type=F32)
    q_ref[...] = (proj[:, 0:W_ATT] * (ATT_SCALE * LOG2E)).astype(BF16)
    k_ref[...] = proj[:, W_ATT:2 * W_ATT].astype(BF16)
    v_ref[...] = proj[:, 2 * W_ATT:3 * W_ATT].astype(BF16)
    u_ref[...] = proj[:, 3 * W_ATT:].astype(BF16)


def _in_proj(x2d, g, w_bf16):
    t = x2d.shape[0]
    tm = min(ROW_TILE, t)
    out = jax.ShapeDtypeStruct((t, W_ATT), BF16)
    row = lambda i: (i, 0)
    return pl.pallas_call(
        _in_proj_kernel,
        grid=(t // tm,),
        in_specs=[pl.BlockSpec((tm, D_MODEL), row),
                  pl.BlockSpec((1, D_MODEL), lambda i: (0, 0)),
                  pl.BlockSpec((D_MODEL, IN_WIDTH), lambda i: (0, 0))],
        out_specs=[pl.BlockSpec((tm, W_ATT), row)] * 4,
        out_shape=[out] * 4,
        compiler_params=_cparams(("parallel",)),
        name="in_proj",
    )(x2d, g, w_bf16)


def _t5_bucket(rel):
    half = N_BUCKETS // 2
    max_exact = half // 2
    ret = jnp.where(rel > 0, half, 0).astype(jnp.int32)
    n = jnp.abs(rel)
    nf = jnp.maximum(n, 1).astype(F32)
    large = max_exact + (jnp.log(nf / max_exact) / math.log(MAX_DISTANCE / max_exact)
                         * (half - max_exact)).astype(jnp.int32)
    large = jnp.minimum(large, half - 1)
    return ret + jnp.where(n < max_exact, n, large)


def _bias_tiles(rel_bias, t):
    i = jnp.arange(t, dtype=jnp.int32)
    d = jnp.arange(-2, 3, dtype=jnp.int32)
    rel = d[:, None, None] * t + i[None, None, :] - i[None, :, None]
    onehot = (_t5_bucket(rel)[..., None] == jnp.arange(N_BUCKETS, dtype=jnp.int32)).astype(F32)
    tiles = jnp.einsum('dqkn,nh->hdqk', onehot, rel_bias.astype(F32), precision=lax.Precision.HIGHEST)
    return tiles * LOG2E


def _attn_kernel(lam_ref, q_ref, k_ref, v_ref, bias_ref, g_ref, o_ref,
                 m_scr, l_scr, acc_scr, sa_scr, sb_scr, mxa_scr, mxb_scr, p_scr, *, t, sub, n_iter):
    qi = pl.program_id(2)
    q = q_ref[0]
    lane = lax.broadcasted_iota(jnp.int32, q.shape, 1)
    zero = jnp.zeros_like(q)
    qs = (jnp.where(lane < HEAD_DIM, q, zero), jnp.where(lane >= HEAD_DIM, q, zero))
    nb = t // LANES

    m_scr[...] = jnp.full(m_scr.shape, -jnp.inf, F32)
    l_scr[...] = jnp.zeros(l_scr.shape, F32)
    acc_scr[...] = jnp.zeros(acc_scr.shape, F32)

    def scores(j, s_scr, mx_scr):
        for mi in range(2):
            mx = None
            for c in range(sub):
                blk = j * sub + c
                kc = k_ref[0, pl.ds(pl.multiple_of(blk * t, t), t), :]
                s = lax.dot_general(qs[mi], kc, (((1,), (1,)), ((), ())), preferred_element_type=F32)
                s = s + bias_ref[0, jnp.clip(blk - qi, -2, 2) + 2]
                s_scr[mi, :, c * t:(c + 1) * t] = s
                for i in range(nb):
                    piece = s[:, i * LANES:(i + 1) * LANES]
                    mx = piece if mx is None else jnp.maximum(mx, piece)
            mx_scr[mi] = mx

    def accumulate(j, s_scr, mx_scr):
        vj = v_ref[0, pl.ds(pl.multiple_of(j * (sub * t), sub * t), sub * t), :]
        for mi in range(2):
            m_prev = m_scr[mi]
            m_next = jnp.maximum(m_prev, jnp.max(mx_scr[mi], axis=1, keepdims=True))
            alpha = jnp.exp2(m_prev - m_next)
            m_scr[mi] = m_next
            lsum = None
            for i in range(sub * nb):
                p = jnp.exp2(s_scr[mi, :, i * LANES:(i + 1) * LANES] - m_next)
                lsum = p if lsum is None else lsum + p
                p_scr[mi, :, i * LANES:(i + 1) * LANES] = p.astype(BF16)
            l_scr[mi] = alpha * l_scr[mi] + lsum
            acc_scr[mi] = alpha * acc_scr[mi] + jnp.dot(p_scr[mi], vj, preferred_element_type=F32)

    scores(0, sa_scr, mxa_scr)
    n_pairs = (n_iter - 1) // 2

    def body(i, carry):
        scores(2 * i + 1, sb_scr, mxb_scr)
        accumulate(2 * i, sa_scr, mxa_scr)
        scores(2 * i + 2, sa_scr, mxa_scr)
        accumulate(2 * i + 1, sb_scr, mxb_scr)
        return carry

    lax.fori_loop(0, n_pairs, body, 0)
    if (n_iter - 1) % 2 == 1:
        scores(n_iter - 1, sb_scr, mxb_scr)
        accumulate(n_iter - 2, sa_scr, mxa_scr)
        accumulate(n_iter - 1, sb_scr, mxb_scr)
    else:
        accumulate(n_iter - 1, sa_scr, mxa_scr)

    o1 = acc_scr[0] / jnp.sum(l_scr[0], axis=1, keepdims=True)
    o2 = acc_scr[1] / jnp.sum(l_scr[1], axis=1, keepdims=True)
    o = o1 - lam_ref[0] * o2
    ms = jnp.mean(o * o, axis=-1, keepdims=True)
    o_ref[0] = (o * lax.rsqrt(ms + RMS_EPS) * g_ref[...]).astype(BF16)


def _attention(q, k, v, bias_tiles, lam, g_scaled, t):
    b, l, _ = q.shape
    nq = l // t
    sub = 2 if nq % 2 == 0 else 1
    kern = functools.partial(_attn_kernel, t=t, sub=sub, n_iter=nq // sub)
    return pl.pallas_call(
        kern,
        grid=(b, N_HEADS, nq),
        in_specs=[pl.BlockSpec(memory_space=pltpu.SMEM),
                  pl.BlockSpec((1, t, V_DIM), lambda bi, h, qi: (bi, qi, h)),
                  pl.BlockSpec((1, l, V_DIM), lambda bi, h, qi: (bi, 0, h)),
                  pl.BlockSpec((1, l, V_DIM), lambda bi, h, qi: (bi, 0, h)),
                  pl.BlockSpec((1, 5, t, t), lambda bi, h, qi: (h, 0, 0, 0)),
                  pl.BlockSpec((1, V_DIM), lambda bi, h, qi: (0, 0))],
        out_specs=pl.BlockSpec((1, t, V_DIM), lambda bi, h, qi: (bi, qi, h)),
        out_shape=jax.ShapeDtypeStruct((b, l, W_ATT), BF16),
        scratch_shapes=[pltpu.VMEM((2, t, LANES), F32),
                        pltpu.VMEM((2, t, LANES), F32),
                        pltpu.VMEM((2, t, V_DIM), F32),
                        pltpu.VMEM((2, t, sub * t), F32),
                        pltpu.VMEM((2, t, sub * t), F32),
                        pltpu.VMEM((2, t, LANES), F32),
                        pltpu.VMEM((2, t, LANES), F32),
                        pltpu.VMEM((2, t, sub * t), BF16)],
        compiler_params=_cparams(("parallel", "parallel", "arbitrary")),
        name="diff_attention",
    )(lam, q, k, v, bias_tiles, g_scaled)


def _ssm_matrices(a_re, a_im, log_dt, b_re, b_im, c_re, c_im, d_skip):
    qn, g, p, hc = SSM_CHUNK, N_GROUPS, SSM_STATE, SSM_GROUP
    n = jnp.arange(qn + 1, dtype=F32)
    pw, bbar, cc = [], [], []
    for d in range(2):
        a = lax.complex(a_re[d].astype(F32), a_im[d].astype(F32))
        dt = jnp.exp(log_dt[d].astype(F32))[:, None]
        adt = a * dt
        a_bar = jnp.exp(adt)
        pw.append(jnp.exp(adt[None] * n[:, None, None]))
        bbar.append(((a_bar - 1.0) / a)[:, :, None]
                    * lax.complex(b_re[d].astype(F32), b_im[d].astype(F32)))
        cc.append(lax.complex(c_re[d].astype(F32), c_im[d].astype(F32)))

    hi = lax.Precision.HIGHEST
    kern = [jnp.einsum('gop,tgp,gpi->tgoi', cc[d], pw[d][:qn], bbar[d], precision=hi).real
            for d in range(2)]
    s_idx = jnp.arange(qn)[:, None]
    t_idx = jnp.arange(qn)[None, :]
    lag = t_idx - s_idx
    taus = jnp.arange(qn)
    sel_f = (lag[:, :, None] == taus).astype(F32)
    sel_b = (-lag[:, :, None] == taus).astype(F32)
    kf = jnp.einsum('stu,ugoi->stgoi', sel_f, kern[0], precision=hi)
    kb = jnp.einsum('stu,ugoi->stgoi', sel_b, kern[1], precision=hi)
    skip = (jnp.eye(qn, dtype=F32)[:, :, None, None, None]
            * (jnp.eye(hc, dtype=F32)[None, None, None] * d_skip.astype(F32).reshape(g, hc)[None, None, :, :, None]))
    m_full = kf + kb + skip
    m_mat = jnp.transpose(m_full, (2, 0, 4, 1, 3)).reshape(g, SSM_ROW, SSM_ROW)

    zeros_p = jnp.zeros((g, SSM_ROW, LANES - p), F32)

    def pad_cols(x):
        return jnp.concatenate([x, zeros_p], axis=-1)

    pf = jnp.einsum('sgp,gpi->gsip', pw[0][:qn][::-1], bbar[0]).reshape(g, SSM_ROW, p)
    pb = jnp.einsum('sgp,gpi->gsip', pw[1][:qn], bbar[1]).reshape(g, SSM_ROW, p)
    p_mat = jnp.concatenate([pad_cols(pf.real), pad_cols(pf.imag),
                             pad_cols(pb.real), pad_cols(pb.imag)], axis=-1)

    wf = jnp.einsum('gop,tgp->gpto', cc[0], pw[0][1:qn + 1]).reshape(g, p, SSM_ROW)
    wb = jnp.einsum('gop,tgp->gpto', cc[1], pw[1][1:qn + 1][::-1]).reshape(g, p, SSM_ROW)
    zeros_r = jnp.zeros((g, LANES - p, SSM_ROW), F32)
    r_mat = jnp.concatenate([wf.real, zeros_r, -wf.imag, zeros_r,
                             wb.real, zeros_r, -wb.imag, zeros_r], axis=1)

    zeros_a = jnp.zeros((g, LANES - p), F32)

    def pad_vec(x):
        return jnp.concatenate([x, zeros_a], axis=-1)

    alpha = jnp.stack([pad_vec(pw[0][qn].real), pad_vec(pw[0][qn].imag),
                       pad_vec(pw[1][qn].real), pad_vec(pw[1][qn].imag)], axis=1)
    return p_mat.astype(BF16), m_mat.astype(BF16), r_mat.astype(BF16), alpha


def _s5_kernel(u_ref, p_ref, m_ref, r_ref, a_ref, y_ref, s_scr, x_scr, *, nc, bsz):
    u = u_ref[0]
    s_scr[...] = jnp.dot(u, p_ref[0], preferred_element_type=F32)
    al = a_ref[0]
    afr = jnp.broadcast_to(al[0:1], (bsz, LANES))
    afi = jnp.broadcast_to(al[1:2], (bsz, LANES))
    abr = jnp.broadcast_to(al[2:3], (bsz, LANES))
    abi = jnp.broadcast_to(al[3:4], (bsz, LANES))
    zero = jnp.zeros((bsz, LANES), F32)

    def step(i, carry):
        fr, fi, br, bi = carry
        rf = pl.multiple_of(i * bsz, bsz)
        rb = pl.multiple_of((nc - 1 - i) * bsz, bsz)
        x_scr[pl.ds(rf, bsz), 0:LANES] = fr
        x_scr[pl.ds(rf, bsz), LANES:2 * LANES] = fi
        x_scr[pl.ds(rb, bsz), 2 * LANES:3 * LANES] = br
        x_scr[pl.ds(rb, bsz), 3 * LANES:4 * LANES] = bi
        sfr = s_scr[pl.ds(rf, bsz), 0:LANES]
        sfi = s_scr[pl.ds(rf, bsz), LANES:2 * LANES]
        sbr = s_scr[pl.ds(rb, bsz), 2 * LANES:3 * LANES]
        sbi = s_scr[pl.ds(rb, bsz), 3 * LANES:4 * LANES]
        return (afr * fr - afi * fi + sfr, afr * fi + afi * fr + sfi,
                abr * br - abi * bi + sbr, abr * bi + abi * br + sbi)

    lax.fori_loop(0, nc, step, (zero, zero, zero, zero))
    y = jnp.dot(u, m_ref[0], preferred_element_type=F32)
    y = y + jnp.dot(x_scr[...].astype(BF16), r_ref[0], preferred_element_type=F32)
    y_ref[0] = y


def _s5(u_grp, p_mat, m_mat, r_mat, alpha, nc, bsz):
    g, rows, _ = u_grp.shape
    kern = functools.partial(_s5_kernel, nc=nc, bsz=bsz)
    mat = pl.BlockSpec((1, SSM_ROW, SSM_ROW), lambda i: (i, 0, 0))
    seq = pl.BlockSpec((1, rows, SSM_ROW), lambda i: (i, 0, 0))
    return pl.pallas_call(
        kern,
        grid=(g,),
        in_specs=[seq, mat, mat, mat, pl.BlockSpec((1, 4, LANES), lambda i: (i, 0, 0))],
        out_specs=seq,
        out_shape=jax.ShapeDtypeStruct((g, rows, SSM_ROW), F32),
        scratch_shapes=[pltpu.VMEM((rows, 4 * LANES), F32), pltpu.VMEM((rows, 4 * LANES), F32)],
        compiler_params=_cparams(("parallel",)),
        name="s5_scan",
    )(u_grp, p_mat, m_mat, r_mat, alpha)


def _post_mix_kernel(x_ref, att_ref, y_ref, wglu_ref, bglu_ref, gs_ref, wout_ref, g2_ref,
                     wrh_ref, wrl_ref, br_ref, x2_ref, h2_ref, ids_ref, gates_ref):
    y = y_ref[...]
    y = 0.5 * y * (1.0 + jnp.tanh(math.sqrt(2.0 / math.pi) * (y + 0.044715 * (y * y * y))))
    z = jnp.dot(y.astype(BF16), wglu_ref[...], preferred_element_type=F32) + bglu_ref[...]
    y = y * (1.0 / (1.0 + jnp.exp(-z)))
    ms = jnp.mean(y * y, axis=-1, keepdims=True)
    ssm = (y * lax.rsqrt(ms + RMS_EPS) * gs_ref[...]).astype(BF16)
    mix = jnp.dot(att_ref[...], wout_ref[0:W_ATT, :], preferred_element_type=F32)
    mix = mix + jnp.dot(ssm, wout_ref[W_ATT:, :], preferred_element_type=F32)
    x2 = x_ref[...] + mix
    x2_ref[...] = x2
    ms2 = jnp.mean(x2 * x2, axis=-1, keepdims=True)
    h2 = x2 * lax.rsqrt(ms2 + RMS_EPS) * g2_ref[...]
    h2_ref[...] = h2
    h_hi = h2.astype(BF16)
    h_lo = (h2 - h_hi.astype(F32)).astype(BF16)
    logits = (jnp.dot(h_hi, wrh_ref[...], preferred_element_type=F32)
              + jnp.dot(h_lo, wrh_ref[...], preferred_element_type=F32)
              + jnp.dot(h_hi, wrl_ref[...], preferred_element_type=F32)) + br_ref[...]
    lane = lax.broadcasted_iota(jnp.int32, logits.shape, 1).astype(F32)
    neg = jnp.float32(-jnp.inf)
    cur = jnp.where(lane < N_EXPERTS, logits, neg)
    ids = jnp.zeros(logits.shape, F32)
    vals = jnp.zeros(logits.shape, F32)
    top = None
    den = None
    for kk in range(TOP_K):
        mx = jnp.max(cur, axis=1, keepdims=True)
        idx = jnp.min(jnp.where(cur == mx, lane, float(LANES)), axis=1, keepdims=True)
        if kk == 0:
            top = mx
        e = jnp.exp(mx - top)
        den = e if den is None else den + e
        ids = jnp.where(lane == kk, idx, ids)
        vals = jnp.where(lane == kk, e, vals)
        cur = jnp.where(lane == idx, neg, cur)
    ids_ref[...] = ids.astype(jnp.int32)
    gates_ref[...] = vals / den


def _post_mix(x2d, att, yssm, wglu, bglu, gs, wout, g2, wr_hi, wr_lo, br):
    t = x2d.shape[0]
    tm = min(ROW_TILE, t)
    row = lambda i: (i, 0)
    const = lambda i: (0, 0)
    return pl.pallas_call(
        _post_mix_kernel,
        grid=(t // tm,),
        in_specs=[pl.BlockSpec((tm, D_MODEL), row),
                  pl.BlockSpec((tm, W_ATT), row),
                  pl.BlockSpec((tm, W_SSM), row),
                  pl.BlockSpec((W_SSM, W_SSM), const),
                  pl.BlockSpec((1, W_SSM), const),
                  pl.BlockSpec((1, W_SSM), const),
                  pl.BlockSpec((D_MODEL, D_MODEL), const),
                  pl.BlockSpec((1, D_MODEL), const),
                  pl.BlockSpec((D_MODEL, LANES), const),
                  pl.BlockSpec((D_MODEL, LANES), const),
                  pl.BlockSpec((1, LANES), const)],
        out_specs=[pl.BlockSpec((tm, D_MODEL), row),
                   pl.BlockSpec((tm, D_MODEL), row),
                   pl.BlockSpec((tm, LANES), row),
                   pl.BlockSpec((tm, LANES), row)],
        out_shape=[jax.ShapeDtypeStruct((t, D_MODEL), F32),
                   jax.ShapeDtypeStruct((t, D_MODEL), F32),
                   jax.ShapeDtypeStruct((t, LANES), jnp.int32),
                   jax.ShapeDtypeStruct((t, LANES), F32)],
        compiler_params=_cparams(("parallel",)),
        name="post_mix",
    )(x2d, att, yssm, wglu, bglu, gs, wout, g2, wr_hi, wr_lo, br)


def _row_copy(src_hbm, row, dst, dst_row, sem):
    return pltpu.make_async_copy(src_hbm.at[pl.ds(row, 1), :], dst.at[pl.ds(dst_row, 1), :], sem)


def _moe_ffn_kernel(be_ref, cur_ref, nxt_ref, h_hbm, w1_ref, b1_ref, w2_ref, b2_ref, y_ref,
                    xbuf, hdn_scr, sem, *, rows):
    i = pl.program_id(0)
    n = pl.num_programs(0)
    slot = i % 2

    def block_copy(dst_slot):
        return pltpu.make_async_copy(h_hbm.at[pl.ds(0, rows), :], xbuf.at[dst_slot], sem.at[dst_slot])

    @pl.when(i == 0)
    def _():
        def issue(g, c):
            for u in range(DMA_UNROLL):
                r = g * DMA_UNROLL + u
                _row_copy(h_hbm, cur_ref[0, 0, r], xbuf.at[0], r, sem.at[0]).start()
            return c

        lax.fori_loop(0, rows // DMA_UNROLL, issue, 0)

    block_copy(slot).wait()
    x = xbuf[slot].astype(BF16)

    n_chunks = (2 * D_FF + D_MODEL) // FFN_CHUNK
    per_chunk = rows // n_chunks

    def issue_next(chunk):
        for r in range(chunk * per_chunk, (chunk + 1) * per_chunk):
            _row_copy(h_hbm, nxt_ref[0, 0, r], xbuf.at[1 - slot], r, sem.at[1 - slot]).start()

    chunk = 0
    for c in range(2 * D_FF // FFN_CHUNK):
        issue_next(chunk)
        chunk += 1
        cols = slice(c * FFN_CHUNK, (c + 1) * FFN_CHUNK)
        hdn_scr[:, cols] = jnp.dot(x, w1_ref[0, :, cols], preferred_element_type=F32) + b1_ref[0, :, cols]
    gate = jnp.minimum(hdn_scr[:, :D_FF], SWIGLU_LIMIT)
    lin = jnp.clip(hdn_scr[:, D_FF:], -SWIGLU_LIMIT, SWIGLU_LIMIT)
    act = (gate * (1.0 / (1.0 + jnp.exp(-SWIGLU_ALPHA * gate))) * (lin + 1.0)).astype(BF16)
    for c in range(D_MODEL // FFN_CHUNK):
        issue_next(chunk)
        chunk += 1
        cols = slice(c * FFN_CHUNK, (c + 1) * FFN_CHUNK)
        y_ref[:, cols] = jnp.dot(act, w2_ref[0, :, cols], preferred_element_type=F32) + b2_ref[0, :, cols]
    for r in range(n_chunks * per_chunk, rows):
        _row_copy(h_hbm, nxt_ref[0, 0, r], xbuf.at[1 - slot], r, sem.at[1 - slot]).start()

    @pl.when(i == n - 1)
    def _():
        block_copy(1 - slot).wait()


def _moe_ffn(block_e, tok_pad, h2, w1, b1, w2, b2):
    n_pad = tok_pad.shape[0]
    rows = EXPERT_ROWS
    steps = n_pad // rows
    tok3 = tok_pad.reshape(steps, 1, rows)
    kern = functools.partial(_moe_ffn_kernel, rows=rows)
    grid_spec = pltpu.PrefetchScalarGridSpec(
        num_scalar_prefetch=1,
        grid=(steps,),
        in_specs=[pl.BlockSpec((1, 1, rows), lambda i, be: (i, 0, 0), memory_space=pltpu.SMEM),
                  pl.BlockSpec((1, 1, rows), lambda i, be: (jnp.minimum(i + 1, steps - 1), 0, 0),
                               memory_space=pltpu.SMEM),
                  pl.BlockSpec(memory_space=pl.ANY),
                  pl.BlockSpec((1, D_MODEL, 2 * D_FF), lambda i, be: (be[i], 0, 0)),
                  pl.BlockSpec((1, 1, 2 * D_FF), lambda i, be: (be[i], 0, 0)),
                  pl.BlockSpec((1, D_FF, D_MODEL), lambda i, be: (be[i], 0, 0)),
                  pl.BlockSpec((1, 1, D_MODEL), lambda i, be: (be[i], 0, 0))],
        out_specs=pl.BlockSpec((rows, D_MODEL), lambda i, be: (i, 0)),
        scratch_shapes=[pltpu.VMEM((2, rows, D_MODEL), F32),
                        pltpu.VMEM((rows, 2 * D_FF), F32),
                        pltpu.SemaphoreType.DMA((2,))],
    )
    return pl.pallas_call(
        kern,
        grid_spec=grid_spec,
        out_shape=jax.ShapeDtypeStruct((n_pad, D_MODEL), F32),
        compiler_params=_cparams(("arbitrary",)),
        name="moe_ffn",
    )(block_e, tok3, tok3, h2, w1, b1, w2, b2)


def _combine_kernel(pos_ref, nxt_ref, x2_ref, gates_ref, gf_ref, y_hbm, o_ref, buf, sem, *, rows):
    i = pl.program_id(0)
    n = pl.num_programs(0)
    slot = i % 2

    def issue(idx_ref, dst_slot):
        def body(r, c):
            for u in range(2):
                for kk in range(TOP_K):
                    row = r * 2 + u
                    _row_copy(y_hbm, idx_ref[0, 0, row * TOP_K + kk], buf.at[dst_slot, kk], row,
                              sem.at[dst_slot]).start()
            return c

        lax.fori_loop(0, rows // 2, body, 0)

    @pl.when(i == 0)
    def _():
        issue(pos_ref, 0)

    @pl.when(i + 1 < n)
    def _():
        issue(nxt_ref, 1 - slot)

    for kk in range(TOP_K):
        pltpu.make_async_copy(y_hbm.at[pl.ds(0, rows), :], buf.at[slot, kk], sem.at[slot]).wait()

    gates = gates_ref[...]
    y = x2_ref[...]
    for kk in range(TOP_K):
        y = y + gates[:, kk:kk + 1] * buf[slot, kk]
    ms = jnp.mean(y * y, axis=-1, keepdims=True)
    o_ref[...] = y * lax.rsqrt(ms + RMS_EPS) * gf_ref[...]


def _combine(pos, x2, gates, gf, y_pad):
    t = x2.shape[0]
    rows = min(COMBINE_ROWS, t)
    steps = t // rows
    kern = functools.partial(_combine_kernel, rows=rows)
    row = lambda i: (i, 0)
    pos3 = pos.reshape(steps, 1, rows * TOP_K)
    return pl.pallas_call(
        kern,
        grid=(steps,),
        in_specs=[pl.BlockSpec((1, 1, rows * TOP_K), lambda i: (i, 0, 0), memory_space=pltpu.SMEM),
                  pl.BlockSpec((1, 1, rows * TOP_K), lambda i: (jnp.minimum(i + 1, steps - 1), 0, 0),
                               memory_space=pltpu.SMEM),
                  pl.BlockSpec((rows, D_MODEL), row),
                  pl.BlockSpec((rows, LANES), row),
                  pl.BlockSpec((1, D_MODEL), lambda i: (0, 0)),
                  pl.BlockSpec(memory_space=pl.ANY)],
        out_specs=pl.BlockSpec((rows, D_MODEL), row),
        out_shape=jax.ShapeDtypeStruct((t, D_MODEL), F32),
        scratch_shapes=[pltpu.VMEM((2, TOP_K, rows, D_MODEL), F32), pltpu.SemaphoreType.DMA((2,))],
        compiler_params=_cparams(("arbitrary",)),
        name="moe_combine",
    )(pos3, pos3, x2, gates, gf, y_pad)


def _dispatch_plan(top_e, n_tok):
    n = n_tok * TOP_K
    blk = EXPERT_ROWS
    flat_e = top_e.reshape(n)
    order = jnp.argsort(flat_e).astype(jnp.int32)
    rank = jnp.argsort(order).astype(jnp.int32)
    st = order // TOP_K
    experts = jnp.arange(N_EXPERTS, dtype=jnp.int32)
    counts = jnp.sum((flat_e[:, None] == experts).astype(jnp.int32), axis=0)
    padded = (counts + blk - 1) // blk * blk
    start = jnp.cumsum(counts) - counts
    pend = jnp.cumsum(padded)
    pstart = pend - padded
    pos = (pstart - start)[flat_e] + rank
    n_blocks = n // blk + N_EXPERTS
    block_start = jnp.arange(n_blocks, dtype=jnp.int32) * blk
    block_e = jnp.minimum(jnp.sum((pend[None, :] <= block_start[:, None]).astype(jnp.int32), axis=1),
                          N_EXPERTS - 1)
    slot = jnp.arange(n_blocks * blk, dtype=jnp.int32)
    slot_e = jnp.repeat(block_e, blk)
    off = slot - pstart[slot_e]
    src = jnp.clip(start[slot_e] + off, 0, n - 1)
    tok_pad = jnp.where(off < counts[slot_e], st[src], 0)
    return tok_pad, pos, block_e


def _trunk(x, prm):
    bsz, l, _ = x.shape
    t = bsz * l
    x2d = x.reshape(t, D_MODEL)
    q, k, v, u = _in_proj(x2d, prm['norm1_g'], prm['w_in'])

    t_att = min(512, l)
    att = _attention(q.reshape(bsz, l, W_ATT), k.reshape(bsz, l, W_ATT), v.reshape(bsz, l, W_ATT),
                     prm['bias_tiles'][t_att], prm['lam'], prm['subln_g'], t_att)

    nc = l // SSM_CHUNK
    u_grp = jnp.transpose(u.reshape(bsz, nc, SSM_CHUNK, N_GROUPS, SSM_GROUP), (3, 1, 0, 2, 4))
    u_grp = u_grp.reshape(N_GROUPS, nc * bsz, SSM_ROW)
    y_grp = _s5(u_grp, prm['ssm_p'], prm['ssm_m'], prm['ssm_r'], prm['ssm_alpha'], nc, bsz)
    yssm = jnp.transpose(y_grp.reshape(N_GROUPS, nc, bsz, SSM_CHUNK, SSM_GROUP), (2, 1, 3, 0, 4))
    yssm = yssm.reshape(t, W_SSM)

    x2, h2, ids, gates = _post_mix(x2d, att.reshape(t, W_ATT), yssm, prm['w_glu'], prm['b_glu'],
                                   prm['ssm_norm_g'], prm['w_out'], prm['norm2_g'],
                                   prm['w_router_hi'], prm['w_router_lo'], prm['b_router'])

    tok_pad, pos, block_e = _dispatch_plan(ids[:, :TOP_K], t)
    y_pad = _moe_ffn(block_e, tok_pad, h2, prm['w_moe1'], prm['b_moe1'], prm['w_moe2'], prm['b_moe2'])
    out = _combine(pos, x2, gates, prm['normf_g'], y_pad)
    return out.reshape(bsz, l, D_MODEL)


def _prepare(seq_lens, rel_bias, norm1_g, w_in, lambda_q1, lambda_k1, lambda_q2, lambda_k2, subln_g,
             ssm_A_re, ssm_A_im, ssm_log_dt, ssm_B_re, ssm_B_im, ssm_C_re, ssm_C_im, ssm_D,
             w_glu, b_glu, ssm_norm_g, w_out, norm2_g, w_router, b_router,
             w_moe1, b_moe1, w_moe2, b_moe2, normf_g):
    layer = 0
    lambda_init = 0.8 - 0.6 * math.exp(-0.3 * layer)
    lam = (jnp.exp(jnp.sum(lambda_q1[layer].astype(F32) * lambda_k1[layer].astype(F32)))
           - jnp.exp(jnp.sum(lambda_q2[layer].astype(F32) * lambda_k2[layer].astype(F32))) + lambda_init)
    p_mat, m_mat, r_mat, alpha = _ssm_matrices(
        ssm_A_re[layer], ssm_A_im[layer], ssm_log_dt[layer], ssm_B_re[layer], ssm_B_im[layer],
        ssm_C_re[layer], ssm_C_im[layer], ssm_D[layer])
    pad_e = LANES - N_EXPERTS
    w_r = jnp.pad(w_router[layer].astype(F32), ((0, 0), (0, pad_e)))
    w_r_hi = w_r.astype(BF16)
    return {
        'norm1_g': norm1_g[layer].reshape(1, D_MODEL).astype(F32),
        'w_in': w_in[layer].astype(BF16),
        'lam': lam.reshape(1).astype(F32),
        'subln_g': (subln_g[layer].astype(F32) * (1.0 - lambda_init)).reshape(1, V_DIM),
        'bias_tiles': {t: _bias_tiles(rel_bias, t) for t in sorted({min(512, l) for l in seq_lens})},
        'ssm_p': p_mat, 'ssm_m': m_mat, 'ssm_r': r_mat, 'ssm_alpha': alpha,
        'w_glu': w_glu[layer].astype(BF16),
        'b_glu': b_glu[layer].reshape(1, W_SSM).astype(F32),
        'ssm_norm_g': ssm_norm_g[layer].reshape(1, W_SSM).astype(F32),
        'w_out': w_out[layer].astype(BF16),
        'norm2_g': norm2_g[layer].reshape(1, D_MODEL).astype(F32),
        'w_router_hi': w_r_hi,
        'w_router_lo': (w_r - w_r_hi.astype(F32)).astype(BF16),
        'b_router': jnp.pad(b_router[layer].astype(F32), (0, pad_e)).reshape(1, LANES),
        'w_moe1': w_moe1[layer].astype(BF16),
        'b_moe1': b_moe1[layer].reshape(N_EXPERTS, 1, 2 * D_FF).astype(F32),
        'w_moe2': w_moe2[layer].astype(BF16),
        'b_moe2': b_moe2[layer].reshape(N_EXPERTS, 1, D_MODEL).astype(F32),
        'normf_g': normf_g.reshape(1, D_MODEL).astype(F32),
    }


def kernel(x_prompt, x_sample, rel_bias, norm1_g, w_in, lambda_q1, lambda_k1, lambda_q2, lambda_k2, subln_g, ssm_A_re, ssm_A_im, ssm_log_dt, ssm_B_re, ssm_B_im, ssm_C_re, ssm_C_im, ssm_D, w_glu, b_glu, ssm_norm_g, w_out, norm2_g, w_router, b_router, w_moe1, b_moe1, w_moe2, b_moe2, normf_g):
    prm = _prepare((x_prompt.shape[1], x_sample.shape[1]), rel_bias, norm1_g, w_in, lambda_q1,
                   lambda_k1, lambda_q2, lambda_k2, subln_g, ssm_A_re, ssm_A_im, ssm_log_dt,
                   ssm_B_re, ssm_B_im, ssm_C_re, ssm_C_im, ssm_D, w_glu, b_glu, ssm_norm_g, w_out,
                   norm2_g, w_router, b_router, w_moe1, b_moe1, w_moe2, b_moe2, normf_g)
    return (_trunk(x_prompt, prm), _trunk(x_sample, prm))
```

```python
import functools
import math

import jax
import jax.numpy as jnp
from jax import lax
from jax.experimental import pallas as pl
from jax.experimental.pallas import tpu as pltpu
from jax.experimental.pallas import tpu_sc as plsc

F32 = jnp.float32
BF16 = jnp.bfloat16

D_MODEL = 1024
W_ATT = 512
W_SSM = 512
HEAD_DIM = 64
N_HEADS = 4
V_DIM = 2 * HEAD_DIM
SSM_GROUP = 16
N_GROUPS = W_SSM // SSM_GROUP
SSM_STATE = 64
IN_WIDTH = 3 * W_ATT + W_SSM
N_BUCKETS = 32
MAX_DISTANCE = 128
N_EXPERTS = 32
TOP_K = 4
D_FF = D_MODEL
SWIGLU_ALPHA = 1.702
SWIGLU_LIMIT = 7.0
RMS_EPS = 1e-6
ATT_SCALE = HEAD_DIM ** -0.5
LOG2E = math.log2(math.e)

LANES = 128
SSM_CHUNK = 32
SSM_ROW = SSM_CHUNK * SSM_GROUP
ROW_TILE = 512
EXPERT_ROWS = 256
SC_WINDOW = 128
SC_COLS = 256
COMBINE_ROWS = 256
VMEM_LIMIT = 48 * 1024 * 1024


def _cparams(sem):
    return pltpu.CompilerParams(dimension_semantics=sem, vmem_limit_bytes=VMEM_LIMIT)


def _in_proj_kernel(x_ref, g_ref, w_ref, q_ref, k_ref, v_ref, u_ref):
    x = x_ref[...]
    ms = jnp.mean(x * x, axis=-1, keepdims=True)
    h = (x * lax.rsqrt(ms + RMS_EPS) * g_ref[...]).astype(BF16)
    proj = jnp.dot(h, w_ref[...], preferred_element_type=F32)
    q_ref[...] = (proj[:, 0:W_ATT] * (ATT_SCALE * LOG2E)).astype(BF16)
    k_ref[...] = proj[:, W_ATT:2 * W_ATT].astype(BF16)
    v_ref[...] = proj[:, 2 * W_ATT:3 * W_ATT].astype(BF16)
    u_ref[...] = proj[:, 3 * W_ATT:].astype(BF16)


def _in_proj(x2d, g, w_bf16):
    t = x2d.shape[0]
    tm = min(ROW_TILE, t)
    out = jax.ShapeDtypeStruct((t, W_ATT), BF16)
    row = lambda i: (i, 0)
    return pl.pallas_call(
        _in_proj_kernel,
        grid=(t // tm,),
        in_specs=[pl.BlockSpec((tm, D_MODEL), row),
                  pl.BlockSpec((1, D_MODEL), lambda i: (0, 0)),
                  pl.BlockSpec((D_MODEL, IN_WIDTH), lambda i: (0, 0))],
        out_specs=[pl.BlockSpec((tm, W_ATT), row)] * 4,
        out_shape=[out] * 4,
        compiler_params=_cparams(("parallel",)),
        name="in_proj",
    )(x2d, g, w_bf16)


def _t5_bucket(rel):
    half = N_BUCKETS // 2
    max_exact = half // 2
    ret = jnp.where(rel > 0, half, 0).astype(jnp.int32)
    n = jnp.abs(rel)
    nf = jnp.maximum(n, 1).astype(F32)
    large = max_exact + (jnp.log(nf / max_exact) / math.log(MAX_DISTANCE / max_exact)
                         * (half - max_exact)).astype(jnp.int32)
    large = jnp.minimum(large, half - 1)
    return ret + jnp.where(n < max_exact, n, large)


def _bias_tiles(rel_bias, t):
    i = jnp.arange(t, dtype=jnp.int32)
    d = jnp.arange(-2, 3, dtype=jnp.int32)
    rel = d[:, None, None] * t + i[None, None, :] - i[None, :, None]
    onehot = (_t5_bucket(rel)[..., None] == jnp.arange(N_BUCKETS, dtype=jnp.int32)).astype(F32)
    tiles = jnp.einsum('dqkn,nh->hdqk', onehot, rel_bias.astype(F32), precision=lax.Precision.HIGHEST)
    return tiles * LOG2E


def _attn_kernel(lam_ref, q_ref, k_ref, v_ref, bias_ref, g_ref, o_ref,
                 m_scr, l_scr, acc_scr, sa_scr, sb_scr, mxa_scr, mxb_scr, p_scr, *, t, sub, n_iter):
    qi = pl.program_id(2)
    q = q_ref[0]
    lane = lax.broadcasted_iota(jnp.int32, q.shape, 1)
    zero = jnp.zeros_like(q)
    qs = (jnp.where(lane < HEAD_DIM, q, zero), jnp.where(lane >= HEAD_DIM, q, zero))
    nb = t // LANES

    m_scr[...] = jnp.full(m_scr.shape, -jnp.inf, F32)
    l_scr[...] = jnp.zeros(l_scr.shape, F32)
    acc_scr[...] = jnp.zeros(acc_scr.shape, F32)

    def scores(j, s_scr, mx_scr):
        for mi in range(2):
            mx = None
            for c in range(sub):
                blk = j * sub + c
                kc = k_ref[0, pl.ds(pl.multiple_of(blk * t, t), t), :]
                s = lax.dot_general(qs[mi], kc, (((1,), (1,)), ((), ())), preferred_element_type=F32)
                s = s + bias_ref[0, jnp.clip(blk - qi, -2, 2) + 2]
                s_scr[mi, :, c * t:(c + 1) * t] = s
                for i in range(nb):
                    piece = s[:, i * LANES:(i + 1) * LANES]
                    mx = piece if mx is None else jnp.maximum(mx, piece)
            mx_scr[mi] = mx

    def accumulate(j, s_scr, mx_scr):
        vj = v_ref[0, pl.ds(pl.multiple_of(j * (sub * t), sub * t), sub * t), :]
        for mi in range(2):
            m_prev = m_scr[mi]
            m_next = jnp.maximum(m_prev, jnp.max(mx_scr[mi], axis=1, keepdims=True))
            alpha = jnp.exp2(m_prev - m_next)
            m_scr[mi] = m_next
            lsum = None
            for i in range(sub * nb):
                p = jnp.exp2(s_scr[mi, :, i * LANES:(i + 1) * LANES] - m_next)
                lsum = p if lsum is None else lsum + p
                p_scr[mi, :, i * LANES:(i + 1) * LANES] = p.astype(BF16)
            l_scr[mi] = alpha * l_scr[mi] + lsum
            acc_scr[mi] = alpha * acc_scr[mi] + jnp.dot(p_scr[mi], vj, preferred_element_type=F32)

    scores(0, sa_scr, mxa_scr)
    n_pairs = (n_iter - 1) // 2

    def body(i, carry):
        scores(2 * i + 1, sb_scr, mxb_scr)
        accumulate(2 * i, sa_scr, mxa_scr)
        scores(2 * i + 2, sa_scr, mxa_scr)
        accumulate(2 * i + 1, sb_scr, mxb_scr)
        return carry

    lax.fori_loop(0, n_pairs, body, 0)
    if (n_iter - 1) % 2 == 1:
        scores(n_iter - 1, sb_scr, mxb_scr)
        accumulate(n_iter - 2, sa_scr, mxa_scr)
        accumulate(n_iter - 1, sb_scr, mxb_scr)
    else:
        accumulate(n_iter - 1, sa_scr, mxa_scr)

    o1 = acc_scr[0] / jnp.sum(l_scr[0], axis=1, keepdims=True)
    o2 = acc_scr[1] / jnp.sum(l_scr[1], axis=1, keepdims=True)
    o = o1 - lam_ref[0] * o2
    ms = jnp.mean(o * o, axis=-1, keepdims=True)
    o_ref[0] = (o * lax.rsqrt(ms + RMS_EPS) * g_ref[...]).astype(BF16)


def _attention(q, k, v, bias_tiles, lam, g_scaled, t):
    b, l, _ = q.shape
    nq = l // t
    sub = 2 if nq % 2 == 0 else 1
    kern = functools.partial(_attn_kernel, t=t, sub=sub, n_iter=nq // sub)
    return pl.pallas_call(
        kern,
        grid=(b, N_HEADS, nq),
        in_specs=[pl.BlockSpec(memory_space=pltpu.SMEM),
                  pl.BlockSpec((1, t, V_DIM), lambda bi, h, qi: (bi, qi, h)),
                  pl.BlockSpec((1, l, V_DIM), lambda bi, h, qi: (bi, 0, h)),
                  pl.BlockSpec((1, l, V_DIM), lambda bi, h, qi: (bi, 0, h)),
                  pl.BlockSpec((1, 5, t, t), lambda bi, h, qi: (h, 0, 0, 0)),
                  pl.BlockSpec((1, V_DIM), lambda bi, h, qi: (0, 0))],
        out_specs=pl.BlockSpec((1, t, V_DIM), lambda bi, h, qi: (bi, qi, h)),
        out_shape=jax.ShapeDtypeStruct((b, l, W_ATT), BF16),
        scratch_shapes=[pltpu.VMEM((2, t, LANES), F32),
                        pltpu.VMEM((2, t, LANES), F32),
                        pltpu.VMEM((2, t, V_DIM), F32),
                        pltpu.VMEM((2, t, sub * t), F32),
                        pltpu.VMEM((2, t, sub * t), F32),
                        pltpu.VMEM((2, t, LANES), F32),
                        pltpu.VMEM((2, t, LANES), F32),
                        pltpu.VMEM((2, t, sub * t), BF16)],
        compiler_params=_cparams(("parallel", "parallel", "arbitrary")),
        name="diff_attention",
    )(lam, q, k, v, bias_tiles, g_scaled)


def _ssm_matrices(a_re, a_im, log_dt, b_re, b_im, c_re, c_im, d_skip):
    qn, g, p, hc = SSM_CHUNK, N_GROUPS, SSM_STATE, SSM_GROUP
    n = jnp.arange(qn + 1, dtype=F32)
    pw, bbar, cc = [], [], []
    for d in range(2):
        a = lax.complex(a_re[d].astype(F32), a_im[d].astype(F32))
        dt = jnp.exp(log_dt[d].astype(F32))[:, None]
        adt = a * dt
        a_bar = jnp.exp(adt)
        pw.append(jnp.exp(adt[None] * n[:, None, None]))
        bbar.append(((a_bar - 1.0) / a)[:, :, None]
                    * lax.complex(b_re[d].astype(F32), b_im[d].astype(F32)))
        cc.append(lax.complex(c_re[d].astype(F32), c_im[d].astype(F32)))

    hi = lax.Precision.HIGHEST
    kern = [jnp.einsum('gop,tgp,gpi->tgoi', cc[d], pw[d][:qn], bbar[d], precision=hi).real
            for d in range(2)]
    s_idx = jnp.arange(qn)[:, None]
    t_idx = jnp.arange(qn)[None, :]
    lag = t_idx - s_idx
    taus = jnp.arange(qn)
    sel_f = (lag[:, :, None] == taus).astype(F32)
    sel_b = (-lag[:, :, None] == taus).astype(F32)
    kf = jnp.einsum('stu,ugoi->stgoi', sel_f, kern[0], precision=hi)
    kb = jnp.einsum('stu,ugoi->stgoi', sel_b, kern[1], precision=hi)
    skip = (jnp.eye(qn, dtype=F32)[:, :, None, None, None]
            * (jnp.eye(hc, dtype=F32)[None, None, None] * d_skip.astype(F32).reshape(g, hc)[None, None, :, :, None]))
    m_full = kf + kb + skip
    m_mat = jnp.transpose(m_full, (2, 0, 4, 1, 3)).reshape(g, SSM_ROW, SSM_ROW)

    zeros_p = jnp.zeros((g, SSM_ROW, LANES - p), F32)

    def pad_cols(x):
        return jnp.concatenate([x, zeros_p], axis=-1)

    pf = jnp.einsum('sgp,gpi->gsip', pw[0][:qn][::-1], bbar[0]).reshape(g, SSM_ROW, p)
    pb = jnp.einsum('sgp,gpi->gsip', pw[1][:qn], bbar[1]).reshape(g, SSM_ROW, p)
    p_mat = jnp.concatenate([pad_cols(pf.real), pad_cols(pf.imag),
                             pad_cols(pb.real), pad_cols(pb.imag)], axis=-1)

    wf = jnp.einsum('gop,tgp->gpto', cc[0], pw[0][1:qn + 1]).reshape(g, p, SSM_ROW)
    wb = jnp.einsum('gop,tgp->gpto', cc[1], pw[1][1:qn + 1][::-1]).reshape(g, p, SSM_ROW)
    zeros_r = jnp.zeros((g, LANES - p, SSM_ROW), F32)
    r_mat = jnp.concatenate([wf.real, zeros_r, -wf.imag, zeros_r,
                             wb.real, zeros_r, -wb.imag, zeros_r], axis=1)

    zeros_a = jnp.zeros((g, LANES - p), F32)

    def pad_vec(x):
        return jnp.concatenate([x, zeros_a], axis=-1)

    alpha = jnp.stack([pad_vec(pw[0][qn].real), pad_vec(pw[0][qn].imag),
                       pad_vec(pw[1][qn].real), pad_vec(pw[1][qn].imag)], axis=1)
    return p_mat.astype(BF16), m_mat.astype(BF16), r_mat.astype(BF16), alpha


def _s5_kernel(u_ref, p_ref, m_ref, r_ref, a_ref, y_ref, s_scr, x_scr, *, nc, bsz):
    u = u_ref[0]
    s_scr[...] = jnp.dot(u, p_ref[0], preferred_element_type=F32)
    al = a_ref[0]
    afr = jnp.broadcast_to(al[0:1], (bsz, LANES))
    afi = jnp.broadcast_to(al[1:2], (bsz, LANES))
    abr = jnp.broadcast_to(al[2:3], (bsz, LANES))
    abi = jnp.broadcast_to(al[3:4], (bsz, LANES))
    zero = jnp.zeros((bsz, LANES), F32)

    def step(i, carry):
        fr, fi, br, bi = carry
        rf = pl.multiple_of(i * bsz, bsz)
        rb = pl.multiple_of((nc - 1 - i) * bsz, bsz)
        x_scr[pl.ds(rf, bsz), 0:LANES] = fr
        x_scr[pl.ds(rf, bsz), LANES:2 * LANES] = fi
        x_scr[pl.ds(rb, bsz), 2 * LANES:3 * LANES] = br
        x_scr[pl.ds(rb, bsz), 3 * LANES:4 * LANES] = bi
        sfr = s_scr[pl.ds(rf, bsz), 0:LANES]
        sfi = s_scr[pl.ds(rf, bsz), LANES:2 * LANES]
        sbr = s_scr[pl.ds(rb, bsz), 2 * LANES:3 * LANES]
        sbi = s_scr[pl.ds(rb, bsz), 3 * LANES:4 * LANES]
        return (afr * fr - afi * fi + sfr, afr * fi + afi * fr + sfi,
                abr * br - abi * bi + sbr, abr * bi + abi * br + sbi)

    lax.fori_loop(0, nc, step, (zero, zero, zero, zero))
    y = jnp.dot(u, m_ref[0], preferred_element_type=F32)
    y = y + jnp.dot(x_scr[...].astype(BF16), r_ref[0], preferred_element_type=F32)
    y_ref[0] = y


def _s5(u_grp, p_mat, m_mat, r_mat, alpha, nc, bsz):
    g, rows, _ = u_grp.shape
    kern = functools.partial(_s5_kernel, nc=nc, bsz=bsz)
    mat = pl.BlockSpec((1, SSM_ROW, SSM_ROW), lambda i: (i, 0, 0))
    seq = pl.BlockSpec((1, rows, SSM_ROW), lambda i: (i, 0, 0))
    return pl.pallas_call(
        kern,
        grid=(g,),
        in_specs=[seq, mat, mat, mat, pl.BlockSpec((1, 4, LANES), lambda i: (i, 0, 0))],
        out_specs=seq,
        out_shape=jax.ShapeDtypeStruct((g, rows, SSM_ROW), F32),
        scratch_shapes=[pltpu.VMEM((rows, 4 * LANES), F32), pltpu.VMEM((rows, 4 * LANES), F32)],
        compiler_params=_cparams(("parallel",)),
        name="s5_scan",
    )(u_grp, p_mat, m_mat, r_mat, alpha)


def _post_mix_kernel(x_ref, att_ref, y_ref, wglu_ref, bglu_ref, gs_ref, wout_ref, g2_ref,
                     wrh_ref, wrl_ref, br_ref, x2_ref, h2_ref, ids_ref, gates_ref):
    y = y_ref[...]
    y = 0.5 * y * (1.0 + jnp.tanh(math.sqrt(2.0 / math.pi) * (y + 0.044715 * (y * y * y))))
    z = jnp.dot(y.astype(BF16), wglu_ref[...], preferred_element_type=F32) + bglu_ref[...]
    y = y * (1.0 / (1.0 + jnp.exp(-z)))
    ms = jnp.mean(y * y, axis=-1, keepdims=True)
    ssm = (y * lax.rsqrt(ms + RMS_EPS) * gs_ref[...]).astype(BF16)
    mix = jnp.dot(att_ref[...], wout_ref[0:W_ATT, :], preferred_element_type=F32)
    mix = mix + jnp.dot(ssm, wout_ref[W_ATT:, :], preferred_element_type=F32)
    x2 = x_ref[...] + mix
    x2_ref[...] = x2
    ms2 = jnp.mean(x2 * x2, axis=-1, keepdims=True)
    h2 = x2 * lax.rsqrt(ms2 + RMS_EPS) * g2_ref[...]
    h2_ref[...] = h2
    h_hi = h2.astype(BF16)
    h_lo = (h2 - h_hi.astype(F32)).astype(BF16)
    logits = (jnp.dot(h_hi, wrh_ref[...], preferred_element_type=F32)
              + jnp.dot(h_lo, wrh_ref[...], preferred_element_type=F32)
              + jnp.dot(h_hi, wrl_ref[...], preferred_element_type=F32)) + br_ref[...]
    lane = lax.broadcasted_iota(jnp.int32, logits.shape, 1).astype(F32)
    neg = jnp.float32(-jnp.inf)
    cur = jnp.where(lane < N_EXPERTS, logits, neg)
    ids = jnp.zeros(logits.shape, F32)
    vals = jnp.zeros(logits.shape, F32)
    top = None
    den = None
    for kk in range(TOP_K):
        mx = jnp.max(cur, axis=1, keepdims=True)
        idx = jnp.min(jnp.where(cur == mx, lane, float(LANES)), axis=1, keepdims=True)
        if kk == 0:
            top = mx
        e = jnp.exp(mx - top)
        den = e if den is None else den + e
        ids = jnp.where(lane == kk, idx, ids)
        vals = jnp.where(lane == kk, e, vals)
        cur = jnp.where(lane == idx, neg, cur)
    ids_ref[...] = ids.astype(jnp.int32)
    gates_ref[...] = vals / den


def _post_mix(x2d, att, yssm, wglu, bglu, gs, wout, g2, wr_hi, wr_lo, br):
    t = x2d.shape[0]
    tm = min(ROW_TILE, t)
    row = lambda i: (i, 0)
    const = lambda i: (0, 0)
    return pl.pallas_call(
        _post_mix_kernel,
        grid=(t // tm,),
        in_specs=[pl.BlockSpec((tm, D_MODEL), row),
                  pl.BlockSpec((tm, W_ATT), row),
                  pl.BlockSpec((tm, W_SSM), row),
                  pl.BlockSpec((W_SSM, W_SSM), const),
                  pl.BlockSpec((1, W_SSM), const),
                  pl.BlockSpec((1, W_SSM), const),
                  pl.BlockSpec((D_MODEL, D_MODEL), const),
                  pl.BlockSpec((1, D_MODEL), const),
                  pl.BlockSpec((D_MODEL, LANES), const),
                  pl.BlockSpec((D_MODEL, LANES), const),
                  pl.BlockSpec((1, LANES), const)],
        out_specs=[pl.BlockSpec((tm, D_MODEL), row),
                   pl.BlockSpec((tm, D_MODEL), row),
                   pl.BlockSpec((tm, LANES), row),
                   pl.BlockSpec((tm, LANES), row)],
        out_shape=[jax.ShapeDtypeStruct((t, D_MODEL), F32),
                   jax.ShapeDtypeStruct((t, D_MODEL), F32),
                   jax.ShapeDtypeStruct((t, LANES), jnp.int32),
                   jax.ShapeDtypeStruct((t, LANES), F32)],
        compiler_params=_cparams(("parallel",)),
        name="post_mix",
    )(x2d, att, yssm, wglu, bglu, gs, wout, g2, wr_hi, wr_lo, br)


def _row_copy(src_hbm, row, dst, dst_row, sem):
    return pltpu.make_async_copy(src_hbm.at[pl.ds(row, 1), :], dst.at[pl.ds(dst_row, 1), :], sem)


def _sc_gather_rows(x, idx):
    t, d = x.shape
    n = idx.shape[0]
    parts = d // SC_COLS
    pieces = x.reshape(t * parts, SC_COLS)
    piece_idx = (idx[:, None] * parts + jnp.arange(parts, dtype=jnp.int32)[None, :]).reshape(1, n * parts)
    mesh = plsc.VectorSubcoreMesh(core_axis_name="core", subcore_axis_name="subcore")

    @pl.kernel(out_type=jax.ShapeDtypeStruct((n * parts, SC_COLS), x.dtype), mesh=mesh)
    def gather(x_hbm, i_hbm, o_hbm):
        def body(i_vmem, o_vmem):
            pltpu.sync_copy(x_hbm.at[i_vmem.at[0]], o_vmem)

        pltpu.emit_pipeline(
            body,
            grid=(n * parts // SC_WINDOW,),
            in_specs=[pl.BlockSpec((1, SC_WINDOW), index_map=lambda i: (0, i))],
            out_specs=[pl.BlockSpec((SC_WINDOW, SC_COLS), index_map=lambda i: (i, 0))],
            core_axis_name=("core", "subcore"),
            dimension_semantics=(pltpu.PARALLEL,),
        )(i_hbm, o_hbm)

    return gather(pieces, piece_idx).reshape(n, d)


def _expert_kernel(be_ref, x_ref, w1_ref, b1_ref, w2_ref, b2_ref, y_ref):
    x = x_ref[...].astype(BF16)
    hdn = jnp.dot(x, w1_ref[0], preferred_element_type=F32) + b1_ref[0]
    gate = jnp.minimum(hdn[:, :D_FF], SWIGLU_LIMIT)
    lin = jnp.clip(hdn[:, D_FF:], -SWIGLU_LIMIT, SWIGLU_LIMIT)
    act = gate * (1.0 / (1.0 + jnp.exp(-SWIGLU_ALPHA * gate))) * (lin + 1.0)
    y_ref[...] = jnp.dot(act.astype(BF16), w2_ref[0], preferred_element_type=F32) + b2_ref[0]


def _experts(block_e, x_pad, w1, b1, w2, b2):
    n_pad = x_pad.shape[0]
    rows = EXPERT_ROWS
    grid_spec = pltpu.PrefetchScalarGridSpec(
        num_scalar_prefetch=1,
        grid=(n_pad // rows,),
        in_specs=[pl.BlockSpec((rows, D_MODEL), lambda i, be: (i, 0)),
                  pl.BlockSpec((1, D_MODEL, 2 * D_FF), lambda i, be: (be[i], 0, 0)),
                  pl.BlockSpec((1, 1, 2 * D_FF), lambda i, be: (be[i], 0, 0)),
                  pl.BlockSpec((1, D_FF, D_MODEL), lambda i, be: (be[i], 0, 0)),
                  pl.BlockSpec((1, 1, D_MODEL), lambda i, be: (be[i], 0, 0))],
        out_specs=pl.BlockSpec((rows, D_MODEL), lambda i, be: (i, 0)),
    )
    return pl.pallas_call(
        _expert_kernel,
        grid_spec=grid_spec,
        out_shape=jax.ShapeDtypeStruct((n_pad, D_MODEL), F32),
        compiler_params=_cparams(("arbitrary",)),
        name="moe_experts",
    )(block_e, x_pad, w1, b1, w2, b2)


def _combine_kernel(pos_ref, nxt_ref, x2_ref, gates_ref, gf_ref, y_hbm, o_ref, buf, sem, *, rows):
    i = pl.program_id(0)
    n = pl.num_programs(0)
    slot = i % 2

    def issue(idx_ref, dst_slot):
        def body(r, c):
            for u in range(2):
                for kk in range(TOP_K):
                    row = r * 2 + u
                    _row_copy(y_hbm, idx_ref[0, 0, row * TOP_K + kk], buf.at[dst_slot, kk], row,
                              sem.at[dst_slot]).start()
            return c

        lax.fori_loop(0, rows // 2, body, 0)

    @pl.when(i == 0)
    def _():
        issue(pos_ref, 0)

    @pl.when(i + 1 < n)
    def _():
        issue(nxt_ref, 1 - slot)

    for kk in range(TOP_K):
        pltpu.make_async_copy(y_hbm.at[pl.ds(0, rows), :], buf.at[slot, kk], sem.at[slot]).wait()

    gates = gates_ref[...]
    y = x2_ref[...]
    for kk in range(TOP_K):
        y = y + gates[:, kk:kk + 1] * buf[slot, kk]
    ms = jnp.mean(y * y, axis=-1, keepdims=True)
    o_ref[...] = y * lax.rsqrt(ms + RMS_EPS) * gf_ref[...]


def _combine(pos, x2, gates, gf, y_pad):
    t = x2.shape[0]
    rows = min(COMBINE_ROWS, t)
    steps = t // rows
    kern = functools.partial(_combine_kernel, rows=rows)
    row = lambda i: (i, 0)
    pos3 = pos.reshape(steps, 1, rows * TOP_K)
    return pl.pallas_call(
        kern,
        grid=(steps,),
        in_specs=[pl.BlockSpec((1, 1, rows * TOP_K), lambda i: (i, 0, 0), memory_space=pltpu.SMEM),
                  pl.BlockSpec((1, 1, rows * TOP_K), lambda i: (jnp.minimum(i + 1, steps - 1), 0, 0),
                               memory_space=pltpu.SMEM),
                  pl.BlockSpec((rows, D_MODEL), row),
                  pl.BlockSpec((rows, LANES), row),
                  pl.BlockSpec((1, D_MODEL), lambda i: (0, 0)),
                  pl.BlockSpec(memory_space=pl.ANY)],
        out_specs=pl.BlockSpec((rows, D_MODEL), row),
        out_shape=jax.ShapeDtypeStruct((t, D_MODEL), F32),
        scratch_shapes=[pltpu.VMEM((2, TOP_K, rows, D_MODEL), F32), pltpu.SemaphoreType.DMA((2,))],
        compiler_params=_cparams(("arbitrary",)),
        name="moe_combine",
    )(pos3, pos3, x2, gates, gf, y_pad)


def _dispatch_plan(top_e, n_tok):
    n = n_tok * TOP_K
    blk = EXPERT_ROWS
    flat_e = top_e.reshape(n)
    order = jnp.argsort(flat_e).astype(jnp.int32)
    rank = jnp.argsort(order).astype(jnp.int32)
    st = order // TOP_K
    experts = jnp.arange(N_EXPERTS, dtype=jnp.int32)
    counts = jnp.sum((flat_e[:, None] == experts).astype(jnp.int32), axis=0)
    padded = (counts + blk - 1) // blk * blk
    start = jnp.cumsum(counts) - counts
    pend = jnp.cumsum(padded)
    pstart = pend - padded
    pos = (pstart - start)[flat_e] + rank
    n_blocks = n // blk + N_EXPERTS
    block_start = jnp.arange(n_blocks, dtype=jnp.int32) * blk
    block_e = jnp.minimum(jnp.sum((pend[None, :] <= block_start[:, None]).astype(jnp.int32), axis=1),
                          N_EXPERTS - 1)
    slot = jnp.arange(n_blocks * blk, dtype=jnp.int32)
    slot_e = jnp.repeat(block_e, blk)
    off = slot - pstart[slot_e]
    src = jnp.clip(start[slot_e] + off, 0, n - 1)
    tok_pad = jnp.where(off < counts[slot_e], st[src], 0)
    return tok_pad, pos, block_e


def _trunk(x, prm):
    bsz, l, _ = x.shape
    t = bsz * l
    x2d = x.reshape(t, D_MODEL)
    q, k, v, u = _in_proj(x2d, prm['norm1_g'], prm['w_in'])

    t_att = min(512, l)
    att = _attention(q.reshape(bsz, l, W_ATT), k.reshape(bsz, l, W_ATT), v.reshape(bsz, l, W_ATT),
                     prm['bias_tiles'][t_att], prm['lam'], prm['subln_g'], t_att)

    nc = l // SSM_CHUNK
    u_grp = jnp.transpose(u.reshape(bsz, nc, SSM_CHUNK, N_GROUPS, SSM_GROUP), (3, 1, 0, 2, 4))
    u_grp = u_grp.reshape(N_GROUPS, nc * bsz, SSM_ROW)
    y_grp = _s5(u_grp, prm['ssm_p'], prm['ssm_m'], prm['ssm_r'], prm['ssm_alpha'], nc, bsz)
    yssm = jnp.transpose(y_grp.reshape(N_GROUPS, nc, bsz, SSM_CHUNK, SSM_GROUP), (2, 1, 3, 0, 4))
    yssm = yssm.reshape(t, W_SSM)

    x2, h2, ids, gates = _post_mix(x2d, att.reshape(t, W_ATT), yssm, prm['w_glu'], prm['b_glu'],
                                   prm['ssm_norm_g'], prm['w_out'], prm['norm2_g'],
                                   prm['w_router_hi'], prm['w_router_lo'], prm['b_router'])

    tok_pad, pos, block_e = _dispatch_plan(ids[:, :TOP_K], t)
    x_pad = _sc_gather_rows(h2, tok_pad)
    y_pad = _experts(block_e, x_pad, prm['w_moe1'], prm['b_moe1'], prm['w_moe2'], prm['b_moe2'])
    out = _combine(pos, x2, gates, prm['normf_g'], y_pad)
    return out.reshape(bsz, l, D_MODEL)


def _prepare(seq_lens, rel_bias, norm1_g, w_in, lambda_q1, lambda_k1, lambda_q2, lambda_k2, subln_g,
             ssm_A_re, ssm_A_im, ssm_log_dt, ssm_B_re, ssm_B_im, ssm_C_re, ssm_C_im, ssm_D,
             w_glu, b_glu, ssm_norm_g, w_out, norm2_g, w_router, b_router,
             w_moe1, b_moe1, w_moe2, b_moe2, normf_g):
    layer = 0
    lambda_init = 0.8 - 0.6 * math.exp(-0.3 * layer)
    lam = (jnp.exp(jnp.sum(lambda_q1[layer].astype(F32) * lambda_k1[layer].astype(F32)))
           - jnp.exp(jnp.sum(lambda_q2[layer].astype(F32) * lambda_k2[layer].astype(F32))) + lambda_init)
    p_mat, m_mat, r_mat, alpha = _ssm_matrices(
        ssm_A_re[layer], ssm_A_im[layer], ssm_log_dt[layer], ssm_B_re[layer], ssm_B_im[layer],
        ssm_C_re[layer], ssm_C_im[layer], ssm_D[layer])
    pad_e = LANES - N_EXPERTS
    w_r = jnp.pad(w_router[layer].astype(F32), ((0, 0), (0, pad_e)))
    w_r_hi = w_r.astype(BF16)
    return {
        'norm1_g': norm1_g[layer].reshape(1, D_MODEL).astype(F32),
        'w_in': w_in[layer].astype(BF16),
        'lam': lam.reshape(1).astype(F32),
        'subln_g': (subln_g[layer].astype(F32) * (1.0 - lambda_init)).reshape(1, V_DIM),
        'bias_tiles': {t: _bias_tiles(rel_bias, t) for t in sorted({min(512, l) for l in seq_lens})},
        'ssm_p': p_mat, 'ssm_m': m_mat, 'ssm_r': r_mat, 'ssm_alpha': alpha,
        'w_glu': w_glu[layer].astype(BF16),
        'b_glu': b_glu[layer].reshape(1, W_SSM).astype(F32),
        'ssm_norm_g': ssm_norm_g[layer].reshape(1, W_SSM).astype(F32),
        'w_out': w_out[layer].astype(BF16),
        'norm2_g': norm2_g[layer].reshape(1, D_MODEL).astype(F32),
        'w_router_hi': w_r_hi,
        'w_router_lo': (w_r - w_r_hi.astype(F32)).astype(BF16),
        'b_router': jnp.pad(b_router[layer].astype(F32), (0, pad_e)).reshape(1, LANES),
        'w_moe1': w_moe1[layer].astype(BF16),
        'b_moe1': b_moe1[layer].reshape(N_EXPERTS, 1, 2 * D_FF).astype(F32),
        'w_moe2': w_moe2[layer].astype(BF16),
        'b_moe2': b_moe2[layer].reshape(N_EXPERTS, 1, D_MODEL).astype(F32),
        'normf_g': normf_g.reshape(1, D_MODEL).astype(F32),
    }


def kernel(x_prompt, x_sample, rel_bias, norm1_g, w_in, lambda_q1, lambda_k1, lambda_q2, lambda_k2, subln_g, ssm_A_re, ssm_A_im, ssm_log_dt, ssm_B_re, ssm_B_im, ssm_C_re, ssm_C_im, ssm_D, w_glu, b_glu, ssm_norm_g, w_out, norm2_g, w_router, b_router, w_moe1, b_moe1, w_moe2, b_moe2, normf_g):
    prm = _prepare((x_prompt.shape[1], x_sample.shape[1]), rel_bias, norm1_g, w_in, lambda_q1,
                   lambda_k1, lambda_q2, lambda_k2, subln_g, ssm_A_re, ssm_A_im, ssm_log_dt,
                   ssm_B_re, ssm_B_im, ssm_C_re, ssm_C_im, ssm_D, w_glu, b_glu, ssm_norm_g, w_out,
                   norm2_g, w_router, b_router, w_moe1, b_moe1, w_moe2, b_moe2, normf_g)
    return (_trunk(x_prompt, prm), _trunk(x_sample, prm))
```

```python
import functools
import math

import jax
import jax.numpy as jnp
from jax import lax
from jax.experimental import pallas as pl
from jax.experimental.pallas import tpu as pltpu
from jax.experimental.pallas import tpu_sc as plsc

F32 = jnp.float32
BF16 = jnp.bfloat16

D_MODEL = 1024
W_ATT = 512
W_SSM = 512
HEAD_DIM = 64
N_HEADS = 4
V_DIM = 2 * HEAD_DIM
SSM_GROUP = 16
N_GROUPS = W_SSM // SSM_GROUP
SSM_STATE = 64
IN_WIDTH = 3 * W_ATT + W_SSM
N_BUCKETS = 32
MAX_DISTANCE = 128
N_EXPERTS = 32
TOP_K = 4
D_FF = D_MODEL
SWIGLU_ALPHA = 1.702
SWIGLU_LIMIT = 7.0
RMS_EPS = 1e-6
ATT_SCALE = HEAD_DIM ** -0.5
LOG2E = math.log2(math.e)

LANES = 128
SSM_CHUNK = 32
SSM_ROW = SSM_CHUNK * SSM_GROUP
ROW_TILE = 512
EXPERT_ROWS = 256
SC_ROWS = 32
COMBINE_ROWS = 256
VMEM_LIMIT = 48 * 1024 * 1024


def _cparams(sem):
    return pltpu.CompilerParams(dimension_semantics=sem, vmem_limit_bytes=VMEM_LIMIT)


def _in_proj_kernel(x_ref, g_ref, w_ref, q_ref, k_ref, v_ref, u_ref):
    x = x_ref[...]
    ms = jnp.mean(x * x, axis=-1, keepdims=True)
    h = (x * lax.rsqrt(ms + RMS_EPS) * g_ref[...]).astype(BF16)
    proj = jnp.dot(h, w_ref[...], preferred_element_type=F32)
    q_ref[...] = (proj[:, 0:W_ATT] * (ATT_SCALE * LOG2E)).astype(BF16)
    k_ref[...] = proj[:, W_ATT:2 * W_ATT].astype(BF16)
    v_ref[...] = proj[:, 2 * W_ATT:3 * W_ATT].astype(BF16)
    u_ref[...] = proj[:, 3 * W_ATT:].astype(BF16)


def _in_proj(x2d, g, w_bf16):
    t = x2d.shape[0]
    tm = min(ROW_TILE, t)
    out = jax.ShapeDtypeStruct((t, W_ATT), BF16)
    row = lambda i: (i, 0)
    return pl.pallas_call(
        _in_proj_kernel,
        grid=(t // tm,),
        in_specs=[pl.BlockSpec((tm, D_MODEL), row),
                  pl.BlockSpec((1, D_MODEL), lambda i: (0, 0)),
                  pl.BlockSpec((D_MODEL, IN_WIDTH), lambda i: (0, 0))],
        out_specs=[pl.BlockSpec((tm, W_ATT), row)] * 4,
        out_shape=[out] * 4,
        compiler_params=_cparams(("parallel",)),
        name="in_proj",
    )(x2d, g, w_bf16)


def _t5_bucket(rel):
    half = N_BUCKETS // 2
    max_exact = half // 2
    ret = jnp.where(rel > 0, half, 0).astype(jnp.int32)
    n = jnp.abs(rel)
    nf = jnp.maximum(n, 1).astype(F32)
    large = max_exact + (jnp.log(nf / max_exact) / math.log(MAX_DISTANCE / max_exact)
                         * (half - max_exact)).astype(jnp.int32)
    large = jnp.minimum(large, half - 1)
    return ret + jnp.where(n < max_exact, n, large)


def _bias_tiles(rel_bias, t):
    i = jnp.arange(t, dtype=jnp.int32)
    d = jnp.arange(-2, 3, dtype=jnp.int32)
    rel = d[:, None, None] * t + i[None, None, :] - i[None, :, None]
    onehot = (_t5_bucket(rel)[..., None] == jnp.arange(N_BUCKETS, dtype=jnp.int32)).astype(F32)
    tiles = jnp.einsum('dqkn,nh->hdqk', onehot, rel_bias.astype(F32), precision=lax.Precision.HIGHEST)
    return tiles * LOG2E


def _attn_kernel(lam_ref, q_ref, k_ref, v_ref, bias_ref, g_ref, o_ref,
                 m_scr, l_scr, acc_scr, sa_scr, sb_scr, mxa_scr, mxb_scr, p_scr, *, t, sub, n_iter):
    qi = pl.program_id(2)
    q = q_ref[0]
    lane = lax.broadcasted_iota(jnp.int32, q.shape, 1)
    zero = jnp.zeros_like(q)
    qs = (jnp.where(lane < HEAD_DIM, q, zero), jnp.where(lane >= HEAD_DIM, q, zero))
    nb = t // LANES

    m_scr[...] = jnp.full(m_scr.shape, -jnp.inf, F32)
    l_scr[...] = jnp.zeros(l_scr.shape, F32)
    acc_scr[...] = jnp.zeros(acc_scr.shape, F32)

    def scores(j, s_scr, mx_scr):
        for mi in range(2):
            mx = None
            for c in range(sub):
                blk = j * sub + c
                kc = k_ref[0, pl.ds(pl.multiple_of(blk * t, t), t), :]
                s = lax.dot_general(qs[mi], kc, (((1,), (1,)), ((), ())), preferred_element_type=F32)
                s = s + bias_ref[0, jnp.clip(blk - qi, -2, 2) + 2]
                s_scr[mi, :, c * t:(c + 1) * t] = s
                for i in range(nb):
                    piece = s[:, i * LANES:(i + 1) * LANES]
                    mx = piece if mx is None else jnp.maximum(mx, piece)
            mx_scr[mi] = mx

    def accumulate(j, s_scr, mx_scr):
        vj = v_ref[0, pl.ds(pl.multiple_of(j * (sub * t), sub * t), sub * t), :]
        for mi in range(2):
            m_prev = m_scr[mi]
            m_next = jnp.maximum(m_prev, jnp.max(mx_scr[mi], axis=1, keepdims=True))
            alpha = jnp.exp2(m_prev - m_next)
            m_scr[mi] = m_next
            lsum = None
            for i in range(sub * nb):
                p = jnp.exp2(s_scr[mi, :, i * LANES:(i + 1) * LANES] - m_next)
                lsum = p if lsum is None else lsum + p
                p_scr[mi, :, i * LANES:(i + 1) * LANES] = p.astype(BF16)
            l_scr[mi] = alpha * l_scr[mi] + lsum
            acc_scr[mi] = alpha * acc_scr[mi] + jnp.dot(p_scr[mi], vj, preferred_element_type=F32)

    scores(0, sa_scr, mxa_scr)
    n_pairs = (n_iter - 1) // 2

    def body(i, carry):
        scores(2 * i + 1, sb_scr, mxb_scr)
        accumulate(2 * i, sa_scr, mxa_scr)
        scores(2 * i + 2, sa_scr, mxa_scr)
        accumulate(2 * i + 1, sb_scr, mxb_scr)
        return carry

    lax.fori_loop(0, n_pairs, body, 0)
    if (n_iter - 1) % 2 == 1:
        scores(n_iter - 1, sb_scr, mxb_scr)
        accumulate(n_iter - 2, sa_scr, mxa_scr)
        accumulate(n_iter - 1, sb_scr, mxb_scr)
    else:
        accumulate(n_iter - 1, sa_scr, mxa_scr)

    o1 = acc_scr[0] / jnp.sum(l_scr[0], axis=1, keepdims=True)
    o2 = acc_scr[1] / jnp.sum(l_scr[1], axis=1, keepdims=True)
    o = o1 - lam_ref[0] * o2
    ms = jnp.mean(o * o, axis=-1, keepdims=True)
    o_ref[0] = (o * lax.rsqrt(ms + RMS_EPS) * g_ref[...]).astype(BF16)


def _attention(q, k, v, bias_tiles, lam, g_scaled, t):
    b, l, _ = q.shape
    nq = l // t
    sub = 2 if nq % 2 == 0 else 1
    kern = functools.partial(_attn_kernel, t=t, sub=sub, n_iter=nq // sub)
    return pl.pallas_call(
        kern,
        grid=(b, N_HEADS, nq),
        in_specs=[pl.BlockSpec(memory_space=pltpu.SMEM),
                  pl.BlockSpec((1, t, V_DIM), lambda bi, h, qi: (bi, qi, h)),
                  pl.BlockSpec((1, l, V_DIM), lambda bi, h, qi: (bi, 0, h)),
                  pl.BlockSpec((1, l, V_DIM), lambda bi, h, qi: (bi, 0, h)),
                  pl.BlockSpec((1, 5, t, t), lambda bi, h, qi: (h, 0, 0, 0)),
                  pl.BlockSpec((1, V_DIM), lambda bi, h, qi: (0, 0))],
        out_specs=pl.BlockSpec((1, t, V_DIM), lambda bi, h, qi: (bi, qi, h)),
        out_shape=jax.ShapeDtypeStruct((b, l, W_ATT), BF16),
        scratch_shapes=[pltpu.VMEM((2, t, LANES), F32),
                        pltpu.VMEM((2, t, LANES), F32),
                        pltpu.VMEM((2, t, V_DIM), F32),
                        pltpu.VMEM((2, t, sub * t), F32),
                        pltpu.VMEM((2, t, sub * t), F32),
                        pltpu.VMEM((2, t, LANES), F32),
                        pltpu.VMEM((2, t, LANES), F32),
                        pltpu.VMEM((2, t, sub * t), BF16)],
        compiler_params=_cparams(("parallel", "parallel", "arbitrary")),
        name="diff_attention",
    )(lam, q, k, v, bias_tiles, g_scaled)


def _ssm_matrices(a_re, a_im, log_dt, b_re, b_im, c_re, c_im, d_skip):
    qn, g, p, hc = SSM_CHUNK, N_GROUPS, SSM_STATE, SSM_GROUP
    n = jnp.arange(qn + 1, dtype=F32)
    pw, bbar, cc = [], [], []
    for d in range(2):
        a = lax.complex(a_re[d].astype(F32), a_im[d].astype(F32))
        dt = jnp.exp(log_dt[d].astype(F32))[:, None]
        adt = a * dt
        a_bar = jnp.exp(adt)
        pw.append(jnp.exp(adt[None] * n[:, None, None]))
        bbar.append(((a_bar - 1.0) / a)[:, :, None]
                    * lax.complex(b_re[d].astype(F32), b_im[d].astype(F32)))
        cc.append(lax.complex(c_re[d].astype(F32), c_im[d].astype(F32)))

    hi = lax.Precision.HIGHEST
    kern = [jnp.einsum('gop,tgp,gpi->tgoi', cc[d], pw[d][:qn], bbar[d], precision=hi).real
            for d in range(2)]
    s_idx = jnp.arange(qn)[:, None]
    t_idx = jnp.arange(qn)[None, :]
    lag = t_idx - s_idx
    taus = jnp.arange(qn)
    sel_f = (lag[:, :, None] == taus).astype(F32)
    sel_b = (-lag[:, :, None] == taus).astype(F32)
    kf = jnp.einsum('stu,ugoi->stgoi', sel_f, kern[0], precision=hi)
    kb = jnp.einsum('stu,ugoi->stgoi', sel_b, kern[1], precision=hi)
    skip = (jnp.eye(qn, dtype=F32)[:, :, None, None, None]
            * (jnp.eye(hc, dtype=F32)[None, None, None] * d_skip.astype(F32).reshape(g, hc)[None, None, :, :, None]))
    m_full = kf + kb + skip
    m_mat = jnp.transpose(m_full, (2, 0, 4, 1, 3)).reshape(g, SSM_ROW, SSM_ROW)

    zeros_p = jnp.zeros((g, SSM_ROW, LANES - p), F32)

    def pad_cols(x):
        return jnp.concatenate([x, zeros_p], axis=-1)

    pf = jnp.einsum('sgp,gpi->gsip', pw[0][:qn][::-1], bbar[0]).reshape(g, SSM_ROW, p)
    pb = jnp.einsum('sgp,gpi->gsip', pw[1][:qn], bbar[1]).reshape(g, SSM_ROW, p)
    p_mat = jnp.concatenate([pad_cols(pf.real), pad_cols(pf.imag),
                             pad_cols(pb.real), pad_cols(pb.imag)], axis=-1)

    wf = jnp.einsum('gop,tgp->gpto', cc[0], pw[0][1:qn + 1]).reshape(g, p, SSM_ROW)
    wb = jnp.einsum('gop,tgp->gpto', cc[1], pw[1][1:qn + 1][::-1]).reshape(g, p, SSM_ROW)
    zeros_r = jnp.zeros((g, LANES - p, SSM_ROW), F32)
    r_mat = jnp.concatenate([wf.real, zeros_r, -wf.imag, zeros_r,
                             wb.real, zeros_r, -wb.imag, zeros_r], axis=1)

    zeros_a = jnp.zeros((g, LANES - p), F32)

    def pad_vec(x):
        return jnp.concatenate([x, zeros_a], axis=-1)

    alpha = jnp.stack([pad_vec(pw[0][qn].real), pad_vec(pw[0][qn].imag),
                       pad_vec(pw[1][qn].real), pad_vec(pw[1][qn].imag)], axis=1)
    return p_mat.astype(BF16), m_mat.astype(BF16), r_mat.astype(BF16), alpha


def _s5_kernel(u_ref, p_ref, m_ref, r_ref, a_ref, y_ref, s_scr, x_scr, *, nc, bsz):
    u = u_ref[0]
    s_scr[...] = jnp.dot(u, p_ref[0], preferred_element_type=F32)
    al = a_ref[0]
    afr = jnp.broadcast_to(al[0:1], (bsz, LANES))
    afi = jnp.broadcast_to(al[1:2], (bsz, LANES))
    abr = jnp.broadcast_to(al[2:3], (bsz, LANES))
    abi = jnp.broadcast_to(al[3:4], (bsz, LANES))
    zero = jnp.zeros((bsz, LANES), F32)

    def step(i, carry):
        fr, fi, br, bi = carry
        rf = pl.multiple_of(i * bsz, bsz)
        rb = pl.multiple_of((nc - 1 - i) * bsz, bsz)
        x_scr[pl.ds(rf, bsz), 0:LANES] = fr
        x_scr[pl.ds(rf, bsz), LANES:2 * LANES] = fi
        x_scr[pl.ds(rb, bsz), 2 * LANES:3 * LANES] = br
        x_scr[pl.ds(rb, bsz), 3 * LANES:4 * LANES] = bi
        sfr = s_scr[pl.ds(rf, bsz), 0:LANES]
        sfi = s_scr[pl.ds(rf, bsz), LANES:2 * LANES]
        sbr = s_scr[pl.ds(rb, bsz), 2 * LANES:3 * LANES]
        sbi = s_scr[pl.ds(rb, bsz), 3 * LANES:4 * LANES]
        return (afr * fr - afi * fi + sfr, afr * fi + afi * fr + sfi,
                abr * br - abi * bi + sbr, abr * bi + abi * br + sbi)

    lax.fori_loop(0, nc, step, (zero, zero, zero, zero))
    y = jnp.dot(u, m_ref[0], preferred_element_type=F32)
    y = y + jnp.dot(x_scr[...].astype(BF16), r_ref[0], preferred_element_type=F32)
    y_ref[0] = y


def _s5(u_grp, p_mat, m_mat, r_mat, alpha, nc, bsz):
    g, rows, _ = u_grp.shape
    kern = functools.partial(_s5_kernel, nc=nc, bsz=bsz)
    mat = pl.BlockSpec((1, SSM_ROW, SSM_ROW), lambda i: (i, 0, 0))
    seq = pl.BlockSpec((1, rows, SSM_ROW), lambda i: (i, 0, 0))
    return pl.pallas_call(
        kern,
        grid=(g,),
        in_specs=[seq, mat, mat, mat, pl.BlockSpec((1, 4, LANES), lambda i: (i, 0, 0))],
        out_specs=seq,
        out_shape=jax.ShapeDtypeStruct((g, rows, SSM_ROW), F32),
        scratch_shapes=[pltpu.VMEM((rows, 4 * LANES), F32), pltpu.VMEM((rows, 4 * LANES), F32)],
        compiler_params=_cparams(("parallel",)),
        name="s5_scan",
    )(u_grp, p_mat, m_mat, r_mat, alpha)


def _post_mix_kernel(x_ref, att_ref, y_ref, wglu_ref, bglu_ref, gs_ref, wout_ref, g2_ref,
                     wrh_ref, wrl_ref, br_ref, x2_ref, h2_ref, ids_ref, gates_ref):
    y = y_ref[...]
    y = 0.5 * y * (1.0 + jnp.tanh(math.sqrt(2.0 / math.pi) * (y + 0.044715 * (y * y * y))))
    z = jnp.dot(y.astype(BF16), wglu_ref[...], preferred_element_type=F32) + bglu_ref[...]
    y = y * (1.0 / (1.0 + jnp.exp(-z)))
    ms = jnp.mean(y * y, axis=-1, keepdims=True)
    ssm = (y * lax.rsqrt(ms + RMS_EPS) * gs_ref[...]).astype(BF16)
    mix = jnp.dot(att_ref[...], wout_ref[0:W_ATT, :], preferred_element_type=F32)
    mix = mix + jnp.dot(ssm, wout_ref[W_ATT:, :], preferred_element_type=F32)
    x2 = x_ref[...] + mix
    x2_ref[...] = x2
    ms2 = jnp.mean(x2 * x2, axis=-1, keepdims=True)
    h2 = x2 * lax.rsqrt(ms2 + RMS_EPS) * g2_ref[...]
    h2_ref[...] = h2
    h_hi = h2.astype(BF16)
    h_lo = (h2 - h_hi.astype(F32)).astype(BF16)
    logits = (jnp.dot(h_hi, wrh_ref[...], preferred_element_type=F32)
              + jnp.dot(h_lo, wrh_ref[...], preferred_element_type=F32)
              + jnp.dot(h_hi, wrl_ref[...], preferred_element_type=F32)) + br_ref[...]
    lane = lax.broadcasted_iota(jnp.int32, logits.shape, 1).astype(F32)
    neg = jnp.float32(-jnp.inf)
    cur = jnp.where(lane < N_EXPERTS, logits, neg)
    ids = jnp.zeros(logits.shape, F32)
    vals = jnp.zeros(logits.shape, F32)
    top = None
    den = None
    for kk in range(TOP_K):
        mx = jnp.max(cur, axis=1, keepdims=True)
        idx = jnp.min(jnp.where(cur == mx, lane, float(LANES)), axis=1, keepdims=True)
        if kk == 0:
            top = mx
        e = jnp.exp(mx - top)
        den = e if den is None else den + e
        ids = jnp.where(lane == kk, idx, ids)
        vals = jnp.where(lane == kk, e, vals)
        cur = jnp.where(lane == idx, neg, cur)
    ids_ref[...] = ids.astype(jnp.int32)
    gates_ref[...] = vals / den


def _post_mix(x2d, att, yssm, wglu, bglu, gs, wout, g2, wr_hi, wr_lo, br):
    t = x2d.shape[0]
    tm = min(ROW_TILE, t)
    row = lambda i: (i, 0)
    const = lambda i: (0, 0)
    return pl.pallas_call(
        _post_mix_kernel,
        grid=(t // tm,),
        in_specs=[pl.BlockSpec((tm, D_MODEL), row),
                  pl.BlockSpec((tm, W_ATT), row),
                  pl.BlockSpec((tm, W_SSM), row),
                  pl.BlockSpec((W_SSM, W_SSM), const),
                  pl.BlockSpec((1, W_SSM), const),
                  pl.BlockSpec((1, W_SSM), const),
                  pl.BlockSpec((D_MODEL, D_MODEL), const),
                  pl.BlockSpec((1, D_MODEL), const),
                  pl.BlockSpec((D_MODEL, LANES), const),
                  pl.BlockSpec((D_MODEL, LANES), const),
                  pl.BlockSpec((1, LANES), const)],
        out_specs=[pl.BlockSpec((tm, D_MODEL), row),
                   pl.BlockSpec((tm, D_MODEL), row),
                   pl.BlockSpec((tm, LANES), row),
                   pl.BlockSpec((tm, LANES), row)],
        out_shape=[jax.ShapeDtypeStruct((t, D_MODEL), F32),
                   jax.ShapeDtypeStruct((t, D_MODEL), F32),
                   jax.ShapeDtypeStruct((t, LANES), jnp.int32),
                   jax.ShapeDtypeStruct((t, LANES), F32)],
        compiler_params=_cparams(("parallel",)),
        name="post_mix",
    )(x2d, att, yssm, wglu, bglu, gs, wout, g2, wr_hi, wr_lo, br)


def _sc_gather_rows(x, idx):
    n = idx.shape[0]
    d = x.shape[1]
    steps = n // SC_ROWS
    idx_rows = jnp.pad(idx.reshape(steps, SC_ROWS), ((0, 0), (0, LANES - SC_ROWS)))
    mesh = plsc.VectorSubcoreMesh(core_axis_name="core", subcore_axis_name="subcore")

    @pl.kernel(out_type=jax.ShapeDtypeStruct((n, d), x.dtype), mesh=mesh)
    def gather(x_hbm, i_hbm, o_hbm):
        def body(i_vmem, o_vmem):
            pltpu.sync_copy(x_hbm.at[i_vmem.at[0, pl.ds(0, SC_ROWS)]], o_vmem)

        pltpu.emit_pipeline(
            body,
            grid=(steps,),
            in_specs=[pl.BlockSpec((1, LANES), index_map=lambda i: (i, 0))],
            out_specs=[pl.BlockSpec((SC_ROWS, d), index_map=lambda i: (i, 0))],
            core_axis_name=("core", "subcore"),
            dimension_semantics=(pltpu.PARALLEL,),
        )(i_hbm, o_hbm)

    return gather(x, idx_rows)


def _expert_kernel(be_ref, x_ref, w1_ref, b1_ref, w2_ref, b2_ref, y_ref):
    x = x_ref[...].astype(BF16)
    hdn = jnp.dot(x, w1_ref[0], preferred_element_type=F32) + b1_ref[0]
    gate = jnp.minimum(hdn[:, :D_FF], SWIGLU_LIMIT)
    lin = jnp.clip(hdn[:, D_FF:], -SWIGLU_LIMIT, SWIGLU_LIMIT)
    act = gate * (1.0 / (1.0 + jnp.exp(-SWIGLU_ALPHA * gate))) * (lin + 1.0)
    y_ref[...] = jnp.dot(act.astype(BF16), w2_ref[0], preferred_element_type=F32) + b2_ref[0]


def _experts(block_e, x_pad, w1, b1, w2, b2):
    n_pad = x_pad.shape[0]
    rows = EXPERT_ROWS
    grid_spec = pltpu.PrefetchScalarGridSpec(
        num_scalar_prefetch=1,
        grid=(n_pad // rows,),
        in_specs=[pl.BlockSpec((rows, D_MODEL), lambda i, be: (i, 0)),
                  pl.BlockSpec((1, D_MODEL, 2 * D_FF), lambda i, be: (be[i], 0, 0)),
                  pl.BlockSpec((1, 1, 2 * D_FF), lambda i, be: (be[i], 0, 0)),
                  pl.BlockSpec((1, D_FF, D_MODEL), lambda i, be: (be[i], 0, 0)),
                  pl.BlockSpec((1, 1, D_MODEL), lambda i, be: (be[i], 0, 0))],
        out_specs=pl.BlockSpec((rows, D_MODEL), lambda i, be: (i, 0)),
    )
    return pl.pallas_call(
        _expert_kernel,
        grid_spec=grid_spec,
        out_shape=jax.ShapeDtypeStruct((n_pad, D_MODEL), F32),
        compiler_params=_cparams(("arbitrary",)),
        name="moe_experts",
    )(block_e, x_pad, w1, b1, w2, b2)


def _combine_kernel(x2_ref, gates_ref, gf_ref, y_ref, o_ref):
    gates = gates_ref[...]
    y = x2_ref[...]
    for kk in range(TOP_K):
        y = y + gates[:, kk:kk + 1] * y_ref[kk]
    ms = jnp.mean(y * y, axis=-1, keepdims=True)
    o_ref[...] = y * lax.rsqrt(ms + RMS_EPS) * gf_ref[...]


def _combine(x2, gates, gf, y_sel):
    t = x2.shape[0]
    rows = min(COMBINE_ROWS, t)
    row = lambda i: (i, 0)
    return pl.pallas_call(
        _combine_kernel,
        grid=(t // rows,),
        in_specs=[pl.BlockSpec((rows, D_MODEL), row),
                  pl.BlockSpec((rows, LANES), row),
                  pl.BlockSpec((1, D_MODEL), lambda i: (0, 0)),
                  pl.BlockSpec((TOP_K, rows, D_MODEL), lambda i: (0, i, 0))],
        out_specs=pl.BlockSpec((rows, D_MODEL), row),
        out_shape=jax.ShapeDtypeStruct((t, D_MODEL), F32),
        compiler_params=_cparams(("parallel",)),
        name="moe_combine",
    )(x2, gates, gf, y_sel)


def _dispatch_plan(top_e, n_tok):
    n = n_tok * TOP_K
    blk = EXPERT_ROWS
    flat_e = top_e.reshape(n)
    order = jnp.argsort(flat_e).astype(jnp.int32)
    rank = jnp.argsort(order).astype(jnp.int32)
    st = order // TOP_K
    experts = jnp.arange(N_EXPERTS, dtype=jnp.int32)
    counts = jnp.sum((flat_e[:, None] == experts).astype(jnp.int32), axis=0)
    padded = (counts + blk - 1) // blk * blk
    start = jnp.cumsum(counts) - counts
    pend = jnp.cumsum(padded)
    pstart = pend - padded
    pos = (pstart - start)[flat_e] + rank
    n_blocks = n // blk + N_EXPERTS
    block_start = jnp.arange(n_blocks, dtype=jnp.int32) * blk
    block_e = jnp.minimum(jnp.sum((pend[None, :] <= block_start[:, None]).astype(jnp.int32), axis=1),
                          N_EXPERTS - 1)
    slot = jnp.arange(n_blocks * blk, dtype=jnp.int32)
    slot_e = jnp.repeat(block_e, blk)
    off = slot - pstart[slot_e]
    src = jnp.clip(start[slot_e] + off, 0, n - 1)
    tok_pad = jnp.where(off < counts[slot_e], st[src], 0)
    return tok_pad, pos, block_e


def _trunk_front(x, prm):
    bsz, l, _ = x.shape
    t = bsz * l
    x2d = x.reshape(t, D_MODEL)
    q, k, v, u = _in_proj(x2d, prm['norm1_g'], prm['w_in'])

    t_att = min(512, l)
    att = _attention(q.reshape(bsz, l, W_ATT), k.reshape(bsz, l, W_ATT), v.reshape(bsz, l, W_ATT),
                     prm['bias_tiles'][t_att], prm['lam'], prm['subln_g'], t_att)

    nc = l // SSM_CHUNK
    u_grp = jnp.transpose(u.reshape(bsz, nc, SSM_CHUNK, N_GROUPS, SSM_GROUP), (3, 1, 0, 2, 4))
    u_grp = u_grp.reshape(N_GROUPS, nc * bsz, SSM_ROW)
    y_grp = _s5(u_grp, prm['ssm_p'], prm['ssm_m'], prm['ssm_r'], prm['ssm_alpha'], nc, bsz)
    yssm = jnp.transpose(y_grp.reshape(N_GROUPS, nc, bsz, SSM_CHUNK, SSM_GROUP), (2, 1, 3, 0, 4))
    yssm = yssm.reshape(t, W_SSM)

    x2, h2, ids, gates = _post_mix(x2d, att.reshape(t, W_ATT), yssm, prm['w_glu'], prm['b_glu'],
                                   prm['ssm_norm_g'], prm['w_out'], prm['norm2_g'],
                                   prm['w_router_hi'], prm['w_router_lo'], prm['b_router'])

    tok_pad, pos, block_e = _dispatch_plan(ids[:, :TOP_K], t)
    return {'shape': x.shape, 'x2': x2, 'h2': h2, 'gates': gates,
            'tok_pad': tok_pad, 'pos': pos, 'block_e': block_e}


def _trunk_back(st, prm):
    t = st['x2'].shape[0]
    x_pad = _sc_gather_rows(st['h2'], st['tok_pad'])
    y_pad = _experts(st['block_e'], x_pad, prm['w_moe1'], prm['b_moe1'], prm['w_moe2'], prm['b_moe2'])
    pos_by_k = st['pos'].reshape(t, TOP_K).T.reshape(TOP_K * t)
    y_sel = _sc_gather_rows(y_pad, pos_by_k).reshape(TOP_K, t, D_MODEL)
    out = _combine(st['x2'], st['gates'], prm['normf_g'], y_sel)
    return out.reshape(st['shape'])


def _prepare(seq_lens, rel_bias, norm1_g, w_in, lambda_q1, lambda_k1, lambda_q2, lambda_k2, subln_g,
             ssm_A_re, ssm_A_im, ssm_log_dt, ssm_B_re, ssm_B_im, ssm_C_re, ssm_C_im, ssm_D,
             w_glu, b_glu, ssm_norm_g, w_out, norm2_g, w_router, b_router,
             w_moe1, b_moe1, w_moe2, b_moe2, normf_g):
    layer = 0
    lambda_init = 0.8 - 0.6 * math.exp(-0.3 * layer)
    lam = (jnp.exp(jnp.sum(lambda_q1[layer].astype(F32) * lambda_k1[layer].astype(F32)))
           - jnp.exp(jnp.sum(lambda_q2[layer].astype(F32) * lambda_k2[layer].astype(F32))) + lambda_init)
    p_mat, m_mat, r_mat, alpha = _ssm_matrices(
        ssm_A_re[layer], ssm_A_im[layer], ssm_log_dt[layer], ssm_B_re[layer], ssm_B_im[layer],
        ssm_C_re[layer], ssm_C_im[layer], ssm_D[layer])
    pad_e = LANES - N_EXPERTS
    w_r = jnp.pad(w_router[layer].astype(F32), ((0, 0), (0, pad_e)))
    w_r_hi = w_r.astype(BF16)
    return {
        'norm1_g': norm1_g[layer].reshape(1, D_MODEL).astype(F32),
        'w_in': w_in[layer].astype(BF16),
        'lam': lam.reshape(1).astype(F32),
        'subln_g': (subln_g[layer].astype(F32) * (1.0 - lambda_init)).reshape(1, V_DIM),
        'bias_tiles': {t: _bias_tiles(rel_bias, t) for t in sorted({min(512, l) for l in seq_lens})},
        'ssm_p': p_mat, 'ssm_m': m_mat, 'ssm_r': r_mat, 'ssm_alpha': alpha,
        'w_glu': w_glu[layer].astype(BF16),
        'b_glu': b_glu[layer].reshape(1, W_SSM).astype(F32),
        'ssm_norm_g': ssm_norm_g[layer].reshape(1, W_SSM).astype(F32),
        'w_out': w_out[layer].astype(BF16),
        'norm2_g': norm2_g[layer].reshape(1, D_MODEL).astype(F32),
        'w_router_hi': w_r_hi,
        'w_router_lo': (w_r - w_r_hi.astype(F32)).astype(BF16),
        'b_router': jnp.pad(b_router[layer].astype(F32), (0, pad_e)).reshape(1, LANES),
        'w_moe1': w_moe1[layer].astype(BF16),
        'b_moe1': b_moe1[layer].reshape(N_EXPERTS, 1, 2 * D_FF).astype(F32),
        'w_moe2': w_moe2[layer].astype(BF16),
        'b_moe2': b_moe2[layer].reshape(N_EXPERTS, 1, D_MODEL).astype(F32),
        'normf_g': normf_g.reshape(1, D_MODEL).astype(F32),
    }


def kernel(x_prompt, x_sample, rel_bias, norm1_g, w_in, lambda_q1, lambda_k1, lambda_q2, lambda_k2, subln_g, ssm_A_re, ssm_A_im, ssm_log_dt, ssm_B_re, ssm_B_im, ssm_C_re, ssm_C_im, ssm_D, w_glu, b_glu, ssm_norm_g, w_out, norm2_g, w_router, b_router, w_moe1, b_moe1, w_moe2, b_moe2, normf_g):
    prm = _prepare((x_prompt.shape[1], x_sample.shape[1]), rel_bias, norm1_g, w_in, lambda_q1,
                   lambda_k1, lambda_q2, lambda_k2, subln_g, ssm_A_re, ssm_A_im, ssm_log_dt,
                   ssm_B_re, ssm_B_im, ssm_C_re, ssm_C_im, ssm_D, w_glu, b_glu, ssm_norm_g, w_out,
                   norm2_g, w_router, b_router, w_moe1, b_moe1, w_moe2, b_moe2, normf_g)
    first = _trunk_front(x_prompt, prm)
    x_sample, first['tok_pad'] = lax.optimization_barrier((x_sample, first['tok_pad']))
    second = _trunk_front(x_sample, prm)
    return (_trunk_back(first, prm), _trunk_back(second, prm))
```

```python
import functools
import math

import jax
import jax.numpy as jnp
from jax import lax
from jax.experimental import pallas as pl
from jax.experimental.pallas import tpu as pltpu
from jax.experimental.pallas import tpu_sc as plsc

F32 = jnp.float32
BF16 = jnp.bfloat16

D_MODEL = 1024
W_ATT = 512
W_SSM = 512
HEAD_DIM = 64
N_HEADS = 4
V_DIM = 2 * HEAD_DIM
SSM_GROUP = 16
N_GROUPS = W_SSM // SSM_GROUP
SSM_STATE = 64
IN_WIDTH = 3 * W_ATT + W_SSM
N_BUCKETS = 32
MAX_DISTANCE = 128
N_EXPERTS = 32
TOP_K = 4
D_FF = D_MODEL
SWIGLU_ALPHA = 1.702
SWIGLU_LIMIT = 7.0
RMS_EPS = 1e-6
ATT_SCALE = HEAD_DIM ** -0.5
LOG2E = math.log2(math.e)

LANES = 128
SSM_CHUNK = 32
SSM_ROW = SSM_CHUNK * SSM_GROUP
ROW_TILE = 512
EXPERT_ROWS = 256
SC_ROWS = 32
SC_CORES = 2
SC_SUBCORES = 16
COMBINE_ROWS = 256
VMEM_LIMIT = 48 * 1024 * 1024


def _cparams(sem):
    return pltpu.CompilerParams(dimension_semantics=sem, vmem_limit_bytes=VMEM_LIMIT)


def _in_proj_kernel(x_ref, g_ref, w_ref, q_ref, k_ref, v_ref, u_ref):
    x = x_ref[...]
    ms = jnp.mean(x * x, axis=-1, keepdims=True)
    h = (x * lax.rsqrt(ms + RMS_EPS) * g_ref[...]).astype(BF16)
    proj = jnp.dot(h, w_ref[...], preferred_element_type=F32)
    q_ref[...] = (proj[:, 0:W_ATT] * (ATT_SCALE * LOG2E)).astype(BF16)
    k_ref[...] = proj[:, W_ATT:2 * W_ATT].astype(BF16)
    v_ref[...] = proj[:, 2 * W_ATT:3 * W_ATT].astype(BF16)
    u_ref[...] = proj[:, 3 * W_ATT:].astype(BF16)


def _in_proj(x2d, g, w_bf16):
    t = x2d.shape[0]
    tm = min(ROW_TILE, t)
    out = jax.ShapeDtypeStruct((t, W_ATT), BF16)
    row = lambda i: (i, 0)
    return pl.pallas_call(
        _in_proj_kernel,
        grid=(t // tm,),
        in_specs=[pl.BlockSpec((tm, D_MODEL), row),
                  pl.BlockSpec((1, D_MODEL), lambda i: (0, 0)),
                  pl.BlockSpec((D_MODEL, IN_WIDTH), lambda i: (0, 0))],
        out_specs=[pl.BlockSpec((tm, W_ATT), row)] * 4,
        out_shape=[out] * 4,
        compiler_params=_cparams(("parallel",)),
        name="in_proj",
    )(x2d, g, w_bf16)


def _t5_bucket(rel):
    half = N_BUCKETS // 2
    max_exact = half // 2
    ret = jnp.where(rel > 0, half, 0).astype(jnp.int32)
    n = jnp.abs(rel)
    nf = jnp.maximum(n, 1).astype(F32)
    large = max_exact + (jnp.log(nf / max_exact) / math.log(MAX_DISTANCE / max_exact)
                         * (half - max_exact)).astype(jnp.int32)
    large = jnp.minimum(large, half - 1)
    return ret + jnp.where(n < max_exact, n, large)


def _bias_tiles(rel_bias, t):
    i = jnp.arange(t, dtype=jnp.int32)
    d = jnp.arange(-2, 3, dtype=jnp.int32)
    rel = d[:, None, None] * t + i[None, None, :] - i[None, :, None]
    onehot = (_t5_bucket(rel)[..., None] == jnp.arange(N_BUCKETS, dtype=jnp.int32)).astype(F32)
    tiles = jnp.einsum('dqkn,nh->hdqk', onehot, rel_bias.astype(F32), precision=lax.Precision.HIGHEST)
    return tiles * LOG2E


def _attn_kernel(lam_ref, q_ref, k_ref, v_ref, bias_ref, g_ref, o_ref,
                 m_scr, l_scr, acc_scr, sa_scr, sb_scr, mxa_scr, mxb_scr, p_scr, *, t, sub, n_iter):
    qi = pl.program_id(2)
    q = q_ref[0]
    lane = lax.broadcasted_iota(jnp.int32, q.shape, 1)
    zero = jnp.zeros_like(q)
    qs = (jnp.where(lane < HEAD_DIM, q, zero), jnp.where(lane >= HEAD_DIM, q, zero))
    nb = t // LANES

    m_scr[...] = jnp.full(m_scr.shape, -jnp.inf, F32)
    l_scr[...] = jnp.zeros(l_scr.shape, F32)
    acc_scr[...] = jnp.zeros(acc_scr.shape, F32)

    def scores(j, s_scr, mx_scr):
        for mi in range(2):
            mx = None
            for c in range(sub):
                blk = j * sub + c
                kc = k_ref[0, pl.ds(pl.multiple_of(blk * t, t), t), :]
                s = lax.dot_general(qs[mi], kc, (((1,), (1,)), ((), ())), preferred_element_type=F32)
                s = s + bias_ref[0, jnp.clip(blk - qi, -2, 2) + 2]
                s_scr[mi, :, c * t:(c + 1) * t] = s
                for i in range(nb):
                    piece = s[:, i * LANES:(i + 1) * LANES]
                    mx = piece if mx is None else jnp.maximum(mx, piece)
            mx_scr[mi] = mx

    def accumulate(j, s_scr, mx_scr):
        vj = v_ref[0, pl.ds(pl.multiple_of(j * (sub * t), sub * t), sub * t), :]
        for mi in range(2):
            m_prev = m_scr[mi]
            m_next = jnp.maximum(m_prev, jnp.max(mx_scr[mi], axis=1, keepdims=True))
            alpha = jnp.exp2(m_prev - m_next)
            m_scr[mi] = m_next
            lsum = None
            for i in range(sub * nb):
                p = jnp.exp2(s_scr[mi, :, i * LANES:(i + 1) * LANES] - m_next)
                lsum = p if lsum is None else lsum + p
                p_scr[mi, :, i * LANES:(i + 1) * LANES] = p.astype(BF16)
            l_scr[mi] = alpha * l_scr[mi] + lsum
            acc_scr[mi] = alpha * acc_scr[mi] + jnp.dot(p_scr[mi], vj, preferred_element_type=F32)

    scores(0, sa_scr, mxa_scr)
    n_pairs = (n_iter - 1) // 2

    def body(i, carry):
        scores(2 * i + 1, sb_scr, mxb_scr)
        accumulate(2 * i, sa_scr, mxa_scr)
        scores(2 * i + 2, sa_scr, mxa_scr)
        accumulate(2 * i + 1, sb_scr, mxb_scr)
        return carry

    lax.fori_loop(0, n_pairs, body, 0)
    if (n_iter - 1) % 2 == 1:
        scores(n_iter - 1, sb_scr, mxb_scr)
        accumulate(n_iter - 2, sa_scr, mxa_scr)
        accumulate(n_iter - 1, sb_scr, mxb_scr)
    else:
        accumulate(n_iter - 1, sa_scr, mxa_scr)

    o1 = acc_scr[0] / jnp.sum(l_scr[0], axis=1, keepdims=True)
    o2 = acc_scr[1] / jnp.sum(l_scr[1], axis=1, keepdims=True)
    o = o1 - lam_ref[0] * o2
    ms = jnp.mean(o * o, axis=-1, keepdims=True)
    o_ref[0] = (o * lax.rsqrt(ms + RMS_EPS) * g_ref[...]).astype(BF16)


def _attention(q, k, v, bias_tiles, lam, g_scaled, t):
    b, l, _ = q.shape
    nq = l // t
    sub = 2 if nq % 2 == 0 else 1
    kern = functools.partial(_attn_kernel, t=t, sub=sub, n_iter=nq // sub)
    return pl.pallas_call(
        kern,
        grid=(b, N_HEADS, nq),
        in_specs=[pl.BlockSpec(memory_space=pltpu.SMEM),
                  pl.BlockSpec((1, t, V_DIM), lambda bi, h, qi: (bi, qi, h)),
                  pl.BlockSpec((1, l, V_DIM), lambda bi, h, qi: (bi, 0, h)),
                  pl.BlockSpec((1, l, V_DIM), lambda bi, h, qi: (bi, 0, h)),
                  pl.BlockSpec((1, 5, t, t), lambda bi, h, qi: (h, 0, 0, 0)),
                  pl.BlockSpec((1, V_DIM), lambda bi, h, qi: (0, 0))],
        out_specs=pl.BlockSpec((1, t, V_DIM), lambda bi, h, qi: (bi, qi, h)),
        out_shape=jax.ShapeDtypeStruct((b, l, W_ATT), BF16),
        scratch_shapes=[pltpu.VMEM((2, t, LANES), F32),
                        pltpu.VMEM((2, t, LANES), F32),
                        pltpu.VMEM((2, t, V_DIM), F32),
                        pltpu.VMEM((2, t, sub * t), F32),
                        pltpu.VMEM((2, t, sub * t), F32),
                        pltpu.VMEM((2, t, LANES), F32),
                        pltpu.VMEM((2, t, LANES), F32),
                        pltpu.VMEM((2, t, sub * t), BF16)],
        compiler_params=_cparams(("parallel", "parallel", "arbitrary")),
        name="diff_attention",
    )(lam, q, k, v, bias_tiles, g_scaled)


def _ssm_matrices(a_re, a_im, log_dt, b_re, b_im, c_re, c_im, d_skip):
    qn, g, p, hc = SSM_CHUNK, N_GROUPS, SSM_STATE, SSM_GROUP
    n = jnp.arange(qn + 1, dtype=F32)
    pw, bbar, cc = [], [], []
    for d in range(2):
        a = lax.complex(a_re[d].astype(F32), a_im[d].astype(F32))
        dt = jnp.exp(log_dt[d].astype(F32))[:, None]
        adt = a * dt
        a_bar = jnp.exp(adt)
        pw.append(jnp.exp(adt[None] * n[:, None, None]))
        bbar.append(((a_bar - 1.0) / a)[:, :, None]
                    * lax.complex(b_re[d].astype(F32), b_im[d].astype(F32)))
        cc.append(lax.complex(c_re[d].astype(F32), c_im[d].astype(F32)))

    hi = lax.Precision.HIGHEST
    kern = [jnp.einsum('gop,tgp,gpi->tgoi', cc[d], pw[d][:qn], bbar[d], precision=hi).real
            for d in range(2)]
    s_idx = jnp.arange(qn)[:, None]
    t_idx = jnp.arange(qn)[None, :]
    lag = t_idx - s_idx
    taus = jnp.arange(qn)
    sel_f = (lag[:, :, None] == taus).astype(F32)
    sel_b = (-lag[:, :, None] == taus).astype(F32)
    kf = jnp.einsum('stu,ugoi->stgoi', sel_f, kern[0], precision=hi)
    kb = jnp.einsum('stu,ugoi->stgoi', sel_b, kern[1], precision=hi)
    skip = (jnp.eye(qn, dtype=F32)[:, :, None, None, None]
            * (jnp.eye(hc, dtype=F32)[None, None, None] * d_skip.astype(F32).reshape(g, hc)[None, None, :, :, None]))
    m_full = kf + kb + skip
    m_mat = jnp.transpose(m_full, (2, 0, 4, 1, 3)).reshape(g, SSM_ROW, SSM_ROW)

    zeros_p = jnp.zeros((g, SSM_ROW, LANES - p), F32)

    def pad_cols(x):
        return jnp.concatenate([x, zeros_p], axis=-1)

    pf = jnp.einsum('sgp,gpi->gsip', pw[0][:qn][::-1], bbar[0]).reshape(g, SSM_ROW, p)
    pb = jnp.einsum('sgp,gpi->gsip', pw[1][:qn], bbar[1]).reshape(g, SSM_ROW, p)
    p_mat = jnp.concatenate([pad_cols(pf.real), pad_cols(pf.imag),
                             pad_cols(pb.real), pad_cols(pb.imag)], axis=-1)

    wf = jnp.einsum('gop,tgp->gpto', cc[0], pw[0][1:qn + 1]).reshape(g, p, SSM_ROW)
    wb = jnp.einsum('gop,tgp->gpto', cc[1], pw[1][1:qn + 1][::-1]).reshape(g, p, SSM_ROW)
    zeros_r = jnp.zeros((g, LANES - p, SSM_ROW), F32)
    r_mat = jnp.concatenate([wf.real, zeros_r, -wf.imag, zeros_r,
                             wb.real, zeros_r, -wb.imag, zeros_r], axis=1)

    zeros_a = jnp.zeros((g, LANES - p), F32)

    def pad_vec(x):
        return jnp.concatenate([x, zeros_a], axis=-1)

    alpha = jnp.stack([pad_vec(pw[0][qn].real), pad_vec(pw[0][qn].imag),
                       pad_vec(pw[1][qn].real), pad_vec(pw[1][qn].imag)], axis=1)
    return p_mat.astype(BF16), m_mat.astype(BF16), r_mat.astype(BF16), alpha


def _s5_kernel(u_ref, p_ref, m_ref, r_ref, a_ref, y_ref, s_scr, x_scr, *, nc, bsz):
    u = u_ref[0]
    s_scr[...] = jnp.dot(u, p_ref[0], preferred_element_type=F32)
    al = a_ref[0]
    afr = jnp.broadcast_to(al[0:1], (bsz, LANES))
    afi = jnp.broadcast_to(al[1:2], (bsz, LANES))
    abr = jnp.broadcast_to(al[2:3], (bsz, LANES))
    abi = jnp.broadcast_to(al[3:4], (bsz, LANES))
    zero = jnp.zeros((bsz, LANES), F32)

    def step(i, carry):
        fr, fi, br, bi = carry
        rf = pl.multiple_of(i * bsz, bsz)
        rb = pl.multiple_of((nc - 1 - i) * bsz, bsz)
        x_scr[pl.ds(rf, bsz), 0:LANES] = fr
        x_scr[pl.ds(rf, bsz), LANES:2 * LANES] = fi
        x_scr[pl.ds(rb, bsz), 2 * LANES:3 * LANES] = br
        x_scr[pl.ds(rb, bsz), 3 * LANES:4 * LANES] = bi
        sfr = s_scr[pl.ds(rf, bsz), 0:LANES]
        sfi = s_scr[pl.ds(rf, bsz), LANES:2 * LANES]
        sbr = s_scr[pl.ds(rb, bsz), 2 * LANES:3 * LANES]
        sbi = s_scr[pl.ds(rb, bsz), 3 * LANES:4 * LANES]
        return (afr * fr - afi * fi + sfr, afr * fi + afi * fr + sfi,
                abr * br - abi * bi + sbr, abr * bi + abi * br + sbi)

    lax.fori_loop(0, nc, step, (zero, zero, zero, zero))
    y = jnp.dot(u, m_ref[0], preferred_element_type=F32)
    y = y + jnp.dot(x_scr[...].astype(BF16), r_ref[0], preferred_element_type=F32)
    y_ref[0] = y


def _s5(u_grp, p_mat, m_mat, r_mat, alpha, nc, bsz):
    g, rows, _ = u_grp.shape
    kern = functools.partial(_s5_kernel, nc=nc, bsz=bsz)
    mat = pl.BlockSpec((1, SSM_ROW, SSM_ROW), lambda i: (i, 0, 0))
    seq = pl.BlockSpec((1, rows, SSM_ROW), lambda i: (i, 0, 0))
    return pl.pallas_call(
        kern,
        grid=(g,),
        in_specs=[seq, mat, mat, mat, pl.BlockSpec((1, 4, LANES), lambda i: (i, 0, 0))],
        out_specs=seq,
        out_shape=jax.ShapeDtypeStruct((g, rows, SSM_ROW), F32),
        scratch_shapes=[pltpu.VMEM((rows, 4 * LANES), F32), pltpu.VMEM((rows, 4 * LANES), F32)],
        compiler_params=_cparams(("parallel",)),
        name="s5_scan",
    )(u_grp, p_mat, m_mat, r_mat, alpha)


def _post_mix_kernel(x_ref, att_ref, y_ref, wglu_ref, bglu_ref, gs_ref, wout_ref, g2_ref,
                     wrh_ref, wrl_ref, br_ref, x2_ref, h2_ref, ids_ref, gates_ref):
    y = y_ref[...]
    y = 0.5 * y * (1.0 + jnp.tanh(math.sqrt(2.0 / math.pi) * (y + 0.044715 * (y * y * y))))
    z = jnp.dot(y.astype(BF16), wglu_ref[...], preferred_element_type=F32) + bglu_ref[...]
    y = y * (1.0 / (1.0 + jnp.exp(-z)))
    ms = jnp.mean(y * y, axis=-1, keepdims=True)
    ssm = (y * lax.rsqrt(ms + RMS_EPS) * gs_ref[...]).astype(BF16)
    mix = jnp.dot(att_ref[...], wout_ref[0:W_ATT, :], preferred_element_type=F32)
    mix = mix + jnp.dot(ssm, wout_ref[W_ATT:, :], preferred_element_type=F32)
    x2 = x_ref[...] + mix
    x2_ref[...] = x2
    ms2 = jnp.mean(x2 * x2, axis=-1, keepdims=True)
    h2 = x2 * lax.rsqrt(ms2 + RMS_EPS) * g2_ref[...]
    h2_ref[...] = h2
    h_hi = h2.astype(BF16)
    h_lo = (h2 - h_hi.astype(F32)).astype(BF16)
    logits = (jnp.dot(h_hi, wrh_ref[...], preferred_element_type=F32)
              + jnp.dot(h_lo, wrh_ref[...], preferred_element_type=F32)
              + jnp.dot(h_hi, wrl_ref[...], preferred_element_type=F32)) + br_ref[...]
    lane = lax.broadcasted_iota(jnp.int32, logits.shape, 1).astype(F32)
    neg = jnp.float32(-jnp.inf)
    cur = jnp.where(lane < N_EXPERTS, logits, neg)
    ids = jnp.zeros(logits.shape, F32)
    vals = jnp.zeros(logits.shape, F32)
    top = None
    den = None
    for kk in range(TOP_K):
        mx = jnp.max(cur, axis=1, keepdims=True)
        idx = jnp.min(jnp.where(cur == mx, lane, float(LANES)), axis=1, keepdims=True)
        if kk == 0:
            top = mx
        e = jnp.exp(mx - top)
        den = e if den is None else den + e
        ids = jnp.where(lane == kk, idx, ids)
        vals = jnp.where(lane == kk, e, vals)
        cur = jnp.where(lane == idx, neg, cur)
    ids_ref[...] = ids.astype(jnp.int32)
    gates_ref[...] = vals / den


def _post_mix(x2d, att, yssm, wglu, bglu, gs, wout, g2, wr_hi, wr_lo, br):
    t = x2d.shape[0]
    tm = min(ROW_TILE, t)
    row = lambda i: (i, 0)
    const = lambda i: (0, 0)
    return pl.pallas_call(
        _post_mix_kernel,
        grid=(t // tm,),
        in_specs=[pl.BlockSpec((tm, D_MODEL), row),
                  pl.BlockSpec((tm, W_ATT), row),
                  pl.BlockSpec((tm, W_SSM), row),
                  pl.BlockSpec((W_SSM, W_SSM), const),
                  pl.BlockSpec((1, W_SSM), const),
                  pl.BlockSpec((1, W_SSM), const),
                  pl.BlockSpec((D_MODEL, D_MODEL), const),
                  pl.BlockSpec((1, D_MODEL), const),
                  pl.BlockSpec((D_MODEL, LANES), const),
                  pl.BlockSpec((D_MODEL, LANES), const),
                  pl.BlockSpec((1, LANES), const)],
        out_specs=[pl.BlockSpec((tm, D_MODEL), row),
                   pl.BlockSpec((tm, D_MODEL), row),
                   pl.BlockSpec((tm, LANES), row),
                   pl.BlockSpec((tm, LANES), row)],
        out_shape=[jax.ShapeDtypeStruct((t, D_MODEL), F32),
                   jax.ShapeDtypeStruct((t, D_MODEL), F32),
                   jax.ShapeDtypeStruct((t, LANES), jnp.int32),
                   jax.ShapeDtypeStruct((t, LANES), F32)],
        compiler_params=_cparams(("parallel",)),
        name="post_mix",
    )(x2d, att, yssm, wglu, bglu, gs, wout, g2, wr_hi, wr_lo, br)


def _sc_gather_rows(x, idx):
    n = idx.shape[0]
    d = x.shape[1]
    steps = n // SC_ROWS
    workers = SC_CORES * SC_SUBCORES
    assert steps % (2 * workers) == 0, (n, steps)
    per_worker = steps // workers
    idx_rows = jnp.pad(idx.reshape(steps, SC_ROWS), ((0, 0), (0, LANES - SC_ROWS)))
    mesh = plsc.VectorSubcoreMesh(core_axis_name="core", subcore_axis_name="subcore")

    @pl.kernel(out_type=jax.ShapeDtypeStruct((n, d), x.dtype), mesh=mesh,
               scratch_types=[pltpu.VMEM((2, 1, LANES), jnp.int32),
                              pltpu.VMEM((2, SC_ROWS, d), x.dtype),
                              pltpu.SemaphoreType.DMA((2,)),
                              pltpu.SemaphoreType.DMA((2,))])
    def gather(x_hbm, i_hbm, o_hbm, idx_v, buf, sem_in, sem_out):
        base = (lax.axis_index("core") * SC_SUBCORES + lax.axis_index("subcore")) * per_worker

        def gather_copy(slot):
            return pltpu.make_async_copy(x_hbm.at[idx_v.at[slot, 0, pl.ds(0, SC_ROWS)]], buf.at[slot],
                                         sem_in.at[slot])

        def out_copy(step, slot):
            return pltpu.make_async_copy(buf.at[slot], o_hbm.at[pl.ds(step * SC_ROWS, SC_ROWS), :],
                                         sem_out.at[slot])

        @pl.loop(0, per_worker // 2)
        def _(j):
            for slot in range(2):
                step = base + 2 * j + slot

                @pl.when(j > 0)
                def _():
                    out_copy(step, slot).wait()

                pltpu.sync_copy(i_hbm.at[pl.ds(step, 1), :], idx_v.at[slot])
                gather_copy(slot).start()
            for slot in range(2):
                gather_copy(slot).wait()
                out_copy(base + 2 * j + slot, slot).start()

        for slot in range(2):
            out_copy(base, slot).wait()

    return gather(x, idx_rows)


def _expert_kernel(be_ref, x_ref, w1_ref, b1_ref, w2_ref, b2_ref, y_ref):
    x = x_ref[...].astype(BF16)
    hdn = jnp.dot(x, w1_ref[0], preferred_element_type=F32) + b1_ref[0]
    gate = jnp.minimum(hdn[:, :D_FF], SWIGLU_LIMIT)
    lin = jnp.clip(hdn[:, D_FF:], -SWIGLU_LIMIT, SWIGLU_LIMIT)
    act = gate * (1.0 / (1.0 + jnp.exp(-SWIGLU_ALPHA * gate))) * (lin + 1.0)
    y_ref[...] = jnp.dot(act.astype(BF16), w2_ref[0], preferred_element_type=F32) + b2_ref[0]


def _experts(block_e, x_pad, w1, b1, w2, b2):
    n_pad = x_pad.shape[0]
    rows = EXPERT_ROWS
    grid_spec = pltpu.PrefetchScalarGridSpec(
        num_scalar_prefetch=1,
        grid=(n_pad // rows,),
        in_specs=[pl.BlockSpec((rows, D_MODEL), lambda i, be: (i, 0)),
                  pl.BlockSpec((1, D_MODEL, 2 * D_FF), lambda i, be: (be[i], 0, 0)),
                  pl.BlockSpec((1, 1, 2 * D_FF), lambda i, be: (be[i], 0, 0)),
                  pl.BlockSpec((1, D_FF, D_MODEL), lambda i, be: (be[i], 0, 0)),
                  pl.BlockSpec((1, 1, D_MODEL), lambda i, be: (be[i], 0, 0))],
        out_specs=pl.BlockSpec((rows, D_MODEL), lambda i, be: (i, 0)),
    )
    return pl.pallas_call(
        _expert_kernel,
        grid_spec=grid_spec,
        out_shape=jax.ShapeDtypeStruct((n_pad, D_MODEL), F32),
        compiler_params=_cparams(("arbitrary",)),
        name="moe_experts",
    )(block_e, x_pad, w1, b1, w2, b2)


def _combine_kernel(x2_ref, gates_ref, gf_ref, y_ref, o_ref):
    gates = gates_ref[...]
    y = x2_ref[...]
    for kk in range(TOP_K):
        y = y + gates[:, kk:kk + 1] * y_ref[kk]
    ms = jnp.mean(y * y, axis=-1, keepdims=True)
    o_ref[...] = y * lax.rsqrt(ms + RMS_EPS) * gf_ref[...]


def _combine(x2, gates, gf, y_sel):
    t = x2.shape[0]
    rows = min(COMBINE_ROWS, t)
    row = lambda i: (i, 0)
    return pl.pallas_call(
        _combine_kernel,
        grid=(t // rows,),
        in_specs=[pl.BlockSpec((rows, D_MODEL), row),
                  pl.BlockSpec((rows, LANES), row),
                  pl.BlockSpec((1, D_MODEL), lambda i: (0, 0)),
                  pl.BlockSpec((TOP_K, rows, D_MODEL), lambda i: (0, i, 0))],
        out_specs=pl.BlockSpec((rows, D_MODEL), row),
        out_shape=jax.ShapeDtypeStruct((t, D_MODEL), F32),
        compiler_params=_cparams(("parallel",)),
        name="moe_combine",
    )(x2, gates, gf, y_sel)


def _dispatch_plan(top_e, n_tok):
    n = n_tok * TOP_K
    blk = EXPERT_ROWS
    flat_e = top_e.reshape(n)
    order = jnp.argsort(flat_e).astype(jnp.int32)
    rank = jnp.argsort(order).astype(jnp.int32)
    st = order // TOP_K
    experts = jnp.arange(N_EXPERTS, dtype=jnp.int32)
    counts = jnp.sum((flat_e[:, None] == experts).astype(jnp.int32), axis=0)
    padded = (counts + blk - 1) // blk * blk
    start = jnp.cumsum(counts) - counts
    pend = jnp.cumsum(padded)
    pstart = pend - padded
    pos = (pstart - start)[flat_e] + rank
    n_blocks = n // blk + N_EXPERTS
    block_start = jnp.arange(n_blocks, dtype=jnp.int32) * blk
    block_e = jnp.minimum(jnp.sum((pend[None, :] <= block_start[:, None]).astype(jnp.int32), axis=1),
                          N_EXPERTS - 1)
    slot = jnp.arange(n_blocks * blk, dtype=jnp.int32)
    slot_e = jnp.repeat(block_e, blk)
    off = slot - pstart[slot_e]
    src = jnp.clip(start[slot_e] + off, 0, n - 1)
    tok_pad = jnp.where(off < counts[slot_e], st[src], 0)
    return tok_pad, pos, block_e


def _trunk_front(x, prm):
    bsz, l, _ = x.shape
    t = bsz * l
    x2d = x.reshape(t, D_MODEL)
    q, k, v, u = _in_proj(x2d, prm['norm1_g'], prm['w_in'])

    t_att = min(512, l)
    att = _attention(q.reshape(bsz, l, W_ATT), k.reshape(bsz, l, W_ATT), v.reshape(bsz, l, W_ATT),
                     prm['bias_tiles'][t_att], prm['lam'], prm['subln_g'], t_att)

    nc = l // SSM_CHUNK
    u_grp = jnp.transpose(u.reshape(bsz, nc, SSM_CHUNK, N_GROUPS, SSM_GROUP), (3, 1, 0, 2, 4))
    u_grp = u_grp.reshape(N_GROUPS, nc * bsz, SSM_ROW)
    y_grp = _s5(u_grp, prm['ssm_p'], prm['ssm_m'], prm['ssm_r'], prm['ssm_alpha'], nc, bsz)
    yssm = jnp.transpose(y_grp.reshape(N_GROUPS, nc, bsz, SSM_CHUNK, SSM_GROUP), (2, 1, 3, 0, 4))
    yssm = yssm.reshape(t, W_SSM)

    x2, h2, ids, gates = _post_mix(x2d, att.reshape(t, W_ATT), yssm, prm['w_glu'], prm['b_glu'],
                                   prm['ssm_norm_g'], prm['w_out'], prm['norm2_g'],
                                   prm['w_router_hi'], prm['w_router_lo'], prm['b_router'])

    tok_pad, pos, block_e = _dispatch_plan(ids[:, :TOP_K], t)
    return {'shape': x.shape, 'x2': x2, 'h2': h2, 'gates': gates,
            'tok_pad': tok_pad, 'pos': pos, 'block_e': block_e}


def _trunk_back(st, prm):
    t = st['x2'].shape[0]
    x_pad = _sc_gather_rows(st['h2'], st['tok_pad'])
    y_pad = _experts(st['block_e'], x_pad, prm['w_moe1'], prm['b_moe1'], prm['w_moe2'], prm['b_moe2'])
    pos_by_k = st['pos'].reshape(t, TOP_K).T.reshape(TOP_K * t)
    y_sel = _sc_gather_rows(y_pad, pos_by_k).reshape(TOP_K, t, D_MODEL)
    out = _combine(st['x2'], st['gates'], prm['normf_g'], y_sel)
    return out.reshape(st['shape'])


def _prepare(seq_lens, rel_bias, norm1_g, w_in, lambda_q1, lambda_k1, lambda_q2, lambda_k2, subln_g,
             ssm_A_re, ssm_A_im, ssm_log_dt, ssm_B_re, ssm_B_im, ssm_C_re, ssm_C_im, ssm_D,
             w_glu, b_glu, ssm_norm_g, w_out, norm2_g, w_router, b_router,
             w_moe1, b_moe1, w_moe2, b_moe2, normf_g):
    layer = 0
    lambda_init = 0.8 - 0.6 * math.exp(-0.3 * layer)
    lam = (jnp.exp(jnp.sum(lambda_q1[layer].astype(F32) * lambda_k1[layer].astype(F32)))
           - jnp.exp(jnp.sum(lambda_q2[layer].astype(F32) * lambda_k2[layer].astype(F32))) + lambda_init)
    p_mat, m_mat, r_mat, alpha = _ssm_matrices(
        ssm_A_re[layer], ssm_A_im[layer], ssm_log_dt[layer], ssm_B_re[layer], ssm_B_im[layer],
        ssm_C_re[layer], ssm_C_im[layer], ssm_D[layer])
    pad_e = LANES - N_EXPERTS
    w_r = jnp.pad(w_router[layer].astype(F32), ((0, 0), (0, pad_e)))
    w_r_hi = w_r.astype(BF16)
    return {
        'norm1_g': norm1_g[layer].reshape(1, D_MODEL).astype(F32),
        'w_in': w_in[layer].astype(BF16),
        'lam': lam.reshape(1).astype(F32),
        'subln_g': (subln_g[layer].astype(F32) * (1.0 - lambda_init)).reshape(1, V_DIM),
        'bias_tiles': {t: _bias_tiles(rel_bias, t) for t in sorted({min(512, l) for l in seq_lens})},
        'ssm_p': p_mat, 'ssm_m': m_mat, 'ssm_r': r_mat, 'ssm_alpha': alpha,
        'w_glu': w_glu[layer].astype(BF16),
        'b_glu': b_glu[layer].reshape(1, W_SSM).astype(F32),
        'ssm_norm_g': ssm_norm_g[layer].reshape(1, W_SSM).astype(F32),
        'w_out': w_out[layer].astype(BF16),
        'norm2_g': norm2_g[layer].reshape(1, D_MODEL).astype(F32),
        'w_router_hi': w_r_hi,
        'w_router_lo': (w_r - w_r_hi.astype(F32)).astype(BF16),
        'b_router': jnp.pad(b_router[layer].astype(F32), (0, pad_e)).reshape(1, LANES),
        'w_moe1': w_moe1[layer].astype(BF16),
        'b_moe1': b_moe1[layer].reshape(N_EXPERTS, 1, 2 * D_FF).astype(F32),
        'w_moe2': w_moe2[layer].astype(BF16),
        'b_moe2': b_moe2[layer].reshape(N_EXPERTS, 1, D_MODEL).astype(F32),
        'normf_g': normf_g.reshape(1, D_MODEL).astype(F32),
    }


def kernel(x_prompt, x_sample, rel_bias, norm1_g, w_in, lambda_q1, lambda_k1, lambda_q2, lambda_k2, subln_g, ssm_A_re, ssm_A_im, ssm_log_dt, ssm_B_re, ssm_B_im, ssm_C_re, ssm_C_im, ssm_D, w_glu, b_glu, ssm_norm_g, w_out, norm2_g, w_router, b_router, w_moe1, b_moe1, w_moe2, b_moe2, normf_g):
    prm = _prepare((x_prompt.shape[1], x_sample.shape[1]), rel_bias, norm1_g, w_in, lambda_q1,
                   lambda_k1, lambda_q2, lambda_k2, subln_g, ssm_A_re, ssm_A_im, ssm_log_dt,
                   ssm_B_re, ssm_B_im, ssm_C_re, ssm_C_im, ssm_D, w_glu, b_glu, ssm_norm_g, w_out,
                   norm2_g, w_router, b_router, w_moe1, b_moe1, w_moe2, b_moe2, normf_g)
    first = _trunk_front(x_prompt, prm)
    x_sample, first['tok_pad'] = lax.optimization_barrier((x_sample, first['tok_pad']))
    second = _trunk_front(x_sample, prm)
    return (_trunk_back(first, prm), _trunk_back(second, prm))
```

```python
import functools
import math

import jax
import jax.numpy as jnp
from jax import lax
from jax.experimental import pallas as pl
from jax.experimental.pallas import tpu as pltpu
from jax.experimental.pallas import tpu_sc as plsc

F32 = jnp.float32
BF16 = jnp.bfloat16

D_MODEL = 1024
W_ATT = 512
W_SSM = 512
HEAD_DIM = 64
N_HEADS = 4
V_DIM = 2 * HEAD_DIM
SSM_GROUP = 16
N_GROUPS = W_SSM // SSM_GROUP
SSM_STATE = 64
IN_WIDTH = 3 * W_ATT + W_SSM
N_BUCKETS = 32
MAX_DISTANCE = 128
N_EXPERTS = 32
TOP_K = 4
D_FF = D_MODEL
SWIGLU_ALPHA = 1.702
SWIGLU_LIMIT = 7.0
RMS_EPS = 1e-6
ATT_SCALE = HEAD_DIM ** -0.5
LOG2E = math.log2(math.e)

LANES = 128
PV_KEYS = 256
SSM_CHUNK = 32
SSM_ROW = SSM_CHUNK * SSM_GROUP
ROW_TILE = 512
EXPERT_ROWS = 256
SC_ROWS = 32
SC_CORES = 2
SC_SUBCORES = 16
COMBINE_ROWS = 256
VMEM_LIMIT = 48 * 1024 * 1024


def _cparams(sem):
    return pltpu.CompilerParams(dimension_semantics=sem, vmem_limit_bytes=VMEM_LIMIT)


def _in_proj_kernel(x_ref, g_ref, w_ref, q_ref, k_ref, v_ref, u_ref):
    x = x_ref[...]
    ms = jnp.mean(x * x, axis=-1, keepdims=True)
    h = (x * lax.rsqrt(ms + RMS_EPS) * g_ref[...]).astype(BF16)
    proj = jnp.dot(h, w_ref[...], preferred_element_type=F32)
    q_ref[...] = (proj[:, 0:W_ATT] * (ATT_SCALE * LOG2E)).astype(BF16)
    k_ref[...] = proj[:, W_ATT:2 * W_ATT].astype(BF16)
    v_ref[...] = proj[:, 2 * W_ATT:3 * W_ATT].astype(BF16)
    u_ref[...] = proj[:, 3 * W_ATT:].astype(BF16)


def _in_proj(x2d, g, w_bf16):
    t = x2d.shape[0]
    tm = min(ROW_TILE, t)
    out = jax.ShapeDtypeStruct((t, W_ATT), BF16)
    row = lambda i: (i, 0)
    return pl.pallas_call(
        _in_proj_kernel,
        grid=(t // tm,),
        in_specs=[pl.BlockSpec((tm, D_MODEL), row),
                  pl.BlockSpec((1, D_MODEL), lambda i: (0, 0)),
                  pl.BlockSpec((D_MODEL, IN_WIDTH), lambda i: (0, 0))],
        out_specs=[pl.BlockSpec((tm, W_ATT), row)] * 4,
        out_shape=[out] * 4,
        compiler_params=_cparams(("parallel",)),
        name="in_proj",
    )(x2d, g, w_bf16)


def _t5_bucket(rel):
    half = N_BUCKETS // 2
    max_exact = half // 2
    ret = jnp.where(rel > 0, half, 0).astype(jnp.int32)
    n = jnp.abs(rel)
    nf = jnp.maximum(n, 1).astype(F32)
    large = max_exact + (jnp.log(nf / max_exact) / math.log(MAX_DISTANCE / max_exact)
                         * (half - max_exact)).astype(jnp.int32)
    large = jnp.minimum(large, half - 1)
    return ret + jnp.where(n < max_exact, n, large)


def _bias_tiles(rel_bias, t):
    i = jnp.arange(t, dtype=jnp.int32)
    d = jnp.arange(-2, 3, dtype=jnp.int32)
    rel = d[:, None, None] * t + i[None, None, :] - i[None, :, None]
    onehot = (_t5_bucket(rel)[..., None] == jnp.arange(N_BUCKETS, dtype=jnp.int32)).astype(F32)
    tiles = jnp.einsum('dqkn,nh->hdqk', onehot, rel_bias.astype(F32), precision=lax.Precision.HIGHEST)
    return tiles * LOG2E


def _attn_kernel(lam_ref, q_ref, k_ref, v_ref, bias_ref, g_ref, o_ref,
                 m_scr, l_scr, acc_scr, sa_scr, sb_scr, mxa_scr, mxb_scr, *, t, sub, n_iter):
    qi = pl.program_id(2)
    q = q_ref[0]
    lane = lax.broadcasted_iota(jnp.int32, q.shape, 1)
    zero = jnp.zeros_like(q)
    qs = (jnp.where(lane < HEAD_DIM, q, zero), jnp.where(lane >= HEAD_DIM, q, zero))
    nb = t // LANES

    m_scr[...] = jnp.full(m_scr.shape, -jnp.inf, F32)
    l_scr[...] = jnp.zeros(l_scr.shape, F32)
    acc_scr[...] = jnp.zeros(acc_scr.shape, F32)

    def scores(j, s_scr, mx_scr):
        for mi in range(2):
            mx = None
            for c in range(sub):
                blk = j * sub + c
                kc = k_ref[0, pl.ds(pl.multiple_of(blk * t, t), t), :]
                s = lax.dot_general(qs[mi], kc, (((1,), (1,)), ((), ())), preferred_element_type=F32)
                s = s + bias_ref[0, jnp.clip(blk - qi, -2, 2) + 2]
                s_scr[mi, :, c * t:(c + 1) * t] = s
                for i in range(nb):
                    piece = s[:, i * LANES:(i + 1) * LANES]
                    mx = piece if mx is None else jnp.maximum(mx, piece)
            mx_scr[mi] = mx

    def accumulate(j, s_scr, mx_scr):
        vj = v_ref[0, pl.ds(pl.multiple_of(j * (sub * t), sub * t), sub * t), :]
        for mi in range(2):
            m_prev = m_scr[mi]
            m_next = jnp.maximum(m_prev, jnp.max(mx_scr[mi], axis=1, keepdims=True))
            alpha = jnp.exp2(m_prev - m_next)
            m_scr[mi] = m_next
            lsum = None
            pv = None
            for i in range(sub * t // PV_KEYS):
                ps = []
                for c in range(PV_KEYS // LANES):
                    lo = i * PV_KEYS + c * LANES
                    p = jnp.exp2(s_scr[mi, :, lo:lo + LANES] - m_next)
                    lsum = p if lsum is None else lsum + p
                    ps.append(p.astype(BF16))
                d = jnp.dot(jnp.concatenate(ps, axis=1), vj[i * PV_KEYS:(i + 1) * PV_KEYS, :],
                            preferred_element_type=F32)
                pv = d if pv is None else pv + d
            l_scr[mi] = alpha * l_scr[mi] + lsum
            acc_scr[mi] = alpha * acc_scr[mi] + pv

    scores(0, sa_scr, mxa_scr)
    n_pairs = (n_iter - 1) // 2

    def body(i, carry):
        scores(2 * i + 1, sb_scr, mxb_scr)
        accumulate(2 * i, sa_scr, mxa_scr)
        scores(2 * i + 2, sa_scr, mxa_scr)
        accumulate(2 * i + 1, sb_scr, mxb_scr)
        return carry

    lax.fori_loop(0, n_pairs, body, 0)
    if (n_iter - 1) % 2 == 1:
        scores(n_iter - 1, sb_scr, mxb_scr)
        accumulate(n_iter - 2, sa_scr, mxa_scr)
        accumulate(n_iter - 1, sb_scr, mxb_scr)
    else:
        accumulate(n_iter - 1, sa_scr, mxa_scr)

    o1 = acc_scr[0] / jnp.sum(l_scr[0], axis=1, keepdims=True)
    o2 = acc_scr[1] / jnp.sum(l_scr[1], axis=1, keepdims=True)
    o = o1 - lam_ref[0] * o2
    ms = jnp.mean(o * o, axis=-1, keepdims=True)
    o_ref[0] = (o * lax.rsqrt(ms + RMS_EPS) * g_ref[...]).astype(BF16)


def _attention(q, k, v, bias_tiles, lam, g_scaled, t):
    b, l, _ = q.shape
    nq = l // t
    sub = 2 if nq % 2 == 0 else 1
    kern = functools.partial(_attn_kernel, t=t, sub=sub, n_iter=nq // sub)
    return pl.pallas_call(
        kern,
        grid=(b, N_HEADS, nq),
        in_specs=[pl.BlockSpec(memory_space=pltpu.SMEM),
                  pl.BlockSpec((1, t, V_DIM), lambda bi, h, qi: (bi, qi, h)),
                  pl.BlockSpec((1, l, V_DIM), lambda bi, h, qi: (bi, 0, h)),
                  pl.BlockSpec((1, l, V_DIM), lambda bi, h, qi: (bi, 0, h)),
                  pl.BlockSpec((1, 5, t, t), lambda bi, h, qi: (h, 0, 0, 0)),
                  pl.BlockSpec((1, V_DIM), lambda bi, h, qi: (0, 0))],
        out_specs=pl.BlockSpec((1, t, V_DIM), lambda bi, h, qi: (bi, qi, h)),
        out_shape=jax.ShapeDtypeStruct((b, l, W_ATT), BF16),
        scratch_shapes=[pltpu.VMEM((2, t, LANES), F32),
                        pltpu.VMEM((2, t, LANES), F32),
                        pltpu.VMEM((2, t, V_DIM), F32),
                        pltpu.VMEM((2, t, sub * t), F32),
                        pltpu.VMEM((2, t, sub * t), F32),
                        pltpu.VMEM((2, t, LANES), F32),
                        pltpu.VMEM((2, t, LANES), F32)],
        compiler_params=_cparams(("parallel", "parallel", "arbitrary")),
        name="diff_attention",
    )(lam, q, k, v, bias_tiles, g_scaled)


def _ssm_matrices(a_re, a_im, log_dt, b_re, b_im, c_re, c_im, d_skip):
    qn, g, p, hc = SSM_CHUNK, N_GROUPS, SSM_STATE, SSM_GROUP
    n = jnp.arange(qn + 1, dtype=F32)
    pw, bbar, cc = [], [], []
    for d in range(2):
        a = lax.complex(a_re[d].astype(F32), a_im[d].astype(F32))
        dt = jnp.exp(log_dt[d].astype(F32))[:, None]
        adt = a * dt
        a_bar = jnp.exp(adt)
        pw.append(jnp.exp(adt[None] * n[:, None, None]))
        bbar.append(((a_bar - 1.0) / a)[:, :, None]
                    * lax.complex(b_re[d].astype(F32), b_im[d].astype(F32)))
        cc.append(lax.complex(c_re[d].astype(F32), c_im[d].astype(F32)))

    hi = lax.Precision.HIGHEST
    kern = [jnp.einsum('gop,tgp,gpi->tgoi', cc[d], pw[d][:qn], bbar[d], precision=hi).real
            for d in range(2)]
    s_idx = jnp.arange(qn)[:, None]
    t_idx = jnp.arange(qn)[None, :]
    lag = t_idx - s_idx
    taus = jnp.arange(qn)
    sel_f = (lag[:, :, None] == taus).astype(F32)
    sel_b = (-lag[:, :, None] == taus).astype(F32)
    kf = jnp.einsum('stu,ugoi->stgoi', sel_f, kern[0], precision=hi)
    kb = jnp.einsum('stu,ugoi->stgoi', sel_b, kern[1], precision=hi)
    skip = (jnp.eye(qn, dtype=F32)[:, :, None, None, None]
            * (jnp.eye(hc, dtype=F32)[None, None, None] * d_skip.astype(F32).reshape(g, hc)[None, None, :, :, None]))
    m_full = kf + kb + skip
    m_mat = jnp.transpose(m_full, (2, 0, 4, 1, 3)).reshape(g, SSM_ROW, SSM_ROW)

    zeros_p = jnp.zeros((g, SSM_ROW, LANES - p), F32)

    def pad_cols(x):
        return jnp.concatenate([x, zeros_p], axis=-1)

    pf = jnp.einsum('sgp,gpi->gsip', pw[0][:qn][::-1], bbar[0]).reshape(g, SSM_ROW, p)
    pb = jnp.einsum('sgp,gpi->gsip', pw[1][:qn], bbar[1]).reshape(g, SSM_ROW, p)
    p_mat = jnp.concatenate([pad_cols(pf.real), pad_cols(pf.imag),
                             pad_cols(pb.real), pad_cols(pb.imag)], axis=-1)

    wf = jnp.einsum('gop,tgp->gpto', cc[0], pw[0][1:qn + 1]).reshape(g, p, SSM_ROW)
    wb = jnp.einsum('gop,tgp->gpto', cc[1], pw[1][1:qn + 1][::-1]).reshape(g, p, SSM_ROW)
    zeros_r = jnp.zeros((g, LANES - p, SSM_ROW), F32)
    r_mat = jnp.concatenate([wf.real, zeros_r, -wf.imag, zeros_r,
                             wb.real, zeros_r, -wb.imag, zeros_r], axis=1)

    zeros_a = jnp.zeros((g, LANES - p), F32)

    def pad_vec(x):
        return jnp.concatenate([x, zeros_a], axis=-1)

    alpha = jnp.stack([pad_vec(pw[0][qn].real), pad_vec(pw[0][qn].imag),
                       pad_vec(pw[1][qn].real), pad_vec(pw[1][qn].imag)], axis=1)
    return p_mat.astype(BF16), m_mat.astype(BF16), r_mat.astype(BF16), alpha


def _s5_kernel(u_ref, p_ref, m_ref, r_ref, a_ref, y_ref, s_scr, x_scr, *, nc, bsz):
    u = u_ref[0]
    s_scr[...] = jnp.dot(u, p_ref[0], preferred_element_type=F32)
    al = a_ref[0]
    afr = jnp.broadcast_to(al[0:1], (bsz, LANES))
    afi = jnp.broadcast_to(al[1:2], (bsz, LANES))
    abr = jnp.broadcast_to(al[2:3], (bsz, LANES))
    abi = jnp.broadcast_to(al[3:4], (bsz, LANES))
    zero = jnp.zeros((bsz, LANES), F32)

    def step(i, carry):
        fr, fi, br, bi = carry
        rf = pl.multiple_of(i * bsz, bsz)
        rb = pl.multiple_of((nc - 1 - i) * bsz, bsz)
        x_scr[pl.ds(rf, bsz), 0:LANES] = fr
        x_scr[pl.ds(rf, bsz), LANES:2 * LANES] = fi
        x_scr[pl.ds(rb, bsz), 2 * LANES:3 * LANES] = br
        x_scr[pl.ds(rb, bsz), 3 * LANES:4 * LANES] = bi
        sfr = s_scr[pl.ds(rf, bsz), 0:LANES]
        sfi = s_scr[pl.ds(rf, bsz), LANES:2 * LANES]
        sbr = s_scr[pl.ds(rb, bsz), 2 * LANES:3 * LANES]
        sbi = s_scr[pl.ds(rb, bsz), 3 * LANES:4 * LANES]
        return (afr * fr - afi * fi + sfr, afr * fi + afi * fr + sfi,
                abr * br - abi * bi + sbr, abr * bi + abi * br + sbi)

    lax.fori_loop(0, nc, step, (zero, zero, zero, zero))
    y = jnp.dot(u, m_ref[0], preferred_element_type=F32)
    y = y + jnp.dot(x_scr[...].astype(BF16), r_ref[0], preferred_element_type=F32)
    y_ref[0] = y


def _s5(u_grp, p_mat, m_mat, r_mat, alpha, nc, bsz):
    g, rows, _ = u_grp.shape
    kern = functools.partial(_s5_kernel, nc=nc, bsz=bsz)
    mat = pl.BlockSpec((1, SSM_ROW, SSM_ROW), lambda i: (i, 0, 0))
    seq = pl.BlockSpec((1, rows, SSM_ROW), lambda i: (i, 0, 0))
    return pl.pallas_call(
        kern,
        grid=(g,),
        in_specs=[seq, mat, mat, mat, pl.BlockSpec((1, 4, LANES), lambda i: (i, 0, 0))],
        out_specs=seq,
        out_shape=jax.ShapeDtypeStruct((g, rows, SSM_ROW), F32),
        scratch_shapes=[pltpu.VMEM((rows, 4 * LANES), F32), pltpu.VMEM((rows, 4 * LANES), F32)],
        compiler_params=_cparams(("parallel",)),
        name="s5_scan",
    )(u_grp, p_mat, m_mat, r_mat, alpha)


def _post_mix_kernel(x_ref, att_ref, y_ref, wglu_ref, bglu_ref, gs_ref, wout_ref, g2_ref,
                     wrh_ref, wrl_ref, br_ref, x2_ref, h2_ref, ids_ref, gates_ref):
    y = y_ref[...]
    y = 0.5 * y * (1.0 + jnp.tanh(math.sqrt(2.0 / math.pi) * (y + 0.044715 * (y * y * y))))
    z = jnp.dot(y.astype(BF16), wglu_ref[...], preferred_element_type=F32) + bglu_ref[...]
    y = y * (1.0 / (1.0 + jnp.exp(-z)))
    ms = jnp.mean(y * y, axis=-1, keepdims=True)
    ssm = (y * lax.rsqrt(ms + RMS_EPS) * gs_ref[...]).astype(BF16)
    mix = jnp.dot(att_ref[...], wout_ref[0:W_ATT, :], preferred_element_type=F32)
    mix = mix + jnp.dot(ssm, wout_ref[W_ATT:, :], preferred_element_type=F32)
    x2 = x_ref[...] + mix
    x2_ref[...] = x2
    ms2 = jnp.mean(x2 * x2, axis=-1, keepdims=True)
    h2 = x2 * lax.rsqrt(ms2 + RMS_EPS) * g2_ref[...]
    h2_ref[...] = h2
    h_hi = h2.astype(BF16)
    h_lo = (h2 - h_hi.astype(F32)).astype(BF16)
    logits = (jnp.dot(h_hi, wrh_ref[...], preferred_element_type=F32)
              + jnp.dot(h_lo, wrh_ref[...], preferred_element_type=F32)
              + jnp.dot(h_hi, wrl_ref[...], preferred_element_type=F32)) + br_ref[...]
    lane = lax.broadcasted_iota(jnp.int32, logits.shape, 1).astype(F32)
    neg = jnp.float32(-jnp.inf)
    cur = jnp.where(lane < N_EXPERTS, logits, neg)
    ids = jnp.zeros(logits.shape, F32)
    vals = jnp.zeros(logits.shape, F32)
    top = None
    den = None
    for kk in range(TOP_K):
        mx = jnp.max(cur, axis=1, keepdims=True)
        idx = jnp.min(jnp.where(cur == mx, lane, float(LANES)), axis=1, keepdims=True)
        if kk == 0:
            top = mx
        e = jnp.exp(mx - top)
        den = e if den is None else den + e
        ids = jnp.where(lane == kk, idx, ids)
        vals = jnp.where(lane == kk, e, vals)
        cur = jnp.where(lane == idx, neg, cur)
    ids_ref[...] = ids.astype(jnp.int32)
    gates_ref[...] = vals / den


def _post_mix(x2d, att, yssm, wglu, bglu, gs, wout, g2, wr_hi, wr_lo, br):
    t = x2d.shape[0]
    tm = min(ROW_TILE, t)
    row = lambda i: (i, 0)
    const = lambda i: (0, 0)
    return pl.pallas_call(
        _post_mix_kernel,
        grid=(t // tm,),
        in_specs=[pl.BlockSpec((tm, D_MODEL), row),
                  pl.BlockSpec((tm, W_ATT), row),
                  pl.BlockSpec((tm, W_SSM), row),
                  pl.BlockSpec((W_SSM, W_SSM), const),
                  pl.BlockSpec((1, W_SSM), const),
                  pl.BlockSpec((1, W_SSM), const),
                  pl.BlockSpec((D_MODEL, D_MODEL), const),
                  pl.BlockSpec((1, D_MODEL), const),
                  pl.BlockSpec((D_MODEL, LANES), const),
                  pl.BlockSpec((D_MODEL, LANES), const),
                  pl.BlockSpec((1, LANES), const)],
        out_specs=[pl.BlockSpec((tm, D_MODEL), row),
                   pl.BlockSpec((tm, D_MODEL), row),
                   pl.BlockSpec((tm, LANES), row),
                   pl.BlockSpec((tm, LANES), row)],
        out_shape=[jax.ShapeDtypeStruct((t, D_MODEL), F32),
                   jax.ShapeDtypeStruct((t, D_MODEL), F32),
                   jax.ShapeDtypeStruct((t, LANES), jnp.int32),
                   jax.ShapeDtypeStruct((t, LANES), F32)],
        compiler_params=_cparams(("parallel",)),
        name="post_mix",
    )(x2d, att, yssm, wglu, bglu, gs, wout, g2, wr_hi, wr_lo, br)


def _sc_gather_rows(x, idx):
    n = idx.shape[0]
    d = x.shape[1]
    steps = n // SC_ROWS
    workers = SC_CORES * SC_SUBCORES
    assert steps % (2 * workers) == 0, (n, steps)
    per_worker = steps // workers
    idx_rows = jnp.pad(idx.reshape(steps, SC_ROWS), ((0, 0), (0, LANES - SC_ROWS)))
    mesh = plsc.VectorSubcoreMesh(core_axis_name="core", subcore_axis_name="subcore")

    @pl.kernel(out_type=jax.ShapeDtypeStruct((n, d), x.dtype), mesh=mesh,
               scratch_types=[pltpu.VMEM((2, 1, LANES), jnp.int32),
                              pltpu.VMEM((2, SC_ROWS, d), x.dtype),
                              pltpu.SemaphoreType.DMA((2,)),
                              pltpu.SemaphoreType.DMA((2,))])
    def gather(x_hbm, i_hbm, o_hbm, idx_v, buf, sem_in, sem_out):
        base = (lax.axis_index("core") * SC_SUBCORES + lax.axis_index("subcore")) * per_worker

        def gather_copy(slot):
            return pltpu.make_async_copy(x_hbm.at[idx_v.at[slot, 0, pl.ds(0, SC_ROWS)]], buf.at[slot],
                                         sem_in.at[slot])

        def out_copy(step, slot):
            return pltpu.make_async_copy(buf.at[slot], o_hbm.at[pl.ds(step * SC_ROWS, SC_ROWS), :],
                                         sem_out.at[slot])

        @pl.loop(0, per_worker // 2)
        def _(j):
            for slot in range(2):
                step = base + 2 * j + slot

                @pl.when(j > 0)
                def _():
                    out_copy(step, slot).wait()

                pltpu.sync_copy(i_hbm.at[pl.ds(step, 1), :], idx_v.at[slot])
                gather_copy(slot).start()
            for slot in range(2):
                gather_copy(slot).wait()
                out_copy(base + 2 * j + slot, slot).start()

        for slot in range(2):
            out_copy(base, slot).wait()

    return gather(x, idx_rows)


def _expert_kernel(be_ref, x_ref, w1_ref, b1_ref, w2_ref, b2_ref, y_ref):
    x = x_ref[...].astype(BF16)
    hdn = jnp.dot(x, w1_ref[0], preferred_element_type=F32) + b1_ref[0]
    gate = jnp.minimum(hdn[:, :D_FF], SWIGLU_LIMIT)
    lin = jnp.clip(hdn[:, D_FF:], -SWIGLU_LIMIT, SWIGLU_LIMIT)
    act = gate * (1.0 / (1.0 + jnp.exp(-SWIGLU_ALPHA * gate))) * (lin + 1.0)
    y_ref[...] = jnp.dot(act.astype(BF16), w2_ref[0], preferred_element_type=F32) + b2_ref[0]


def _experts(block_e, x_pad, w1, b1, w2, b2):
    n_pad = x_pad.shape[0]
    rows = EXPERT_ROWS
    grid_spec = pltpu.PrefetchScalarGridSpec(
        num_scalar_prefetch=1,
        grid=(n_pad // rows,),
        in_specs=[pl.BlockSpec((rows, D_MODEL), lambda i, be: (i, 0)),
                  pl.BlockSpec((1, D_MODEL, 2 * D_FF), lambda i, be: (be[i], 0, 0)),
                  pl.BlockSpec((1, 1, 2 * D_FF), lambda i, be: (be[i], 0, 0)),
                  pl.BlockSpec((1, D_FF, D_MODEL), lambda i, be: (be[i], 0, 0)),
                  pl.BlockSpec((1, 1, D_MODEL), lambda i, be: (be[i], 0, 0))],
        out_specs=pl.BlockSpec((rows, D_MODEL), lambda i, be: (i, 0)),
    )
    return pl.pallas_call(
        _expert_kernel,
        grid_spec=grid_spec,
        out_shape=jax.ShapeDtypeStruct((n_pad, D_MODEL), F32),
        compiler_params=_cparams(("arbitrary",)),
        name="moe_experts",
    )(block_e, x_pad, w1, b1, w2, b2)


def _combine_kernel(x2_ref, gates_ref, gf_ref, y_ref, o_ref):
    gates = gates_ref[...]
    y = x2_ref[...]
    for kk in range(TOP_K):
        y = y + gates[:, kk:kk + 1] * y_ref[kk]
    ms = jnp.mean(y * y, axis=-1, keepdims=True)
    o_ref[...] = y * lax.rsqrt(ms + RMS_EPS) * gf_ref[...]


def _combine(x2, gates, gf, y_sel):
    t = x2.shape[0]
    rows = min(COMBINE_ROWS, t)
    row = lambda i: (i, 0)
    return pl.pallas_call(
        _combine_kernel,
        grid=(t // rows,),
        in_specs=[pl.BlockSpec((rows, D_MODEL), row),
                  pl.BlockSpec((rows, LANES), row),
                  pl.BlockSpec((1, D_MODEL), lambda i: (0, 0)),
                  pl.BlockSpec((TOP_K, rows, D_MODEL), lambda i: (0, i, 0))],
        out_specs=pl.BlockSpec((rows, D_MODEL), row),
        out_shape=jax.ShapeDtypeStruct((t, D_MODEL), F32),
        compiler_params=_cparams(("parallel",)),
        name="moe_combine",
    )(x2, gates, gf, y_sel)


def _dispatch_plan(top_e, n_tok):
    n = n_tok * TOP_K
    blk = EXPERT_ROWS
    flat_e = top_e.reshape(n)
    order = jnp.argsort(flat_e).astype(jnp.int32)
    rank = jnp.argsort(order).astype(jnp.int32)
    st = order // TOP_K
    experts = jnp.arange(N_EXPERTS, dtype=jnp.int32)
    counts = jnp.sum((flat_e[:, None] == experts).astype(jnp.int32), axis=0)
    padded = (counts + blk - 1) // blk * blk
    start = jnp.cumsum(counts) - counts
    pend = jnp.cumsum(padded)
    pstart = pend - padded
    pos = (pstart - start)[flat_e] + rank
    n_blocks = n // blk + N_EXPERTS
    block_start = jnp.arange(n_blocks, dtype=jnp.int32) * blk
    block_e = jnp.minimum(jnp.sum((pend[None, :] <= block_start[:, None]).astype(jnp.int32), axis=1),
                          N_EXPERTS - 1)
    slot = jnp.arange(n_blocks * blk, dtype=jnp.int32)
    slot_e = jnp.repeat(block_e, blk)
    off = slot - pstart[slot_e]
    src = jnp.clip(start[slot_e] + off, 0, n - 1)
    tok_pad = jnp.where(off < counts[slot_e], st[src], 0)
    return tok_pad, pos, block_e


def _trunk_front(x, prm):
    bsz, l, _ = x.shape
    t = bsz * l
    x2d = x.reshape(t, D_MODEL)
    q, k, v, u = _in_proj(x2d, prm['norm1_g'], prm['w_in'])

    t_att = min(512, l)
    att = _attention(q.reshape(bsz, l, W_ATT), k.reshape(bsz, l, W_ATT), v.reshape(bsz, l, W_ATT),
                     prm['bias_tiles'][t_att], prm['lam'], prm['subln_g'], t_att)

    nc = l // SSM_CHUNK
    u_grp = jnp.transpose(u.reshape(bsz, nc, SSM_CHUNK, N_GROUPS, SSM_GROUP), (3, 1, 0, 2, 4))
    u_grp = u_grp.reshape(N_GROUPS, nc * bsz, SSM_ROW)
    y_grp = _s5(u_grp, prm['ssm_p'], prm['ssm_m'], prm['ssm_r'], prm['ssm_alpha'], nc, bsz)
    yssm = jnp.transpose(y_grp.reshape(N_GROUPS, nc, bsz, SSM_CHUNK, SSM_GROUP), (2, 1, 3, 0, 4))
    yssm = yssm.reshape(t, W_SSM)

    x2, h2, ids, gates = _post_mix(x2d, att.reshape(t, W_ATT), yssm, prm['w_glu'], prm['b_glu'],
                                   prm['ssm_norm_g'], prm['w_out'], prm['norm2_g'],
                                   prm['w_router_hi'], prm['w_router_lo'], prm['b_router'])

    tok_pad, pos, block_e = _dispatch_plan(ids[:, :TOP_K], t)
    return {'shape': x.shape, 'x2': x2, 'h2': h2, 'gates': gates,
            'tok_pad': tok_pad, 'pos': pos, 'block_e': block_e}


def _trunk_back(st, prm):
    t = st['x2'].shape[0]
    x_pad = _sc_gather_rows(st['h2'], st['tok_pad'])
    y_pad = _experts(st['block_e'], x_pad, prm['w_moe1'], prm['b_moe1'], prm['w_moe2'], prm['b_moe2'])
    pos_by_k = st['pos'].reshape(t, TOP_K).T.reshape(TOP_K * t)
    y_sel = _sc_gather_rows(y_pad, pos_by_k).reshape(TOP_K, t, D_MODEL)
    out = _combine(st['x2'], st['gates'], prm['normf_g'], y_sel)
    return out.reshape(st['shape'])


def _prepare(seq_lens, rel_bias, norm1_g, w_in, lambda_q1, lambda_k1, lambda_q2, lambda_k2, subln_g,
             ssm_A_re, ssm_A_im, ssm_log_dt, ssm_B_re, ssm_B_im, ssm_C_re, ssm_C_im, ssm_D,
             w_glu, b_glu, ssm_norm_g, w_out, norm2_g, w_router, b_router,
             w_moe1, b_moe1, w_moe2, b_moe2, normf_g):
    layer = 0
    lambda_init = 0.8 - 0.6 * math.exp(-0.3 * layer)
    lam = (jnp.exp(jnp.sum(lambda_q1[layer].astype(F32) * lambda_k1[layer].astype(F32)))
           - jnp.exp(jnp.sum(lambda_q2[layer].astype(F32) * lambda_k2[layer].astype(F32))) + lambda_init)
    p_mat, m_mat, r_mat, alpha = _ssm_matrices(
        ssm_A_re[layer], ssm_A_im[layer], ssm_log_dt[layer], ssm_B_re[layer], ssm_B_im[layer],
        ssm_C_re[layer], ssm_C_im[layer], ssm_D[layer])
    pad_e = LANES - N_EXPERTS
    w_r = jnp.pad(w_router[layer].astype(F32), ((0, 0), (0, pad_e)))
    w_r_hi = w_r.astype(BF16)
    return {
        'norm1_g': norm1_g[layer].reshape(1, D_MODEL).astype(F32),
        'w_in': w_in[layer].astype(BF16),
        'lam': lam.reshape(1).astype(F32),
        'subln_g': (subln_g[layer].astype(F32) * (1.0 - lambda_init)).reshape(1, V_DIM),
        'bias_tiles': {t: _bias_tiles(rel_bias, t) for t in sorted({min(512, l) for l in seq_lens})},
        'ssm_p': p_mat, 'ssm_m': m_mat, 'ssm_r': r_mat, 'ssm_alpha': alpha,
        'w_glu': w_glu[layer].astype(BF16),
        'b_glu': b_glu[layer].reshape(1, W_SSM).astype(F32),
        'ssm_norm_g': ssm_norm_g[layer].reshape(1, W_SSM).astype(F32),
        'w_out': w_out[layer].astype(BF16),
        'norm2_g': norm2_g[layer].reshape(1, D_MODEL).astype(F32),
        'w_router_hi': w_r_hi,
        'w_router_lo': (w_r - w_r_hi.astype(F32)).astype(BF16),
        'b_router': jnp.pad(b_router[layer].astype(F32), (0, pad_e)).reshape(1, LANES),
        'w_moe1': w_moe1[layer].astype(BF16),
        'b_moe1': b_moe1[layer].reshape(N_EXPERTS, 1, 2 * D_FF).astype(F32),
        'w_moe2': w_moe2[layer].astype(BF16),
        'b_moe2': b_moe2[layer].reshape(N_EXPERTS, 1, D_MODEL).astype(F32),
        'normf_g': normf_g.reshape(1, D_MODEL).astype(F32),
    }


def kernel(x_prompt, x_sample, rel_bias, norm1_g, w_in, lambda_q1, lambda_k1, lambda_q2, lambda_k2, subln_g, ssm_A_re, ssm_A_im, ssm_log_dt, ssm_B_re, ssm_B_im, ssm_C_re, ssm_C_im, ssm_D, w_glu, b_glu, ssm_norm_g, w_out, norm2_g, w_router, b_router, w_moe1, b_moe1, w_moe2, b_moe2, normf_g):
    prm = _prepare((x_prompt.shape[1], x_sample.shape[1]), rel_bias, norm1_g, w_in, lambda_q1,
                   lambda_k1, lambda_q2, lambda_k2, subln_g, ssm_A_re, ssm_A_im, ssm_log_dt,
                   ssm_B_re, ssm_B_im, ssm_C_re, ssm_C_im, ssm_D, w_glu, b_glu, ssm_norm_g, w_out,
                   norm2_g, w_router, b_router, w_moe1, b_moe1, w_moe2, b_moe2, normf_g)
    first = _trunk_front(x_prompt, prm)
    x_sample, first['tok_pad'] = lax.optimization_barrier((x_sample, first['tok_pad']))
    second = _trunk_front(x_sample, prm)
    return (_trunk_back(first, prm), _trunk_back(second, prm))
```

```python
import functools
import math

import jax
import jax.numpy as jnp
from jax import lax
from jax.experimental import pallas as pl
from jax.experimental.pallas import tpu as pltpu
from jax.experimental.pallas import tpu_sc as plsc

F32 = jnp.float32
BF16 = jnp.bfloat16

D_MODEL = 1024
W_ATT = 512
W_SSM = 512
HEAD_DIM = 64
N_HEADS = 4
V_DIM = 2 * HEAD_DIM
SSM_GROUP = 16
N_GROUPS = W_SSM // SSM_GROUP
SSM_STATE = 64
IN_WIDTH = 3 * W_ATT + W_SSM
N_BUCKETS = 32
MAX_DISTANCE = 128
N_EXPERTS = 32
TOP_K = 4
D_FF = D_MODEL
SWIGLU_ALPHA = 1.702
SWIGLU_LIMIT = 7.0
RMS_EPS = 1e-6
ATT_SCALE = HEAD_DIM ** -0.5
LOG2E = math.log2(math.e)

LANES = 128
PV_KEYS = 256
SSM_CHUNK = LANES
SSM_ROW = SSM_CHUNK * SSM_GROUP
ROW_TILE = 1024
EXPERT_ROWS = 512
SC_ROWS = 32
SC_CORES = 2
SC_SUBCORES = 16
COMBINE_ROWS = 256
VMEM_LIMIT = 56 * 1024 * 1024


def _cparams(sem):
    return pltpu.CompilerParams(dimension_semantics=sem, vmem_limit_bytes=VMEM_LIMIT)


def _in_proj_kernel(x_ref, g_ref, w_ref, q_ref, k_ref, v_ref, ut_ref):
    x = x_ref[...]
    ms = jnp.mean(x * x, axis=-1, keepdims=True)
    h = (x * lax.rsqrt(ms + RMS_EPS) * g_ref[...]).astype(BF16)
    proj = jnp.dot(h, w_ref[...], preferred_element_type=F32)
    q_ref[...] = (proj[:, 0:W_ATT] * (ATT_SCALE * LOG2E)).astype(BF16)
    k_ref[...] = proj[:, W_ATT:2 * W_ATT].astype(BF16)
    v_ref[...] = proj[:, 2 * W_ATT:3 * W_ATT].astype(BF16)
    ut = proj[:, 3 * W_ATT:].T
    for j in range(ut_ref.shape[1]):
        ut_ref[:, j, :] = ut[:, j * SSM_CHUNK:(j + 1) * SSM_CHUNK]


def _in_proj(x2d, g, w_bf16):
    t = x2d.shape[0]
    tm = min(ROW_TILE, t)
    out = jax.ShapeDtypeStruct((t, W_ATT), BF16)
    row = lambda i: (i, 0)
    return pl.pallas_call(
        _in_proj_kernel,
        grid=(t // tm,),
        in_specs=[pl.BlockSpec((tm, D_MODEL), row),
                  pl.BlockSpec((1, D_MODEL), lambda i: (0, 0)),
                  pl.BlockSpec((D_MODEL, IN_WIDTH), lambda i: (0, 0))],
        out_specs=[pl.BlockSpec((tm, W_ATT), row)] * 3
        + [pl.BlockSpec((W_SSM, tm // SSM_CHUNK, SSM_CHUNK), lambda i: (0, i, 0))],
        out_shape=[out] * 3 + [jax.ShapeDtypeStruct((W_SSM, t // SSM_CHUNK, SSM_CHUNK), F32)],
        compiler_params=_cparams(("parallel",)),
        name="in_proj",
    )(x2d, g, w_bf16)


def _t5_bucket(rel):
    half = N_BUCKETS // 2
    max_exact = half // 2
    ret = jnp.where(rel > 0, half, 0).astype(jnp.int32)
    n = jnp.abs(rel)
    nf = jnp.maximum(n, 1).astype(F32)
    large = max_exact + (jnp.log(nf / max_exact) / math.log(MAX_DISTANCE / max_exact)
                         * (half - max_exact)).astype(jnp.int32)
    large = jnp.minimum(large, half - 1)
    return ret + jnp.where(n < max_exact, n, large)


def _bias_tiles(rel_bias, t):
    i = jnp.arange(t, dtype=jnp.int32)
    d = jnp.arange(-2, 3, dtype=jnp.int32)
    rel = d[:, None, None] * t + i[None, None, :] - i[None, :, None]
    onehot = (_t5_bucket(rel)[..., None] == jnp.arange(N_BUCKETS, dtype=jnp.int32)).astype(F32)
    tiles = jnp.einsum('dqkn,nh->hdqk', onehot, rel_bias.astype(F32), precision=lax.Precision.HIGHEST)
    return tiles * LOG2E


def _attn_kernel(lam_ref, q_ref, k_ref, v_ref, bias_ref, g_ref, o_ref,
                 m_scr, l_scr, acc_scr, sa_scr, sb_scr, mxa_scr, mxb_scr, *, t, sub, n_iter):
    qi = pl.program_id(2)
    q = q_ref[0]
    lane = lax.broadcasted_iota(jnp.int32, q.shape, 1)
    zero = jnp.zeros_like(q)
    qs = (jnp.where(lane < HEAD_DIM, q, zero), jnp.where(lane >= HEAD_DIM, q, zero))
    nb = t // LANES

    m_scr[...] = jnp.full(m_scr.shape, -jnp.inf, F32)
    l_scr[...] = jnp.zeros(l_scr.shape, F32)
    acc_scr[...] = jnp.zeros(acc_scr.shape, F32)

    def scores(j, s_scr, mx_scr):
        for mi in range(2):
            mx = None
            for c in range(sub):
                blk = j * sub + c
                kc = k_ref[0, pl.ds(pl.multiple_of(blk * t, t), t), :]
                s = lax.dot_general(qs[mi], kc, (((1,), (1,)), ((), ())), preferred_element_type=F32)
                s = s + bias_ref[0, jnp.clip(blk - qi, -2, 2) + 2]
                s_scr[mi, :, c * t:(c + 1) * t] = s
                for i in range(nb):
                    piece = s[:, i * LANES:(i + 1) * LANES]
                    mx = piece if mx is None else jnp.maximum(mx, piece)
            mx_scr[mi] = mx

    def accumulate(j, s_scr, mx_scr):
        vj = v_ref[0, pl.ds(pl.multiple_of(j * (sub * t), sub * t), sub * t), :]
        for mi in range(2):
            m_prev = m_scr[mi]
            m_next = jnp.maximum(m_prev, jnp.max(mx_scr[mi], axis=1, keepdims=True))
            alpha = jnp.exp2(m_prev - m_next)
            m_scr[mi] = m_next
            lsum = None
            pv = None
            for i in range(sub * t // PV_KEYS):
                ps = []
                for c in range(PV_KEYS // LANES):
                    lo = i * PV_KEYS + c * LANES
                    p = jnp.exp2(s_scr[mi, :, lo:lo + LANES] - m_next)
                    lsum = p if lsum is None else lsum + p
                    ps.append(p.astype(BF16))
                d = jnp.dot(jnp.concatenate(ps, axis=1), vj[i * PV_KEYS:(i + 1) * PV_KEYS, :],
                            preferred_element_type=F32)
                pv = d if pv is None else pv + d
            l_scr[mi] = alpha * l_scr[mi] + lsum
            acc_scr[mi] = alpha * acc_scr[mi] + pv

    scores(0, sa_scr, mxa_scr)
    n_pairs = (n_iter - 1) // 2

    def body(i, carry):
        scores(2 * i + 1, sb_scr, mxb_scr)
        accumulate(2 * i, sa_scr, mxa_scr)
        scores(2 * i + 2, sa_scr, mxa_scr)
        accumulate(2 * i + 1, sb_scr, mxb_scr)
        return carry

    lax.fori_loop(0, n_pairs, body, 0)
    if (n_iter - 1) % 2 == 1:
        scores(n_iter - 1, sb_scr, mxb_scr)
        accumulate(n_iter - 2, sa_scr, mxa_scr)
        accumulate(n_iter - 1, sb_scr, mxb_scr)
    else:
        accumulate(n_iter - 1, sa_scr, mxa_scr)

    o1 = acc_scr[0] / jnp.sum(l_scr[0], axis=1, keepdims=True)
    o2 = acc_scr[1] / jnp.sum(l_scr[1], axis=1, keepdims=True)
    o = o1 - lam_ref[0] * o2
    ms = jnp.mean(o * o, axis=-1, keepdims=True)
    o_ref[0] = (o * lax.rsqrt(ms + RMS_EPS) * g_ref[...]).astype(BF16)


def _attention(q, k, v, bias_tiles, lam, g_scaled, t):
    b, l, _ = q.shape
    nq = l // t
    sub = 2 if nq % 2 == 0 else 1
    kern = functools.partial(_attn_kernel, t=t, sub=sub, n_iter=nq // sub)
    return pl.pallas_call(
        kern,
        grid=(b, N_HEADS, nq),
        in_specs=[pl.BlockSpec(memory_space=pltpu.SMEM),
                  pl.BlockSpec((1, t, V_DIM), lambda bi, h, qi: (bi, qi, h)),
                  pl.BlockSpec((1, l, V_DIM), lambda bi, h, qi: (bi, 0, h)),
                  pl.BlockSpec((1, l, V_DIM), lambda bi, h, qi: (bi, 0, h)),
                  pl.BlockSpec((1, 5, t, t), lambda bi, h, qi: (h, 0, 0, 0)),
                  pl.BlockSpec((1, V_DIM), lambda bi, h, qi: (0, 0))],
        out_specs=pl.BlockSpec((1, t, V_DIM), lambda bi, h, qi: (bi, qi, h)),
        out_shape=jax.ShapeDtypeStruct((b, l, W_ATT), BF16),
        scratch_shapes=[pltpu.VMEM((2, t, LANES), F32),
                        pltpu.VMEM((2, t, LANES), F32),
                        pltpu.VMEM((2, t, V_DIM), F32),
                        pltpu.VMEM((2, t, sub * t), F32),
                        pltpu.VMEM((2, t, sub * t), F32),
                        pltpu.VMEM((2, t, LANES), F32),
                        pltpu.VMEM((2, t, LANES), F32)],
        compiler_params=_cparams(("parallel", "parallel", "arbitrary")),
        name="diff_attention",
    )(lam, q, k, v, bias_tiles, g_scaled)


def _ssm_matrices(a_re, a_im, log_dt, b_re, b_im, c_re, c_im, d_skip):
    qn, g, p, hc = SSM_CHUNK, N_GROUPS, SSM_STATE, SSM_GROUP
    n = jnp.arange(qn + 1, dtype=F32)
    pw, bbar, cc = [], [], []
    for d in range(2):
        a = lax.complex(a_re[d].astype(F32), a_im[d].astype(F32))
        dt = jnp.exp(log_dt[d].astype(F32))[:, None]
        adt = a * dt
        a_bar = jnp.exp(adt)
        pw.append(jnp.exp(adt[None] * n[:, None, None]))
        bbar.append(((a_bar - 1.0) / a)[:, :, None]
                    * lax.complex(b_re[d].astype(F32), b_im[d].astype(F32)))
        cc.append(lax.complex(c_re[d].astype(F32), c_im[d].astype(F32)))

    hi = lax.Precision.HIGHEST
    kern = [jnp.einsum('gop,tgp,gpi->tgoi', cc[d], pw[d][:qn], bbar[d], precision=hi).real
            for d in range(2)]
    s_idx = jnp.arange(qn)[:, None]
    t_idx = jnp.arange(qn)[None, :]
    lag = t_idx - s_idx
    taus = jnp.arange(qn)
    sel_f = (lag[:, :, None] == taus).astype(F32)
    sel_b = (-lag[:, :, None] == taus).astype(F32)
    kf = jnp.einsum('stu,ugoi->gisot', sel_f, kern[0], precision=hi)
    kb = jnp.einsum('stu,ugoi->gisot', sel_b, kern[1], precision=hi)
    skip = (jnp.eye(qn, dtype=F32)[None, None, :, None, :] * jnp.eye(hc, dtype=F32)[None, :, None, :, None]
            * d_skip.astype(F32).reshape(g, hc)[:, :, None, None, None])
    m_mat = (kf + kb + skip).reshape(g, SSM_ROW, SSM_ROW)

    zeros_p = jnp.zeros((g, SSM_ROW, LANES - p), F32)

    def pad_cols(x):
        return jnp.concatenate([x, zeros_p], axis=-1)

    pf = jnp.einsum('sgp,gpi->gisp', pw[0][:qn][::-1], bbar[0]).reshape(g, SSM_ROW, p)
    pb = jnp.einsum('sgp,gpi->gisp', pw[1][:qn], bbar[1]).reshape(g, SSM_ROW, p)
    p_mat = jnp.concatenate([pad_cols(pf.real), pad_cols(pf.imag),
                             pad_cols(pb.real), pad_cols(pb.imag)], axis=-1)

    wf = jnp.einsum('gop,tgp->gpot', cc[0], pw[0][1:qn + 1]).reshape(g, p, SSM_ROW)
    wb = jnp.einsum('gop,tgp->gpot', cc[1], pw[1][1:qn + 1][::-1]).reshape(g, p, SSM_ROW)
    zeros_r = jnp.zeros((g, LANES - p, SSM_ROW), F32)
    r_mat = jnp.concatenate([wf.real, zeros_r, -wf.imag, zeros_r,
                             wb.real, zeros_r, -wb.imag, zeros_r], axis=1)

    zeros_a = jnp.zeros((g, LANES - p), F32)

    def pad_vec(x):
        return jnp.concatenate([x, zeros_a], axis=-1)

    alpha = jnp.stack([pad_vec(pw[0][qn].real), pad_vec(pw[0][qn].imag),
                       pad_vec(pw[1][qn].real), pad_vec(pw[1][qn].imag)], axis=1)
    return p_mat.astype(BF16), m_mat.astype(BF16), r_mat.astype(BF16), alpha


def _s5_kernel(u_ref, p_ref, m_ref, r_ref, a_ref, y_ref, s_scr, x_scr, *, nc, bsz):
    u = jnp.concatenate([u_ref[h] for h in range(SSM_GROUP)], axis=-1).astype(BF16)
    s = jnp.dot(u, p_ref[0], preferred_element_type=F32)
    for part in range(4):
        s_scr[part] = s[:, part * LANES:(part + 1) * LANES]
    al = a_ref[0]
    afr = jnp.broadcast_to(al[0:1], (bsz, LANES))
    afi = jnp.broadcast_to(al[1:2], (bsz, LANES))
    abr = jnp.broadcast_to(al[2:3], (bsz, LANES))
    abi = jnp.broadcast_to(al[3:4], (bsz, LANES))
    fr = fi = br = bi = jnp.zeros((bsz, LANES), F32)
    for c in range(nc):
        rf = pl.ds(c, bsz, stride=nc)
        rb = pl.ds(nc - 1 - c, bsz, stride=nc)
        x_scr[0, rf, :] = fr
        x_scr[1, rf, :] = fi
        x_scr[2, rb, :] = br
        x_scr[3, rb, :] = bi
        sfr = s_scr[0, rf, :]
        sfi = s_scr[1, rf, :]
        sbr = s_scr[2, rb, :]
        sbi = s_scr[3, rb, :]
        fr, fi = afr * fr - afi * fi + sfr, afr * fi + afi * fr + sfi
        br, bi = abr * br - abi * bi + sbr, abr * bi + abi * br + sbi
    y = jnp.dot(u, m_ref[0], preferred_element_type=F32)
    x_in = jnp.concatenate([x_scr[part] for part in range(4)], axis=-1).astype(BF16)
    y = y + jnp.dot(x_in, r_ref[0], preferred_element_type=F32)
    for h in range(SSM_GROUP):
        y_ref[h] = y[:, h * SSM_CHUNK:(h + 1) * SSM_CHUNK]


def _s5(u_t, p_mat, m_mat, r_mat, alpha, nc, bsz):
    rows = u_t.shape[1]
    kern = functools.partial(_s5_kernel, nc=nc, bsz=bsz)
    seq = pl.BlockSpec((SSM_GROUP, rows, SSM_CHUNK), lambda i: (i, 0, 0))
    return pl.pallas_call(
        kern,
        grid=(N_GROUPS,),
        in_specs=[seq,
                  pl.BlockSpec((1, SSM_ROW, 4 * LANES), lambda i: (i, 0, 0)),
                  pl.BlockSpec((1, SSM_ROW, SSM_ROW), lambda i: (i, 0, 0)),
                  pl.BlockSpec((1, 4 * LANES, SSM_ROW), lambda i: (i, 0, 0)),
                  pl.BlockSpec((1, 4, LANES), lambda i: (i, 0, 0))],
        out_specs=seq,
        out_shape=jax.ShapeDtypeStruct(u_t.shape, F32),
        scratch_shapes=[pltpu.VMEM((4, rows, LANES), F32), pltpu.VMEM((4, rows, LANES), F32)],
        compiler_params=_cparams(("parallel",)),
        name="s5_scan",
    )(u_t, p_mat, m_mat, r_mat, alpha)


def _post_mix_kernel(x_ref, att_ref, y_ref, wglu_ref, bglu_ref, gs_ref, wout_ref, g2_ref,
                     wrh_ref, wrl_ref, br_ref, x2_ref, h2_ref, ids_ref, gates_ref):
    y = jnp.concatenate([y_ref[:, j, :].T for j in range(y_ref.shape[1])], axis=0)
    y = 0.5 * y * (1.0 + jnp.tanh(math.sqrt(2.0 / math.pi) * (y + 0.044715 * (y * y * y))))
    z = jnp.dot(y.astype(BF16), wglu_ref[...], preferred_element_type=F32) + bglu_ref[...]
    y = y * (1.0 / (1.0 + jnp.exp(-z)))
    ms = jnp.mean(y * y, axis=-1, keepdims=True)
    ssm = (y * lax.rsqrt(ms + RMS_EPS) * gs_ref[...]).astype(BF16)
    mix = jnp.dot(att_ref[...], wout_ref[0:W_ATT, :], preferred_element_type=F32)
    mix = mix + jnp.dot(ssm, wout_ref[W_ATT:, :], preferred_element_type=F32)
    x2 = x_ref[...] + mix
    x2_ref[...] = x2
    ms2 = jnp.mean(x2 * x2, axis=-1, keepdims=True)
    h2 = x2 * lax.rsqrt(ms2 + RMS_EPS) * g2_ref[...]
    h2_ref[...] = h2
    h_hi = h2.astype(BF16)
    h_lo = (h2 - h_hi.astype(F32)).astype(BF16)
    logits = (jnp.dot(h_hi, wrh_ref[...], preferred_element_type=F32)
              + jnp.dot(h_lo, wrh_ref[...], preferred_element_type=F32)
              + jnp.dot(h_hi, wrl_ref[...], preferred_element_type=F32)) + br_ref[...]
    lane = lax.broadcasted_iota(jnp.int32, logits.shape, 1).astype(F32)
    neg = jnp.float32(-jnp.inf)
    cur = jnp.where(lane < N_EXPERTS, logits, neg)
    ids = jnp.zeros(logits.shape, F32)
    vals = jnp.zeros(logits.shape, F32)
    top = None
    den = None
    for kk in range(TOP_K):
        mx = jnp.max(cur, axis=1, keepdims=True)
        idx = jnp.min(jnp.where(cur == mx, lane, float(LANES)), axis=1, keepdims=True)
        if kk == 0:
            top = mx
        e = jnp.exp(mx - top)
        den = e if den is None else den + e
        ids = jnp.where(lane == kk, idx, ids)
        vals = jnp.where(lane == kk, e, vals)
        cur = jnp.where(lane == idx, neg, cur)
    ids_ref[...] = ids.astype(jnp.int32)
    gates_ref[...] = vals / den


def _post_mix(x2d, att, yssm, wglu, bglu, gs, wout, g2, wr_hi, wr_lo, br):
    t = x2d.shape[0]
    tm = min(ROW_TILE, t)
    row = lambda i: (i, 0)
    const = lambda i: (0, 0)
    return pl.pallas_call(
        _post_mix_kernel,
        grid=(t // tm,),
        in_specs=[pl.BlockSpec((tm, D_MODEL), row),
                  pl.BlockSpec((tm, W_ATT), row),
                  pl.BlockSpec((W_SSM, tm // SSM_CHUNK, SSM_CHUNK), lambda i: (0, i, 0)),
                  pl.BlockSpec((W_SSM, W_SSM), const),
                  pl.BlockSpec((1, W_SSM), const),
                  pl.BlockSpec((1, W_SSM), const),
                  pl.BlockSpec((D_MODEL, D_MODEL), const),
                  pl.BlockSpec((1, D_MODEL), const),
                  pl.BlockSpec((D_MODEL, LANES), const),
                  pl.BlockSpec((D_MODEL, LANES), const),
                  pl.BlockSpec((1, LANES), const)],
        out_specs=[pl.BlockSpec((tm, D_MODEL), row),
                   pl.BlockSpec((tm, D_MODEL), row),
                   pl.BlockSpec((tm, LANES), row),
                   pl.BlockSpec((tm, LANES), row)],
        out_shape=[jax.ShapeDtypeStruct((t, D_MODEL), F32),
                   jax.ShapeDtypeStruct((t, D_MODEL), F32),
                   jax.ShapeDtypeStruct((t, LANES), jnp.int32),
                   jax.ShapeDtypeStruct((t, LANES), F32)],
        compiler_params=_cparams(("parallel",)),
        name="post_mix",
    )(x2d, att, yssm, wglu, bglu, gs, wout, g2, wr_hi, wr_lo, br)


def _sc_gather_rows(x, idx):
    n = idx.shape[0]
    d = x.shape[1]
    steps = n // SC_ROWS
    workers = SC_CORES * SC_SUBCORES
    assert steps % (2 * workers) == 0, (n, steps)
    per_worker = steps // workers
    idx_rows = jnp.pad(idx.reshape(steps, SC_ROWS), ((0, 0), (0, LANES - SC_ROWS)))
    mesh = plsc.VectorSubcoreMesh(core_axis_name="core", subcore_axis_name="subcore")

    @pl.kernel(out_type=jax.ShapeDtypeStruct((n, d), x.dtype), mesh=mesh,
               scratch_types=[pltpu.VMEM((2, 1, LANES), jnp.int32),
                              pltpu.VMEM((2, SC_ROWS, d), x.dtype),
                              pltpu.SemaphoreType.DMA((2,)),
                              pltpu.SemaphoreType.DMA((2,))])
    def gather(x_hbm, i_hbm, o_hbm, idx_v, buf, sem_in, sem_out):
        base = (lax.axis_index("core") * SC_SUBCORES + lax.axis_index("subcore")) * per_worker

        def gather_copy(slot):
            return pltpu.make_async_copy(x_hbm.at[idx_v.at[slot, 0, pl.ds(0, SC_ROWS)]], buf.at[slot],
                                         sem_in.at[slot])

        def out_copy(step, slot):
            return pltpu.make_async_copy(buf.at[slot], o_hbm.at[pl.ds(step * SC_ROWS, SC_ROWS), :],
                                         sem_out.at[slot])

        @pl.loop(0, per_worker // 2)
        def _(j):
            for slot in range(2):
                step = base + 2 * j + slot

                @pl.when(j > 0)
                def _():
                    out_copy(step, slot).wait()

                pltpu.sync_copy(i_hbm.at[pl.ds(step, 1), :], idx_v.at[slot])
                gather_copy(slot).start()
            for slot in range(2):
                gather_copy(slot).wait()
                out_copy(base + 2 * j + slot, slot).start()

        for slot in range(2):
            out_copy(base, slot).wait()

    return gather(x, idx_rows)


def _expert_kernel(be_ref, x_ref, w1_ref, b1_ref, w2_ref, b2_ref, y_ref):
    x = x_ref[...].astype(BF16)
    hdn = jnp.dot(x, w1_ref[0], preferred_element_type=F32) + b1_ref[0]
    gate = jnp.minimum(hdn[:, :D_FF], SWIGLU_LIMIT)
    lin = jnp.clip(hdn[:, D_FF:], -SWIGLU_LIMIT, SWIGLU_LIMIT)
    act = gate * (1.0 / (1.0 + jnp.exp(-SWIGLU_ALPHA * gate))) * (lin + 1.0)
    y_ref[...] = jnp.dot(act.astype(BF16), w2_ref[0], preferred_element_type=F32) + b2_ref[0]


def _experts(block_e, x_pad, w1, b1, w2, b2):
    n_pad = x_pad.shape[0]
    rows = EXPERT_ROWS
    grid_spec = pltpu.PrefetchScalarGridSpec(
        num_scalar_prefetch=1,
        grid=(n_pad // rows,),
        in_specs=[pl.BlockSpec((rows, D_MODEL), lambda i, be: (i, 0)),
                  pl.BlockSpec((1, D_MODEL, 2 * D_FF), lambda i, be: (be[i], 0, 0)),
                  pl.BlockSpec((1, 1, 2 * D_FF), lambda i, be: (be[i], 0, 0)),
                  pl.BlockSpec((1, D_FF, D_MODEL), lambda i, be: (be[i], 0, 0)),
                  pl.BlockSpec((1, 1, D_MODEL), lambda i, be: (be[i], 0, 0))],
        out_specs=pl.BlockSpec((rows, D_MODEL), lambda i, be: (i, 0)),
    )
    return pl.pallas_call(
        _expert_kernel,
        grid_spec=grid_spec,
        out_shape=jax.ShapeDtypeStruct((n_pad, D_MODEL), F32),
        compiler_params=_cparams(("arbitrary",)),
        name="moe_experts",
    )(block_e, x_pad, w1, b1, w2, b2)


def _combine_kernel(x2_ref, gates_ref, gf_ref, y_ref, o_ref):
    gates = gates_ref[...]
    y = x2_ref[...]
    for kk in range(TOP_K):
        y = y + gates[:, kk:kk + 1] * y_ref[kk]
    ms = jnp.mean(y * y, axis=-1, keepdims=True)
    o_ref[...] = y * lax.rsqrt(ms + RMS_EPS) * gf_ref[...]


def _combine(x2, gates, gf, y_sel):
    t = x2.shape[0]
    rows = min(COMBINE_ROWS, t)
    row = lambda i: (i, 0)
    return pl.pallas_call(
        _combine_kernel,
        grid=(t // rows,),
        in_specs=[pl.BlockSpec((rows, D_MODEL), row),
                  pl.BlockSpec((rows, LANES), row),
                  pl.BlockSpec((1, D_MODEL), lambda i: (0, 0)),
                  pl.BlockSpec((TOP_K, rows, D_MODEL), lambda i: (0, i, 0))],
        out_specs=pl.BlockSpec((rows, D_MODEL), row),
        out_shape=jax.ShapeDtypeStruct((t, D_MODEL), F32),
        compiler_params=_cparams(("parallel",)),
        name="moe_combine",
    )(x2, gates, gf, y_sel)


def _dispatch_plan(top_e, n_tok):
    n = n_tok * TOP_K
    blk = EXPERT_ROWS
    flat_e = top_e.reshape(n)
    order = jnp.argsort(flat_e).astype(jnp.int32)
    rank = jnp.argsort(order).astype(jnp.int32)
    st = order // TOP_K
    experts = jnp.arange(N_EXPERTS, dtype=jnp.int32)
    counts = jnp.sum((flat_e[:, None] == experts).astype(jnp.int32), axis=0)
    padded = (counts + blk - 1) // blk * blk
    start = jnp.cumsum(counts) - counts
    pend = jnp.cumsum(padded)
    pstart = pend - padded
    pos = (pstart - start)[flat_e] + rank
    n_blocks = n // blk + N_EXPERTS
    block_start = jnp.arange(n_blocks, dtype=jnp.int32) * blk
    block_e = jnp.minimum(jnp.sum((pend[None, :] <= block_start[:, None]).astype(jnp.int32), axis=1),
                          N_EXPERTS - 1)
    slot = jnp.arange(n_blocks * blk, dtype=jnp.int32)
    slot_e = jnp.repeat(block_e, blk)
    off = slot - pstart[slot_e]
    src = jnp.clip(start[slot_e] + off, 0, n - 1)
    tok_pad = jnp.where(off < counts[slot_e], st[src], 0)
    return tok_pad, pos, block_e


def _trunk_front(x, prm):
    bsz, l, _ = x.shape
    t = bsz * l
    x2d = x.reshape(t, D_MODEL)
    q, k, v, u_t = _in_proj(x2d, prm['norm1_g'], prm['w_in'])

    t_att = min(512, l)
    att = _attention(q.reshape(bsz, l, W_ATT), k.reshape(bsz, l, W_ATT), v.reshape(bsz, l, W_ATT),
                     prm['bias_tiles'][t_att], prm['lam'], prm['subln_g'], t_att)

    yssm = _s5(u_t, prm['ssm_p'], prm['ssm_m'], prm['ssm_r'], prm['ssm_alpha'], l // SSM_CHUNK, bsz)

    x2, h2, ids, gates = _post_mix(x2d, att.reshape(t, W_ATT), yssm, prm['w_glu'], prm['b_glu'],
                                   prm['ssm_norm_g'], prm['w_out'], prm['norm2_g'],
                                   prm['w_router_hi'], prm['w_router_lo'], prm['b_router'])

    tok_pad, pos, block_e = _dispatch_plan(ids[:, :TOP_K], t)
    return {'shape': x.shape, 'x2': x2, 'h2': h2, 'gates': gates,
            'tok_pad': tok_pad, 'pos': pos, 'block_e': block_e}


def _trunk_back(st, prm):
    t = st['x2'].shape[0]
    x_pad = _sc_gather_rows(st['h2'], st['tok_pad'])
    y_pad = _experts(st['block_e'], x_pad, prm['w_moe1'], prm['b_moe1'], prm['w_moe2'], prm['b_moe2'])
    pos_by_k = st['pos'].reshape(t, TOP_K).T.reshape(TOP_K * t)
    y_sel = _sc_gather_rows(y_pad, pos_by_k).reshape(TOP_K, t, D_MODEL)
    out = _combine(st['x2'], st['gates'], prm['normf_g'], y_sel)
    return out.reshape(st['shape'])


def _prepare(seq_lens, rel_bias, norm1_g, w_in, lambda_q1, lambda_k1, lambda_q2, lambda_k2, subln_g,
             ssm_A_re, ssm_A_im, ssm_log_dt, ssm_B_re, ssm_B_im, ssm_C_re, ssm_C_im, ssm_D,
             w_glu, b_glu, ssm_norm_g, w_out, norm2_g, w_router, b_router,
             w_moe1, b_moe1, w_moe2, b_moe2, normf_g):
    layer = 0
    lambda_init = 0.8 - 0.6 * math.exp(-0.3 * layer)
    lam = (jnp.exp(jnp.sum(lambda_q1[layer].astype(F32) * lambda_k1[layer].astype(F32)))
           - jnp.exp(jnp.sum(lambda_q2[layer].astype(F32) * lambda_k2[layer].astype(F32))) + lambda_init)
    p_mat, m_mat, r_mat, alpha = _ssm_matrices(
        ssm_A_re[layer], ssm_A_im[layer], ssm_log_dt[layer], ssm_B_re[layer], ssm_B_im[layer],
        ssm_C_re[layer], ssm_C_im[layer], ssm_D[layer])
    pad_e = LANES - N_EXPERTS
    w_r = jnp.pad(w_router[layer].astype(F32), ((0, 0), (0, pad_e)))
    w_r_hi = w_r.astype(BF16)
    return {
        'norm1_g': norm1_g[layer].reshape(1, D_MODEL).astype(F32),
        'w_in': w_in[layer].astype(BF16),
        'lam': lam.reshape(1).astype(F32),
        'subln_g': (subln_g[layer].astype(F32) * (1.0 - lambda_init)).reshape(1, V_DIM),
        'bias_tiles': {t: _bias_tiles(rel_bias, t) for t in sorted({min(512, l) for l in seq_lens})},
        'ssm_p': p_mat, 'ssm_m': m_mat, 'ssm_r': r_mat, 'ssm_alpha': alpha,
        'w_glu': w_glu[layer].astype(BF16),
        'b_glu': b_glu[layer].reshape(1, W_SSM).astype(F32),
        'ssm_norm_g': ssm_norm_g[layer].reshape(1, W_SSM).astype(F32),
        'w_out': w_out[layer].astype(BF16),
        'norm2_g': norm2_g[layer].reshape(1, D_MODEL).astype(F32),
        'w_router_hi': w_r_hi,
        'w_router_lo': (w_r - w_r_hi.astype(F32)).astype(BF16),
        'b_router': jnp.pad(b_router[layer].astype(F32), (0, pad_e)).reshape(1, LANES),
        'w_moe1': w_moe1[layer].astype(BF16),
        'b_moe1': b_moe1[layer].reshape(N_EXPERTS, 1, 2 * D_FF).astype(F32),
        'w_moe2': w_moe2[layer].astype(BF16),
        'b_moe2': b_moe2[layer].reshape(N_EXPERTS, 1, D_MODEL).astype(F32),
        'normf_g': normf_g.reshape(1, D_MODEL).astype(F32),
    }


def kernel(x_prompt, x_sample, rel_bias, norm1_g, w_in, lambda_q1, lambda_k1, lambda_q2, lambda_k2, subln_g, ssm_A_re, ssm_A_im, ssm_log_dt, ssm_B_re, ssm_B_im, ssm_C_re, ssm_C_im, ssm_D, w_glu, b_glu, ssm_norm_g, w_out, norm2_g, w_router, b_router, w_moe1, b_moe1, w_moe2, b_moe2, normf_g):
    prm = _prepare((x_prompt.shape[1], x_sample.shape[1]), rel_bias, norm1_g, w_in, lambda_q1,
                   lambda_k1, lambda_q2, lambda_k2, subln_g, ssm_A_re, ssm_A_im, ssm_log_dt,
                   ssm_B_re, ssm_B_im, ssm_C_re, ssm_C_im, ssm_D, w_glu, b_glu, ssm_norm_g, w_out,
                   norm2_g, w_router, b_router, w_moe1, b_moe1, w_moe2, b_moe2, normf_g)
    first = _trunk_front(x_prompt, prm)
    x_sample, first['tok_pad'] = lax.optimization_barrier((x_sample, first['tok_pad']))
    second = _trunk_front(x_sample, prm)
    return (_trunk_back(first, prm), _trunk_back(second, prm))
```

```python
import functools
import math

import jax
import jax.numpy as jnp
from jax import lax
from jax.experimental import pallas as pl
from jax.experimental.pallas import tpu as pltpu
from jax.experimental.pallas import tpu_sc as plsc

F32 = jnp.float32
BF16 = jnp.bfloat16

D_MODEL = 1024
W_ATT = 512
W_SSM = 512
HEAD_DIM = 64
N_HEADS = 4
V_DIM = 2 * HEAD_DIM
SSM_GROUP = 16
N_GROUPS = W_SSM // SSM_GROUP
SSM_STATE = 64
IN_WIDTH = 3 * W_ATT + W_SSM
N_BUCKETS = 32
MAX_DISTANCE = 128
N_EXPERTS = 32
TOP_K = 4
D_FF = D_MODEL
SWIGLU_ALPHA = 1.702
SWIGLU_LIMIT = 7.0
RMS_EPS = 1e-6
ATT_SCALE = HEAD_DIM ** -0.5
LOG2E = math.log2(math.e)

LANES = 128
PV_KEYS = 256
SSM_CHUNK = LANES
SSM_ROW = SSM_CHUNK * SSM_GROUP
ROW_TILE = 1024
EXPERT_ROWS = 512
SC_ROWS = 32
SC_CORES = 2
SC_SUBCORES = 16
COMBINE_ROWS = 256
VMEM_LIMIT = 56 * 1024 * 1024


def _cparams(sem):
    return pltpu.CompilerParams(dimension_semantics=sem, vmem_limit_bytes=VMEM_LIMIT)


def _in_proj_kernel(x_ref, g_ref, w_ref, q_ref, k_ref, v_ref, ut_ref):
    x = x_ref[...]
    ms = jnp.mean(x * x, axis=-1, keepdims=True)
    h = (x * lax.rsqrt(ms + RMS_EPS) * g_ref[...]).astype(BF16)
    proj = jnp.dot(h, w_ref[...], preferred_element_type=F32)
    q_ref[...] = (proj[:, 0:W_ATT] * (ATT_SCALE * LOG2E)).astype(BF16)
    k_ref[...] = proj[:, W_ATT:2 * W_ATT].astype(BF16)
    v_ref[...] = proj[:, 2 * W_ATT:3 * W_ATT].astype(BF16)
    ut = proj[:, 3 * W_ATT:].T
    for j in range(ut_ref.shape[1]):
        ut_ref[:, j, :] = ut[:, j * SSM_CHUNK:(j + 1) * SSM_CHUNK]


def _in_proj(x2d, g, w_bf16):
    t = x2d.shape[0]
    tm = min(ROW_TILE, t)
    out = jax.ShapeDtypeStruct((t, W_ATT), BF16)
    row = lambda i: (i, 0)
    return pl.pallas_call(
        _in_proj_kernel,
        grid=(t // tm,),
        in_specs=[pl.BlockSpec((tm, D_MODEL), row),
                  pl.BlockSpec((1, D_MODEL), lambda i: (0, 0)),
                  pl.BlockSpec((D_MODEL, IN_WIDTH), lambda i: (0, 0))],
        out_specs=[pl.BlockSpec((tm, W_ATT), row)] * 3
        + [pl.BlockSpec((W_SSM, tm // SSM_CHUNK, SSM_CHUNK), lambda i: (0, i, 0))],
        out_shape=[out] * 3 + [jax.ShapeDtypeStruct((W_SSM, t // SSM_CHUNK, SSM_CHUNK), F32)],
        compiler_params=_cparams(("parallel",)),
        name="in_proj",
    )(x2d, g, w_bf16)


def _t5_bucket(rel):
    half = N_BUCKETS // 2
    max_exact = half // 2
    ret = jnp.where(rel > 0, half, 0).astype(jnp.int32)
    n = jnp.abs(rel)
    nf = jnp.maximum(n, 1).astype(F32)
    large = max_exact + (jnp.log(nf / max_exact) / math.log(MAX_DISTANCE / max_exact)
                         * (half - max_exact)).astype(jnp.int32)
    large = jnp.minimum(large, half - 1)
    return ret + jnp.where(n < max_exact, n, large)


def _bias_tiles(rel_bias, t):
    i = jnp.arange(t, dtype=jnp.int32)
    d = jnp.arange(-2, 3, dtype=jnp.int32)
    rel = d[:, None, None] * t + i[None, None, :] - i[None, :, None]
    onehot = (_t5_bucket(rel)[..., None] == jnp.arange(N_BUCKETS, dtype=jnp.int32)).astype(F32)
    tiles = jnp.einsum('dqkn,nh->hdqk', onehot, rel_bias.astype(F32), precision=lax.Precision.HIGHEST)
    return tiles * LOG2E


def _attn_kernel(lam_ref, q_ref, k_ref, v_ref, bias_ref, g_ref, o_ref,
                 m_scr, l_scr, acc_scr, sa_scr, sb_scr, mxa_scr, mxb_scr, *, t, sub, n_iter):
    qi = pl.program_id(2)
    q = q_ref[0]
    lane = lax.broadcasted_iota(jnp.int32, q.shape, 1)
    zero = jnp.zeros_like(q)
    qs = (jnp.where(lane < HEAD_DIM, q, zero), jnp.where(lane >= HEAD_DIM, q, zero))
    nb = t // LANES

    m_scr[...] = jnp.full(m_scr.shape, -jnp.inf, F32)
    l_scr[...] = jnp.zeros(l_scr.shape, F32)
    acc_scr[...] = jnp.zeros(acc_scr.shape, F32)

    def scores(j, s_scr, mx_scr):
        for mi in range(2):
            mx = None
            for c in range(sub):
                blk = j * sub + c
                kc = k_ref[0, pl.ds(pl.multiple_of(blk * t, t), t), :]
                s = lax.dot_general(qs[mi], kc, (((1,), (1,)), ((), ())), preferred_element_type=F32)
                s = s + bias_ref[0, jnp.clip(blk - qi, -2, 2) + 2]
                s_scr[mi, :, c * t:(c + 1) * t] = s
                for i in range(nb):
                    piece = s[:, i * LANES:(i + 1) * LANES]
                    mx = piece if mx is None else jnp.maximum(mx, piece)
            mx_scr[mi] = mx

    def accumulate(j, s_scr, mx_scr):
        vj = v_ref[0, pl.ds(pl.multiple_of(j * (sub * t), sub * t), sub * t), :]
        for mi in range(2):
            m_prev = m_scr[mi]
            m_next = jnp.maximum(m_prev, jnp.max(mx_scr[mi], axis=1, keepdims=True))
            alpha = jnp.exp2(m_prev - m_next)
            m_scr[mi] = m_next
            lsum = None
            pv = None
            for i in range(sub * t // PV_KEYS):
                ps = []
                for c in range(PV_KEYS // LANES):
                    lo = i * PV_KEYS + c * LANES
                    p = jnp.exp2(s_scr[mi, :, lo:lo + LANES] - m_next)
                    lsum = p if lsum is None else lsum + p
                    ps.append(p.astype(BF16))
                d = jnp.dot(jnp.concatenate(ps, axis=1), vj[i * PV_KEYS:(i + 1) * PV_KEYS, :],
                            preferred_element_type=F32)
                pv = d if pv is None else pv + d
            l_scr[mi] = alpha * l_scr[mi] + lsum
            acc_scr[mi] = alpha * acc_scr[mi] + pv

    scores(0, sa_scr, mxa_scr)
    n_pairs = (n_iter - 1) // 2

    def body(i, carry):
        scores(2 * i + 1, sb_scr, mxb_scr)
        accumulate(2 * i, sa_scr, mxa_scr)
        scores(2 * i + 2, sa_scr, mxa_scr)
        accumulate(2 * i + 1, sb_scr, mxb_scr)
        return carry

    lax.fori_loop(0, n_pairs, body, 0)
    if (n_iter - 1) % 2 == 1:
        scores(n_iter - 1, sb_scr, mxb_scr)
        accumulate(n_iter - 2, sa_scr, mxa_scr)
        accumulate(n_iter - 1, sb_scr, mxb_scr)
    else:
        accumulate(n_iter - 1, sa_scr, mxa_scr)

    o1 = acc_scr[0] / jnp.sum(l_scr[0], axis=1, keepdims=True)
    o2 = acc_scr[1] / jnp.sum(l_scr[1], axis=1, keepdims=True)
    o = o1 - lam_ref[0] * o2
    ms = jnp.mean(o * o, axis=-1, keepdims=True)
    o_ref[0] = (o * lax.rsqrt(ms + RMS_EPS) * g_ref[...]).astype(BF16)


def _attention(q, k, v, bias_tiles, lam, g_scaled, t):
    b, l, _ = q.shape
    nq = l // t
    sub = 2 if nq % 2 == 0 else 1
    kern = functools.partial(_attn_kernel, t=t, sub=sub, n_iter=nq // sub)
    return pl.pallas_call(
        kern,
        grid=(b, N_HEADS, nq),
        in_specs=[pl.BlockSpec(memory_space=pltpu.SMEM),
                  pl.BlockSpec((1, t, V_DIM), lambda bi, h, qi: (bi, qi, h)),
                  pl.BlockSpec((1, l, V_DIM), lambda bi, h, qi: (bi, 0, h)),
                  pl.BlockSpec((1, l, V_DIM), lambda bi, h, qi: (bi, 0, h)),
                  pl.BlockSpec((1, 5, t, t), lambda bi, h, qi: (h, 0, 0, 0)),
                  pl.BlockSpec((1, V_DIM), lambda bi, h, qi: (0, 0))],
        out_specs=pl.BlockSpec((1, t, V_DIM), lambda bi, h, qi: (bi, qi, h)),
        out_shape=jax.ShapeDtypeStruct((b, l, W_ATT), BF16),
        scratch_shapes=[pltpu.VMEM((2, t, LANES), F32),
                        pltpu.VMEM((2, t, LANES), F32),
                        pltpu.VMEM((2, t, V_DIM), F32),
                        pltpu.VMEM((2, t, sub * t), F32),
                        pltpu.VMEM((2, t, sub * t), F32),
                        pltpu.VMEM((2, t, LANES), F32),
                        pltpu.VMEM((2, t, LANES), F32)],
        compiler_params=_cparams(("parallel", "parallel", "arbitrary")),
        name="diff_attention",
    )(lam, q, k, v, bias_tiles, g_scaled)


def _ssm_matrices(a_re, a_im, log_dt, b_re, b_im, c_re, c_im, d_skip):
    qn, g, p, hc = SSM_CHUNK, N_GROUPS, SSM_STATE, SSM_GROUP
    n = jnp.arange(qn + 1, dtype=F32)
    pw, bbar, cc = [], [], []
    for d in range(2):
        a = lax.complex(a_re[d].astype(F32), a_im[d].astype(F32))
        dt = jnp.exp(log_dt[d].astype(F32))[:, None]
        adt = a * dt
        a_bar = jnp.exp(adt)
        pw.append(jnp.exp(adt[None] * n[:, None, None]))
        bbar.append(((a_bar - 1.0) / a)[:, :, None]
                    * lax.complex(b_re[d].astype(F32), b_im[d].astype(F32)))
        cc.append(lax.complex(c_re[d].astype(F32), c_im[d].astype(F32)))

    hi = lax.Precision.HIGHEST
    kern = [jnp.einsum('gop,tgp,gpi->tgoi', cc[d], pw[d][:qn], bbar[d], precision=hi).real
            for d in range(2)]
    s_idx = jnp.arange(qn)[:, None]
    t_idx = jnp.arange(qn)[None, :]
    lag = t_idx - s_idx
    taus = jnp.arange(qn)
    sel_f = (lag[:, :, None] == taus).astype(F32)
    sel_b = (-lag[:, :, None] == taus).astype(F32)
    kf = jnp.einsum('stu,ugoi->gisot', sel_f, kern[0], precision=hi)
    kb = jnp.einsum('stu,ugoi->gisot', sel_b, kern[1], precision=hi)
    skip = (jnp.eye(qn, dtype=F32)[None, None, :, None, :] * jnp.eye(hc, dtype=F32)[None, :, None, :, None]
            * d_skip.astype(F32).reshape(g, hc)[:, :, None, None, None])
    m_mat = (kf + kb + skip).reshape(g, SSM_ROW, SSM_ROW)

    zeros_p = jnp.zeros((g, SSM_ROW, LANES - p), F32)

    def pad_cols(x):
        return jnp.concatenate([x, zeros_p], axis=-1)

    pf = jnp.einsum('sgp,gpi->gisp', pw[0][:qn][::-1], bbar[0]).reshape(g, SSM_ROW, p)
    pb = jnp.einsum('sgp,gpi->gisp', pw[1][:qn], bbar[1]).reshape(g, SSM_ROW, p)
    p_mat = jnp.concatenate([pad_cols(pf.real), pad_cols(pf.imag),
                             pad_cols(pb.real), pad_cols(pb.imag)], axis=-1)

    wf = jnp.einsum('gop,tgp->gpot', cc[0], pw[0][1:qn + 1]).reshape(g, p, SSM_ROW)
    wb = jnp.einsum('gop,tgp->gpot', cc[1], pw[1][1:qn + 1][::-1]).reshape(g, p, SSM_ROW)
    zeros_r = jnp.zeros((g, LANES - p, SSM_ROW), F32)
    r_mat = jnp.concatenate([wf.real, zeros_r, -wf.imag, zeros_r,
                             wb.real, zeros_r, -wb.imag, zeros_r], axis=1)

    zeros_a = jnp.zeros((g, LANES - p), F32)

    def pad_vec(x):
        return jnp.concatenate([x, zeros_a], axis=-1)

    alpha = jnp.stack([pad_vec(pw[0][qn].real), pad_vec(pw[0][qn].imag),
                       pad_vec(pw[1][qn].real), pad_vec(pw[1][qn].imag)], axis=1)
    return p_mat.astype(BF16), m_mat.astype(BF16), r_mat.astype(BF16), alpha


def _s5_kernel(u_ref, p_ref, m_ref, r_ref, a_ref, y_ref, s_scr, x_scr, *, nc, bsz):
    u = jnp.concatenate([u_ref[h] for h in range(SSM_GROUP)], axis=-1).astype(BF16)
    s = jnp.dot(u, p_ref[0], preferred_element_type=F32)
    for part in range(4):
        s_scr[part] = s[:, part * LANES:(part + 1) * LANES]
    al = a_ref[0]
    afr = jnp.broadcast_to(al[0:1], (bsz, LANES))
    afi = jnp.broadcast_to(al[1:2], (bsz, LANES))
    abr = jnp.broadcast_to(al[2:3], (bsz, LANES))
    abi = jnp.broadcast_to(al[3:4], (bsz, LANES))
    fr = fi = br = bi = jnp.zeros((bsz, LANES), F32)
    for c in range(nc):
        rf = pl.ds(c, bsz, stride=nc)
        rb = pl.ds(nc - 1 - c, bsz, stride=nc)
        x_scr[0, rf, :] = fr
        x_scr[1, rf, :] = fi
        x_scr[2, rb, :] = br
        x_scr[3, rb, :] = bi
        sfr = s_scr[0, rf, :]
        sfi = s_scr[1, rf, :]
        sbr = s_scr[2, rb, :]
        sbi = s_scr[3, rb, :]
        fr, fi = afr * fr - afi * fi + sfr, afr * fi + afi * fr + sfi
        br, bi = abr * br - abi * bi + sbr, abr * bi + abi * br + sbi
    y = jnp.dot(u, m_ref[0], preferred_element_type=F32)
    x_in = jnp.concatenate([x_scr[part] for part in range(4)], axis=-1).astype(BF16)
    y = y + jnp.dot(x_in, r_ref[0], preferred_element_type=F32)
    for h in range(SSM_GROUP):
        y_ref[h] = y[:, h * SSM_CHUNK:(h + 1) * SSM_CHUNK]


def _s5(u_t, p_mat, m_mat, r_mat, alpha, nc, bsz):
    rows = u_t.shape[1]
    kern = functools.partial(_s5_kernel, nc=nc, bsz=bsz)
    seq = pl.BlockSpec((SSM_GROUP, rows, SSM_CHUNK), lambda i: (i, 0, 0))
    return pl.pallas_call(
        kern,
        grid=(N_GROUPS,),
        in_specs=[seq,
                  pl.BlockSpec((1, SSM_ROW, 4 * LANES), lambda i: (i, 0, 0)),
                  pl.BlockSpec((1, SSM_ROW, SSM_ROW), lambda i: (i, 0, 0)),
                  pl.BlockSpec((1, 4 * LANES, SSM_ROW), lambda i: (i, 0, 0)),
                  pl.BlockSpec((1, 4, LANES), lambda i: (i, 0, 0))],
        out_specs=seq,
        out_shape=jax.ShapeDtypeStruct(u_t.shape, F32),
        scratch_shapes=[pltpu.VMEM((4, rows, LANES), F32), pltpu.VMEM((4, rows, LANES), F32)],
        compiler_params=_cparams(("parallel",)),
        name="s5_scan",
    )(u_t, p_mat, m_mat, r_mat, alpha)


def _post_mix_kernel(x_ref, att_ref, y_ref, wglu_ref, bglu_ref, gs_ref, wout_ref, g2_ref,
                     wrh_ref, wrl_ref, br_ref, x2_ref, h2_ref, ids_ref, gates_ref):
    y = jnp.concatenate([y_ref[:, j, :].T for j in range(y_ref.shape[1])], axis=0)
    y = 0.5 * y * (1.0 + jnp.tanh(math.sqrt(2.0 / math.pi) * (y + 0.044715 * (y * y * y))))
    z = jnp.dot(y.astype(BF16), wglu_ref[...], preferred_element_type=F32) + bglu_ref[...]
    y = y * (1.0 / (1.0 + jnp.exp(-z)))
    ms = jnp.mean(y * y, axis=-1, keepdims=True)
    ssm = (y * lax.rsqrt(ms + RMS_EPS) * gs_ref[...]).astype(BF16)
    mix = jnp.dot(att_ref[...], wout_ref[0:W_ATT, :], preferred_element_type=F32)
    mix = mix + jnp.dot(ssm, wout_ref[W_ATT:, :], preferred_element_type=F32)
    x2 = x_ref[...] + mix
    x2_ref[...] = x2
    ms2 = jnp.mean(x2 * x2, axis=-1, keepdims=True)
    h2 = x2 * lax.rsqrt(ms2 + RMS_EPS) * g2_ref[...]
    h2_ref[...] = h2
    h_hi = h2.astype(BF16)
    h_lo = (h2 - h_hi.astype(F32)).astype(BF16)
    logits = (jnp.dot(h_hi, wrh_ref[...], preferred_element_type=F32)
              + jnp.dot(h_lo, wrh_ref[...], preferred_element_type=F32)
              + jnp.dot(h_hi, wrl_ref[...], preferred_element_type=F32)) + br_ref[...]
    lane = lax.broadcasted_iota(jnp.int32, logits.shape, 1).astype(F32)
    neg = jnp.float32(-jnp.inf)
    cur = jnp.where(lane < N_EXPERTS, logits, neg)
    ids = jnp.zeros(logits.shape, F32)
    vals = jnp.zeros(logits.shape, F32)
    top = None
    den = None
    for kk in range(TOP_K):
        mx = jnp.max(cur, axis=1, keepdims=True)
        idx = jnp.min(jnp.where(cur == mx, lane, float(LANES)), axis=1, keepdims=True)
        if kk == 0:
            top = mx
        e = jnp.exp(mx - top)
        den = e if den is None else den + e
        ids = jnp.where(lane == kk, idx, ids)
        vals = jnp.where(lane == kk, e, vals)
        cur = jnp.where(lane == idx, neg, cur)
    ids_ref[...] = ids.astype(jnp.int32)
    gates_ref[...] = vals / den


def _post_mix(x2d, att, yssm, wglu, bglu, gs, wout, g2, wr_hi, wr_lo, br):
    t = x2d.shape[0]
    tm = min(ROW_TILE, t)
    row = lambda i: (i, 0)
    const = lambda i: (0, 0)
    return pl.pallas_call(
        _post_mix_kernel,
        grid=(t // tm,),
        in_specs=[pl.BlockSpec((tm, D_MODEL), row),
                  pl.BlockSpec((tm, W_ATT), row),
                  pl.BlockSpec((W_SSM, tm // SSM_CHUNK, SSM_CHUNK), lambda i: (0, i, 0)),
                  pl.BlockSpec((W_SSM, W_SSM), const),
                  pl.BlockSpec((1, W_SSM), const),
                  pl.BlockSpec((1, W_SSM), const),
                  pl.BlockSpec((D_MODEL, D_MODEL), const),
                  pl.BlockSpec((1, D_MODEL), const),
                  pl.BlockSpec((D_MODEL, LANES), const),
                  pl.BlockSpec((D_MODEL, LANES), const),
                  pl.BlockSpec((1, LANES), const)],
        out_specs=[pl.BlockSpec((tm, D_MODEL), row),
                   pl.BlockSpec((tm, D_MODEL), row),
                   pl.BlockSpec((tm, LANES), row),
                   pl.BlockSpec((tm, LANES), row)],
        out_shape=[jax.ShapeDtypeStruct((t, D_MODEL), F32),
                   jax.ShapeDtypeStruct((t, D_MODEL), F32),
                   jax.ShapeDtypeStruct((t, LANES), jnp.int32),
                   jax.ShapeDtypeStruct((t, LANES), F32)],
        compiler_params=_cparams(("parallel",)),
        name="post_mix",
    )(x2d, att, yssm, wglu, bglu, gs, wout, g2, wr_hi, wr_lo, br)


def _sc_gather_rows(x, idx):
    n = idx.shape[0]
    d = x.shape[1]
    steps = n // SC_ROWS
    workers = SC_CORES * SC_SUBCORES
    assert steps % (2 * workers) == 0, (n, steps)
    per_worker = steps // workers
    idx_rows = jnp.pad(idx.reshape(steps, SC_ROWS), ((0, 0), (0, LANES - SC_ROWS)))
    mesh = plsc.VectorSubcoreMesh(core_axis_name="core", subcore_axis_name="subcore")

    @pl.kernel(out_type=jax.ShapeDtypeStruct((n, d), x.dtype), mesh=mesh,
               scratch_types=[pltpu.VMEM((2, 1, LANES), jnp.int32),
                              pltpu.VMEM((2, SC_ROWS, d), x.dtype),
                              pltpu.SemaphoreType.DMA((2,)),
                              pltpu.SemaphoreType.DMA((2,))])
    def gather(x_hbm, i_hbm, o_hbm, idx_v, buf, sem_in, sem_out):
        base = (lax.axis_index("core") * SC_SUBCORES + lax.axis_index("subcore")) * per_worker

        def gather_copy(slot):
            return pltpu.make_async_copy(x_hbm.at[idx_v.at[slot, 0, pl.ds(0, SC_ROWS)]], buf.at[slot],
                                         sem_in.at[slot])

        def out_copy(step, slot):
            return pltpu.make_async_copy(buf.at[slot], o_hbm.at[pl.ds(step * SC_ROWS, SC_ROWS), :],
                                         sem_out.at[slot])

        @pl.loop(0, per_worker // 2)
        def _(j):
            for slot in range(2):
                step = base + 2 * j + slot

                @pl.when(j > 0)
                def _():
                    out_copy(step, slot).wait()

                pltpu.sync_copy(i_hbm.at[pl.ds(step, 1), :], idx_v.at[slot])
                gather_copy(slot).start()
            for slot in range(2):
                gather_copy(slot).wait()
                out_copy(base + 2 * j + slot, slot).start()

        for slot in range(2):
            out_copy(base, slot).wait()

    return gather(x, idx_rows)


def _expert_kernel(be_ref, x_ref, w1_ref, b1_ref, w2_ref, b2_ref, y_ref, w1_bf, w2_bf):
    i = pl.program_id(0)

    @pl.when(jnp.logical_or(i == 0, be_ref[i] != be_ref[jnp.maximum(i - 1, 0)]))
    def _():
        w1_bf[...] = w1_ref[0].astype(BF16)
        w2_bf[...] = w2_ref[0].astype(BF16)

    x = x_ref[...].astype(BF16)
    hdn = jnp.dot(x, w1_bf[...], preferred_element_type=F32) + b1_ref[0]
    gate = jnp.minimum(hdn[:, :D_FF], SWIGLU_LIMIT)
    lin = jnp.clip(hdn[:, D_FF:], -SWIGLU_LIMIT, SWIGLU_LIMIT)
    act = gate * (1.0 / (1.0 + jnp.exp(-SWIGLU_ALPHA * gate))) * (lin + 1.0)
    y_ref[...] = jnp.dot(act.astype(BF16), w2_bf[...], preferred_element_type=F32) + b2_ref[0]


def _experts(block_e, x_pad, w1, b1, w2, b2):
    n_pad = x_pad.shape[0]
    rows = EXPERT_ROWS
    grid_spec = pltpu.PrefetchScalarGridSpec(
        num_scalar_prefetch=1,
        grid=(n_pad // rows,),
        in_specs=[pl.BlockSpec((rows, D_MODEL), lambda i, be: (i, 0)),
                  pl.BlockSpec((1, D_MODEL, 2 * D_FF), lambda i, be: (be[i], 0, 0)),
                  pl.BlockSpec((1, 1, 2 * D_FF), lambda i, be: (be[i], 0, 0)),
                  pl.BlockSpec((1, D_FF, D_MODEL), lambda i, be: (be[i], 0, 0)),
                  pl.BlockSpec((1, 1, D_MODEL), lambda i, be: (be[i], 0, 0))],
        out_specs=pl.BlockSpec((rows, D_MODEL), lambda i, be: (i, 0)),
        scratch_shapes=[pltpu.VMEM((D_MODEL, 2 * D_FF), BF16), pltpu.VMEM((D_FF, D_MODEL), BF16)],
    )
    return pl.pallas_call(
        _expert_kernel,
        grid_spec=grid_spec,
        out_shape=jax.ShapeDtypeStruct((n_pad, D_MODEL), F32),
        compiler_params=_cparams(("arbitrary",)),
        name="moe_experts",
    )(block_e, x_pad, w1, b1, w2, b2)


def _combine_kernel(x2_ref, gates_ref, gf_ref, y_ref, o_ref):
    gates = gates_ref[...]
    y = x2_ref[...]
    for kk in range(TOP_K):
        y = y + gates[:, kk:kk + 1] * y_ref[kk]
    ms = jnp.mean(y * y, axis=-1, keepdims=True)
    o_ref[...] = y * lax.rsqrt(ms + RMS_EPS) * gf_ref[...]


def _combine(x2, gates, gf, y_sel):
    t = x2.shape[0]
    rows = min(COMBINE_ROWS, t)
    row = lambda i: (i, 0)
    return pl.pallas_call(
        _combine_kernel,
        grid=(t // rows,),
        in_specs=[pl.BlockSpec((rows, D_MODEL), row),
                  pl.BlockSpec((rows, LANES), row),
                  pl.BlockSpec((1, D_MODEL), lambda i: (0, 0)),
                  pl.BlockSpec((TOP_K, rows, D_MODEL), lambda i: (0, i, 0))],
        out_specs=pl.BlockSpec((rows, D_MODEL), row),
        out_shape=jax.ShapeDtypeStruct((t, D_MODEL), F32),
        compiler_params=_cparams(("parallel",)),
        name="moe_combine",
    )(x2, gates, gf, y_sel)


def _dispatch_plan(top_e, n_tok):
    n = n_tok * TOP_K
    blk = EXPERT_ROWS
    flat_e = top_e.reshape(n)
    order = jnp.argsort(flat_e).astype(jnp.int32)
    rank = jnp.argsort(order).astype(jnp.int32)
    st = order // TOP_K
    experts = jnp.arange(N_EXPERTS, dtype=jnp.int32)
    counts = jnp.sum((flat_e[:, None] == experts).astype(jnp.int32), axis=0)
    padded = (counts + blk - 1) // blk * blk
    start = jnp.cumsum(counts) - counts
    pend = jnp.cumsum(padded)
    pstart = pend - padded
    pos = (pstart - start)[flat_e] + rank
    n_blocks = n // blk + N_EXPERTS
    block_start = jnp.arange(n_blocks, dtype=jnp.int32) * blk
    block_e = jnp.minimum(jnp.sum((pend[None, :] <= block_start[:, None]).astype(jnp.int32), axis=1),
                          N_EXPERTS - 1)
    slot = jnp.arange(n_blocks * blk, dtype=jnp.int32)
    slot_e = jnp.repeat(block_e, blk)
    off = slot - pstart[slot_e]
    src = jnp.clip(start[slot_e] + off, 0, n - 1)
    tok_pad = jnp.where(off < counts[slot_e], st[src], 0)
    return tok_pad, pos, block_e


def _trunk_front(x, prm):
    bsz, l, _ = x.shape
    t = bsz * l
    x2d = x.reshape(t, D_MODEL)
    q, k, v, u_t = _in_proj(x2d, prm['norm1_g'], prm['w_in'])

    t_att = min(512, l)
    att = _attention(q.reshape(bsz, l, W_ATT), k.reshape(bsz, l, W_ATT), v.reshape(bsz, l, W_ATT),
                     prm['bias_tiles'][t_att], prm['lam'], prm['subln_g'], t_att)

    yssm = _s5(u_t, prm['ssm_p'], prm['ssm_m'], prm['ssm_r'], prm['ssm_alpha'], l // SSM_CHUNK, bsz)

    x2, h2, ids, gates = _post_mix(x2d, att.reshape(t, W_ATT), yssm, prm['w_glu'], prm['b_glu'],
                                   prm['ssm_norm_g'], prm['w_out'], prm['norm2_g'],
                                   prm['w_router_hi'], prm['w_router_lo'], prm['b_router'])

    tok_pad, pos, block_e = _dispatch_plan(ids[:, :TOP_K], t)
    return {'shape': x.shape, 'x2': x2, 'h2': h2, 'gates': gates,
            'tok_pad': tok_pad, 'pos': pos, 'block_e': block_e}


def _trunk_back(st, x_pad, prm):
    t = st['x2'].shape[0]
    y_pad = _experts(st['block_e'], x_pad, prm['w_moe1'], prm['b_moe1'], prm['w_moe2'], prm['b_moe2'])
    pos_by_k = st['pos'].reshape(t, TOP_K).T.reshape(TOP_K * t)
    y_sel = _sc_gather_rows(y_pad, pos_by_k).reshape(TOP_K, t, D_MODEL)
    out = _combine(st['x2'], st['gates'], prm['normf_g'], y_sel)
    return out.reshape(st['shape'])


def _prepare(seq_lens, rel_bias, norm1_g, w_in, lambda_q1, lambda_k1, lambda_q2, lambda_k2, subln_g,
             ssm_A_re, ssm_A_im, ssm_log_dt, ssm_B_re, ssm_B_im, ssm_C_re, ssm_C_im, ssm_D,
             w_glu, b_glu, ssm_norm_g, w_out, norm2_g, w_router, b_router,
             w_moe1, b_moe1, w_moe2, b_moe2, normf_g):
    layer = 0
    lambda_init = 0.8 - 0.6 * math.exp(-0.3 * layer)
    lam = (jnp.exp(jnp.sum(lambda_q1[layer].astype(F32) * lambda_k1[layer].astype(F32)))
           - jnp.exp(jnp.sum(lambda_q2[layer].astype(F32) * lambda_k2[layer].astype(F32))) + lambda_init)
    p_mat, m_mat, r_mat, alpha = _ssm_matrices(
        ssm_A_re[layer], ssm_A_im[layer], ssm_log_dt[layer], ssm_B_re[layer], ssm_B_im[layer],
        ssm_C_re[layer], ssm_C_im[layer], ssm_D[layer])
    pad_e = LANES - N_EXPERTS
    w_r = jnp.pad(w_router[layer].astype(F32), ((0, 0), (0, pad_e)))
    w_r_hi = w_r.astype(BF16)
    return {
        'norm1_g': norm1_g[layer].reshape(1, D_MODEL).astype(F32),
        'w_in': w_in[layer].astype(BF16),
        'lam': lam.reshape(1).astype(F32),
        'subln_g': (subln_g[layer].astype(F32) * (1.0 - lambda_init)).reshape(1, V_DIM),
        'bias_tiles': {t: _bias_tiles(rel_bias, t) for t in sorted({min(512, l) for l in seq_lens})},
        'ssm_p': p_mat, 'ssm_m': m_mat, 'ssm_r': r_mat, 'ssm_alpha': alpha,
        'w_glu': w_glu[layer].astype(BF16),
        'b_glu': b_glu[layer].reshape(1, W_SSM).astype(F32),
        'ssm_norm_g': ssm_norm_g[layer].reshape(1, W_SSM).astype(F32),
        'w_out': w_out[layer].astype(BF16),
        'norm2_g': norm2_g[layer].reshape(1, D_MODEL).astype(F32),
        'w_router_hi': w_r_hi,
        'w_router_lo': (w_r - w_r_hi.astype(F32)).astype(BF16),
        'b_router': jnp.pad(b_router[layer].astype(F32), (0, pad_e)).reshape(1, LANES),
        'w_moe1': w_moe1[layer].astype(F32),
        'b_moe1': b_moe1[layer].reshape(N_EXPERTS, 1, 2 * D_FF).astype(F32),
        'w_moe2': w_moe2[layer].astype(F32),
        'b_moe2': b_moe2[layer].reshape(N_EXPERTS, 1, D_MODEL).astype(F32),
        'normf_g': normf_g.reshape(1, D_MODEL).astype(F32),
    }


def kernel(x_prompt, x_sample, rel_bias, norm1_g, w_in, lambda_q1, lambda_k1, lambda_q2, lambda_k2, subln_g, ssm_A_re, ssm_A_im, ssm_log_dt, ssm_B_re, ssm_B_im, ssm_C_re, ssm_C_im, ssm_D, w_glu, b_glu, ssm_norm_g, w_out, norm2_g, w_router, b_router, w_moe1, b_moe1, w_moe2, b_moe2, normf_g):
    prm = _prepare((x_prompt.shape[1], x_sample.shape[1]), rel_bias, norm1_g, w_in, lambda_q1,
                   lambda_k1, lambda_q2, lambda_k2, subln_g, ssm_A_re, ssm_A_im, ssm_log_dt,
                   ssm_B_re, ssm_B_im, ssm_C_re, ssm_C_im, ssm_D, w_glu, b_glu, ssm_norm_g, w_out,
                   norm2_g, w_router, b_router, w_moe1, b_moe1, w_moe2, b_moe2, normf_g)
    first = _trunk_front(x_prompt, prm)
    x_sample, tok_pad = lax.optimization_barrier((x_sample, first['tok_pad']))
    x_pad_first = _sc_gather_rows(first['h2'], tok_pad)
    second = _trunk_front(x_sample, prm)
    x_pad_second = _sc_gather_rows(second['h2'], second['tok_pad'])
    return (_trunk_back(first, x_pad_first, prm), _trunk_back(second, x_pad_second, prm))
```

```python
import functools
import math

import jax
import jax.numpy as jnp
from jax import lax
from jax.experimental import pallas as pl
from jax.experimental.pallas import tpu as pltpu
from jax.experimental.pallas import tpu_sc as plsc

F32 = jnp.float32
BF16 = jnp.bfloat16

D_MODEL = 1024
W_ATT = 512
W_SSM = 512
HEAD_DIM = 64
N_HEADS = 4
V_DIM = 2 * HEAD_DIM
SSM_GROUP = 16
N_GROUPS = W_SSM // SSM_GROUP
SSM_STATE = 64
IN_WIDTH = 3 * W_ATT + W_SSM
N_BUCKETS = 32
MAX_DISTANCE = 128
N_EXPERTS = 32
TOP_K = 4
D_FF = D_MODEL
SWIGLU_ALPHA = 1.702
SWIGLU_LIMIT = 7.0
RMS_EPS = 1e-6
ATT_SCALE = HEAD_DIM ** -0.5
LOG2E = math.log2(math.e)

LANES = 128
PV_KEYS = 256
SSM_CHUNK = LANES
SSM_ROW = SSM_CHUNK * SSM_GROUP
ROW_TILE = 1024
EXPERT_ROWS = 512
SC_ROWS = 32
SC_CORES = 2
SC_SUBCORES = 16
COMBINE_ROWS = 256
VMEM_LIMIT = 56 * 1024 * 1024


def _cparams(sem):
    return pltpu.CompilerParams(dimension_semantics=sem, vmem_limit_bytes=VMEM_LIMIT)


def _in_proj_kernel(x_ref, g_ref, w_ref, q_ref, k_ref, v_ref, ut_ref):
    x = x_ref[...]
    ms = jnp.mean(x * x, axis=-1, keepdims=True)
    h = (x * lax.rsqrt(ms + RMS_EPS) * g_ref[...]).astype(BF16)
    proj = jnp.dot(h, w_ref[...], preferred_element_type=F32)
    q_ref[...] = (proj[:, 0:W_ATT] * (ATT_SCALE * LOG2E)).astype(BF16)
    k_ref[...] = proj[:, W_ATT:2 * W_ATT].astype(BF16)
    v_ref[...] = proj[:, 2 * W_ATT:3 * W_ATT].astype(BF16)
    ut = proj[:, 3 * W_ATT:].T
    for j in range(ut_ref.shape[1]):
        ut_ref[:, j, :] = ut[:, j * SSM_CHUNK:(j + 1) * SSM_CHUNK]


def _in_proj(x2d, g, w_bf16):
    t = x2d.shape[0]
    tm = min(ROW_TILE, t)
    out = jax.ShapeDtypeStruct((t, W_ATT), BF16)
    row = lambda i: (i, 0)
    return pl.pallas_call(
        _in_proj_kernel,
        grid=(t // tm,),
        in_specs=[pl.BlockSpec((tm, D_MODEL), row),
                  pl.BlockSpec((1, D_MODEL), lambda i: (0, 0)),
                  pl.BlockSpec((D_MODEL, IN_WIDTH), lambda i: (0, 0))],
        out_specs=[pl.BlockSpec((tm, W_ATT), row)] * 3
        + [pl.BlockSpec((W_SSM, tm // SSM_CHUNK, SSM_CHUNK), lambda i: (0, i, 0))],
        out_shape=[out] * 3 + [jax.ShapeDtypeStruct((W_SSM, t // SSM_CHUNK, SSM_CHUNK), F32)],
        compiler_params=_cparams(("parallel",)),
        name="in_proj",
    )(x2d, g, w_bf16)


def _t5_bucket(rel):
    half = N_BUCKETS // 2
    max_exact = half // 2
    ret = jnp.where(rel > 0, half, 0).astype(jnp.int32)
    n = jnp.abs(rel)
    nf = jnp.maximum(n, 1).astype(F32)
    large = max_exact + (jnp.log(nf / max_exact) / math.log(MAX_DISTANCE / max_exact)
                         * (half - max_exact)).astype(jnp.int32)
    large = jnp.minimum(large, half - 1)
    return ret + jnp.where(n < max_exact, n, large)


def _bias_tiles(rel_bias, t):
    i = jnp.arange(t, dtype=jnp.int32)
    d = jnp.arange(-2, 3, dtype=jnp.int32)
    rel = d[:, None, None] * t + i[None, None, :] - i[None, :, None]
    onehot = (_t5_bucket(rel)[..., None] == jnp.arange(N_BUCKETS, dtype=jnp.int32)).astype(F32)
    tiles = jnp.einsum('dqkn,nh->hdqk', onehot, rel_bias.astype(F32), precision=lax.Precision.HIGHEST)
    return tiles * LOG2E


def _attn_kernel(lam_ref, q_ref, k_ref, v_ref, bias_ref, g_ref, o_ref,
                 m_scr, l_scr, acc_scr, sa_scr, sb_scr, mxa_scr, mxb_scr, *, t, sub, n_iter):
    qi = pl.program_id(2)
    q = q_ref[0]
    lane = lax.broadcasted_iota(jnp.int32, q.shape, 1)
    zero = jnp.zeros_like(q)
    qs = (jnp.where(lane < HEAD_DIM, q, zero), jnp.where(lane >= HEAD_DIM, q, zero))
    nb = t // LANES

    m_scr[...] = jnp.full(m_scr.shape, -jnp.inf, F32)
    l_scr[...] = jnp.zeros(l_scr.shape, F32)
    acc_scr[...] = jnp.zeros(acc_scr.shape, F32)

    def scores(j, s_scr, mx_scr):
        for mi in range(2):
            mx = None
            for c in range(sub):
                blk = j * sub + c
                kc = k_ref[0, pl.ds(pl.multiple_of(blk * t, t), t), :]
                s = lax.dot_general(qs[mi], kc, (((1,), (1,)), ((), ())), preferred_element_type=F32)
                s = s + bias_ref[0, jnp.clip(blk - qi, -2, 2) + 2]
                s_scr[mi, :, c * t:(c + 1) * t] = s
                for i in range(nb):
                    piece = s[:, i * LANES:(i + 1) * LANES]
                    mx = piece if mx is None else jnp.maximum(mx, piece)
            mx_scr[mi] = mx

    def accumulate(j, s_scr, mx_scr):
        vj = v_ref[0, pl.ds(pl.multiple_of(j * (sub * t), sub * t), sub * t), :]
        for mi in range(2):
            m_prev = m_scr[mi]
            m_next = jnp.maximum(m_prev, jnp.max(mx_scr[mi], axis=1, keepdims=True))
            alpha = jnp.exp2(m_prev - m_next)
            m_scr[mi] = m_next
            lsum = None
            pv = None
            for i in range(sub * t // PV_KEYS):
                ps = []
                for c in range(PV_KEYS // LANES):
                    lo = i * PV_KEYS + c * LANES
                    p = jnp.exp2(s_scr[mi, :, lo:lo + LANES] - m_next)
                    lsum = p if lsum is None else lsum + p
                    ps.append(p.astype(BF16))
                d = jnp.dot(jnp.concatenate(ps, axis=1), vj[i * PV_KEYS:(i + 1) * PV_KEYS, :],
                            preferred_element_type=F32)
                pv = d if pv is None else pv + d
            l_scr[mi] = alpha * l_scr[mi] + lsum
            acc_scr[mi] = alpha * acc_scr[mi] + pv

    scores(0, sa_scr, mxa_scr)
    n_pairs = (n_iter - 1) // 2

    def body(i, carry):
        scores(2 * i + 1, sb_scr, mxb_scr)
        accumulate(2 * i, sa_scr, mxa_scr)
        scores(2 * i + 2, sa_scr, mxa_scr)
        accumulate(2 * i + 1, sb_scr, mxb_scr)
        return carry

    lax.fori_loop(0, n_pairs, body, 0)
    if (n_iter - 1) % 2 == 1:
        scores(n_iter - 1, sb_scr, mxb_scr)
        accumulate(n_iter - 2, sa_scr, mxa_scr)
        accumulate(n_iter - 1, sb_scr, mxb_scr)
    else:
        accumulate(n_iter - 1, sa_scr, mxa_scr)

    o1 = acc_scr[0] / jnp.sum(l_scr[0], axis=1, keepdims=True)
    o2 = acc_scr[1] / jnp.sum(l_scr[1], axis=1, keepdims=True)
    o = o1 - lam_ref[0] * o2
    ms = jnp.mean(o * o, axis=-1, keepdims=True)
    o_ref[0] = (o * lax.rsqrt(ms + RMS_EPS) * g_ref[...]).astype(BF16)


def _attention(q, k, v, bias_tiles, lam, g_scaled, t):
    b, l, _ = q.shape
    nq = l // t
    sub = 2 if nq % 2 == 0 else 1
    kern = functools.partial(_attn_kernel, t=t, sub=sub, n_iter=nq // sub)
    return pl.pallas_call(
        kern,
        grid=(b, N_HEADS, nq),
        in_specs=[pl.BlockSpec(memory_space=pltpu.SMEM),
                  pl.BlockSpec((1, t, V_DIM), lambda bi, h, qi: (bi, qi, h)),
                  pl.BlockSpec((1, l, V_DIM), lambda bi, h, qi: (bi, 0, h)),
                  pl.BlockSpec((1, l, V_DIM), lambda bi, h, qi: (bi, 0, h)),
                  pl.BlockSpec((1, 5, t, t), lambda bi, h, qi: (h, 0, 0, 0)),
                  pl.BlockSpec((1, V_DIM), lambda bi, h, qi: (0, 0))],
        out_specs=pl.BlockSpec((1, t, V_DIM), lambda bi, h, qi: (bi, qi, h)),
        out_shape=jax.ShapeDtypeStruct((b, l, W_ATT), BF16),
        scratch_shapes=[pltpu.VMEM((2, t, LANES), F32),
                        pltpu.VMEM((2, t, LANES), F32),
                        pltpu.VMEM((2, t, V_DIM), F32),
                        pltpu.VMEM((2, t, sub * t), F32),
                        pltpu.VMEM((2, t, sub * t), F32),
                        pltpu.VMEM((2, t, LANES), F32),
                        pltpu.VMEM((2, t, LANES), F32)],
        compiler_params=_cparams(("parallel", "parallel", "arbitrary")),
        name="diff_attention",
    )(lam, q, k, v, bias_tiles, g_scaled)


def _ssm_matrices(a_re, a_im, log_dt, b_re, b_im, c_re, c_im, d_skip):
    qn, g, p, hc = SSM_CHUNK, N_GROUPS, SSM_STATE, SSM_GROUP
    n = jnp.arange(qn + 1, dtype=F32)
    pw, bbar, cc = [], [], []
    for d in range(2):
        a = lax.complex(a_re[d].astype(F32), a_im[d].astype(F32))
        dt = jnp.exp(log_dt[d].astype(F32))[:, None]
        adt = a * dt
        a_bar = jnp.exp(adt)
        pw.append(jnp.exp(adt[None] * n[:, None, None]))
        bbar.append(((a_bar - 1.0) / a)[:, :, None]
                    * lax.complex(b_re[d].astype(F32), b_im[d].astype(F32)))
        cc.append(lax.complex(c_re[d].astype(F32), c_im[d].astype(F32)))

    hi = lax.Precision.HIGHEST
    kern = [jnp.einsum('gop,tgp,gpi->tgoi', cc[d], pw[d][:qn], bbar[d], precision=hi).real
            for d in range(2)]
    s_idx = jnp.arange(qn)[:, None]
    t_idx = jnp.arange(qn)[None, :]
    lag = t_idx - s_idx
    taus = jnp.arange(qn)
    sel_f = (lag[:, :, None] == taus).astype(F32)
    sel_b = (-lag[:, :, None] == taus).astype(F32)
    kf = jnp.einsum('stu,ugoi->gisot', sel_f, kern[0], precision=hi)
    kb = jnp.einsum('stu,ugoi->gisot', sel_b, kern[1], precision=hi)
    skip = (jnp.eye(qn, dtype=F32)[None, None, :, None, :] * jnp.eye(hc, dtype=F32)[None, :, None, :, None]
            * d_skip.astype(F32).reshape(g, hc)[:, :, None, None, None])
    m_mat = (kf + kb + skip).reshape(g, SSM_ROW, SSM_ROW)

    zeros_p = jnp.zeros((g, SSM_ROW, LANES - p), F32)

    def pad_cols(x):
        return jnp.concatenate([x, zeros_p], axis=-1)

    pf = jnp.einsum('sgp,gpi->gisp', pw[0][:qn][::-1], bbar[0]).reshape(g, SSM_ROW, p)
    pb = jnp.einsum('sgp,gpi->gisp', pw[1][:qn], bbar[1]).reshape(g, SSM_ROW, p)
    p_mat = jnp.concatenate([pad_cols(pf.real), pad_cols(pf.imag),
                             pad_cols(pb.real), pad_cols(pb.imag)], axis=-1)

    wf = jnp.einsum('gop,tgp->gpot', cc[0], pw[0][1:qn + 1]).reshape(g, p, SSM_ROW)
    wb = jnp.einsum('gop,tgp->gpot', cc[1], pw[1][1:qn + 1][::-1]).reshape(g, p, SSM_ROW)
    zeros_r = jnp.zeros((g, LANES - p, SSM_ROW), F32)
    r_mat = jnp.concatenate([wf.real, zeros_r, -wf.imag, zeros_r,
                             wb.real, zeros_r, -wb.imag, zeros_r], axis=1)

    zeros_a = jnp.zeros((g, LANES - p), F32)

    def pad_vec(x):
        return jnp.concatenate([x, zeros_a], axis=-1)

    alpha = jnp.stack([pad_vec(pw[0][qn].real), pad_vec(pw[0][qn].imag),
                       pad_vec(pw[1][qn].real), pad_vec(pw[1][qn].imag)], axis=1)
    return p_mat.astype(BF16), m_mat.astype(BF16), r_mat.astype(BF16), alpha


def _s5_kernel(u_ref, p_ref, m_ref, r_ref, a_ref, y_ref, s_scr, x_scr, *, nc, bsz):
    u = jnp.concatenate([u_ref[h] for h in range(SSM_GROUP)], axis=-1).astype(BF16)
    s = jnp.dot(u, p_ref[0], preferred_element_type=F32)
    for part in range(4):
        s_scr[part] = s[:, part * LANES:(part + 1) * LANES]
    al = a_ref[0]
    afr = jnp.broadcast_to(al[0:1], (bsz, LANES))
    afi = jnp.broadcast_to(al[1:2], (bsz, LANES))
    abr = jnp.broadcast_to(al[2:3], (bsz, LANES))
    abi = jnp.broadcast_to(al[3:4], (bsz, LANES))
    fr = fi = br = bi = jnp.zeros((bsz, LANES), F32)
    for c in range(nc):
        rf = pl.ds(c, bsz, stride=nc)
        rb = pl.ds(nc - 1 - c, bsz, stride=nc)
        x_scr[0, rf, :] = fr
        x_scr[1, rf, :] = fi
        x_scr[2, rb, :] = br
        x_scr[3, rb, :] = bi
        sfr = s_scr[0, rf, :]
        sfi = s_scr[1, rf, :]
        sbr = s_scr[2, rb, :]
        sbi = s_scr[3, rb, :]
        fr, fi = afr * fr - afi * fi + sfr, afr * fi + afi * fr + sfi
        br, bi = abr * br - abi * bi + sbr, abr * bi + abi * br + sbi
    y = jnp.dot(u, m_ref[0], preferred_element_type=F32)
    x_in = jnp.concatenate([x_scr[part] for part in range(4)], axis=-1).astype(BF16)
    y = y + jnp.dot(x_in, r_ref[0], preferred_element_type=F32)
    for h in range(SSM_GROUP):
        y_ref[h] = y[:, h * SSM_CHUNK:(h + 1) * SSM_CHUNK]


def _s5(u_t, p_mat, m_mat, r_mat, alpha, nc, bsz):
    rows = u_t.shape[1]
    kern = functools.partial(_s5_kernel, nc=nc, bsz=bsz)
    seq = pl.BlockSpec((SSM_GROUP, rows, SSM_CHUNK), lambda i: (i, 0, 0))
    return pl.pallas_call(
        kern,
        grid=(N_GROUPS,),
        in_specs=[seq,
                  pl.BlockSpec((1, SSM_ROW, 4 * LANES), lambda i: (i, 0, 0)),
                  pl.BlockSpec((1, SSM_ROW, SSM_ROW), lambda i: (i, 0, 0)),
                  pl.BlockSpec((1, 4 * LANES, SSM_ROW), lambda i: (i, 0, 0)),
                  pl.BlockSpec((1, 4, LANES), lambda i: (i, 0, 0))],
        out_specs=seq,
        out_shape=jax.ShapeDtypeStruct(u_t.shape, F32),
        scratch_shapes=[pltpu.VMEM((4, rows, LANES), F32), pltpu.VMEM((4, rows, LANES), F32)],
        compiler_params=_cparams(("parallel",)),
        name="s5_scan",
    )(u_t, p_mat, m_mat, r_mat, alpha)


def _post_mix_kernel(x_ref, att_ref, y_ref, wglu_ref, bglu_ref, gs_ref, wout_ref, g2_ref,
                     wrh_ref, wrl_ref, br_ref, x2_ref, h2_ref, ids_ref, gates_ref):
    y = jnp.concatenate([y_ref[:, j, :].T for j in range(y_ref.shape[1])], axis=0)
    y = 0.5 * y * (1.0 + jnp.tanh(math.sqrt(2.0 / math.pi) * (y + 0.044715 * (y * y * y))))
    z = jnp.dot(y.astype(BF16), wglu_ref[...], preferred_element_type=F32) + bglu_ref[...]
    y = y * (1.0 / (1.0 + jnp.exp(-z)))
    ms = jnp.mean(y * y, axis=-1, keepdims=True)
    ssm = (y * lax.rsqrt(ms + RMS_EPS) * gs_ref[...]).astype(BF16)
    mix = jnp.dot(att_ref[...], wout_ref[0:W_ATT, :], preferred_element_type=F32)
    mix = mix + jnp.dot(ssm, wout_ref[W_ATT:, :], preferred_element_type=F32)
    x2 = x_ref[...] + mix
    x2_ref[...] = x2
    ms2 = jnp.mean(x2 * x2, axis=-1, keepdims=True)
    h2 = x2 * lax.rsqrt(ms2 + RMS_EPS) * g2_ref[...]
    h2_ref[...] = h2
    h_hi = h2.astype(BF16)
    h_lo = (h2 - h_hi.astype(F32)).astype(BF16)
    logits = (jnp.dot(h_hi, wrh_ref[...], preferred_element_type=F32)
              + jnp.dot(h_lo, wrh_ref[...], preferred_element_type=F32)
              + jnp.dot(h_hi, wrl_ref[...], preferred_element_type=F32)) + br_ref[...]
    lane = lax.broadcasted_iota(jnp.int32, logits.shape, 1).astype(F32)
    neg = jnp.float32(-jnp.inf)
    cur = jnp.where(lane < N_EXPERTS, logits, neg)
    ids = jnp.zeros(logits.shape, F32)
    vals = jnp.zeros(logits.shape, F32)
    top = None
    den = None
    for kk in range(TOP_K):
        mx = jnp.max(cur, axis=1, keepdims=True)
        idx = jnp.min(jnp.where(cur == mx, lane, float(LANES)), axis=1, keepdims=True)
        if kk == 0:
            top = mx
        e = jnp.exp(mx - top)
        den = e if den is None else den + e
        ids = jnp.where(lane == kk, idx, ids)
        vals = jnp.where(lane == kk, e, vals)
        cur = jnp.where(lane == idx, neg, cur)
    ids_ref[...] = ids.astype(jnp.int32)
    gates_ref[...] = vals / den


def _post_mix(x2d, att, yssm, wglu, bglu, gs, wout, g2, wr_hi, wr_lo, br):
    t = x2d.shape[0]
    tm = min(ROW_TILE, t)
    row = lambda i: (i, 0)
    const = lambda i: (0, 0)
    return pl.pallas_call(
        _post_mix_kernel,
        grid=(t // tm,),
        in_specs=[pl.BlockSpec((tm, D_MODEL), row),
                  pl.BlockSpec((tm, W_ATT), row),
                  pl.BlockSpec((W_SSM, tm // SSM_CHUNK, SSM_CHUNK), lambda i: (0, i, 0)),
                  pl.BlockSpec((W_SSM, W_SSM), const),
                  pl.BlockSpec((1, W_SSM), const),
                  pl.BlockSpec((1, W_SSM), const),
                  pl.BlockSpec((D_MODEL, D_MODEL), const),
                  pl.BlockSpec((1, D_MODEL), const),
                  pl.BlockSpec((D_MODEL, LANES), const),
                  pl.BlockSpec((D_MODEL, LANES), const),
                  pl.BlockSpec((1, LANES), const)],
        out_specs=[pl.BlockSpec((tm, D_MODEL), row),
                   pl.BlockSpec((tm, D_MODEL), row),
                   pl.BlockSpec((tm, LANES), row),
                   pl.BlockSpec((tm, LANES), row)],
        out_shape=[jax.ShapeDtypeStruct((t, D_MODEL), F32),
                   jax.ShapeDtypeStruct((t, D_MODEL), F32),
                   jax.ShapeDtypeStruct((t, LANES), jnp.int32),
                   jax.ShapeDtypeStruct((t, LANES), F32)],
        compiler_params=_cparams(("parallel",)),
        name="post_mix",
    )(x2d, att, yssm, wglu, bglu, gs, wout, g2, wr_hi, wr_lo, br)


def _sc_gather_rows(x, idx):
    n = idx.shape[0]
    d = x.shape[1]
    steps = n // SC_ROWS
    workers = SC_CORES * SC_SUBCORES
    assert steps % (2 * workers) == 0, (n, steps)
    per_worker = steps // workers
    idx_rows = jnp.pad(idx.reshape(steps, SC_ROWS), ((0, 0), (0, LANES - SC_ROWS)))
    mesh = plsc.VectorSubcoreMesh(core_axis_name="core", subcore_axis_name="subcore")

    row_bytes = d * x.dtype.itemsize
    cost = pl.CostEstimate(flops=0, transcendentals=0,
                           bytes_accessed=2 * n * row_bytes + idx_rows.size * idx_rows.dtype.itemsize)

    @pl.kernel(out_type=jax.ShapeDtypeStruct((n, d), x.dtype), mesh=mesh, cost_estimate=cost,
               name="sc_gather_rows",
               scratch_types=[pltpu.VMEM((2, 1, LANES), jnp.int32),
                              pltpu.VMEM((2, SC_ROWS, d), x.dtype),
                              pltpu.SemaphoreType.DMA((2,)),
                              pltpu.SemaphoreType.DMA((2,))])
    def gather(x_hbm, i_hbm, o_hbm, idx_v, buf, sem_in, sem_out):
        base = (lax.axis_index("core") * SC_SUBCORES + lax.axis_index("subcore")) * per_worker

        def gather_copy(slot):
            return pltpu.make_async_copy(x_hbm.at[idx_v.at[slot, 0, pl.ds(0, SC_ROWS)]], buf.at[slot],
                                         sem_in.at[slot])

        def out_copy(step, slot):
            return pltpu.make_async_copy(buf.at[slot], o_hbm.at[pl.ds(step * SC_ROWS, SC_ROWS), :],
                                         sem_out.at[slot])

        @pl.loop(0, per_worker // 2)
        def _(j):
            for slot in range(2):
                step = base + 2 * j + slot

                @pl.when(j > 0)
                def _():
                    out_copy(step, slot).wait()

                pltpu.sync_copy(i_hbm.at[pl.ds(step, 1), :], idx_v.at[slot])
                gather_copy(slot).start()
            for slot in range(2):
                gather_copy(slot).wait()
                out_copy(base + 2 * j + slot, slot).start()

        for slot in range(2):
            out_copy(base, slot).wait()

    return gather(x, idx_rows)


def _expert_kernel(be_ref, x_ref, w1_ref, b1_ref, w2_ref, b2_ref, y_ref, w1_bf, w2_bf):
    i = pl.program_id(0)

    @pl.when(jnp.logical_or(i == 0, be_ref[i] != be_ref[jnp.maximum(i - 1, 0)]))
    def _():
        w1_bf[...] = w1_ref[0].astype(BF16)
        w2_bf[...] = w2_ref[0].astype(BF16)

    x = x_ref[...].astype(BF16)
    hdn = jnp.dot(x, w1_bf[...], preferred_element_type=F32) + b1_ref[0]
    gate = jnp.minimum(hdn[:, :D_FF], SWIGLU_LIMIT)
    lin = jnp.clip(hdn[:, D_FF:], -SWIGLU_LIMIT, SWIGLU_LIMIT)
    act = gate * (1.0 / (1.0 + jnp.exp(-SWIGLU_ALPHA * gate))) * (lin + 1.0)
    y_ref[...] = jnp.dot(act.astype(BF16), w2_bf[...], preferred_element_type=F32) + b2_ref[0]


def _experts(block_e, x_pad, w1, b1, w2, b2):
    n_pad = x_pad.shape[0]
    rows = EXPERT_ROWS
    grid_spec = pltpu.PrefetchScalarGridSpec(
        num_scalar_prefetch=1,
        grid=(n_pad // rows,),
        in_specs=[pl.BlockSpec((rows, D_MODEL), lambda i, be: (i, 0)),
                  pl.BlockSpec((1, D_MODEL, 2 * D_FF), lambda i, be: (be[i], 0, 0)),
                  pl.BlockSpec((1, 1, 2 * D_FF), lambda i, be: (be[i], 0, 0)),
                  pl.BlockSpec((1, D_FF, D_MODEL), lambda i, be: (be[i], 0, 0)),
                  pl.BlockSpec((1, 1, D_MODEL), lambda i, be: (be[i], 0, 0))],
        out_specs=pl.BlockSpec((rows, D_MODEL), lambda i, be: (i, 0)),
        scratch_shapes=[pltpu.VMEM((D_MODEL, 2 * D_FF), BF16), pltpu.VMEM((D_FF, D_MODEL), BF16)],
    )
    return pl.pallas_call(
        _expert_kernel,
        grid_spec=grid_spec,
        out_shape=jax.ShapeDtypeStruct((n_pad, D_MODEL), F32),
        compiler_params=_cparams(("arbitrary",)),
        name="moe_experts",
    )(block_e, x_pad, w1, b1, w2, b2)


def _combine_kernel(x2_ref, gates_ref, gf_ref, y_ref, o_ref):
    gates = gates_ref[...]
    y = x2_ref[...]
    for kk in range(TOP_K):
        y = y + gates[:, kk:kk + 1] * y_ref[kk]
    ms = jnp.mean(y * y, axis=-1, keepdims=True)
    o_ref[...] = y * lax.rsqrt(ms + RMS_EPS) * gf_ref[...]


def _combine(x2, gates, gf, y_sel):
    t = x2.shape[0]
    rows = min(COMBINE_ROWS, t)
    row = lambda i: (i, 0)
    return pl.pallas_call(
        _combine_kernel,
        grid=(t // rows,),
        in_specs=[pl.BlockSpec((rows, D_MODEL), row),
                  pl.BlockSpec((rows, LANES), row),
                  pl.BlockSpec((1, D_MODEL), lambda i: (0, 0)),
                  pl.BlockSpec((TOP_K, rows, D_MODEL), lambda i: (0, i, 0))],
        out_specs=pl.BlockSpec((rows, D_MODEL), row),
        out_shape=jax.ShapeDtypeStruct((t, D_MODEL), F32),
        compiler_params=_cparams(("parallel",)),
        name="moe_combine",
    )(x2, gates, gf, y_sel)


def _dispatch_plan(top_e, n_tok):
    n = n_tok * TOP_K
    blk = EXPERT_ROWS
    flat_e = top_e.reshape(n)
    order = jnp.argsort(flat_e).astype(jnp.int32)
    rank = jnp.argsort(order).astype(jnp.int32)
    st = order // TOP_K
    experts = jnp.arange(N_EXPERTS, dtype=jnp.int32)
    counts = jnp.sum((flat_e[:, None] == experts).astype(jnp.int32), axis=0)
    padded = (counts + blk - 1) // blk * blk
    start = jnp.cumsum(counts) - counts
    pend = jnp.cumsum(padded)
    pstart = pend - padded
    pos = (pstart - start)[flat_e] + rank
    n_blocks = n // blk + N_EXPERTS
    block_start = jnp.arange(n_blocks, dtype=jnp.int32) * blk
    block_e = jnp.minimum(jnp.sum((pend[None, :] <= block_start[:, None]).astype(jnp.int32), axis=1),
                          N_EXPERTS - 1)
    slot = jnp.arange(n_blocks * blk, dtype=jnp.int32)
    slot_e = jnp.repeat(block_e, blk)
    off = slot - pstart[slot_e]
    src = jnp.clip(start[slot_e] + off, 0, n - 1)
    tok_pad = jnp.where(off < counts[slot_e], st[src], 0)
    return tok_pad, pos, block_e


def _trunk_front(x, prm):
    bsz, l, _ = x.shape
    t = bsz * l
    x2d = x.reshape(t, D_MODEL)
    q, k, v, u_t = _in_proj(x2d, prm['norm1_g'], prm['w_in'])

    t_att = min(512, l)
    att = _attention(q.reshape(bsz, l, W_ATT), k.reshape(bsz, l, W_ATT), v.reshape(bsz, l, W_ATT),
                     prm['bias_tiles'][t_att], prm['lam'], prm['subln_g'], t_att)

    yssm = _s5(u_t, prm['ssm_p'], prm['ssm_m'], prm['ssm_r'], prm['ssm_alpha'], l // SSM_CHUNK, bsz)

    x2, h2, ids, gates = _post_mix(x2d, att.reshape(t, W_ATT), yssm, prm['w_glu'], prm['b_glu'],
                                   prm['ssm_norm_g'], prm['w_out'], prm['norm2_g'],
                                   prm['w_router_hi'], prm['w_router_lo'], prm['b_router'])

    tok_pad, pos, block_e = _dispatch_plan(ids[:, :TOP_K], t)
    return {'shape': x.shape, 'x2': x2, 'h2': h2, 'gates': gates,
            'tok_pad': tok_pad, 'pos': pos, 'block_e': block_e}


def _trunk_back(st, x_pad, prm):
    t = st['x2'].shape[0]
    y_pad = _experts(st['block_e'], x_pad, prm['w_moe1'], prm['b_moe1'], prm['w_moe2'], prm['b_moe2'])
    pos_by_k = st['pos'].reshape(t, TOP_K).T.reshape(TOP_K * t)
    y_sel = _sc_gather_rows(y_pad, pos_by_k).reshape(TOP_K, t, D_MODEL)
    out = _combine(st['x2'], st['gates'], prm['normf_g'], y_sel)
    return out.reshape(st['shape'])


def _prepare(seq_lens, rel_bias, norm1_g, w_in, lambda_q1, lambda_k1, lambda_q2, lambda_k2, subln_g,
             ssm_A_re, ssm_A_im, ssm_log_dt, ssm_B_re, ssm_B_im, ssm_C_re, ssm_C_im, ssm_D,
             w_glu, b_glu, ssm_norm_g, w_out, norm2_g, w_router, b_router,
             w_moe1, b_moe1, w_moe2, b_moe2, normf_g):
    layer = 0
    lambda_init = 0.8 - 0.6 * math.exp(-0.3 * layer)
    lam = (jnp.exp(jnp.sum(lambda_q1[layer].astype(F32) * lambda_k1[layer].astype(F32)))
           - jnp.exp(jnp.sum(lambda_q2[layer].astype(F32) * lambda_k2[layer].astype(F32))) + lambda_init)
    p_mat, m_mat, r_mat, alpha = _ssm_matrices(
        ssm_A_re[layer], ssm_A_im[layer], ssm_log_dt[layer], ssm_B_re[layer], ssm_B_im[layer],
        ssm_C_re[layer], ssm_C_im[layer], ssm_D[layer])
    pad_e = LANES - N_EXPERTS
    w_r = jnp.pad(w_router[layer].astype(F32), ((0, 0), (0, pad_e)))
    w_r_hi = w_r.astype(BF16)
    return {
        'norm1_g': norm1_g[layer].reshape(1, D_MODEL).astype(F32),
        'w_in': w_in[layer].astype(BF16),
        'lam': lam.reshape(1).astype(F32),
        'subln_g': (subln_g[layer].astype(F32) * (1.0 - lambda_init)).reshape(1, V_DIM),
        'bias_tiles': {t: _bias_tiles(rel_bias, t) for t in sorted({min(512, l) for l in seq_lens})},
        'ssm_p': p_mat, 'ssm_m': m_mat, 'ssm_r': r_mat, 'ssm_alpha': alpha,
        'w_glu': w_glu[layer].astype(BF16),
        'b_glu': b_glu[layer].reshape(1, W_SSM).astype(F32),
        'ssm_norm_g': ssm_norm_g[layer].reshape(1, W_SSM).astype(F32),
        'w_out': w_out[layer].astype(BF16),
        'norm2_g': norm2_g[layer].reshape(1, D_MODEL).astype(F32),
        'w_router_hi': w_r_hi,
        'w_router_lo': (w_r - w_r_hi.astype(F32)).astype(BF16),
        'b_router': jnp.pad(b_router[layer].astype(F32), (0, pad_e)).reshape(1, LANES),
        'w_moe1': w_moe1[layer].astype(F32),
        'b_moe1': b_moe1[layer].reshape(N_EXPERTS, 1, 2 * D_FF).astype(F32),
        'w_moe2': w_moe2[layer].astype(F32),
        'b_moe2': b_moe2[layer].reshape(N_EXPERTS, 1, D_MODEL).astype(F32),
        'normf_g': normf_g.reshape(1, D_MODEL).astype(F32),
    }


def kernel(x_prompt, x_sample, rel_bias, norm1_g, w_in, lambda_q1, lambda_k1, lambda_q2, lambda_k2, subln_g, ssm_A_re, ssm_A_im, ssm_log_dt, ssm_B_re, ssm_B_im, ssm_C_re, ssm_C_im, ssm_D, w_glu, b_glu, ssm_norm_g, w_out, norm2_g, w_router, b_router, w_moe1, b_moe1, w_moe2, b_moe2, normf_g):
    prm = _prepare((x_prompt.shape[1], x_sample.shape[1]), rel_bias, norm1_g, w_in, lambda_q1,
                   lambda_k1, lambda_q2, lambda_k2, subln_g, ssm_A_re, ssm_A_im, ssm_log_dt,
                   ssm_B_re, ssm_B_im, ssm_C_re, ssm_C_im, ssm_D, w_glu, b_glu, ssm_norm_g, w_out,
                   norm2_g, w_router, b_router, w_moe1, b_moe1, w_moe2, b_moe2, normf_g)
    first = _trunk_front(x_prompt, prm)
    x_sample, tok_pad = lax.optimization_barrier((x_sample, first['tok_pad']))
    x_pad_first = _sc_gather_rows(first['h2'], tok_pad)
    second = _trunk_front(x_sample, prm)
    x_pad_second = _sc_gather_rows(second['h2'], second['tok_pad'])
    return (_trunk_back(first, x_pad_first, prm), _trunk_back(second, x_pad_second, prm))
```

```python
import functools
import math

import jax
import jax.numpy as jnp
from jax import lax
from jax.experimental import pallas as pl
from jax.experimental.pallas import tpu as pltpu
from jax.experimental.pallas import tpu_sc as plsc

F32 = jnp.float32
BF16 = jnp.bfloat16
PACKED = jnp.uint32

D_MODEL = 1024
W_ATT = 512
W_SSM = 512
HEAD_DIM = 64
N_HEADS = 4
V_DIM = 2 * HEAD_DIM
SSM_GROUP = 16
N_GROUPS = W_SSM // SSM_GROUP
SSM_STATE = 64
IN_WIDTH = 3 * W_ATT + W_SSM
N_BUCKETS = 32
MAX_DISTANCE = 128
N_EXPERTS = 32
TOP_K = 4
D_FF = D_MODEL
SWIGLU_ALPHA = 1.702
SWIGLU_LIMIT = 7.0
RMS_EPS = 1e-6
ATT_SCALE = HEAD_DIM ** -0.5
LOG2E = math.log2(math.e)

LANES = 128
PV_KEYS = 256
SSM_CHUNK = LANES
SSM_ROW = SSM_CHUNK * SSM_GROUP
ROW_TILE = 1024
EXPERT_ROWS = 512
SC_STEP_BYTES = 128 * 1024
SC_CORES = 2
SC_SUBCORES = 16
COMBINE_ROWS = 256
VMEM_LIMIT = 56 * 1024 * 1024


def _pack_bf16_pair(lo, hi):
    lo_bits = lax.bitcast_convert_type(lo.astype(BF16).astype(F32), PACKED)
    hi_bits = lax.bitcast_convert_type(hi.astype(BF16).astype(F32), PACKED)
    return (hi_bits & jnp.uint32(0xFFFF0000)) | (lo_bits >> 16)


def _unpack_bf16_pair(words):
    lo = lax.bitcast_convert_type(words << 16, F32)
    hi = lax.bitcast_convert_type(words & jnp.uint32(0xFFFF0000), F32)
    return lo, hi


def _cparams(sem):
    return pltpu.CompilerParams(dimension_semantics=sem, vmem_limit_bytes=VMEM_LIMIT)


def _in_proj_kernel(x_ref, g_ref, w_ref, q_ref, k_ref, v_ref, ut_ref):
    x = x_ref[...]
    ms = jnp.mean(x * x, axis=-1, keepdims=True)
    h = (x * lax.rsqrt(ms + RMS_EPS) * g_ref[...]).astype(BF16)
    proj = jnp.dot(h, w_ref[...], preferred_element_type=F32)
    q_ref[...] = (proj[:, 0:W_ATT] * (ATT_SCALE * LOG2E)).astype(BF16)
    k_ref[...] = proj[:, W_ATT:2 * W_ATT].astype(BF16)
    v_ref[...] = proj[:, 2 * W_ATT:3 * W_ATT].astype(BF16)
    ut = proj[:, 3 * W_ATT:].T
    for j in range(ut_ref.shape[1]):
        ut_ref[:, j, :] = ut[:, j * SSM_CHUNK:(j + 1) * SSM_CHUNK]


def _in_proj(x2d, g, w_bf16):
    t = x2d.shape[0]
    tm = min(ROW_TILE, t)
    out = jax.ShapeDtypeStruct((t, W_ATT), BF16)
    row = lambda i: (i, 0)
    return pl.pallas_call(
        _in_proj_kernel,
        grid=(t // tm,),
        in_specs=[pl.BlockSpec((tm, D_MODEL), row),
                  pl.BlockSpec((1, D_MODEL), lambda i: (0, 0)),
                  pl.BlockSpec((D_MODEL, IN_WIDTH), lambda i: (0, 0))],
        out_specs=[pl.BlockSpec((tm, W_ATT), row)] * 3
        + [pl.BlockSpec((W_SSM, tm // SSM_CHUNK, SSM_CHUNK), lambda i: (0, i, 0))],
        out_shape=[out] * 3 + [jax.ShapeDtypeStruct((W_SSM, t // SSM_CHUNK, SSM_CHUNK), F32)],
        compiler_params=_cparams(("parallel",)),
        name="in_proj",
    )(x2d, g, w_bf16)


def _t5_bucket(rel):
    half = N_BUCKETS // 2
    max_exact = half // 2
    ret = jnp.where(rel > 0, half, 0).astype(jnp.int32)
    n = jnp.abs(rel)
    nf = jnp.maximum(n, 1).astype(F32)
    large = max_exact + (jnp.log(nf / max_exact) / math.log(MAX_DISTANCE / max_exact)
                         * (half - max_exact)).astype(jnp.int32)
    large = jnp.minimum(large, half - 1)
    return ret + jnp.where(n < max_exact, n, large)


def _bias_tiles(rel_bias, t):
    i = jnp.arange(t, dtype=jnp.int32)
    d = jnp.arange(-2, 3, dtype=jnp.int32)
    rel = d[:, None, None] * t + i[None, None, :] - i[None, :, None]
    onehot = (_t5_bucket(rel)[..., None] == jnp.arange(N_BUCKETS, dtype=jnp.int32)).astype(F32)
    tiles = jnp.einsum('dqkn,nh->hdqk', onehot, rel_bias.astype(F32), precision=lax.Precision.HIGHEST)
    return tiles * LOG2E


def _attn_kernel(lam_ref, q_ref, k_ref, v_ref, bias_ref, g_ref, o_ref,
                 m_scr, l_scr, acc_scr, sa_scr, sb_scr, mxa_scr, mxb_scr, *, t, sub, n_iter):
    qi = pl.program_id(2)
    q = q_ref[0]
    lane = lax.broadcasted_iota(jnp.int32, q.shape, 1)
    zero = jnp.zeros_like(q)
    qs = (jnp.where(lane < HEAD_DIM, q, zero), jnp.where(lane >= HEAD_DIM, q, zero))
    nb = t // LANES

    m_scr[...] = jnp.full(m_scr.shape, -jnp.inf, F32)
    l_scr[...] = jnp.zeros(l_scr.shape, F32)
    acc_scr[...] = jnp.zeros(acc_scr.shape, F32)

    def scores(j, s_scr, mx_scr):
        for mi in range(2):
            mx = None
            for c in range(sub):
                blk = j * sub + c
                kc = k_ref[0, pl.ds(pl.multiple_of(blk * t, t), t), :]
                s = lax.dot_general(qs[mi], kc, (((1,), (1,)), ((), ())), preferred_element_type=F32)
                s = s + bias_ref[0, jnp.clip(blk - qi, -2, 2) + 2]
                s_scr[mi, :, c * t:(c + 1) * t] = s
                for i in range(nb):
                    piece = s[:, i * LANES:(i + 1) * LANES]
                    mx = piece if mx is None else jnp.maximum(mx, piece)
            mx_scr[mi] = mx

    def accumulate(j, s_scr, mx_scr):
        vj = v_ref[0, pl.ds(pl.multiple_of(j * (sub * t), sub * t), sub * t), :]
        for mi in range(2):
            m_prev = m_scr[mi]
            m_next = jnp.maximum(m_prev, jnp.max(mx_scr[mi], axis=1, keepdims=True))
            alpha = jnp.exp2(m_prev - m_next)
            m_scr[mi] = m_next
            lsum = None
            pv = None
            for i in range(sub * t // PV_KEYS):
                ps = []
                for c in range(PV_KEYS // LANES):
                    lo = i * PV_KEYS + c * LANES
                    p = jnp.exp2(s_scr[mi, :, lo:lo + LANES] - m_next)
                    lsum = p if lsum is None else lsum + p
                    ps.append(p.astype(BF16))
                d = jnp.dot(jnp.concatenate(ps, axis=1), vj[i * PV_KEYS:(i + 1) * PV_KEYS, :],
                            preferred_element_type=F32)
                pv = d if pv is None else pv + d
            l_scr[mi] = alpha * l_scr[mi] + lsum
            acc_scr[mi] = alpha * acc_scr[mi] + pv

    scores(0, sa_scr, mxa_scr)
    n_pairs = (n_iter - 1) // 2

    def body(i, carry):
        scores(2 * i + 1, sb_scr, mxb_scr)
        accumulate(2 * i, sa_scr, mxa_scr)
        scores(2 * i + 2, sa_scr, mxa_scr)
        accumulate(2 * i + 1, sb_scr, mxb_scr)
        return carry

    lax.fori_loop(0, n_pairs, body, 0)
    if (n_iter - 1) % 2 == 1:
        scores(n_iter - 1, sb_scr, mxb_scr)
        accumulate(n_iter - 2, sa_scr, mxa_scr)
        accumulate(n_iter - 1, sb_scr, mxb_scr)
    else:
        accumulate(n_iter - 1, sa_scr, mxa_scr)

    o1 = acc_scr[0] / jnp.sum(l_scr[0], axis=1, keepdims=True)
    o2 = acc_scr[1] / jnp.sum(l_scr[1], axis=1, keepdims=True)
    o = o1 - lam_ref[0] * o2
    ms = jnp.mean(o * o, axis=-1, keepdims=True)
    o_ref[0] = (o * lax.rsqrt(ms + RMS_EPS) * g_ref[...]).astype(BF16)


def _attention(q, k, v, bias_tiles, lam, g_scaled, t):
    b, l, _ = q.shape
    nq = l // t
    sub = 2 if nq % 2 == 0 else 1
    kern = functools.partial(_attn_kernel, t=t, sub=sub, n_iter=nq // sub)
    return pl.pallas_call(
        kern,
        grid=(b, N_HEADS, nq),
        in_specs=[pl.BlockSpec(memory_space=pltpu.SMEM),
                  pl.BlockSpec((1, t, V_DIM), lambda bi, h, qi: (bi, qi, h)),
                  pl.BlockSpec((1, l, V_DIM), lambda bi, h, qi: (bi, 0, h)),
                  pl.BlockSpec((1, l, V_DIM), lambda bi, h, qi: (bi, 0, h)),
                  pl.BlockSpec((1, 5, t, t), lambda bi, h, qi: (h, 0, 0, 0)),
                  pl.BlockSpec((1, V_DIM), lambda bi, h, qi: (0, 0))],
        out_specs=pl.BlockSpec((1, t, V_DIM), lambda bi, h, qi: (bi, qi, h)),
        out_shape=jax.ShapeDtypeStruct((b, l, W_ATT), BF16),
        scratch_shapes=[pltpu.VMEM((2, t, LANES), F32),
                        pltpu.VMEM((2, t, LANES), F32),
                        pltpu.VMEM((2, t, V_DIM), F32),
                        pltpu.VMEM((2, t, sub * t), F32),
                        pltpu.VMEM((2, t, sub * t), F32),
                        pltpu.VMEM((2, t, LANES), F32),
                        pltpu.VMEM((2, t, LANES), F32)],
        compiler_params=_cparams(("parallel", "parallel", "arbitrary")),
        name="diff_attention",
    )(lam, q, k, v, bias_tiles, g_scaled)


def _ssm_matrices(a_re, a_im, log_dt, b_re, b_im, c_re, c_im, d_skip):
    qn, g, p, hc = SSM_CHUNK, N_GROUPS, SSM_STATE, SSM_GROUP
    n = jnp.arange(qn + 1, dtype=F32)
    pw, bbar, cc = [], [], []
    for d in range(2):
        a = lax.complex(a_re[d].astype(F32), a_im[d].astype(F32))
        dt = jnp.exp(log_dt[d].astype(F32))[:, None]
        adt = a * dt
        a_bar = jnp.exp(adt)
        pw.append(jnp.exp(adt[None] * n[:, None, None]))
        bbar.append(((a_bar - 1.0) / a)[:, :, None]
                    * lax.complex(b_re[d].astype(F32), b_im[d].astype(F32)))
        cc.append(lax.complex(c_re[d].astype(F32), c_im[d].astype(F32)))

    hi = lax.Precision.HIGHEST
    kern = [jnp.einsum('gop,tgp,gpi->tgoi', cc[d], pw[d][:qn], bbar[d], precision=hi).real
            for d in range(2)]
    s_idx = jnp.arange(qn)[:, None]
    t_idx = jnp.arange(qn)[None, :]
    lag = t_idx - s_idx
    taus = jnp.arange(qn)
    sel_f = (lag[:, :, None] == taus).astype(F32)
    sel_b = (-lag[:, :, None] == taus).astype(F32)
    kf = jnp.einsum('stu,ugoi->gisot', sel_f, kern[0], precision=hi)
    kb = jnp.einsum('stu,ugoi->gisot', sel_b, kern[1], precision=hi)
    skip = (jnp.eye(qn, dtype=F32)[None, None, :, None, :] * jnp.eye(hc, dtype=F32)[None, :, None, :, None]
            * d_skip.astype(F32).reshape(g, hc)[:, :, None, None, None])
    m_mat = (kf + kb + skip).reshape(g, SSM_ROW, SSM_ROW)

    zeros_p = jnp.zeros((g, SSM_ROW, LANES - p), F32)

    def pad_cols(x):
        return jnp.concatenate([x, zeros_p], axis=-1)

    pf = jnp.einsum('sgp,gpi->gisp', pw[0][:qn][::-1], bbar[0]).reshape(g, SSM_ROW, p)
    pb = jnp.einsum('sgp,gpi->gisp', pw[1][:qn], bbar[1]).reshape(g, SSM_ROW, p)
    p_mat = jnp.concatenate([pad_cols(pf.real), pad_cols(pf.imag),
                             pad_cols(pb.real), pad_cols(pb.imag)], axis=-1)

    wf = jnp.einsum('gop,tgp->gpot', cc[0], pw[0][1:qn + 1]).reshape(g, p, SSM_ROW)
    wb = jnp.einsum('gop,tgp->gpot', cc[1], pw[1][1:qn + 1][::-1]).reshape(g, p, SSM_ROW)
    zeros_r = jnp.zeros((g, LANES - p, SSM_ROW), F32)
    r_mat = jnp.concatenate([wf.real, zeros_r, -wf.imag, zeros_r,
                             wb.real, zeros_r, -wb.imag, zeros_r], axis=1)

    zeros_a = jnp.zeros((g, LANES - p), F32)

    def pad_vec(x):
        return jnp.concatenate([x, zeros_a], axis=-1)

    alpha = jnp.stack([pad_vec(pw[0][qn].real), pad_vec(pw[0][qn].imag),
                       pad_vec(pw[1][qn].real), pad_vec(pw[1][qn].imag)], axis=1)
    return p_mat.astype(BF16), m_mat.astype(BF16), r_mat.astype(BF16), alpha


def _s5_kernel(u_ref, p_ref, m_ref, r_ref, a_ref, y_ref, s_scr, x_scr, *, nc, bsz):
    u = jnp.concatenate([u_ref[h] for h in range(SSM_GROUP)], axis=-1).astype(BF16)
    s = jnp.dot(u, p_ref[0], preferred_element_type=F32)
    for part in range(4):
        s_scr[part] = s[:, part * LANES:(part + 1) * LANES]
    al = a_ref[0]
    afr = jnp.broadcast_to(al[0:1], (bsz, LANES))
    afi = jnp.broadcast_to(al[1:2], (bsz, LANES))
    abr = jnp.broadcast_to(al[2:3], (bsz, LANES))
    abi = jnp.broadcast_to(al[3:4], (bsz, LANES))
    fr = fi = br = bi = jnp.zeros((bsz, LANES), F32)
    for c in range(nc):
        rf = pl.ds(c, bsz, stride=nc)
        rb = pl.ds(nc - 1 - c, bsz, stride=nc)
        x_scr[0, rf, :] = fr
        x_scr[1, rf, :] = fi
        x_scr[2, rb, :] = br
        x_scr[3, rb, :] = bi
        sfr = s_scr[0, rf, :]
        sfi = s_scr[1, rf, :]
        sbr = s_scr[2, rb, :]
        sbi = s_scr[3, rb, :]
        fr, fi = afr * fr - afi * fi + sfr, afr * fi + afi * fr + sfi
        br, bi = abr * br - abi * bi + sbr, abr * bi + abi * br + sbi
    y = jnp.dot(u, m_ref[0], preferred_element_type=F32)
    x_in = jnp.concatenate([x_scr[part] for part in range(4)], axis=-1).astype(BF16)
    y = y + jnp.dot(x_in, r_ref[0], preferred_element_type=F32)
    for h in range(SSM_GROUP):
        y_ref[h] = y[:, h * SSM_CHUNK:(h + 1) * SSM_CHUNK]


def _s5(u_t, p_mat, m_mat, r_mat, alpha, nc, bsz):
    rows = u_t.shape[1]
    kern = functools.partial(_s5_kernel, nc=nc, bsz=bsz)
    seq = pl.BlockSpec((SSM_GROUP, rows, SSM_CHUNK), lambda i: (i, 0, 0))
    return pl.pallas_call(
        kern,
        grid=(N_GROUPS,),
        in_specs=[seq,
                  pl.BlockSpec((1, SSM_ROW, 4 * LANES), lambda i: (i, 0, 0)),
                  pl.BlockSpec((1, SSM_ROW, SSM_ROW), lambda i: (i, 0, 0)),
                  pl.BlockSpec((1, 4 * LANES, SSM_ROW), lambda i: (i, 0, 0)),
                  pl.BlockSpec((1, 4, LANES), lambda i: (i, 0, 0))],
        out_specs=seq,
        out_shape=jax.ShapeDtypeStruct(u_t.shape, F32),
        scratch_shapes=[pltpu.VMEM((4, rows, LANES), F32), pltpu.VMEM((4, rows, LANES), F32)],
        compiler_params=_cparams(("parallel",)),
        name="s5_scan",
    )(u_t, p_mat, m_mat, r_mat, alpha)


def _post_mix_kernel(x_ref, att_ref, y_ref, wglu_ref, bglu_ref, gs_ref, wout_ref, g2_ref,
                     wrh_ref, wrl_ref, br_ref, x2_ref, h2_ref, ids_ref, gates_ref):
    y = jnp.concatenate([y_ref[:, j, :].T for j in range(y_ref.shape[1])], axis=0)
    y = 0.5 * y * (1.0 + jnp.tanh(math.sqrt(2.0 / math.pi) * (y + 0.044715 * (y * y * y))))
    z = jnp.dot(y.astype(BF16), wglu_ref[...], preferred_element_type=F32) + bglu_ref[...]
    y = y * (1.0 / (1.0 + jnp.exp(-z)))
    ms = jnp.mean(y * y, axis=-1, keepdims=True)
    ssm = (y * lax.rsqrt(ms + RMS_EPS) * gs_ref[...]).astype(BF16)
    mix = jnp.dot(att_ref[...], wout_ref[0:W_ATT, :], preferred_element_type=F32)
    mix = mix + jnp.dot(ssm, wout_ref[W_ATT:, :], preferred_element_type=F32)
    x2 = x_ref[...] + mix
    x2_ref[...] = x2
    ms2 = jnp.mean(x2 * x2, axis=-1, keepdims=True)
    h2 = x2 * lax.rsqrt(ms2 + RMS_EPS) * g2_ref[...]
    h2_ref[...] = _pack_bf16_pair(h2[:, :D_MODEL // 2], h2[:, D_MODEL // 2:])
    h_hi = h2.astype(BF16)
    h_lo = (h2 - h_hi.astype(F32)).astype(BF16)
    logits = (jnp.dot(h_hi, wrh_ref[...], preferred_element_type=F32)
              + jnp.dot(h_lo, wrh_ref[...], preferred_element_type=F32)
              + jnp.dot(h_hi, wrl_ref[...], preferred_element_type=F32)) + br_ref[...]
    lane = lax.broadcasted_iota(jnp.int32, logits.shape, 1).astype(F32)
    neg = jnp.float32(-jnp.inf)
    cur = jnp.where(lane < N_EXPERTS, logits, neg)
    ids = jnp.zeros(logits.shape, F32)
    vals = jnp.zeros(logits.shape, F32)
    top = None
    den = None
    for kk in range(TOP_K):
        mx = jnp.max(cur, axis=1, keepdims=True)
        idx = jnp.min(jnp.where(cur == mx, lane, float(LANES)), axis=1, keepdims=True)
        if kk == 0:
            top = mx
        e = jnp.exp(mx - top)
        den = e if den is None else den + e
        ids = jnp.where(lane == kk, idx, ids)
        vals = jnp.where(lane == kk, e, vals)
        cur = jnp.where(lane == idx, neg, cur)
    ids_ref[...] = ids.astype(jnp.int32)
    gates_ref[...] = vals / den


def _post_mix(x2d, att, yssm, wglu, bglu, gs, wout, g2, wr_hi, wr_lo, br):
    t = x2d.shape[0]
    tm = min(ROW_TILE, t)
    row = lambda i: (i, 0)
    const = lambda i: (0, 0)
    return pl.pallas_call(
        _post_mix_kernel,
        grid=(t // tm,),
        in_specs=[pl.BlockSpec((tm, D_MODEL), row),
                  pl.BlockSpec((tm, W_ATT), row),
                  pl.BlockSpec((W_SSM, tm // SSM_CHUNK, SSM_CHUNK), lambda i: (0, i, 0)),
                  pl.BlockSpec((W_SSM, W_SSM), const),
                  pl.BlockSpec((1, W_SSM), const),
                  pl.BlockSpec((1, W_SSM), const),
                  pl.BlockSpec((D_MODEL, D_MODEL), const),
                  pl.BlockSpec((1, D_MODEL), const),
                  pl.BlockSpec((D_MODEL, LANES), const),
                  pl.BlockSpec((D_MODEL, LANES), const),
                  pl.BlockSpec((1, LANES), const)],
        out_specs=[pl.BlockSpec((tm, D_MODEL), row),
                   pl.BlockSpec((tm, D_MODEL // 2), row),
                   pl.BlockSpec((tm, LANES), row),
                   pl.BlockSpec((tm, LANES), row)],
        out_shape=[jax.ShapeDtypeStruct((t, D_MODEL), F32),
                   jax.ShapeDtypeStruct((t, D_MODEL // 2), PACKED),
                   jax.ShapeDtypeStruct((t, LANES), jnp.int32),
                   jax.ShapeDtypeStruct((t, LANES), F32)],
        compiler_params=_cparams(("parallel",)),
        name="post_mix",
    )(x2d, att, yssm, wglu, bglu, gs, wout, g2, wr_hi, wr_lo, br)


def _sc_gather_rows(x, idx):
    n = idx.shape[0]
    d = x.shape[1]
    row_bytes = d * x.dtype.itemsize
    step_rows = SC_STEP_BYTES // row_bytes
    steps = n // step_rows
    workers = SC_CORES * SC_SUBCORES
    assert steps % (2 * workers) == 0, (n, steps)
    per_worker = steps // workers
    idx_rows = jnp.pad(idx.reshape(steps, step_rows), ((0, 0), (0, LANES - step_rows)))
    mesh = plsc.VectorSubcoreMesh(core_axis_name="core", subcore_axis_name="subcore")

    cost = pl.CostEstimate(flops=0, transcendentals=0,
                           bytes_accessed=2 * n * row_bytes + idx_rows.size * idx_rows.dtype.itemsize)

    @pl.kernel(out_type=jax.ShapeDtypeStruct((n, d), x.dtype), mesh=mesh, cost_estimate=cost,
               name="sc_gather_rows",
               scratch_types=[pltpu.VMEM((2, 1, LANES), jnp.int32),
                              pltpu.VMEM((2, step_rows, d), x.dtype),
                              pltpu.SemaphoreType.DMA((2,)),
                              pltpu.SemaphoreType.DMA((2,))])
    def gather(x_hbm, i_hbm, o_hbm, idx_v, buf, sem_in, sem_out):
        base = (lax.axis_index("core") * SC_SUBCORES + lax.axis_index("subcore")) * per_worker

        def gather_copy(slot):
            return pltpu.make_async_copy(x_hbm.at[idx_v.at[slot, 0, pl.ds(0, step_rows)]], buf.at[slot],
                                         sem_in.at[slot])

        def out_copy(step, slot):
            return pltpu.make_async_copy(buf.at[slot], o_hbm.at[pl.ds(step * step_rows, step_rows), :],
                                         sem_out.at[slot])

        @pl.loop(0, per_worker // 2)
        def _(j):
            for slot in range(2):
                step = base + 2 * j + slot

                @pl.when(j > 0)
                def _():
                    out_copy(step, slot).wait()

                pltpu.sync_copy(i_hbm.at[pl.ds(step, 1), :], idx_v.at[slot])
                gather_copy(slot).start()
            for slot in range(2):
                gather_copy(slot).wait()
                out_copy(base + 2 * j + slot, slot).start()

        for slot in range(2):
            out_copy(base, slot).wait()

    return gather(x, idx_rows)


def _expert_kernel(be_ref, x_ref, w1_ref, b1_ref, w2_ref, b2_ref, y_ref, w1_bf, w2_bf):
    i = pl.program_id(0)

    @pl.when(jnp.logical_or(i == 0, be_ref[i] != be_ref[jnp.maximum(i - 1, 0)]))
    def _():
        w1_bf[...] = w1_ref[0].astype(BF16)
        w2_bf[...] = w2_ref[0].astype(BF16)

    x = jnp.concatenate(_unpack_bf16_pair(x_ref[...]), axis=1).astype(BF16)
    hdn = jnp.dot(x, w1_bf[...], preferred_element_type=F32) + b1_ref[0]
    gate = jnp.minimum(hdn[:, :D_FF], SWIGLU_LIMIT)
    lin = jnp.clip(hdn[:, D_FF:], -SWIGLU_LIMIT, SWIGLU_LIMIT)
    act = gate * (1.0 / (1.0 + jnp.exp(-SWIGLU_ALPHA * gate))) * (lin + 1.0)
    y_ref[...] = jnp.dot(act.astype(BF16), w2_bf[...], preferred_element_type=F32) + b2_ref[0]


def _experts(block_e, x_pad, w1, b1, w2, b2):
    n_pad = x_pad.shape[0]
    rows = EXPERT_ROWS
    grid_spec = pltpu.PrefetchScalarGridSpec(
        num_scalar_prefetch=1,
        grid=(n_pad // rows,),
        in_specs=[pl.BlockSpec((rows, D_MODEL // 2), lambda i, be: (i, 0)),
                  pl.BlockSpec((1, D_MODEL, 2 * D_FF), lambda i, be: (be[i], 0, 0)),
                  pl.BlockSpec((1, 1, 2 * D_FF), lambda i, be: (be[i], 0, 0)),
                  pl.BlockSpec((1, D_FF, D_MODEL), lambda i, be: (be[i], 0, 0)),
                  pl.BlockSpec((1, 1, D_MODEL), lambda i, be: (be[i], 0, 0))],
        out_specs=pl.BlockSpec((rows, D_MODEL), lambda i, be: (i, 0)),
        scratch_shapes=[pltpu.VMEM((D_MODEL, 2 * D_FF), BF16), pltpu.VMEM((D_FF, D_MODEL), BF16)],
    )
    return pl.pallas_call(
        _expert_kernel,
        grid_spec=grid_spec,
        out_shape=jax.ShapeDtypeStruct((n_pad, D_MODEL), F32),
        compiler_params=_cparams(("arbitrary",)),
        name="moe_experts",
    )(block_e, x_pad, w1, b1, w2, b2)


def _combine_kernel(x2_ref, gates_ref, gf_ref, y_ref, o_ref):
    gates = gates_ref[...]
    y = x2_ref[...]
    for kk in range(TOP_K):
        y = y + gates[:, kk:kk + 1] * y_ref[kk]
    ms = jnp.mean(y * y, axis=-1, keepdims=True)
    o_ref[...] = y * lax.rsqrt(ms + RMS_EPS) * gf_ref[...]


def _combine(x2, gates, gf, y_sel):
    t = x2.shape[0]
    rows = min(COMBINE_ROWS, t)
    row = lambda i: (i, 0)
    return pl.pallas_call(
        _combine_kernel,
        grid=(t // rows,),
        in_specs=[pl.BlockSpec((rows, D_MODEL), row),
                  pl.BlockSpec((rows, LANES), row),
                  pl.BlockSpec((1, D_MODEL), lambda i: (0, 0)),
                  pl.BlockSpec((TOP_K, rows, D_MODEL), lambda i: (0, i, 0))],
        out_specs=pl.BlockSpec((rows, D_MODEL), row),
        out_shape=jax.ShapeDtypeStruct((t, D_MODEL), F32),
        compiler_params=_cparams(("parallel",)),
        name="moe_combine",
    )(x2, gates, gf, y_sel)


def _dispatch_plan(top_e, n_tok):
    n = n_tok * TOP_K
    blk = EXPERT_ROWS
    flat_e = top_e.reshape(n)
    order = jnp.argsort(flat_e).astype(jnp.int32)
    rank = jnp.argsort(order).astype(jnp.int32)
    st = order // TOP_K
    experts = jnp.arange(N_EXPERTS, dtype=jnp.int32)
    counts = jnp.sum((flat_e[:, None] == experts).astype(jnp.int32), axis=0)
    padded = (counts + blk - 1) // blk * blk
    start = jnp.cumsum(counts) - counts
    pend = jnp.cumsum(padded)
    pstart = pend - padded
    pos = (pstart - start)[flat_e] + rank
    n_blocks = n // blk + N_EXPERTS
    block_start = jnp.arange(n_blocks, dtype=jnp.int32) * blk
    block_e = jnp.minimum(jnp.sum((pend[None, :] <= block_start[:, None]).astype(jnp.int32), axis=1),
                          N_EXPERTS - 1)
    slot = jnp.arange(n_blocks * blk, dtype=jnp.int32)
    slot_e = jnp.repeat(block_e, blk)
    off = slot - pstart[slot_e]
    src = jnp.clip(start[slot_e] + off, 0, n - 1)
    tok_pad = jnp.where(off < counts[slot_e], st[src], 0)
    return tok_pad, pos, block_e


def _trunk_front(x, prm):
    bsz, l, _ = x.shape
    t = bsz * l
    x2d = x.reshape(t, D_MODEL)
    q, k, v, u_t = _in_proj(x2d, prm['norm1_g'], prm['w_in'])

    t_att = min(512, l)
    att = _attention(q.reshape(bsz, l, W_ATT), k.reshape(bsz, l, W_ATT), v.reshape(bsz, l, W_ATT),
                     prm['bias_tiles'][t_att], prm['lam'], prm['subln_g'], t_att)

    yssm = _s5(u_t, prm['ssm_p'], prm['ssm_m'], prm['ssm_r'], prm['ssm_alpha'], l // SSM_CHUNK, bsz)

    x2, h2, ids, gates = _post_mix(x2d, att.reshape(t, W_ATT), yssm, prm['w_glu'], prm['b_glu'],
                                   prm['ssm_norm_g'], prm['w_out'], prm['norm2_g'],
                                   prm['w_router_hi'], prm['w_router_lo'], prm['b_router'])

    tok_pad, pos, block_e = _dispatch_plan(ids[:, :TOP_K], t)
    return {'shape': x.shape, 'x2': x2, 'h2': h2, 'gates': gates,
            'tok_pad': tok_pad, 'pos': pos, 'block_e': block_e}


def _trunk_back(st, x_pad, prm):
    t = st['x2'].shape[0]
    y_pad = _experts(st['block_e'], x_pad, prm['w_moe1'], prm['b_moe1'], prm['w_moe2'], prm['b_moe2'])
    pos_by_k = st['pos'].reshape(t, TOP_K).T.reshape(TOP_K * t)
    y_sel = _sc_gather_rows(y_pad, pos_by_k).reshape(TOP_K, t, D_MODEL)
    out = _combine(st['x2'], st['gates'], prm['normf_g'], y_sel)
    return out.reshape(st['shape'])


def _prepare(seq_lens, rel_bias, norm1_g, w_in, lambda_q1, lambda_k1, lambda_q2, lambda_k2, subln_g,
             ssm_A_re, ssm_A_im, ssm_log_dt, ssm_B_re, ssm_B_im, ssm_C_re, ssm_C_im, ssm_D,
             w_glu, b_glu, ssm_norm_g, w_out, norm2_g, w_router, b_router,
             w_moe1, b_moe1, w_moe2, b_moe2, normf_g):
    layer = 0
    lambda_init = 0.8 - 0.6 * math.exp(-0.3 * layer)
    lam = (jnp.exp(jnp.sum(lambda_q1[layer].astype(F32) * lambda_k1[layer].astype(F32)))
           - jnp.exp(jnp.sum(lambda_q2[layer].astype(F32) * lambda_k2[layer].astype(F32))) + lambda_init)
    p_mat, m_mat, r_mat, alpha = _ssm_matrices(
        ssm_A_re[layer], ssm_A_im[layer], ssm_log_dt[layer], ssm_B_re[layer], ssm_B_im[layer],
        ssm_C_re[layer], ssm_C_im[layer], ssm_D[layer])
    pad_e = LANES - N_EXPERTS
    w_r = jnp.pad(w_router[layer].astype(F32), ((0, 0), (0, pad_e)))
    w_r_hi = w_r.astype(BF16)
    return {
        'norm1_g': norm1_g[layer].reshape(1, D_MODEL).astype(F32),
        'w_in': w_in[layer].astype(BF16),
        'lam': lam.reshape(1).astype(F32),
        'subln_g': (subln_g[layer].astype(F32) * (1.0 - lambda_init)).reshape(1, V_DIM),
        'bias_tiles': {t: _bias_tiles(rel_bias, t) for t in sorted({min(512, l) for l in seq_lens})},
        'ssm_p': p_mat, 'ssm_m': m_mat, 'ssm_r': r_mat, 'ssm_alpha': alpha,
        'w_glu': w_glu[layer].astype(BF16),
        'b_glu': b_glu[layer].reshape(1, W_SSM).astype(F32),
        'ssm_norm_g': ssm_norm_g[layer].reshape(1, W_SSM).astype(F32),
        'w_out': w_out[layer].astype(BF16),
        'norm2_g': norm2_g[layer].reshape(1, D_MODEL).astype(F32),
        'w_router_hi': w_r_hi,
        'w_router_lo': (w_r - w_r_hi.astype(F32)).astype(BF16),
        'b_router': jnp.pad(b_router[layer].astype(F32), (0, pad_e)).reshape(1, LANES),
        'w_moe1': w_moe1[layer].astype(F32),
        'b_moe1': b_moe1[layer].reshape(N_EXPERTS, 1, 2 * D_FF).astype(F32),
        'w_moe2': w_moe2[layer].astype(F32),
        'b_moe2': b_moe2[layer].reshape(N_EXPERTS, 1, D_MODEL).astype(F32),
        'normf_g': normf_g.reshape(1, D_MODEL).astype(F32),
    }


def kernel(x_prompt, x_sample, rel_bias, norm1_g, w_in, lambda_q1, lambda_k1, lambda_q2, lambda_k2, subln_g, ssm_A_re, ssm_A_im, ssm_log_dt, ssm_B_re, ssm_B_im, ssm_C_re, ssm_C_im, ssm_D, w_glu, b_glu, ssm_norm_g, w_out, norm2_g, w_router, b_router, w_moe1, b_moe1, w_moe2, b_moe2, normf_g):
    prm = _prepare((x_prompt.shape[1], x_sample.shape[1]), rel_bias, norm1_g, w_in, lambda_q1,
                   lambda_k1, lambda_q2, lambda_k2, subln_g, ssm_A_re, ssm_A_im, ssm_log_dt,
                   ssm_B_re, ssm_B_im, ssm_C_re, ssm_C_im, ssm_D, w_glu, b_glu, ssm_norm_g, w_out,
                   norm2_g, w_router, b_router, w_moe1, b_moe1, w_moe2, b_moe2, normf_g)
    first = _trunk_front(x_prompt, prm)
    x_sample, tok_pad = lax.optimization_barrier((x_sample, first['tok_pad']))
    x_pad_first = _sc_gather_rows(first['h2'], tok_pad)
    second = _trunk_front(x_sample, prm)
    x_pad_second = _sc_gather_rows(second['h2'], second['tok_pad'])
    return (_trunk_back(first, x_pad_first, prm), _trunk_back(second, x_pad_second, prm))
```

```python
import functools
import math

import jax
import jax.numpy as jnp
from jax import lax
from jax.experimental import pallas as pl
from jax.experimental.pallas import tpu as pltpu
from jax.experimental.pallas import tpu_sc as plsc

F32 = jnp.float32
BF16 = jnp.bfloat16
PACKED = jnp.uint32

D_MODEL = 1024
W_ATT = 512
W_SSM = 512
HEAD_DIM = 64
N_HEADS = 4
V_DIM = 2 * HEAD_DIM
SSM_GROUP = 16
N_GROUPS = W_SSM // SSM_GROUP
SSM_STATE = 64
IN_WIDTH = 3 * W_ATT + W_SSM
N_BUCKETS = 32
MAX_DISTANCE = 128
N_EXPERTS = 32
TOP_K = 4
D_FF = D_MODEL
SWIGLU_ALPHA = 1.702
SWIGLU_LIMIT = 7.0
RMS_EPS = 1e-6
ATT_SCALE = HEAD_DIM ** -0.5
LOG2E = math.log2(math.e)

LANES = 128
PV_KEYS = 256
SSM_CHUNK = LANES
SSM_ROW = SSM_CHUNK * SSM_GROUP
ROW_TILE = 1024
EXPERT_ROWS = 512
SC_STEP_BYTES = 128 * 1024
SC_CORES = 2
SC_SUBCORES = 16
COMBINE_ROWS = 256
VMEM_LIMIT = 56 * 1024 * 1024


def _pack_bf16_pair(lo, hi):
    lo_bits = lax.bitcast_convert_type(lo.astype(BF16).astype(F32), PACKED)
    hi_bits = lax.bitcast_convert_type(hi.astype(BF16).astype(F32), PACKED)
    return (hi_bits & jnp.uint32(0xFFFF0000)) | (lo_bits >> 16)


def _unpack_bf16_pair(words):
    lo = lax.bitcast_convert_type(words << 16, F32)
    hi = lax.bitcast_convert_type(words & jnp.uint32(0xFFFF0000), F32)
    return lo, hi


def _cparams(sem):
    return pltpu.CompilerParams(dimension_semantics=sem, vmem_limit_bytes=VMEM_LIMIT)


def _in_proj_kernel(x_ref, g_ref, w_ref, q_ref, k_ref, v_ref, ut_ref):
    x = x_ref[...]
    ms = jnp.mean(x * x, axis=-1, keepdims=True)
    h = (x * lax.rsqrt(ms + RMS_EPS) * g_ref[...]).astype(BF16)
    proj = jnp.dot(h, w_ref[...], preferred_element_type=F32)
    q_ref[...] = (proj[:, 0:W_ATT] * (ATT_SCALE * LOG2E)).astype(BF16)
    k_ref[...] = proj[:, W_ATT:2 * W_ATT].astype(BF16)
    v_ref[...] = proj[:, 2 * W_ATT:3 * W_ATT].astype(BF16)
    ut = proj[:, 3 * W_ATT:].T
    for j in range(ut_ref.shape[1]):
        ut_ref[:, j, :] = ut[:, j * SSM_CHUNK:(j + 1) * SSM_CHUNK]


def _in_proj(x2d, g, w_bf16):
    t = x2d.shape[0]
    tm = min(ROW_TILE, t)
    out = jax.ShapeDtypeStruct((t, W_ATT), BF16)
    row = lambda i: (i, 0)
    return pl.pallas_call(
        _in_proj_kernel,
        grid=(t // tm,),
        in_specs=[pl.BlockSpec((tm, D_MODEL), row),
                  pl.BlockSpec((1, D_MODEL), lambda i: (0, 0)),
                  pl.BlockSpec((D_MODEL, IN_WIDTH), lambda i: (0, 0))],
        out_specs=[pl.BlockSpec((tm, W_ATT), row)] * 3
        + [pl.BlockSpec((W_SSM, tm // SSM_CHUNK, SSM_CHUNK), lambda i: (0, i, 0))],
        out_shape=[out] * 3 + [jax.ShapeDtypeStruct((W_SSM, t // SSM_CHUNK, SSM_CHUNK), F32)],
        compiler_params=_cparams(("parallel",)),
        name="in_proj",
    )(x2d, g, w_bf16)


def _t5_bucket(rel):
    half = N_BUCKETS // 2
    max_exact = half // 2
    ret = jnp.where(rel > 0, half, 0).astype(jnp.int32)
    n = jnp.abs(rel)
    nf = jnp.maximum(n, 1).astype(F32)
    large = max_exact + (jnp.log(nf / max_exact) / math.log(MAX_DISTANCE / max_exact)
                         * (half - max_exact)).astype(jnp.int32)
    large = jnp.minimum(large, half - 1)
    return ret + jnp.where(n < max_exact, n, large)


def _bias_tiles(rel_bias, t):
    i = jnp.arange(t, dtype=jnp.int32)
    d = jnp.arange(-2, 3, dtype=jnp.int32)
    rel = d[:, None, None] * t + i[None, None, :] - i[None, :, None]
    onehot = (_t5_bucket(rel)[..., None] == jnp.arange(N_BUCKETS, dtype=jnp.int32)).astype(F32)
    tiles = jnp.einsum('dqkn,nh->hdqk', onehot, rel_bias.astype(F32), precision=lax.Precision.HIGHEST)
    return tiles * LOG2E


def _attn_kernel(lam_ref, q_ref, k_ref, v_ref, bias_ref, g_ref, o_ref,
                 m_scr, l_scr, acc_scr, sa_scr, sb_scr, mxa_scr, mxb_scr, *, t, sub, n_iter):
    qi = pl.program_id(2)
    q = q_ref[0]
    lane = lax.broadcasted_iota(jnp.int32, q.shape, 1)
    zero = jnp.zeros_like(q)
    qs = (jnp.where(lane < HEAD_DIM, q, zero), jnp.where(lane >= HEAD_DIM, q, zero))
    nb = t // LANES

    m_scr[...] = jnp.full(m_scr.shape, -jnp.inf, F32)
    l_scr[...] = jnp.zeros(l_scr.shape, F32)
    acc_scr[...] = jnp.zeros(acc_scr.shape, F32)

    def scores(j, s_scr, mx_scr):
        for mi in range(2):
            mx = None
            for c in range(sub):
                blk = j * sub + c
                kc = k_ref[0, pl.ds(pl.multiple_of(blk * t, t), t), :]
                s = lax.dot_general(qs[mi], kc, (((1,), (1,)), ((), ())), preferred_element_type=F32)
                s = s + bias_ref[0, jnp.clip(blk - qi, -2, 2) + 2]
                s_scr[mi, :, c * t:(c + 1) * t] = s
                for i in range(nb):
                    piece = s[:, i * LANES:(i + 1) * LANES]
                    mx = piece if mx is None else jnp.maximum(mx, piece)
            mx_scr[mi] = mx

    def accumulate(j, s_scr, mx_scr):
        vj = v_ref[0, pl.ds(pl.multiple_of(j * (sub * t), sub * t), sub * t), :]
        for mi in range(2):
            m_prev = m_scr[mi]
            m_next = jnp.maximum(m_prev, jnp.max(mx_scr[mi], axis=1, keepdims=True))
            alpha = jnp.exp2(m_prev - m_next)
            m_scr[mi] = m_next
            lsum = None
            pv = None
            for i in range(sub * t // PV_KEYS):
                ps = []
                for c in range(PV_KEYS // LANES):
                    lo = i * PV_KEYS + c * LANES
                    p = jnp.exp2(s_scr[mi, :, lo:lo + LANES] - m_next)
                    lsum = p if lsum is None else lsum + p
                    ps.append(p.astype(BF16))
                d = jnp.dot(jnp.concatenate(ps, axis=1), vj[i * PV_KEYS:(i + 1) * PV_KEYS, :],
                            preferred_element_type=F32)
                pv = d if pv is None else pv + d
            l_scr[mi] = alpha * l_scr[mi] + lsum
            acc_scr[mi] = alpha * acc_scr[mi] + pv

    scores(0, sa_scr, mxa_scr)
    n_pairs = (n_iter - 1) // 2

    def body(i, carry):
        scores(2 * i + 1, sb_scr, mxb_scr)
        accumulate(2 * i, sa_scr, mxa_scr)
        scores(2 * i + 2, sa_scr, mxa_scr)
        accumulate(2 * i + 1, sb_scr, mxb_scr)
        return carry

    lax.fori_loop(0, n_pairs, body, 0)
    if (n_iter - 1) % 2 == 1:
        scores(n_iter - 1, sb_scr, mxb_scr)
        accumulate(n_iter - 2, sa_scr, mxa_scr)
        accumulate(n_iter - 1, sb_scr, mxb_scr)
    else:
        accumulate(n_iter - 1, sa_scr, mxa_scr)

    o1 = acc_scr[0] / jnp.sum(l_scr[0], axis=1, keepdims=True)
    o2 = acc_scr[1] / jnp.sum(l_scr[1], axis=1, keepdims=True)
    o = o1 - lam_ref[0] * o2
    ms = jnp.mean(o * o, axis=-1, keepdims=True)
    o_ref[0] = (o * lax.rsqrt(ms + RMS_EPS) * g_ref[...]).astype(BF16)


def _attention(q, k, v, bias_tiles, lam, g_scaled, t):
    b, l, _ = q.shape
    nq = l // t
    sub = 2 if nq % 2 == 0 else 1
    kern = functools.partial(_attn_kernel, t=t, sub=sub, n_iter=nq // sub)
    return pl.pallas_call(
        kern,
        grid=(b, N_HEADS, nq),
        in_specs=[pl.BlockSpec(memory_space=pltpu.SMEM),
                  pl.BlockSpec((1, t, V_DIM), lambda bi, h, qi: (bi, qi, h)),
                  pl.BlockSpec((1, l, V_DIM), lambda bi, h, qi: (bi, 0, h)),
                  pl.BlockSpec((1, l, V_DIM), lambda bi, h, qi: (bi, 0, h)),
                  pl.BlockSpec((1, 5, t, t), lambda bi, h, qi: (h, 0, 0, 0)),
                  pl.BlockSpec((1, V_DIM), lambda bi, h, qi: (0, 0))],
        out_specs=pl.BlockSpec((1, t, V_DIM), lambda bi, h, qi: (bi, qi, h)),
        out_shape=jax.ShapeDtypeStruct((b, l, W_ATT), BF16),
        scratch_shapes=[pltpu.VMEM((2, t, LANES), F32),
                        pltpu.VMEM((2, t, LANES), F32),
                        pltpu.VMEM((2, t, V_DIM), F32),
                        pltpu.VMEM((2, t, sub * t), F32),
                        pltpu.VMEM((2, t, sub * t), F32),
                        pltpu.VMEM((2, t, LANES), F32),
                        pltpu.VMEM((2, t, LANES), F32)],
        compiler_params=_cparams(("parallel", "parallel", "arbitrary")),
        name="diff_attention",
    )(lam, q, k, v, bias_tiles, g_scaled)


def _ssm_matrices(a_re, a_im, log_dt, b_re, b_im, c_re, c_im, d_skip):
    qn, g, p, hc = SSM_CHUNK, N_GROUPS, SSM_STATE, SSM_GROUP
    n = jnp.arange(qn + 1, dtype=F32)
    pw, bbar, cc = [], [], []
    for d in range(2):
        a = lax.complex(a_re[d].astype(F32), a_im[d].astype(F32))
        dt = jnp.exp(log_dt[d].astype(F32))[:, None]
        adt = a * dt
        a_bar = jnp.exp(adt)
        pw.append(jnp.exp(adt[None] * n[:, None, None]))
        bbar.append(((a_bar - 1.0) / a)[:, :, None]
                    * lax.complex(b_re[d].astype(F32), b_im[d].astype(F32)))
        cc.append(lax.complex(c_re[d].astype(F32), c_im[d].astype(F32)))

    hi = lax.Precision.HIGHEST
    kern = [jnp.einsum('gop,tgp,gpi->tgoi', cc[d], pw[d][:qn], bbar[d], precision=hi).real
            for d in range(2)]
    kf = jnp.transpose(kern[0], (1, 3, 2, 0))
    kb = jnp.transpose(kern[1], (1, 3, 2, 0))
    skip = jnp.eye(hc, dtype=F32)[None] * d_skip.astype(F32).reshape(g, hc)[:, :, None]
    taps = jnp.concatenate([kb[..., :0:-1], (kf[..., 0] + kb[..., 0] + skip)[..., None], kf[..., 1:]], axis=-1)
    m_mat = jnp.stack([taps[..., qn - 1 - s:2 * qn - 1 - s] for s in range(qn)], axis=2)
    m_mat = m_mat.reshape(g, SSM_ROW, SSM_ROW)

    zeros_p = jnp.zeros((g, SSM_ROW, LANES - p), F32)

    def pad_cols(x):
        return jnp.concatenate([x, zeros_p], axis=-1)

    pf = jnp.einsum('sgp,gpi->gisp', pw[0][:qn][::-1], bbar[0]).reshape(g, SSM_ROW, p)
    pb = jnp.einsum('sgp,gpi->gisp', pw[1][:qn], bbar[1]).reshape(g, SSM_ROW, p)
    p_mat = jnp.concatenate([pad_cols(pf.real), pad_cols(pf.imag),
                             pad_cols(pb.real), pad_cols(pb.imag)], axis=-1)

    wf = jnp.einsum('gop,tgp->gpot', cc[0], pw[0][1:qn + 1]).reshape(g, p, SSM_ROW)
    wb = jnp.einsum('gop,tgp->gpot', cc[1], pw[1][1:qn + 1][::-1]).reshape(g, p, SSM_ROW)
    zeros_r = jnp.zeros((g, LANES - p, SSM_ROW), F32)
    r_mat = jnp.concatenate([wf.real, zeros_r, -wf.imag, zeros_r,
                             wb.real, zeros_r, -wb.imag, zeros_r], axis=1)

    zeros_a = jnp.zeros((g, LANES - p), F32)

    def pad_vec(x):
        return jnp.concatenate([x, zeros_a], axis=-1)

    alpha = jnp.stack([pad_vec(pw[0][qn].real), pad_vec(pw[0][qn].imag),
                       pad_vec(pw[1][qn].real), pad_vec(pw[1][qn].imag)], axis=1)
    return p_mat.astype(BF16), m_mat.astype(BF16), r_mat.astype(BF16), alpha


def _s5_kernel(u_ref, p_ref, m_ref, r_ref, a_ref, y_ref, s_scr, x_scr, *, nc, bsz):
    u = jnp.concatenate([u_ref[h] for h in range(SSM_GROUP)], axis=-1).astype(BF16)
    s = jnp.dot(u, p_ref[0], preferred_element_type=F32)
    for part in range(4):
        s_scr[part] = s[:, part * LANES:(part + 1) * LANES]
    al = a_ref[0]
    afr = jnp.broadcast_to(al[0:1], (bsz, LANES))
    afi = jnp.broadcast_to(al[1:2], (bsz, LANES))
    abr = jnp.broadcast_to(al[2:3], (bsz, LANES))
    abi = jnp.broadcast_to(al[3:4], (bsz, LANES))
    fr = fi = br = bi = jnp.zeros((bsz, LANES), F32)
    for c in range(nc):
        rf = pl.ds(c, bsz, stride=nc)
        rb = pl.ds(nc - 1 - c, bsz, stride=nc)
        x_scr[0, rf, :] = fr
        x_scr[1, rf, :] = fi
        x_scr[2, rb, :] = br
        x_scr[3, rb, :] = bi
        sfr = s_scr[0, rf, :]
        sfi = s_scr[1, rf, :]
        sbr = s_scr[2, rb, :]
        sbi = s_scr[3, rb, :]
        fr, fi = afr * fr - afi * fi + sfr, afr * fi + afi * fr + sfi
        br, bi = abr * br - abi * bi + sbr, abr * bi + abi * br + sbi
    y = jnp.dot(u, m_ref[0], preferred_element_type=F32)
    x_in = jnp.concatenate([x_scr[part] for part in range(4)], axis=-1).astype(BF16)
    y = y + jnp.dot(x_in, r_ref[0], preferred_element_type=F32)
    for h in range(SSM_GROUP):
        y_ref[h] = y[:, h * SSM_CHUNK:(h + 1) * SSM_CHUNK]


def _s5(u_t, p_mat, m_mat, r_mat, alpha, nc, bsz):
    rows = u_t.shape[1]
    kern = functools.partial(_s5_kernel, nc=nc, bsz=bsz)
    seq = pl.BlockSpec((SSM_GROUP, rows, SSM_CHUNK), lambda i: (i, 0, 0))
    return pl.pallas_call(
        kern,
        grid=(N_GROUPS,),
        in_specs=[seq,
                  pl.BlockSpec((1, SSM_ROW, 4 * LANES), lambda i: (i, 0, 0)),
                  pl.BlockSpec((1, SSM_ROW, SSM_ROW), lambda i: (i, 0, 0)),
                  pl.BlockSpec((1, 4 * LANES, SSM_ROW), lambda i: (i, 0, 0)),
                  pl.BlockSpec((1, 4, LANES), lambda i: (i, 0, 0))],
        out_specs=seq,
        out_shape=jax.ShapeDtypeStruct(u_t.shape, F32),
        scratch_shapes=[pltpu.VMEM((4, rows, LANES), F32), pltpu.VMEM((4, rows, LANES), F32)],
        compiler_params=_cparams(("parallel",)),
        name="s5_scan",
    )(u_t, p_mat, m_mat, r_mat, alpha)


def _post_mix_kernel(x_ref, att_ref, y_ref, wglu_ref, bglu_ref, gs_ref, wout_ref, g2_ref,
                     wrh_ref, wrl_ref, br_ref, x2_ref, h2_ref, ids_ref, gates_ref):
    y = jnp.concatenate([y_ref[:, j, :].T for j in range(y_ref.shape[1])], axis=0)
    y = 0.5 * y * (1.0 + jnp.tanh(math.sqrt(2.0 / math.pi) * (y + 0.044715 * (y * y * y))))
    z = jnp.dot(y.astype(BF16), wglu_ref[...], preferred_element_type=F32) + bglu_ref[...]
    y = y * (1.0 / (1.0 + jnp.exp(-z)))
    ms = jnp.mean(y * y, axis=-1, keepdims=True)
    ssm = (y * lax.rsqrt(ms + RMS_EPS) * gs_ref[...]).astype(BF16)
    mix = jnp.dot(att_ref[...], wout_ref[0:W_ATT, :], preferred_element_type=F32)
    mix = mix + jnp.dot(ssm, wout_ref[W_ATT:, :], preferred_element_type=F32)
    x2 = x_ref[...] + mix
    x2_ref[...] = x2
    ms2 = jnp.mean(x2 * x2, axis=-1, keepdims=True)
    h2 = x2 * lax.rsqrt(ms2 + RMS_EPS) * g2_ref[...]
    h2_ref[...] = _pack_bf16_pair(h2[:, :D_MODEL // 2], h2[:, D_MODEL // 2:])
    h_hi = h2.astype(BF16)
    h_lo = (h2 - h_hi.astype(F32)).astype(BF16)
    logits = (jnp.dot(h_hi, wrh_ref[...], preferred_element_type=F32)
              + jnp.dot(h_lo, wrh_ref[...], preferred_element_type=F32)
              + jnp.dot(h_hi, wrl_ref[...], preferred_element_type=F32)) + br_ref[...]
    lane = lax.broadcasted_iota(jnp.int32, logits.shape, 1).astype(F32)
    neg = jnp.float32(-jnp.inf)
    cur = jnp.where(lane < N_EXPERTS, logits, neg)
    ids = jnp.zeros(logits.shape, F32)
    vals = jnp.zeros(logits.shape, F32)
    top = None
    den = None
    for kk in range(TOP_K):
        mx = jnp.max(cur, axis=1, keepdims=True)
        idx = jnp.min(jnp.where(cur == mx, lane, float(LANES)), axis=1, keepdims=True)
        if kk == 0:
            top = mx
        e = jnp.exp(mx - top)
        den = e if den is None else den + e
        ids = jnp.where(lane == kk, idx, ids)
        vals = jnp.where(lane == kk, e, vals)
        cur = jnp.where(lane == idx, neg, cur)
    ids_ref[...] = ids.astype(jnp.int32)
    gates_ref[...] = vals / den


def _post_mix(x2d, att, yssm, wglu, bglu, gs, wout, g2, wr_hi, wr_lo, br):
    t = x2d.shape[0]
    tm = min(ROW_TILE, t)
    row = lambda i: (i, 0)
    const = lambda i: (0, 0)
    return pl.pallas_call(
        _post_mix_kernel,
        grid=(t // tm,),
        in_specs=[pl.BlockSpec((tm, D_MODEL), row),
                  pl.BlockSpec((tm, W_ATT), row),
                  pl.BlockSpec((W_SSM, tm // SSM_CHUNK, SSM_CHUNK), lambda i: (0, i, 0)),
                  pl.BlockSpec((W_SSM, W_SSM), const),
                  pl.BlockSpec((1, W_SSM), const),
                  pl.BlockSpec((1, W_SSM), const),
                  pl.BlockSpec((D_MODEL, D_MODEL), const),
                  pl.BlockSpec((1, D_MODEL), const),
                  pl.BlockSpec((D_MODEL, LANES), const),
                  pl.BlockSpec((D_MODEL, LANES), const),
                  pl.BlockSpec((1, LANES), const)],
        out_specs=[pl.BlockSpec((tm, D_MODEL), row),
                   pl.BlockSpec((tm, D_MODEL // 2), row),
                   pl.BlockSpec((tm, LANES), row),
                   pl.BlockSpec((tm, LANES), row)],
        out_shape=[jax.ShapeDtypeStruct((t, D_MODEL), F32),
                   jax.ShapeDtypeStruct((t, D_MODEL // 2), PACKED),
                   jax.ShapeDtypeStruct((t, LANES), jnp.int32),
                   jax.ShapeDtypeStruct((t, LANES), F32)],
        compiler_params=_cparams(("parallel",)),
        name="post_mix",
    )(x2d, att, yssm, wglu, bglu, gs, wout, g2, wr_hi, wr_lo, br)


def _sc_gather_rows(x, idx):
    n = idx.shape[0]
    d = x.shape[1]
    row_bytes = d * x.dtype.itemsize
    step_rows = SC_STEP_BYTES // row_bytes
    steps = n // step_rows
    workers = SC_CORES * SC_SUBCORES
    assert steps % (2 * workers) == 0, (n, steps)
    per_worker = steps // workers
    idx_rows = jnp.pad(idx.reshape(steps, step_rows), ((0, 0), (0, LANES - step_rows)))
    mesh = plsc.VectorSubcoreMesh(core_axis_name="core", subcore_axis_name="subcore")

    cost = pl.CostEstimate(flops=0, transcendentals=0,
                           bytes_accessed=2 * n * row_bytes + idx_rows.size * idx_rows.dtype.itemsize)

    @pl.kernel(out_type=jax.ShapeDtypeStruct((n, d), x.dtype), mesh=mesh, cost_estimate=cost,
               name="sc_gather_rows",
               scratch_types=[pltpu.VMEM((2, 1, LANES), jnp.int32),
                              pltpu.VMEM((2, step_rows, d), x.dtype),
                              pltpu.SemaphoreType.DMA((2,)),
                              pltpu.SemaphoreType.DMA((2,))])
    def gather(x_hbm, i_hbm, o_hbm, idx_v, buf, sem_in, sem_out):
        base = (lax.axis_index("core") * SC_SUBCORES + lax.axis_index("subcore")) * per_worker

        def gather_copy(slot):
            return pltpu.make_async_copy(x_hbm.at[idx_v.at[slot, 0, pl.ds(0, step_rows)]], buf.at[slot],
                                         sem_in.at[slot])

        def out_copy(step, slot):
            return pltpu.make_async_copy(buf.at[slot], o_hbm.at[pl.ds(step * step_rows, step_rows), :],
                                         sem_out.at[slot])

        @pl.loop(0, per_worker // 2)
        def _(j):
            for slot in range(2):
                step = base + 2 * j + slot

                @pl.when(j > 0)
                def _():
                    out_copy(step, slot).wait()

                pltpu.sync_copy(i_hbm.at[pl.ds(step, 1), :], idx_v.at[slot])
                gather_copy(slot).start()
            for slot in range(2):
                gather_copy(slot).wait()
                out_copy(base + 2 * j + slot, slot).start()

        for slot in range(2):
            out_copy(base, slot).wait()

    return gather(x, idx_rows)


def _expert_kernel(be_ref, x_ref, w1_ref, b1_ref, w2_ref, b2_ref, y_ref, w1_bf, w2_bf):
    i = pl.program_id(0)

    @pl.when(jnp.logical_or(i == 0, be_ref[i] != be_ref[jnp.maximum(i - 1, 0)]))
    def _():
        w1_bf[...] = w1_ref[0].astype(BF16)
        w2_bf[...] = w2_ref[0].astype(BF16)

    x = jnp.concatenate(_unpack_bf16_pair(x_ref[...]), axis=1).astype(BF16)
    hdn = jnp.dot(x, w1_bf[...], preferred_element_type=F32) + b1_ref[0]
    gate = jnp.minimum(hdn[:, :D_FF], SWIGLU_LIMIT)
    lin = jnp.clip(hdn[:, D_FF:], -SWIGLU_LIMIT, SWIGLU_LIMIT)
    act = gate * (1.0 / (1.0 + jnp.exp(-SWIGLU_ALPHA * gate))) * (lin + 1.0)
    y = jnp.dot(act.astype(BF16), w2_bf[...], preferred_element_type=F32) + b2_ref[0]
    y_ref[...] = _pack_bf16_pair(y[:, :D_MODEL // 2], y[:, D_MODEL // 2:])


def _experts(block_e, x_pad, w1, b1, w2, b2):
    n_pad = x_pad.shape[0]
    rows = EXPERT_ROWS
    grid_spec = pltpu.PrefetchScalarGridSpec(
        num_scalar_prefetch=1,
        grid=(n_pad // rows,),
        in_specs=[pl.BlockSpec((rows, D_MODEL // 2), lambda i, be: (i, 0)),
                  pl.BlockSpec((1, D_MODEL, 2 * D_FF), lambda i, be: (be[i], 0, 0)),
                  pl.BlockSpec((1, 1, 2 * D_FF), lambda i, be: (be[i], 0, 0)),
                  pl.BlockSpec((1, D_FF, D_MODEL), lambda i, be: (be[i], 0, 0)),
                  pl.BlockSpec((1, 1, D_MODEL), lambda i, be: (be[i], 0, 0))],
        out_specs=pl.BlockSpec((rows, D_MODEL // 2), lambda i, be: (i, 0)),
        scratch_shapes=[pltpu.VMEM((D_MODEL, 2 * D_FF), BF16), pltpu.VMEM((D_FF, D_MODEL), BF16)],
    )
    return pl.pallas_call(
        _expert_kernel,
        grid_spec=grid_spec,
        out_shape=jax.ShapeDtypeStruct((n_pad, D_MODEL // 2), PACKED),
        compiler_params=_cparams(("arbitrary",)),
        name="moe_experts",
    )(block_e, x_pad, w1, b1, w2, b2)


def _combine_kernel(x2_ref, gates_ref, gf_ref, y_ref, o_ref):
    gates = gates_ref[...]
    lo = x2_ref[:, :D_MODEL // 2]
    hi = x2_ref[:, D_MODEL // 2:]
    for kk in range(TOP_K):
        y_lo, y_hi = _unpack_bf16_pair(y_ref[kk])
        lo = lo + gates[:, kk:kk + 1] * y_lo
        hi = hi + gates[:, kk:kk + 1] * y_hi
    y = jnp.concatenate([lo, hi], axis=1)
    ms = jnp.mean(y * y, axis=-1, keepdims=True)
    o_ref[...] = y * lax.rsqrt(ms + RMS_EPS) * gf_ref[...]


def _combine(x2, gates, gf, y_sel):
    t = x2.shape[0]
    rows = min(COMBINE_ROWS, t)
    row = lambda i: (i, 0)
    return pl.pallas_call(
        _combine_kernel,
        grid=(t // rows,),
        in_specs=[pl.BlockSpec((rows, D_MODEL), row),
                  pl.BlockSpec((rows, LANES), row),
                  pl.BlockSpec((1, D_MODEL), lambda i: (0, 0)),
                  pl.BlockSpec((TOP_K, rows, D_MODEL // 2), lambda i: (0, i, 0))],
        out_specs=pl.BlockSpec((rows, D_MODEL), row),
        out_shape=jax.ShapeDtypeStruct((t, D_MODEL), F32),
        compiler_params=_cparams(("parallel",)),
        name="moe_combine",
    )(x2, gates, gf, y_sel)


def _dispatch_plan(top_e, n_tok):
    n = n_tok * TOP_K
    blk = EXPERT_ROWS
    flat_e = top_e.reshape(n)
    order = jnp.argsort(flat_e).astype(jnp.int32)
    rank = jnp.argsort(order).astype(jnp.int32)
    st = order // TOP_K
    experts = jnp.arange(N_EXPERTS, dtype=jnp.int32)
    counts = jnp.sum((flat_e[:, None] == experts).astype(jnp.int32), axis=0)
    padded = (counts + blk - 1) // blk * blk
    start = jnp.cumsum(counts) - counts
    pend = jnp.cumsum(padded)
    pstart = pend - padded
    pos = (pstart - start)[flat_e] + rank
    n_blocks = n // blk + N_EXPERTS
    block_start = jnp.arange(n_blocks, dtype=jnp.int32) * blk
    block_e = jnp.minimum(jnp.sum((pend[None, :] <= block_start[:, None]).astype(jnp.int32), axis=1),
                          N_EXPERTS - 1)
    slot = jnp.arange(n_blocks * blk, dtype=jnp.int32)
    slot_e = jnp.repeat(block_e, blk)
    off = slot - pstart[slot_e]
    src = jnp.clip(start[slot_e] + off, 0, n - 1)
    tok_pad = jnp.where(off < counts[slot_e], st[src], 0)
    return tok_pad, pos, block_e


def _trunk_front(x, prm):
    bsz, l, _ = x.shape
    t = bsz * l
    x2d = x.reshape(t, D_MODEL)
    q, k, v, u_t = _in_proj(x2d, prm['norm1_g'], prm['w_in'])

    t_att = min(512, l)
    att = _attention(q.reshape(bsz, l, W_ATT), k.reshape(bsz, l, W_ATT), v.reshape(bsz, l, W_ATT),
                     prm['bias_tiles'][t_att], prm['lam'], prm['subln_g'], t_att)

    yssm = _s5(u_t, prm['ssm_p'], prm['ssm_m'], prm['ssm_r'], prm['ssm_alpha'], l // SSM_CHUNK, bsz)

    x2, h2, ids, gates = _post_mix(x2d, att.reshape(t, W_ATT), yssm, prm['w_glu'], prm['b_glu'],
                                   prm['ssm_norm_g'], prm['w_out'], prm['norm2_g'],
                                   prm['w_router_hi'], prm['w_router_lo'], prm['b_router'])

    tok_pad, pos, block_e = _dispatch_plan(ids[:, :TOP_K], t)
    return {'shape': x.shape, 'x2': x2, 'h2': h2, 'gates': gates,
            'tok_pad': tok_pad, 'pos': pos, 'block_e': block_e}


def _trunk_back(st, x_pad, prm):
    t = st['x2'].shape[0]
    y_pad = _experts(st['block_e'], x_pad, prm['w_moe1'], prm['b_moe1'], prm['w_moe2'], prm['b_moe2'])
    pos_by_k = st['pos'].reshape(t, TOP_K).T.reshape(TOP_K * t)
    y_sel = _sc_gather_rows(y_pad, pos_by_k).reshape(TOP_K, t, D_MODEL // 2)
    out = _combine(st['x2'], st['gates'], prm['normf_g'], y_sel)
    return out.reshape(st['shape'])


def _prepare(seq_lens, rel_bias, norm1_g, w_in, lambda_q1, lambda_k1, lambda_q2, lambda_k2, subln_g,
             ssm_A_re, ssm_A_im, ssm_log_dt, ssm_B_re, ssm_B_im, ssm_C_re, ssm_C_im, ssm_D,
             w_glu, b_glu, ssm_norm_g, w_out, norm2_g, w_router, b_router,
             w_moe1, b_moe1, w_moe2, b_moe2, normf_g):
    layer = 0
    lambda_init = 0.8 - 0.6 * math.exp(-0.3 * layer)
    lam = (jnp.exp(jnp.sum(lambda_q1[layer].astype(F32) * lambda_k1[layer].astype(F32)))
           - jnp.exp(jnp.sum(lambda_q2[layer].astype(F32) * lambda_k2[layer].astype(F32))) + lambda_init)
    p_mat, m_mat, r_mat, alpha = _ssm_matrices(
        ssm_A_re[layer], ssm_A_im[layer], ssm_log_dt[layer], ssm_B_re[layer], ssm_B_im[layer],
        ssm_C_re[layer], ssm_C_im[layer], ssm_D[layer])
    pad_e = LANES - N_EXPERTS
    w_r = jnp.pad(w_router[layer].astype(F32), ((0, 0), (0, pad_e)))
    w_r_hi = w_r.astype(BF16)
    return {
        'norm1_g': norm1_g[layer].reshape(1, D_MODEL).astype(F32),
        'w_in': w_in[layer].astype(BF16),
        'lam': lam.reshape(1).astype(F32),
        'subln_g': (subln_g[layer].astype(F32) * (1.0 - lambda_init)).reshape(1, V_DIM),
        'bias_tiles': {t: _bias_tiles(rel_bias, t) for t in sorted({min(512, l) for l in seq_lens})},
        'ssm_p': p_mat, 'ssm_m': m_mat, 'ssm_r': r_mat, 'ssm_alpha': alpha,
        'w_glu': w_glu[layer].astype(BF16),
        'b_glu': b_glu[layer].reshape(1, W_SSM).astype(F32),
        'ssm_norm_g': ssm_norm_g[layer].reshape(1, W_SSM).astype(F32),
        'w_out': w_out[layer].astype(BF16),
        'norm2_g': norm2_g[layer].reshape(1, D_MODEL).astype(F32),
        'w_router_hi': w_r_hi,
        'w_router_lo': (w_r - w_r_hi.astype(F32)).astype(BF16),
        'b_router': jnp.pad(b_router[layer].astype(F32), (0, pad_e)).reshape(1, LANES),
        'w_moe1': w_moe1[layer].astype(F32),
        'b_moe1': b_moe1[layer].reshape(N_EXPERTS, 1, 2 * D_FF).astype(F32),
        'w_moe2': w_moe2[layer].astype(F32),
        'b_moe2': b_moe2[layer].reshape(N_EXPERTS, 1, D_MODEL).astype(F32),
        'normf_g': normf_g.reshape(1, D_MODEL).astype(F32),
    }


def kernel(x_prompt, x_sample, rel_bias, norm1_g, w_in, lambda_q1, lambda_k1, lambda_q2, lambda_k2, subln_g, ssm_A_re, ssm_A_im, ssm_log_dt, ssm_B_re, ssm_B_im, ssm_C_re, ssm_C_im, ssm_D, w_glu, b_glu, ssm_norm_g, w_out, norm2_g, w_router, b_router, w_moe1, b_moe1, w_moe2, b_moe2, normf_g):
    prm = _prepare((x_prompt.shape[1], x_sample.shape[1]), rel_bias, norm1_g, w_in, lambda_q1,
                   lambda_k1, lambda_q2, lambda_k2, subln_g, ssm_A_re, ssm_A_im, ssm_log_dt,
                   ssm_B_re, ssm_B_im, ssm_C_re, ssm_C_im, ssm_D, w_glu, b_glu, ssm_norm_g, w_out,
                   norm2_g, w_router, b_router, w_moe1, b_moe1, w_moe2, b_moe2, normf_g)
    first = _trunk_front(x_prompt, prm)
    x_sample, tok_pad = lax.optimization_barrier((x_sample, first['tok_pad']))
    x_pad_first = _sc_gather_rows(first['h2'], tok_pad)
    second = _trunk_front(x_sample, prm)
    x_pad_second = _sc_gather_rows(second['h2'], second['tok_pad'])
    return (_trunk_back(first, x_pad_first, prm), _trunk_back(second, x_pad_second, prm))
```

```python
import functools
import math

import jax
import jax.numpy as jnp
from jax import lax
from jax.experimental import pallas as pl
from jax.experimental.pallas import tpu as pltpu
from jax.experimental.pallas import tpu_sc as plsc

F32 = jnp.float32
BF16 = jnp.bfloat16
PACKED = jnp.uint32

D_MODEL = 1024
W_ATT = 512
W_SSM = 512
HEAD_DIM = 64
N_HEADS = 4
V_DIM = 2 * HEAD_DIM
SSM_GROUP = 16
N_GROUPS = W_SSM // SSM_GROUP
SSM_STATE = 64
IN_WIDTH = 3 * W_ATT + W_SSM
N_BUCKETS = 32
MAX_DISTANCE = 128
N_EXPERTS = 32
TOP_K = 4
D_FF = D_MODEL
SWIGLU_ALPHA = 1.702
SWIGLU_LIMIT = 7.0
RMS_EPS = 1e-6
ATT_SCALE = HEAD_DIM ** -0.5
LOG2E = math.log2(math.e)

LANES = 128
PV_KEYS = 256
SSM_CHUNK = LANES
SSM_ROW = SSM_CHUNK * SSM_GROUP
ROW_TILE = 1024
EXPERT_ROWS = 512
SC_STEP_BYTES = 128 * 1024
SC_CORES = 2
SC_SUBCORES = 16
COMBINE_ROWS = 256
VMEM_LIMIT = 56 * 1024 * 1024


def _pack_bf16_pair(lo, hi):
    lo_bits = lax.bitcast_convert_type(lo.astype(BF16).astype(F32), PACKED)
    hi_bits = lax.bitcast_convert_type(hi.astype(BF16).astype(F32), PACKED)
    return (hi_bits & jnp.uint32(0xFFFF0000)) | (lo_bits >> 16)


def _unpack_bf16_pair(words):
    lo = lax.bitcast_convert_type(words << 16, F32)
    hi = lax.bitcast_convert_type(words & jnp.uint32(0xFFFF0000), F32)
    return lo, hi


def _cparams(sem):
    return pltpu.CompilerParams(dimension_semantics=sem, vmem_limit_bytes=VMEM_LIMIT)


def _in_proj_kernel(x_ref, g_ref, w_ref, q_ref, k_ref, v_ref, ut_ref):
    x = x_ref[...]
    ms = jnp.mean(x * x, axis=-1, keepdims=True)
    h = (x * lax.rsqrt(ms + RMS_EPS) * g_ref[...]).astype(BF16)
    proj = jnp.dot(h, w_ref[...], preferred_element_type=F32)
    q_ref[...] = (proj[:, 0:W_ATT] * (ATT_SCALE * LOG2E)).astype(BF16)
    k_ref[...] = proj[:, W_ATT:2 * W_ATT].astype(BF16)
    v_ref[...] = proj[:, 2 * W_ATT:3 * W_ATT].astype(BF16)
    ut = proj[:, 3 * W_ATT:].T
    for j in range(ut_ref.shape[1]):
        ut_ref[:, j, :] = ut[:, j * SSM_CHUNK:(j + 1) * SSM_CHUNK]


def _in_proj(x2d, g, w_bf16):
    t = x2d.shape[0]
    tm = min(ROW_TILE, t)
    out = jax.ShapeDtypeStruct((t, W_ATT), BF16)
    row = lambda i: (i, 0)
    return pl.pallas_call(
        _in_proj_kernel,
        grid=(t // tm,),
        in_specs=[pl.BlockSpec((tm, D_MODEL), row),
                  pl.BlockSpec((1, D_MODEL), lambda i: (0, 0)),
                  pl.BlockSpec((D_MODEL, IN_WIDTH), lambda i: (0, 0))],
        out_specs=[pl.BlockSpec((tm, W_ATT), row)] * 3
        + [pl.BlockSpec((W_SSM, tm // SSM_CHUNK, SSM_CHUNK), lambda i: (0, i, 0))],
        out_shape=[out] * 3 + [jax.ShapeDtypeStruct((W_SSM, t // SSM_CHUNK, SSM_CHUNK), F32)],
        compiler_params=_cparams(("parallel",)),
        name="in_proj",
    )(x2d, g, w_bf16)


def _t5_bucket(rel):
    half = N_BUCKETS // 2
    max_exact = half // 2
    ret = jnp.where(rel > 0, half, 0).astype(jnp.int32)
    n = jnp.abs(rel)
    nf = jnp.maximum(n, 1).astype(F32)
    large = max_exact + (jnp.log(nf / max_exact) / math.log(MAX_DISTANCE / max_exact)
                         * (half - max_exact)).astype(jnp.int32)
    large = jnp.minimum(large, half - 1)
    return ret + jnp.where(n < max_exact, n, large)


def _bias_tiles(rel_bias, t):
    i = jnp.arange(t, dtype=jnp.int32)
    d = jnp.arange(-2, 3, dtype=jnp.int32)
    rel = d[:, None, None] * t + i[None, None, :] - i[None, :, None]
    onehot = (_t5_bucket(rel)[..., None] == jnp.arange(N_BUCKETS, dtype=jnp.int32)).astype(F32)
    tiles = jnp.einsum('dqkn,nh->hdqk', onehot, rel_bias.astype(F32), precision=lax.Precision.HIGHEST)
    return tiles * LOG2E


def _attn_kernel(lam_ref, q_ref, k_ref, v_ref, bias_ref, g_ref, o_ref,
                 qs_scr, m_scr, l_scr, acc_scr, sa_scr, sb_scr, mxa_scr, mxb_scr,
                 *, t, sub, n_iter, tiles):
    q0 = pl.program_id(2) * tiles
    nb = t // LANES
    stream = tiles * n_iter

    for ti in range(tiles):
        q = q_ref[0, ti * t:(ti + 1) * t, :]
        lane = lax.broadcasted_iota(jnp.int32, q.shape, 1)
        zero = jnp.zeros_like(q)
        qs_scr[ti, 0] = jnp.where(lane < HEAD_DIM, q, zero)
        qs_scr[ti, 1] = jnp.where(lane >= HEAD_DIM, q, zero)

    def scores(i, s_scr, mx_scr):
        tile, j = i // n_iter, i % n_iter
        for mi in range(2):
            qm = qs_scr[tile, mi]
            mx = None
            for c in range(sub):
                blk = j * sub + c
                kc = k_ref[0, pl.ds(pl.multiple_of(blk * t, t), t), :]
                s = lax.dot_general(qm, kc, (((1,), (1,)), ((), ())), preferred_element_type=F32)
                s = s + bias_ref[0, jnp.clip(blk - (q0 + tile), -2, 2) + 2]
                s_scr[mi, :, c * t:(c + 1) * t] = s
                for r in range(nb):
                    piece = s[:, r * LANES:(r + 1) * LANES]
                    mx = piece if mx is None else jnp.maximum(mx, piece)
            mx_scr[mi] = mx

    def accumulate(i, s_scr, mx_scr):
        j = i % n_iter
        fresh = j == 0
        vj = v_ref[0, pl.ds(pl.multiple_of(j * (sub * t), sub * t), sub * t), :]
        for mi in range(2):
            m_prev = jnp.where(fresh, -jnp.inf, m_scr[mi])
            m_next = jnp.maximum(m_prev, jnp.max(mx_scr[mi], axis=1, keepdims=True))
            alpha = jnp.exp2(m_prev - m_next)
            m_scr[mi] = m_next
            lsum = None
            pv = None
            for r in range(sub * t // PV_KEYS):
                ps = []
                for c in range(PV_KEYS // LANES):
                    lo = r * PV_KEYS + c * LANES
                    p = jnp.exp2(s_scr[mi, :, lo:lo + LANES] - m_next)
                    lsum = p if lsum is None else lsum + p
                    ps.append(p.astype(BF16))
                d = jnp.dot(jnp.concatenate(ps, axis=1), vj[r * PV_KEYS:(r + 1) * PV_KEYS, :],
                            preferred_element_type=F32)
                pv = d if pv is None else pv + d
            l_scr[mi] = alpha * jnp.where(fresh, 0.0, l_scr[mi]) + lsum
            acc_scr[mi] = alpha * jnp.where(fresh, 0.0, acc_scr[mi]) + pv

    def finish(tile):
        o1 = acc_scr[0] / jnp.sum(l_scr[0], axis=1, keepdims=True)
        o2 = acc_scr[1] / jnp.sum(l_scr[1], axis=1, keepdims=True)
        o = o1 - lam_ref[0] * o2
        ms = jnp.mean(o * o, axis=-1, keepdims=True)
        o_ref[0, pl.ds(pl.multiple_of(tile * t, t), t), :] = (
            o * lax.rsqrt(ms + RMS_EPS) * g_ref[...]).astype(BF16)

    scores(0, sa_scr, mxa_scr)
    n_pairs = (stream - 1) // 2

    def body(i, carry):
        scores(2 * i + 1, sb_scr, mxb_scr)
        accumulate(2 * i, sa_scr, mxa_scr)
        scores(2 * i + 2, sa_scr, mxa_scr)
        accumulate(2 * i + 1, sb_scr, mxb_scr)
        if tiles > 1:
            @pl.when((2 * i + 1) % n_iter == n_iter - 1)
            def _():
                finish((2 * i + 1) // n_iter)
        return carry

    lax.fori_loop(0, n_pairs, body, 0)
    if (stream - 1) % 2 == 1:
        scores(stream - 1, sb_scr, mxb_scr)
        accumulate(stream - 2, sa_scr, mxa_scr)
        accumulate(stream - 1, sb_scr, mxb_scr)
    else:
        accumulate(stream - 1, sa_scr, mxa_scr)
    finish(tiles - 1)


def _attention(q, k, v, bias_tiles, lam, g_scaled, t):
    b, l, _ = q.shape
    nq = l // t
    sub = 2 if nq % 2 == 0 else 1
    n_iter = nq // sub
    tiles = 1
    if n_iter % 2 == 0:
        tiles = 4 if nq % 4 == 0 else 2
    kern = functools.partial(_attn_kernel, t=t, sub=sub, n_iter=n_iter, tiles=tiles)
    return pl.pallas_call(
        kern,
        grid=(b, N_HEADS, nq // tiles),
        in_specs=[pl.BlockSpec(memory_space=pltpu.SMEM),
                  pl.BlockSpec((1, tiles * t, V_DIM), lambda bi, h, qi: (bi, qi, h)),
                  pl.BlockSpec((1, l, V_DIM), lambda bi, h, qi: (bi, 0, h)),
                  pl.BlockSpec((1, l, V_DIM), lambda bi, h, qi: (bi, 0, h)),
                  pl.BlockSpec((1, 5, t, t), lambda bi, h, qi: (h, 0, 0, 0)),
                  pl.BlockSpec((1, V_DIM), lambda bi, h, qi: (0, 0))],
        out_specs=pl.BlockSpec((1, tiles * t, V_DIM), lambda bi, h, qi: (bi, qi, h)),
        out_shape=jax.ShapeDtypeStruct((b, l, W_ATT), BF16),
        scratch_shapes=[pltpu.VMEM((tiles, 2, t, V_DIM), BF16),
                        pltpu.VMEM((2, t, LANES), F32),
                        pltpu.VMEM((2, t, LANES), F32),
                        pltpu.VMEM((2, t, V_DIM), F32),
                        pltpu.VMEM((2, t, sub * t), F32),
                        pltpu.VMEM((2, t, sub * t), F32),
                        pltpu.VMEM((2, t, LANES), F32),
                        pltpu.VMEM((2, t, LANES), F32)],
        compiler_params=_cparams(("parallel", "parallel", "arbitrary")),
        name="diff_attention",
    )(lam, q, k, v, bias_tiles, g_scaled)


def _ssm_matrices(a_re, a_im, log_dt, b_re, b_im, c_re, c_im, d_skip):
    qn, g, p, hc = SSM_CHUNK, N_GROUPS, SSM_STATE, SSM_GROUP
    n = jnp.arange(qn + 1, dtype=F32)
    pw, bbar, cc = [], [], []
    for d in range(2):
        a = lax.complex(a_re[d].astype(F32), a_im[d].astype(F32))
        dt = jnp.exp(log_dt[d].astype(F32))[:, None]
        adt = a * dt
        a_bar = jnp.exp(adt)
        pw.append(jnp.exp(adt[None] * n[:, None, None]))
        bbar.append(((a_bar - 1.0) / a)[:, :, None]
                    * lax.complex(b_re[d].astype(F32), b_im[d].astype(F32)))
        cc.append(lax.complex(c_re[d].astype(F32), c_im[d].astype(F32)))

    hi = lax.Precision.HIGHEST
    kern = [jnp.einsum('gop,tgp,gpi->tgoi', cc[d], pw[d][:qn], bbar[d], precision=hi).real
            for d in range(2)]
    kf = jnp.transpose(kern[0], (1, 3, 2, 0))
    kb = jnp.transpose(kern[1], (1, 3, 2, 0))
    skip = jnp.eye(hc, dtype=F32)[None] * d_skip.astype(F32).reshape(g, hc)[:, :, None]
    taps = jnp.concatenate([kb[..., :0:-1], (kf[..., 0] + kb[..., 0] + skip)[..., None], kf[..., 1:]], axis=-1)
    taps = taps.astype(BF16)
    m_mat = jnp.stack([taps[..., qn - 1 - s:2 * qn - 1 - s] for s in range(qn)], axis=2)
    m_mat = m_mat.reshape(g, SSM_ROW, SSM_ROW)

    zeros_p = jnp.zeros((g, SSM_ROW, LANES - p), F32)

    def pad_cols(x):
        return jnp.concatenate([x, zeros_p], axis=-1)

    pf = jnp.einsum('sgp,gpi->gisp', pw[0][:qn][::-1], bbar[0]).reshape(g, SSM_ROW, p)
    pb = jnp.einsum('sgp,gpi->gisp', pw[1][:qn], bbar[1]).reshape(g, SSM_ROW, p)
    p_mat = jnp.concatenate([pad_cols(pf.real), pad_cols(pf.imag),
                             pad_cols(pb.real), pad_cols(pb.imag)], axis=-1)

    wf = jnp.einsum('gop,tgp->gpot', cc[0], pw[0][1:qn + 1]).reshape(g, p, SSM_ROW)
    wb = jnp.einsum('gop,tgp->gpot', cc[1], pw[1][1:qn + 1][::-1]).reshape(g, p, SSM_ROW)
    zeros_r = jnp.zeros((g, LANES - p, SSM_ROW), F32)
    r_mat = jnp.concatenate([wf.real, zeros_r, -wf.imag, zeros_r,
                             wb.real, zeros_r, -wb.imag, zeros_r], axis=1)

    zeros_a = jnp.zeros((g, LANES - p), F32)

    def pad_vec(x):
        return jnp.concatenate([x, zeros_a], axis=-1)

    alpha = jnp.stack([pad_vec(pw[0][qn].real), pad_vec(pw[0][qn].imag),
                       pad_vec(pw[1][qn].real), pad_vec(pw[1][qn].imag)], axis=1)
    return p_mat.astype(BF16), m_mat.astype(BF16), r_mat.astype(BF16), alpha


def _s5_kernel(u_ref, p_ref, m_ref, r_ref, a_ref, y_ref, s_scr, x_scr, *, nc, bsz):
    u = jnp.concatenate([u_ref[h] for h in range(SSM_GROUP)], axis=-1).astype(BF16)
    s = jnp.dot(u, p_ref[0], preferred_element_type=F32)
    for part in range(4):
        s_scr[part] = s[:, part * LANES:(part + 1) * LANES]
    al = a_ref[0]
    afr = jnp.broadcast_to(al[0:1], (bsz, LANES))
    afi = jnp.broadcast_to(al[1:2], (bsz, LANES))
    abr = jnp.broadcast_to(al[2:3], (bsz, LANES))
    abi = jnp.broadcast_to(al[3:4], (bsz, LANES))
    fr = fi = br = bi = jnp.zeros((bsz, LANES), F32)
    for c in range(nc):
        rf = pl.ds(c, bsz, stride=nc)
        rb = pl.ds(nc - 1 - c, bsz, stride=nc)
        x_scr[0, rf, :] = fr
        x_scr[1, rf, :] = fi
        x_scr[2, rb, :] = br
        x_scr[3, rb, :] = bi
        sfr = s_scr[0, rf, :]
        sfi = s_scr[1, rf, :]
        sbr = s_scr[2, rb, :]
        sbi = s_scr[3, rb, :]
        fr, fi = afr * fr - afi * fi + sfr, afr * fi + afi * fr + sfi
        br, bi = abr * br - abi * bi + sbr, abr * bi + abi * br + sbi
    y = jnp.dot(u, m_ref[0], preferred_element_type=F32)
    x_in = jnp.concatenate([x_scr[part] for part in range(4)], axis=-1).astype(BF16)
    y = y + jnp.dot(x_in, r_ref[0], preferred_element_type=F32)
    for h in range(SSM_GROUP):
        y_ref[h] = y[:, h * SSM_CHUNK:(h + 1) * SSM_CHUNK]


def _s5(u_t, p_mat, m_mat, r_mat, alpha, nc, bsz):
    rows = u_t.shape[1]
    kern = functools.partial(_s5_kernel, nc=nc, bsz=bsz)
    seq = pl.BlockSpec((SSM_GROUP, rows, SSM_CHUNK), lambda i: (i, 0, 0))
    return pl.pallas_call(
        kern,
        grid=(N_GROUPS,),
        in_specs=[seq,
                  pl.BlockSpec((1, SSM_ROW, 4 * LANES), lambda i: (i, 0, 0)),
                  pl.BlockSpec((1, SSM_ROW, SSM_ROW), lambda i: (i, 0, 0)),
                  pl.BlockSpec((1, 4 * LANES, SSM_ROW), lambda i: (i, 0, 0)),
                  pl.BlockSpec((1, 4, LANES), lambda i: (i, 0, 0))],
        out_specs=seq,
        out_shape=jax.ShapeDtypeStruct(u_t.shape, F32),
        scratch_shapes=[pltpu.VMEM((4, rows, LANES), F32), pltpu.VMEM((4, rows, LANES), F32)],
        compiler_params=_cparams(("parallel",)),
        name="s5_scan",
    )(u_t, p_mat, m_mat, r_mat, alpha)


def _post_mix_kernel(x_ref, att_ref, y_ref, wglu_ref, bglu_ref, gs_ref, wout_ref, g2_ref,
                     wrh_ref, wrl_ref, br_ref, x2_ref, h2_ref, ids_ref, gates_ref):
    y = jnp.concatenate([y_ref[:, j, :].T for j in range(y_ref.shape[1])], axis=0)
    y = 0.5 * y * (1.0 + jnp.tanh(math.sqrt(2.0 / math.pi) * (y + 0.044715 * (y * y * y))))
    z = jnp.dot(y.astype(BF16), wglu_ref[...], preferred_element_type=F32) + bglu_ref[...]
    y = y * (1.0 / (1.0 + jnp.exp(-z)))
    ms = jnp.mean(y * y, axis=-1, keepdims=True)
    ssm = (y * lax.rsqrt(ms + RMS_EPS) * gs_ref[...]).astype(BF16)
    mix = jnp.dot(att_ref[...], wout_ref[0:W_ATT, :], preferred_element_type=F32)
    mix = mix + jnp.dot(ssm, wout_ref[W_ATT:, :], preferred_element_type=F32)
    x2 = x_ref[...] + mix
    x2_ref[...] = x2
    ms2 = jnp.mean(x2 * x2, axis=-1, keepdims=True)
    h2 = x2 * lax.rsqrt(ms2 + RMS_EPS) * g2_ref[...]
    h2_ref[...] = _pack_bf16_pair(h2[:, :D_MODEL // 2], h2[:, D_MODEL // 2:])
    h_hi = h2.astype(BF16)
    h_lo = (h2 - h_hi.astype(F32)).astype(BF16)
    logits = (jnp.dot(h_hi, wrh_ref[...], preferred_element_type=F32)
              + jnp.dot(h_lo, wrh_ref[...], preferred_element_type=F32)
              + jnp.dot(h_hi, wrl_ref[...], preferred_element_type=F32)) + br_ref[...]
    lane = lax.broadcasted_iota(jnp.int32, logits.shape, 1).astype(F32)
    neg = jnp.float32(-jnp.inf)
    cur = jnp.where(lane < N_EXPERTS, logits, neg)
    ids = jnp.zeros(logits.shape, F32)
    vals = jnp.zeros(logits.shape, F32)
    top = None
    den = None
    for kk in range(TOP_K):
        mx = jnp.max(cur, axis=1, keepdims=True)
        idx = jnp.min(jnp.where(cur == mx, lane, float(LANES)), axis=1, keepdims=True)
        if kk == 0:
            top = mx
        e = jnp.exp(mx - top)
        den = e if den is None else den + e
        ids = jnp.where(lane == kk, idx, ids)
        vals = jnp.where(lane == kk, e, vals)
        cur = jnp.where(lane == idx, neg, cur)
    ids_ref[...] = ids.astype(jnp.int32)
    gates_ref[...] = vals / den


def _post_mix(x2d, att, yssm, wglu, bglu, gs, wout, g2, wr_hi, wr_lo, br):
    t = x2d.shape[0]
    tm = min(ROW_TILE, t)
    row = lambda i: (i, 0)
    const = lambda i: (0, 0)
    return pl.pallas_call(
        _post_mix_kernel,
        grid=(t // tm,),
        in_specs=[pl.BlockSpec((tm, D_MODEL), row),
                  pl.BlockSpec((tm, W_ATT), row),
                  pl.BlockSpec((W_SSM, tm // SSM_CHUNK, SSM_CHUNK), lambda i: (0, i, 0)),
                  pl.BlockSpec((W_SSM, W_SSM), const),
                  pl.BlockSpec((1, W_SSM), const),
                  pl.BlockSpec((1, W_SSM), const),
                  pl.BlockSpec((D_MODEL, D_MODEL), const),
                  pl.BlockSpec((1, D_MODEL), const),
                  pl.BlockSpec((D_MODEL, LANES), const),
                  pl.BlockSpec((D_MODEL, LANES), const),
                  pl.BlockSpec((1, LANES), const)],
        out_specs=[pl.BlockSpec((tm, D_MODEL), row),
                   pl.BlockSpec((tm, D_MODEL // 2), row),
                   pl.BlockSpec((tm, LANES), row),
                   pl.BlockSpec((tm, LANES), row)],
        out_shape=[jax.ShapeDtypeStruct((t, D_MODEL), F32),
                   jax.ShapeDtypeStruct((t, D_MODEL // 2), PACKED),
                   jax.ShapeDtypeStruct((t, LANES), jnp.int32),
                   jax.ShapeDtypeStruct((t, LANES), F32)],
        compiler_params=_cparams(("parallel",)),
        name="post_mix",
    )(x2d, att, yssm, wglu, bglu, gs, wout, g2, wr_hi, wr_lo, br)


def _sc_gather_rows(x, idx):
    n = idx.shape[0]
    d = x.shape[1]
    row_bytes = d * x.dtype.itemsize
    step_rows = SC_STEP_BYTES // row_bytes
    steps = n // step_rows
    workers = SC_CORES * SC_SUBCORES
    assert steps % (2 * workers) == 0, (n, steps)
    per_worker = steps // workers
    idx_rows = jnp.pad(idx.reshape(steps, step_rows), ((0, 0), (0, LANES - step_rows)))
    mesh = plsc.VectorSubcoreMesh(core_axis_name="core", subcore_axis_name="subcore")

    cost = pl.CostEstimate(flops=0, transcendentals=0,
                           bytes_accessed=2 * n * row_bytes + idx_rows.size * idx_rows.dtype.itemsize)

    @pl.kernel(out_type=jax.ShapeDtypeStruct((n, d), x.dtype), mesh=mesh, cost_estimate=cost,
               name="sc_gather_rows",
               scratch_types=[pltpu.VMEM((2, 1, LANES), jnp.int32),
                              pltpu.VMEM((2, step_rows, d), x.dtype),
                              pltpu.SemaphoreType.DMA((2,)),
                              pltpu.SemaphoreType.DMA((2,))])
    def gather(x_hbm, i_hbm, o_hbm, idx_v, buf, sem_in, sem_out):
        base = (lax.axis_index("core") * SC_SUBCORES + lax.axis_index("subcore")) * per_worker

        def gather_copy(slot):
            return pltpu.make_async_copy(x_hbm.at[idx_v.at[slot, 0, pl.ds(0, step_rows)]], buf.at[slot],
                                         sem_in.at[slot])

        def out_copy(step, slot):
            return pltpu.make_async_copy(buf.at[slot], o_hbm.at[pl.ds(step * step_rows, step_rows), :],
                                         sem_out.at[slot])

        @pl.loop(0, per_worker // 2)
        def _(j):
            for slot in range(2):
                step = base + 2 * j + slot

                @pl.when(j > 0)
                def _():
                    out_copy(step, slot).wait()

                pltpu.sync_copy(i_hbm.at[pl.ds(step, 1), :], idx_v.at[slot])
                gather_copy(slot).start()
            for slot in range(2):
                gather_copy(slot).wait()
                out_copy(base + 2 * j + slot, slot).start()

        for slot in range(2):
            out_copy(base, slot).wait()

    return gather(x, idx_rows)


def _expert_kernel(be_ref, x_ref, w1_ref, b1_ref, w2_ref, b2_ref, y_ref, w1_bf, w2_bf):
    i = pl.program_id(0)

    @pl.when(jnp.logical_or(i == 0, be_ref[i] != be_ref[jnp.maximum(i - 1, 0)]))
    def _():
        w1_bf[...] = w1_ref[0].astype(BF16)
        w2_bf[...] = w2_ref[0].astype(BF16)

    x = jnp.concatenate(_unpack_bf16_pair(x_ref[...]), axis=1).astype(BF16)
    hdn = jnp.dot(x, w1_bf[...], preferred_element_type=F32) + b1_ref[0]
    gate = jnp.minimum(hdn[:, :D_FF], SWIGLU_LIMIT)
    lin = jnp.clip(hdn[:, D_FF:], -SWIGLU_LIMIT, SWIGLU_LIMIT)
    act = gate * (1.0 / (1.0 + jnp.exp(-SWIGLU_ALPHA * gate))) * (lin + 1.0)
    y = jnp.dot(act.astype(BF16), w2_bf[...], preferred_element_type=F32) + b2_ref[0]
    y_ref[...] = _pack_bf16_pair(y[:, :D_MODEL // 2], y[:, D_MODEL // 2:])


def _experts(block_e, x_pad, w1, b1, w2, b2):
    n_pad = x_pad.shape[0]
    rows = EXPERT_ROWS
    grid_spec = pltpu.PrefetchScalarGridSpec(
        num_scalar_prefetch=1,
        grid=(n_pad // rows,),
        in_specs=[pl.BlockSpec((rows, D_MODEL // 2), lambda i, be: (i, 0)),
                  pl.BlockSpec((1, D_MODEL, 2 * D_FF), lambda i, be: (be[i], 0, 0)),
                  pl.BlockSpec((1, 1, 2 * D_FF), lambda i, be: (be[i], 0, 0)),
                  pl.BlockSpec((1, D_FF, D_MODEL), lambda i, be: (be[i], 0, 0)),
                  pl.BlockSpec((1, 1, D_MODEL), lambda i, be: (be[i], 0, 0))],
        out_specs=pl.BlockSpec((rows, D_MODEL // 2), lambda i, be: (i, 0)),
        scratch_shapes=[pltpu.VMEM((D_MODEL, 2 * D_FF), BF16), pltpu.VMEM((D_FF, D_MODEL), BF16)],
    )
    return pl.pallas_call(
        _expert_kernel,
        grid_spec=grid_spec,
        out_shape=jax.ShapeDtypeStruct((n_pad, D_MODEL // 2), PACKED),
        compiler_params=_cparams(("arbitrary",)),
        name="moe_experts",
    )(block_e, x_pad, w1, b1, w2, b2)


def _combine_kernel(x2_ref, gates_ref, gf_ref, y_ref, o_ref):
    gates = gates_ref[...]
    lo = x2_ref[:, :D_MODEL // 2]
    hi = x2_ref[:, D_MODEL // 2:]
    for kk in range(TOP_K):
        y_lo, y_hi = _unpack_bf16_pair(y_ref[kk])
        lo = lo + gates[:, kk:kk + 1] * y_lo
        hi = hi + gates[:, kk:kk + 1] * y_hi
    y = jnp.concatenate([lo, hi], axis=1)
    ms = jnp.mean(y * y, axis=-1, keepdims=True)
    o_ref[...] = y * lax.rsqrt(ms + RMS_EPS) * gf_ref[...]


def _combine(x2, gates, gf, y_sel):
    t = x2.shape[0]
    rows = min(COMBINE_ROWS, t)
    row = lambda i: (i, 0)
    return pl.pallas_call(
        _combine_kernel,
        grid=(t // rows,),
        in_specs=[pl.BlockSpec((rows, D_MODEL), row),
                  pl.BlockSpec((rows, LANES), row),
                  pl.BlockSpec((1, D_MODEL), lambda i: (0, 0)),
                  pl.BlockSpec((TOP_K, rows, D_MODEL // 2), lambda i: (0, i, 0))],
        out_specs=pl.BlockSpec((rows, D_MODEL), row),
        out_shape=jax.ShapeDtypeStruct((t, D_MODEL), F32),
        compiler_params=_cparams(("parallel",)),
        name="moe_combine",
    )(x2, gates, gf, y_sel)


def _dispatch_plan(top_e, n_tok):
    n = n_tok * TOP_K
    blk = EXPERT_ROWS
    flat_e = top_e.reshape(n)
    order = jnp.argsort(flat_e).astype(jnp.int32)
    rank = jnp.argsort(order).astype(jnp.int32)
    st = order // TOP_K
    experts = jnp.arange(N_EXPERTS, dtype=jnp.int32)
    counts = jnp.sum((flat_e[:, None] == experts).astype(jnp.int32), axis=0)
    padded = (counts + blk - 1) // blk * blk
    start = jnp.cumsum(counts) - counts
    pend = jnp.cumsum(padded)
    pstart = pend - padded
    pos = (pstart - start)[flat_e] + rank
    n_blocks = n // blk + N_EXPERTS
    block_start = jnp.arange(n_blocks, dtype=jnp.int32) * blk
    block_e = jnp.minimum(jnp.sum((pend[None, :] <= block_start[:, None]).astype(jnp.int32), axis=1),
                          N_EXPERTS - 1)
    slot = jnp.arange(n_blocks * blk, dtype=jnp.int32)
    slot_e = jnp.repeat(block_e, blk)
    off = slot - pstart[slot_e]
    src = jnp.clip(start[slot_e] + off, 0, n - 1)
    tok_pad = jnp.where(off < counts[slot_e], st[src], 0)
    return tok_pad, pos, block_e


def _trunk_front(x, prm):
    bsz, l, _ = x.shape
    t = bsz * l
    x2d = x.reshape(t, D_MODEL)
    q, k, v, u_t = _in_proj(x2d, prm['norm1_g'], prm['w_in'])

    t_att = min(512, l)
    att = _attention(q.reshape(bsz, l, W_ATT), k.reshape(bsz, l, W_ATT), v.reshape(bsz, l, W_ATT),
                     prm['bias_tiles'][t_att], prm['lam'], prm['subln_g'], t_att)

    yssm = _s5(u_t, prm['ssm_p'], prm['ssm_m'], prm['ssm_r'], prm['ssm_alpha'], l // SSM_CHUNK, bsz)

    x2, h2, ids, gates = _post_mix(x2d, att.reshape(t, W_ATT), yssm, prm['w_glu'], prm['b_glu'],
                                   prm['ssm_norm_g'], prm['w_out'], prm['norm2_g'],
                                   prm['w_router_hi'], prm['w_router_lo'], prm['b_router'])

    tok_pad, pos, block_e = _dispatch_plan(ids[:, :TOP_K], t)
    return {'shape': x.shape, 'x2': x2, 'h2': h2, 'gates': gates,
            'tok_pad': tok_pad, 'pos': pos, 'block_e': block_e}


def _trunk_back(st, x_pad, prm):
    t = st['x2'].shape[0]
    y_pad = _experts(st['block_e'], x_pad, prm['w_moe1'], prm['b_moe1'], prm['w_moe2'], prm['b_moe2'])
    pos_by_k = st['pos'].reshape(t, TOP_K).T.reshape(TOP_K * t)
    y_sel = _sc_gather_rows(y_pad, pos_by_k).reshape(TOP_K, t, D_MODEL // 2)
    out = _combine(st['x2'], st['gates'], prm['normf_g'], y_sel)
    return out.reshape(st['shape'])


def _prepare(seq_lens, rel_bias, norm1_g, w_in, lambda_q1, lambda_k1, lambda_q2, lambda_k2, subln_g,
             ssm_A_re, ssm_A_im, ssm_log_dt, ssm_B_re, ssm_B_im, ssm_C_re, ssm_C_im, ssm_D,
             w_glu, b_glu, ssm_norm_g, w_out, norm2_g, w_router, b_router,
             w_moe1, b_moe1, w_moe2, b_moe2, normf_g):
    layer = 0
    lambda_init = 0.8 - 0.6 * math.exp(-0.3 * layer)
    lam = (jnp.exp(jnp.sum(lambda_q1[layer].astype(F32) * lambda_k1[layer].astype(F32)))
           - jnp.exp(jnp.sum(lambda_q2[layer].astype(F32) * lambda_k2[layer].astype(F32))) + lambda_init)
    p_mat, m_mat, r_mat, alpha = _ssm_matrices(
        ssm_A_re[layer], ssm_A_im[layer], ssm_log_dt[layer], ssm_B_re[layer], ssm_B_im[layer],
        ssm_C_re[layer], ssm_C_im[layer], ssm_D[layer])
    pad_e = LANES - N_EXPERTS
    w_r = jnp.pad(w_router[layer].astype(F32), ((0, 0), (0, pad_e)))
    w_r_hi = w_r.astype(BF16)
    return {
        'norm1_g': norm1_g[layer].reshape(1, D_MODEL).astype(F32),
        'w_in': w_in[layer].astype(BF16),
        'lam': lam.reshape(1).astype(F32),
        'subln_g': (subln_g[layer].astype(F32) * (1.0 - lambda_init)).reshape(1, V_DIM),
        'bias_tiles': {t: _bias_tiles(rel_bias, t) for t in sorted({min(512, l) for l in seq_lens})},
        'ssm_p': p_mat, 'ssm_m': m_mat, 'ssm_r': r_mat, 'ssm_alpha': alpha,
        'w_glu': w_glu[layer].astype(BF16),
        'b_glu': b_glu[layer].reshape(1, W_SSM).astype(F32),
        'ssm_norm_g': ssm_norm_g[layer].reshape(1, W_SSM).astype(F32),
        'w_out': w_out[layer].astype(BF16),
        'norm2_g': norm2_g[layer].reshape(1, D_MODEL).astype(F32),
        'w_router_hi': w_r_hi,
        'w_router_lo': (w_r - w_r_hi.astype(F32)).astype(BF16),
        'b_router': jnp.pad(b_router[layer].astype(F32), (0, pad_e)).reshape(1, LANES),
        'w_moe1': w_moe1[layer].astype(F32),
        'b_moe1': b_moe1[layer].reshape(N_EXPERTS, 1, 2 * D_FF).astype(F32),
        'w_moe2': w_moe2[layer].astype(F32),
        'b_moe2': b_moe2[layer].reshape(N_EXPERTS, 1, D_MODEL).astype(F32),
        'normf_g': normf_g.reshape(1, D_MODEL).astype(F32),
    }


def kernel(x_prompt, x_sample, rel_bias, norm1_g, w_in, lambda_q1, lambda_k1, lambda_q2, lambda_k2, subln_g, ssm_A_re, ssm_A_im, ssm_log_dt, ssm_B_re, ssm_B_im, ssm_C_re, ssm_C_im, ssm_D, w_glu, b_glu, ssm_norm_g, w_out, norm2_g, w_router, b_router, w_moe1, b_moe1, w_moe2, b_moe2, normf_g):
    prm = _prepare((x_prompt.shape[1], x_sample.shape[1]), rel_bias, norm1_g, w_in, lambda_q1,
                   lambda_k1, lambda_q2, lambda_k2, subln_g, ssm_A_re, ssm_A_im, ssm_log_dt,
                   ssm_B_re, ssm_B_im, ssm_C_re, ssm_C_im, ssm_D, w_glu, b_glu, ssm_norm_g, w_out,
                   norm2_g, w_router, b_router, w_moe1, b_moe1, w_moe2, b_moe2, normf_g)
    first = _trunk_front(x_prompt, prm)
    x_sample, tok_pad = lax.optimization_barrier((x_sample, first['tok_pad']))
    x_pad_first = _sc_gather_rows(first['h2'], tok_pad)
    second = _trunk_front(x_sample, prm)
    x_pad_second = _sc_gather_rows(second['h2'], second['tok_pad'])
    return (_trunk_back(first, x_pad_first, prm), _trunk_back(second, x_pad_second, prm))
```

```python
import functools
import math

import jax
import jax.numpy as jnp
from jax import lax
from jax.experimental import pallas as pl
from jax.experimental.pallas import tpu as pltpu
from jax.experimental.pallas import tpu_sc as plsc

F32 = jnp.float32
BF16 = jnp.bfloat16
PACKED = jnp.uint32

D_MODEL = 1024
W_ATT = 512
W_SSM = 512
HEAD_DIM = 64
N_HEADS = 4
V_DIM = 2 * HEAD_DIM
SSM_GROUP = 16
N_GROUPS = W_SSM // SSM_GROUP
SSM_STATE = 64
IN_WIDTH = 3 * W_ATT + W_SSM
N_BUCKETS = 32
MAX_DISTANCE = 128
N_EXPERTS = 32
TOP_K = 4
D_FF = D_MODEL
SWIGLU_ALPHA = 1.702
SWIGLU_LIMIT = 7.0
RMS_EPS = 1e-6
ATT_SCALE = HEAD_DIM ** -0.5
LOG2E = math.log2(math.e)

LANES = 128
PV_KEYS = 256
SSM_CHUNK = LANES
SSM_ROW = SSM_CHUNK * SSM_GROUP
ROW_TILE = 1024
EXPERT_ROWS = 512
SC_STEP_BYTES = 128 * 1024
SC_CORES = 2
SC_SUBCORES = 16
COMBINE_ROWS = 256
VMEM_LIMIT = 56 * 1024 * 1024


def _pack_bf16_pair(lo, hi):
    lo_bits = lax.bitcast_convert_type(lo.astype(BF16).astype(F32), PACKED)
    hi_bits = lax.bitcast_convert_type(hi.astype(BF16).astype(F32), PACKED)
    return (hi_bits & jnp.uint32(0xFFFF0000)) | (lo_bits >> 16)


def _unpack_bf16_pair(words):
    lo = lax.bitcast_convert_type(words << 16, F32)
    hi = lax.bitcast_convert_type(words & jnp.uint32(0xFFFF0000), F32)
    return lo, hi


def _cparams(sem):
    return pltpu.CompilerParams(dimension_semantics=sem, vmem_limit_bytes=VMEM_LIMIT)


def _in_proj_kernel(x_ref, g_ref, w_ref, q_ref, k_ref, v_ref, ut_ref):
    x = x_ref[...]
    ms = jnp.mean(x * x, axis=-1, keepdims=True)
    h = (x * lax.rsqrt(ms + RMS_EPS) * g_ref[...]).astype(BF16)
    proj = jnp.dot(h, w_ref[...], preferred_element_type=F32)
    q_ref[...] = (proj[:, 0:W_ATT] * (ATT_SCALE * LOG2E)).astype(BF16)
    k_ref[...] = proj[:, W_ATT:2 * W_ATT].astype(BF16)
    v_ref[...] = proj[:, 2 * W_ATT:3 * W_ATT].astype(BF16)
    ut = proj[:, 3 * W_ATT:].T
    for j in range(ut_ref.shape[1]):
        ut_ref[:, j, :] = ut[:, j * SSM_CHUNK:(j + 1) * SSM_CHUNK]


def _in_proj(x2d, g, w_bf16):
    t = x2d.shape[0]
    tm = min(ROW_TILE, t)
    out = jax.ShapeDtypeStruct((t, W_ATT), BF16)
    row = lambda i: (i, 0)
    return pl.pallas_call(
        _in_proj_kernel,
        grid=(t // tm,),
        in_specs=[pl.BlockSpec((tm, D_MODEL), row),
                  pl.BlockSpec((1, D_MODEL), lambda i: (0, 0)),
                  pl.BlockSpec((D_MODEL, IN_WIDTH), lambda i: (0, 0))],
        out_specs=[pl.BlockSpec((tm, W_ATT), row)] * 3
        + [pl.BlockSpec((W_SSM, tm // SSM_CHUNK, SSM_CHUNK), lambda i: (0, i, 0))],
        out_shape=[out] * 3 + [jax.ShapeDtypeStruct((W_SSM, t // SSM_CHUNK, SSM_CHUNK), F32)],
        compiler_params=_cparams(("parallel",)),
        name="in_proj",
    )(x2d, g, w_bf16)


def _t5_bucket(rel):
    half = N_BUCKETS // 2
    max_exact = half // 2
    ret = jnp.where(rel > 0, half, 0).astype(jnp.int32)
    n = jnp.abs(rel)
    nf = jnp.maximum(n, 1).astype(F32)
    large = max_exact + (jnp.log(nf / max_exact) / math.log(MAX_DISTANCE / max_exact)
                         * (half - max_exact)).astype(jnp.int32)
    large = jnp.minimum(large, half - 1)
    return ret + jnp.where(n < max_exact, n, large)


def _bias_tiles(rel_bias, t):
    i = jnp.arange(t, dtype=jnp.int32)
    d = jnp.arange(-2, 3, dtype=jnp.int32)
    rel = d[:, None, None] * t + i[None, None, :] - i[None, :, None]
    onehot = (_t5_bucket(rel)[..., None] == jnp.arange(N_BUCKETS, dtype=jnp.int32)).astype(F32)
    tiles = jnp.einsum('dqkn,nh->hdqk', onehot, rel_bias.astype(F32), precision=lax.Precision.HIGHEST)
    return tiles * LOG2E


def _attn_kernel(lam_ref, q_ref, k_ref, v_ref, bias_ref, g_ref, o_ref,
                 m_scr, acc_scr, sa_scr, sb_scr, mxa_scr, mxb_scr, *, t, sub, n_iter):
    qi = pl.program_id(2)
    q = q_ref[0]
    lane = lax.broadcasted_iota(jnp.int32, q.shape, 1)
    zero = jnp.zeros_like(q)
    qs = (jnp.where(lane < HEAD_DIM, q, zero), jnp.where(lane >= HEAD_DIM, q, zero))
    nb = t // LANES

    m_scr[...] = jnp.full(m_scr.shape, -jnp.inf, F32)
    acc_scr[...] = jnp.zeros(acc_scr.shape, F32)

    def scores(j, s_scr, mx_scr):
        for mi in range(2):
            mx = None
            for c in range(sub):
                blk = j * sub + c
                kc = k_ref[0, pl.ds(pl.multiple_of(blk * t, t), t), :]
                s = lax.dot_general(qs[mi], kc, (((1,), (1,)), ((), ())), preferred_element_type=F32)
                s = s + bias_ref[0, jnp.clip(blk - qi, -2, 2) + 2]
                s_scr[mi, :, c * t:(c + 1) * t] = s
                for i in range(nb):
                    piece = s[:, i * LANES:(i + 1) * LANES]
                    mx = piece if mx is None else jnp.maximum(mx, piece)
            mx_scr[mi] = mx

    def accumulate(j, s_scr, mx_scr):
        vj = v_ref[0, pl.ds(pl.multiple_of(j * (sub * t), sub * t), sub * t), :]
        vj = jnp.concatenate([vj, jnp.ones_like(vj)], axis=1)
        for mi in range(2):
            m_prev = m_scr[mi]
            m_next = jnp.maximum(m_prev, jnp.max(mx_scr[mi], axis=1, keepdims=True))
            alpha = jnp.exp2(m_prev - m_next)
            m_scr[mi] = m_next
            pv = None
            for i in range(sub * t // PV_KEYS):
                ps = []
                for c in range(PV_KEYS // LANES):
                    lo = i * PV_KEYS + c * LANES
                    p = jnp.exp2(s_scr[mi, :, lo:lo + LANES] - m_next)
                    ps.append(p.astype(BF16))
                d = jnp.dot(jnp.concatenate(ps, axis=1), vj[i * PV_KEYS:(i + 1) * PV_KEYS, :],
                            preferred_element_type=F32)
                pv = d if pv is None else pv + d
            acc_scr[mi] = jnp.concatenate([alpha, alpha], axis=1) * acc_scr[mi] + pv

    scores(0, sa_scr, mxa_scr)
    n_pairs = (n_iter - 1) // 2

    def body(i, carry):
        scores(2 * i + 1, sb_scr, mxb_scr)
        accumulate(2 * i, sa_scr, mxa_scr)
        scores(2 * i + 2, sa_scr, mxa_scr)
        accumulate(2 * i + 1, sb_scr, mxb_scr)
        return carry

    lax.fori_loop(0, n_pairs, body, 0)
    if (n_iter - 1) % 2 == 1:
        scores(n_iter - 1, sb_scr, mxb_scr)
        accumulate(n_iter - 2, sa_scr, mxa_scr)
        accumulate(n_iter - 1, sb_scr, mxb_scr)
    else:
        accumulate(n_iter - 1, sa_scr, mxa_scr)

    o1 = acc_scr[0, :, :V_DIM] / acc_scr[0, :, V_DIM:]
    o2 = acc_scr[1, :, :V_DIM] / acc_scr[1, :, V_DIM:]
    o = o1 - lam_ref[0] * o2
    ms = jnp.mean(o * o, axis=-1, keepdims=True)
    o_ref[0] = (o * lax.rsqrt(ms + RMS_EPS) * g_ref[...]).astype(BF16)


def _attention(q, k, v, bias_tiles, lam, g_scaled, t):
    b, l, _ = q.shape
    nq = l // t
    sub = 2 if nq % 2 == 0 else 1
    kern = functools.partial(_attn_kernel, t=t, sub=sub, n_iter=nq // sub)
    return pl.pallas_call(
        kern,
        grid=(b, N_HEADS, nq),
        in_specs=[pl.BlockSpec(memory_space=pltpu.SMEM),
                  pl.BlockSpec((1, t, V_DIM), lambda bi, h, qi: (bi, qi, h)),
                  pl.BlockSpec((1, l, V_DIM), lambda bi, h, qi: (bi, 0, h)),
                  pl.BlockSpec((1, l, V_DIM), lambda bi, h, qi: (bi, 0, h)),
                  pl.BlockSpec((1, 5, t, t), lambda bi, h, qi: (h, 0, 0, 0)),
                  pl.BlockSpec((1, V_DIM), lambda bi, h, qi: (0, 0))],
        out_specs=pl.BlockSpec((1, t, V_DIM), lambda bi, h, qi: (bi, qi, h)),
        out_shape=jax.ShapeDtypeStruct((b, l, W_ATT), BF16),
        scratch_shapes=[pltpu.VMEM((2, t, LANES), F32),
                        pltpu.VMEM((2, t, 2 * V_DIM), F32),
                        pltpu.VMEM((2, t, sub * t), F32),
                        pltpu.VMEM((2, t, sub * t), F32),
                        pltpu.VMEM((2, t, LANES), F32),
                        pltpu.VMEM((2, t, LANES), F32)],
        compiler_params=_cparams(("parallel", "parallel", "arbitrary")),
        name="diff_attention",
    )(lam, q, k, v, bias_tiles, g_scaled)


def _ssm_matrices(a_re, a_im, log_dt, b_re, b_im, c_re, c_im, d_skip):
    qn, g, p, hc = SSM_CHUNK, N_GROUPS, SSM_STATE, SSM_GROUP
    n = jnp.arange(qn + 1, dtype=F32)
    pw, bbar, cc = [], [], []
    for d in range(2):
        a = lax.complex(a_re[d].astype(F32), a_im[d].astype(F32))
        dt = jnp.exp(log_dt[d].astype(F32))[:, None]
        adt = a * dt
        a_bar = jnp.exp(adt)
        pw.append(jnp.exp(adt[None] * n[:, None, None]))
        bbar.append(((a_bar - 1.0) / a)[:, :, None]
                    * lax.complex(b_re[d].astype(F32), b_im[d].astype(F32)))
        cc.append(lax.complex(c_re[d].astype(F32), c_im[d].astype(F32)))

    hi = lax.Precision.HIGHEST
    kern = [jnp.einsum('gop,tgp,gpi->tgoi', cc[d], pw[d][:qn], bbar[d], precision=hi).real
            for d in range(2)]
    kf = jnp.transpose(kern[0], (1, 3, 2, 0))
    kb = jnp.transpose(kern[1], (1, 3, 2, 0))
    skip = jnp.eye(hc, dtype=F32)[None] * d_skip.astype(F32).reshape(g, hc)[:, :, None]
    taps = jnp.concatenate([kb[..., :0:-1], (kf[..., 0] + kb[..., 0] + skip)[..., None], kf[..., 1:]], axis=-1)
    m_mat = jnp.stack([taps[..., qn - 1 - s:2 * qn - 1 - s] for s in range(qn)], axis=2)
    m_mat = m_mat.reshape(g, SSM_ROW, SSM_ROW)

    zeros_p = jnp.zeros((g, SSM_ROW, LANES - p), F32)

    def pad_cols(x):
        return jnp.concatenate([x, zeros_p], axis=-1)

    pf = jnp.einsum('sgp,gpi->gisp', pw[0][:qn][::-1], bbar[0]).reshape(g, SSM_ROW, p)
    pb = jnp.einsum('sgp,gpi->gisp', pw[1][:qn], bbar[1]).reshape(g, SSM_ROW, p)
    p_mat = jnp.concatenate([pad_cols(pf.real), pad_cols(pf.imag),
                             pad_cols(pb.real), pad_cols(pb.imag)], axis=-1)

    wf = jnp.einsum('gop,tgp->gpot', cc[0], pw[0][1:qn + 1]).reshape(g, p, SSM_ROW)
    wb = jnp.einsum('gop,tgp->gpot', cc[1], pw[1][1:qn + 1][::-1]).reshape(g, p, SSM_ROW)
    zeros_r = jnp.zeros((g, LANES - p, SSM_ROW), F32)
    r_mat = jnp.concatenate([wf.real, zeros_r, -wf.imag, zeros_r,
                             wb.real, zeros_r, -wb.imag, zeros_r], axis=1)

    zeros_a = jnp.zeros((g, LANES - p), F32)

    def pad_vec(x):
        return jnp.concatenate([x, zeros_a], axis=-1)

    alpha = jnp.stack([pad_vec(pw[0][qn].real), pad_vec(pw[0][qn].imag),
                       pad_vec(pw[1][qn].real), pad_vec(pw[1][qn].imag)], axis=1)
    return p_mat.astype(BF16), m_mat.astype(BF16), r_mat.astype(BF16), alpha


def _s5_kernel(u_ref, p_ref, m_ref, r_ref, a_ref, y_ref, s_scr, x_scr, *, nc, bsz):
    u = jnp.concatenate([u_ref[h] for h in range(SSM_GROUP)], axis=-1).astype(BF16)
    s = jnp.dot(u, p_ref[0], preferred_element_type=F32)
    for part in range(4):
        s_scr[part] = s[:, part * LANES:(part + 1) * LANES]
    al = a_ref[0]
    afr = jnp.broadcast_to(al[0:1], (bsz, LANES))
    afi = jnp.broadcast_to(al[1:2], (bsz, LANES))
    abr = jnp.broadcast_to(al[2:3], (bsz, LANES))
    abi = jnp.broadcast_to(al[3:4], (bsz, LANES))
    fr = fi = br = bi = jnp.zeros((bsz, LANES), F32)
    for c in range(nc):
        rf = pl.ds(c, bsz, stride=nc)
        rb = pl.ds(nc - 1 - c, bsz, stride=nc)
        x_scr[0, rf, :] = fr
        x_scr[1, rf, :] = fi
        x_scr[2, rb, :] = br
        x_scr[3, rb, :] = bi
        sfr = s_scr[0, rf, :]
        sfi = s_scr[1, rf, :]
        sbr = s_scr[2, rb, :]
        sbi = s_scr[3, rb, :]
        fr, fi = afr * fr - afi * fi + sfr, afr * fi + afi * fr + sfi
        br, bi = abr * br - abi * bi + sbr, abr * bi + abi * br + sbi
    y = jnp.dot(u, m_ref[0], preferred_element_type=F32)
    x_in = jnp.concatenate([x_scr[part] for part in range(4)], axis=-1).astype(BF16)
    y = y + jnp.dot(x_in, r_ref[0], preferred_element_type=F32)
    for h in range(SSM_GROUP):
        y_ref[h] = y[:, h * SSM_CHUNK:(h + 1) * SSM_CHUNK]


def _s5(u_t, p_mat, m_mat, r_mat, alpha, nc, bsz):
    rows = u_t.shape[1]
    kern = functools.partial(_s5_kernel, nc=nc, bsz=bsz)
    seq = pl.BlockSpec((SSM_GROUP, rows, SSM_CHUNK), lambda i: (i, 0, 0))
    return pl.pallas_call(
        kern,
        grid=(N_GROUPS,),
        in_specs=[seq,
                  pl.BlockSpec((1, SSM_ROW, 4 * LANES), lambda i: (i, 0, 0)),
                  pl.BlockSpec((1, SSM_ROW, SSM_ROW), lambda i: (i, 0, 0)),
                  pl.BlockSpec((1, 4 * LANES, SSM_ROW), lambda i: (i, 0, 0)),
                  pl.BlockSpec((1, 4, LANES), lambda i: (i, 0, 0))],
        out_specs=seq,
        out_shape=jax.ShapeDtypeStruct(u_t.shape, F32),
        scratch_shapes=[pltpu.VMEM((4, rows, LANES), F32), pltpu.VMEM((4, rows, LANES), F32)],
        compiler_params=_cparams(("parallel",)),
        name="s5_scan",
    )(u_t, p_mat, m_mat, r_mat, alpha)


def _post_mix_kernel(x_ref, att_ref, y_ref, wglu_ref, bglu_ref, gs_ref, wout_ref, g2_ref,
                     wrh_ref, wrl_ref, br_ref, x2_ref, h2_ref, ids_ref, gates_ref):
    y = jnp.concatenate([y_ref[:, j, :].T for j in range(y_ref.shape[1])], axis=0)
    y = 0.5 * y * (1.0 + jnp.tanh(math.sqrt(2.0 / math.pi) * (y + 0.044715 * (y * y * y))))
    z = jnp.dot(y.astype(BF16), wglu_ref[...], preferred_element_type=F32) + bglu_ref[...]
    y = y * (1.0 / (1.0 + jnp.exp(-z)))
    ms = jnp.mean(y * y, axis=-1, keepdims=True)
    ssm = (y * lax.rsqrt(ms + RMS_EPS) * gs_ref[...]).astype(BF16)
    mix = jnp.dot(att_ref[...], wout_ref[0:W_ATT, :], preferred_element_type=F32)
    mix = mix + jnp.dot(ssm, wout_ref[W_ATT:, :], preferred_element_type=F32)
    x2 = x_ref[...] + mix
    x2_ref[...] = x2
    ms2 = jnp.mean(x2 * x2, axis=-1, keepdims=True)
    h2 = x2 * lax.rsqrt(ms2 + RMS_EPS) * g2_ref[...]
    h2_ref[...] = _pack_bf16_pair(h2[:, :D_MODEL // 2], h2[:, D_MODEL // 2:])
    h_hi = h2.astype(BF16)
    h_lo = (h2 - h_hi.astype(F32)).astype(BF16)
    logits = (jnp.dot(h_hi, wrh_ref[...], preferred_element_type=F32)
              + jnp.dot(h_lo, wrh_ref[...], preferred_element_type=F32)
              + jnp.dot(h_hi, wrl_ref[...], preferred_element_type=F32)) + br_ref[...]
    lane = lax.broadcasted_iota(jnp.int32, logits.shape, 1).astype(F32)
    neg = jnp.float32(-jnp.inf)
    cur = jnp.where(lane < N_EXPERTS, logits, neg)
    ids = jnp.zeros(logits.shape, F32)
    vals = jnp.zeros(logits.shape, F32)
    top = None
    den = None
    for kk in range(TOP_K):
        mx = jnp.max(cur, axis=1, keepdims=True)
        idx = jnp.min(jnp.where(cur == mx, lane, float(LANES)), axis=1, keepdims=True)
        if kk == 0:
            top = mx
        e = jnp.exp(mx - top)
        den = e if den is None else den + e
        ids = jnp.where(lane == kk, idx, ids)
        vals = jnp.where(lane == kk, e, vals)
        cur = jnp.where(lane == idx, neg, cur)
    ids_ref[...] = ids.astype(jnp.int32)
    gates_ref[...] = vals / den


def _post_mix(x2d, att, yssm, wglu, bglu, gs, wout, g2, wr_hi, wr_lo, br):
    t = x2d.shape[0]
    tm = min(ROW_TILE, t)
    row = lambda i: (i, 0)
    const = lambda i: (0, 0)
    return pl.pallas_call(
        _post_mix_kernel,
        grid=(t // tm,),
        in_specs=[pl.BlockSpec((tm, D_MODEL), row),
                  pl.BlockSpec((tm, W_ATT), row),
                  pl.BlockSpec((W_SSM, tm // SSM_CHUNK, SSM_CHUNK), lambda i: (0, i, 0)),
                  pl.BlockSpec((W_SSM, W_SSM), const),
                  pl.BlockSpec((1, W_SSM), const),
                  pl.BlockSpec((1, W_SSM), const),
                  pl.BlockSpec((D_MODEL, D_MODEL), const),
                  pl.BlockSpec((1, D_MODEL), const),
                  pl.BlockSpec((D_MODEL, LANES), const),
                  pl.BlockSpec((D_MODEL, LANES), const),
                  pl.BlockSpec((1, LANES), const)],
        out_specs=[pl.BlockSpec((tm, D_MODEL), row),
                   pl.BlockSpec((tm, D_MODEL // 2), row),
                   pl.BlockSpec((tm, LANES), row),
                   pl.BlockSpec((tm, LANES), row)],
        out_shape=[jax.ShapeDtypeStruct((t, D_MODEL), F32),
                   jax.ShapeDtypeStruct((t, D_MODEL // 2), PACKED),
                   jax.ShapeDtypeStruct((t, LANES), jnp.int32),
                   jax.ShapeDtypeStruct((t, LANES), F32)],
        compiler_params=_cparams(("parallel",)),
        name="post_mix",
    )(x2d, att, yssm, wglu, bglu, gs, wout, g2, wr_hi, wr_lo, br)


def _sc_gather_rows(x, idx):
    n = idx.shape[0]
    d = x.shape[1]
    row_bytes = d * x.dtype.itemsize
    step_rows = SC_STEP_BYTES // row_bytes
    steps = n // step_rows
    workers = SC_CORES * SC_SUBCORES
    assert steps % (2 * workers) == 0, (n, steps)
    per_worker = steps // workers
    idx_rows = jnp.pad(idx.reshape(steps, step_rows), ((0, 0), (0, LANES - step_rows)))
    mesh = plsc.VectorSubcoreMesh(core_axis_name="core", subcore_axis_name="subcore")

    cost = pl.CostEstimate(flops=0, transcendentals=0,
                           bytes_accessed=2 * n * row_bytes + idx_rows.size * idx_rows.dtype.itemsize)

    @pl.kernel(out_type=jax.ShapeDtypeStruct((n, d), x.dtype), mesh=mesh, cost_estimate=cost,
               name="sc_gather_rows",
               scratch_types=[pltpu.VMEM((2, 1, LANES), jnp.int32),
                              pltpu.VMEM((2, step_rows, d), x.dtype),
                              pltpu.SemaphoreType.DMA((2,)),
                              pltpu.SemaphoreType.DMA((2,))])
    def gather(x_hbm, i_hbm, o_hbm, idx_v, buf, sem_in, sem_out):
        base = (lax.axis_index("core") * SC_SUBCORES + lax.axis_index("subcore")) * per_worker

        def gather_copy(slot):
            return pltpu.make_async_copy(x_hbm.at[idx_v.at[slot, 0, pl.ds(0, step_rows)]], buf.at[slot],
                                         sem_in.at[slot])

        def out_copy(step, slot):
            return pltpu.make_async_copy(buf.at[slot], o_hbm.at[pl.ds(step * step_rows, step_rows), :],
                                         sem_out.at[slot])

        @pl.loop(0, per_worker // 2)
        def _(j):
            for slot in range(2):
                step = base + 2 * j + slot

                @pl.when(j > 0)
                def _():
                    out_copy(step, slot).wait()

                pltpu.sync_copy(i_hbm.at[pl.ds(step, 1), :], idx_v.at[slot])
                gather_copy(slot).start()
            for slot in range(2):
                gather_copy(slot).wait()
                out_copy(base + 2 * j + slot, slot).start()

        for slot in range(2):
            out_copy(base, slot).wait()

    return gather(x, idx_rows)


def _expert_kernel(be_ref, x_ref, w1_ref, b1_ref, w2_ref, b2_ref, y_ref, w1_bf, w2_bf):
    i = pl.program_id(0)

    @pl.when(jnp.logical_or(i == 0, be_ref[i] != be_ref[jnp.maximum(i - 1, 0)]))
    def _():
        w1_bf[...] = w1_ref[0].astype(BF16)
        w2_bf[...] = w2_ref[0].astype(BF16)

    x = jnp.concatenate(_unpack_bf16_pair(x_ref[...]), axis=1).astype(BF16)
    hdn = jnp.dot(x, w1_bf[...], preferred_element_type=F32) + b1_ref[0]
    gate = jnp.minimum(hdn[:, :D_FF], SWIGLU_LIMIT)
    lin = jnp.clip(hdn[:, D_FF:], -SWIGLU_LIMIT, SWIGLU_LIMIT)
    act = gate * (1.0 / (1.0 + jnp.exp(-SWIGLU_ALPHA * gate))) * (lin + 1.0)
    y = jnp.dot(act.astype(BF16), w2_bf[...], preferred_element_type=F32) + b2_ref[0]
    y_ref[...] = _pack_bf16_pair(y[:, :D_MODEL // 2], y[:, D_MODEL // 2:])


def _experts(block_e, x_pad, w1, b1, w2, b2):
    n_pad = x_pad.shape[0]
    rows = EXPERT_ROWS
    grid_spec = pltpu.PrefetchScalarGridSpec(
        num_scalar_prefetch=1,
        grid=(n_pad // rows,),
        in_specs=[pl.BlockSpec((rows, D_MODEL // 2), lambda i, be: (i, 0)),
                  pl.BlockSpec((1, D_MODEL, 2 * D_FF), lambda i, be: (be[i], 0, 0)),
                  pl.BlockSpec((1, 1, 2 * D_FF), lambda i, be: (be[i], 0, 0)),
                  pl.BlockSpec((1, D_FF, D_MODEL), lambda i, be: (be[i], 0, 0)),
                  pl.BlockSpec((1, 1, D_MODEL), lambda i, be: (be[i], 0, 0))],
        out_specs=pl.BlockSpec((rows, D_MODEL // 2), lambda i, be: (i, 0)),
        scratch_shapes=[pltpu.VMEM((D_MODEL, 2 * D_FF), BF16), pltpu.VMEM((D_FF, D_MODEL), BF16)],
    )
    return pl.pallas_call(
        _expert_kernel,
        grid_spec=grid_spec,
        out_shape=jax.ShapeDtypeStruct((n_pad, D_MODEL // 2), PACKED),
        compiler_params=_cparams(("arbitrary",)),
        name="moe_experts",
    )(block_e, x_pad, w1, b1, w2, b2)


def _combine_kernel(x2_ref, gates_ref, gf_ref, y_ref, o_ref):
    gates = gates_ref[...]
    lo = x2_ref[:, :D_MODEL // 2]
    hi = x2_ref[:, D_MODEL // 2:]
    for kk in range(TOP_K):
        y_lo, y_hi = _unpack_bf16_pair(y_ref[kk])
        lo = lo + gates[:, kk:kk + 1] * y_lo
        hi = hi + gates[:, kk:kk + 1] * y_hi
    y = jnp.concatenate([lo, hi], axis=1)
    ms = jnp.mean(y * y, axis=-1, keepdims=True)
    o_ref[...] = y * lax.rsqrt(ms + RMS_EPS) * gf_ref[...]


def _combine(x2, gates, gf, y_sel):
    t = x2.shape[0]
    rows = min(COMBINE_ROWS, t)
    row = lambda i: (i, 0)
    return pl.pallas_call(
        _combine_kernel,
        grid=(t // rows,),
        in_specs=[pl.BlockSpec((rows, D_MODEL), row),
                  pl.BlockSpec((rows, LANES), row),
                  pl.BlockSpec((1, D_MODEL), lambda i: (0, 0)),
                  pl.BlockSpec((TOP_K, rows, D_MODEL // 2), lambda i: (0, i, 0))],
        out_specs=pl.BlockSpec((rows, D_MODEL), row),
        out_shape=jax.ShapeDtypeStruct((t, D_MODEL), F32),
        compiler_params=_cparams(("parallel",)),
        name="moe_combine",
    )(x2, gates, gf, y_sel)


def _dispatch_plan(top_e, n_tok):
    n = n_tok * TOP_K
    blk = EXPERT_ROWS
    flat_e = top_e.reshape(n)
    order = jnp.argsort(flat_e).astype(jnp.int32)
    rank = jnp.argsort(order).astype(jnp.int32)
    st = order // TOP_K
    experts = jnp.arange(N_EXPERTS, dtype=jnp.int32)
    counts = jnp.sum((flat_e[:, None] == experts).astype(jnp.int32), axis=0)
    padded = (counts + blk - 1) // blk * blk
    start = jnp.cumsum(counts) - counts
    pend = jnp.cumsum(padded)
    pstart = pend - padded
    pos = (pstart - start)[flat_e] + rank
    n_blocks = n // blk + N_EXPERTS
    block_start = jnp.arange(n_blocks, dtype=jnp.int32) * blk
    block_e = jnp.minimum(jnp.sum((pend[None, :] <= block_start[:, None]).astype(jnp.int32), axis=1),
                          N_EXPERTS - 1)
    slot = jnp.arange(n_blocks * blk, dtype=jnp.int32)
    slot_e = jnp.repeat(block_e, blk)
    off = slot - pstart[slot_e]
    src = jnp.clip(start[slot_e] + off, 0, n - 1)
    tok_pad = jnp.where(off < counts[slot_e], st[src], 0)
    return tok_pad, pos, block_e


def _trunk_front(x, prm):
    bsz, l, _ = x.shape
    t = bsz * l
    x2d = x.reshape(t, D_MODEL)
    q, k, v, u_t = _in_proj(x2d, prm['norm1_g'], prm['w_in'])

    t_att = min(512, l)
    att = _attention(q.reshape(bsz, l, W_ATT), k.reshape(bsz, l, W_ATT), v.reshape(bsz, l, W_ATT),
                     prm['bias_tiles'][t_att], prm['lam'], prm['subln_g'], t_att)

    yssm = _s5(u_t, prm['ssm_p'], prm['ssm_m'], prm['ssm_r'], prm['ssm_alpha'], l // SSM_CHUNK, bsz)

    x2, h2, ids, gates = _post_mix(x2d, att.reshape(t, W_ATT), yssm, prm['w_glu'], prm['b_glu'],
                                   prm['ssm_norm_g'], prm['w_out'], prm['norm2_g'],
                                   prm['w_router_hi'], prm['w_router_lo'], prm['b_router'])

    tok_pad, pos, block_e = _dispatch_plan(ids[:, :TOP_K], t)
    return {'shape': x.shape, 'x2': x2, 'h2': h2, 'gates': gates,
            'tok_pad': tok_pad, 'pos': pos, 'block_e': block_e}


def _trunk_back(st, x_pad, prm):
    t = st['x2'].shape[0]
    y_pad = _experts(st['block_e'], x_pad, prm['w_moe1'], prm['b_moe1'], prm['w_moe2'], prm['b_moe2'])
    pos_by_k = st['pos'].reshape(t, TOP_K).T.reshape(TOP_K * t)
    y_sel = _sc_gather_rows(y_pad, pos_by_k).reshape(TOP_K, t, D_MODEL // 2)
    out = _combine(st['x2'], st['gates'], prm['normf_g'], y_sel)
    return out.reshape(st['shape'])


def _prepare(seq_lens, rel_bias, norm1_g, w_in, lambda_q1, lambda_k1, lambda_q2, lambda_k2, subln_g,
             ssm_A_re, ssm_A_im, ssm_log_dt, ssm_B_re, ssm_B_im, ssm_C_re, ssm_C_im, ssm_D,
             w_glu, b_glu, ssm_norm_g, w_out, norm2_g, w_router, b_router,
             w_moe1, b_moe1, w_moe2, b_moe2, normf_g):
    layer = 0
    lambda_init = 0.8 - 0.6 * math.exp(-0.3 * layer)
    lam = (jnp.exp(jnp.sum(lambda_q1[layer].astype(F32) * lambda_k1[layer].astype(F32)))
           - jnp.exp(jnp.sum(lambda_q2[layer].astype(F32) * lambda_k2[layer].astype(F32))) + lambda_init)
    p_mat, m_mat, r_mat, alpha = _ssm_matrices(
        ssm_A_re[layer], ssm_A_im[layer], ssm_log_dt[layer], ssm_B_re[layer], ssm_B_im[layer],
        ssm_C_re[layer], ssm_C_im[layer], ssm_D[layer])
    pad_e = LANES - N_EXPERTS
    w_r = jnp.pad(w_router[layer].astype(F32), ((0, 0), (0, pad_e)))
    w_r_hi = w_r.astype(BF16)
    return {
        'norm1_g': norm1_g[layer].reshape(1, D_MODEL).astype(F32),
        'w_in': w_in[layer].astype(BF16),
        'lam': lam.reshape(1).astype(F32),
        'subln_g': (subln_g[layer].astype(F32) * (1.0 - lambda_init)).reshape(1, V_DIM),
        'bias_tiles': {t: _bias_tiles(rel_bias, t) for t in sorted({min(512, l) for l in seq_lens})},
        'ssm_p': p_mat, 'ssm_m': m_mat, 'ssm_r': r_mat, 'ssm_alpha': alpha,
        'w_glu': w_glu[layer].astype(BF16),
        'b_glu': b_glu[layer].reshape(1, W_SSM).astype(F32),
        'ssm_norm_g': ssm_norm_g[layer].reshape(1, W_SSM).astype(F32),
        'w_out': w_out[layer].astype(BF16),
        'norm2_g': norm2_g[layer].reshape(1, D_MODEL).astype(F32),
        'w_router_hi': w_r_hi,
        'w_router_lo': (w_r - w_r_hi.astype(F32)).astype(BF16),
        'b_router': jnp.pad(b_router[layer].astype(F32), (0, pad_e)).reshape(1, LANES),
        'w_moe1': w_moe1[layer].astype(F32),
        'b_moe1': b_moe1[layer].reshape(N_EXPERTS, 1, 2 * D_FF).astype(F32),
        'w_moe2': w_moe2[layer].astype(F32),
        'b_moe2': b_moe2[layer].reshape(N_EXPERTS, 1, D_MODEL).astype(F32),
        'normf_g': normf_g.reshape(1, D_MODEL).astype(F32),
    }


def kernel(x_prompt, x_sample, rel_bias, norm1_g, w_in, lambda_q1, lambda_k1, lambda_q2, lambda_k2, subln_g, ssm_A_re, ssm_A_im, ssm_log_dt, ssm_B_re, ssm_B_im, ssm_C_re, ssm_C_im, ssm_D, w_glu, b_glu, ssm_norm_g, w_out, norm2_g, w_router, b_router, w_moe1, b_moe1, w_moe2, b_moe2, normf_g):
    prm = _prepare((x_prompt.shape[1], x_sample.shape[1]), rel_bias, norm1_g, w_in, lambda_q1,
                   lambda_k1, lambda_q2, lambda_k2, subln_g, ssm_A_re, ssm_A_im, ssm_log_dt,
                   ssm_B_re, ssm_B_im, ssm_C_re, ssm_C_im, ssm_D, w_glu, b_glu, ssm_norm_g, w_out,
                   norm2_g, w_router, b_router, w_moe1, b_moe1, w_moe2, b_moe2, normf_g)
    first = _trunk_front(x_prompt, prm)
    x_sample, tok_pad = lax.optimization_barrier((x_sample, first['tok_pad']))
    x_pad_first = _sc_gather_rows(first['h2'], tok_pad)
    second = _trunk_front(x_sample, prm)
    x_pad_second = _sc_gather_rows(second['h2'], second['tok_pad'])
    return (_trunk_back(first, x_pad_first, prm), _trunk_back(second, x_pad_second, prm))
```

```python
import functools
import math

import jax
import jax.numpy as jnp
from jax import lax
from jax.experimental import pallas as pl
from jax.experimental.pallas import tpu as pltpu
from jax.experimental.pallas import tpu_sc as plsc

F32 = jnp.float32
BF16 = jnp.bfloat16
PACKED = jnp.uint32

D_MODEL = 1024
W_ATT = 512
W_SSM = 512
HEAD_DIM = 64
N_HEADS = 4
V_DIM = 2 * HEAD_DIM
SSM_GROUP = 16
N_GROUPS = W_SSM // SSM_GROUP
SSM_STATE = 64
IN_WIDTH = 3 * W_ATT + W_SSM
N_BUCKETS = 32
MAX_DISTANCE = 128
N_EXPERTS = 32
TOP_K = 4
D_FF = D_MODEL
SWIGLU_ALPHA = 1.702
SWIGLU_LIMIT = 7.0
RMS_EPS = 1e-6
ATT_SCALE = HEAD_DIM ** -0.5
LOG2E = math.log2(math.e)

LANES = 128
PV_KEYS = 256
SSM_CHUNK = LANES
SSM_ROW = SSM_CHUNK * SSM_GROUP
ROW_TILE = 1024
EXPERT_ROWS = 512
SC_STEP_BYTES = 128 * 1024
SC_CORES = 2
SC_SUBCORES = 16
COMBINE_ROWS = 256
VMEM_LIMIT = 56 * 1024 * 1024


def _pack_bf16_pair(lo, hi):
    lo_bits = lax.bitcast_convert_type(lo.astype(BF16).astype(F32), PACKED)
    hi_bits = lax.bitcast_convert_type(hi.astype(BF16).astype(F32), PACKED)
    return (hi_bits & jnp.uint32(0xFFFF0000)) | (lo_bits >> 16)


def _unpack_bf16_pair(words):
    lo = lax.bitcast_convert_type(words << 16, F32)
    hi = lax.bitcast_convert_type(words & jnp.uint32(0xFFFF0000), F32)
    return lo, hi


def _cparams(sem):
    return pltpu.CompilerParams(dimension_semantics=sem, vmem_limit_bytes=VMEM_LIMIT)


def _in_proj_kernel(x_ref, g_ref, w_ref, q_ref, k_ref, v_ref, ut_ref):
    x = x_ref[...]
    ms = jnp.mean(x * x, axis=-1, keepdims=True)
    h = (x * lax.rsqrt(ms + RMS_EPS) * g_ref[...]).astype(BF16)
    proj = jnp.dot(h, w_ref[...], preferred_element_type=F32)
    q_ref[...] = (proj[:, 0:W_ATT] * (ATT_SCALE * LOG2E)).astype(BF16)
    k_ref[...] = proj[:, W_ATT:2 * W_ATT].astype(BF16)
    v_ref[...] = proj[:, 2 * W_ATT:3 * W_ATT].astype(BF16)
    ut = proj[:, 3 * W_ATT:].T
    for j in range(ut_ref.shape[1]):
        ut_ref[:, j, :] = ut[:, j * SSM_CHUNK:(j + 1) * SSM_CHUNK]


def _in_proj(x2d, g, w_bf16):
    t = x2d.shape[0]
    tm = min(ROW_TILE, t)
    out = jax.ShapeDtypeStruct((t, W_ATT), BF16)
    row = lambda i: (i, 0)
    return pl.pallas_call(
        _in_proj_kernel,
        grid=(t // tm,),
        in_specs=[pl.BlockSpec((tm, D_MODEL), row),
                  pl.BlockSpec((1, D_MODEL), lambda i: (0, 0)),
                  pl.BlockSpec((D_MODEL, IN_WIDTH), lambda i: (0, 0))],
        out_specs=[pl.BlockSpec((tm, W_ATT), row)] * 3
        + [pl.BlockSpec((W_SSM, tm // SSM_CHUNK, SSM_CHUNK), lambda i: (0, i, 0))],
        out_shape=[out] * 3 + [jax.ShapeDtypeStruct((W_SSM, t // SSM_CHUNK, SSM_CHUNK), F32)],
        compiler_params=_cparams(("parallel",)),
        name="in_proj",
    )(x2d, g, w_bf16)


def _t5_bucket(rel):
    half = N_BUCKETS // 2
    max_exact = half // 2
    ret = jnp.where(rel > 0, half, 0).astype(jnp.int32)
    n = jnp.abs(rel)
    nf = jnp.maximum(n, 1).astype(F32)
    large = max_exact + (jnp.log(nf / max_exact) / math.log(MAX_DISTANCE / max_exact)
                         * (half - max_exact)).astype(jnp.int32)
    large = jnp.minimum(large, half - 1)
    return ret + jnp.where(n < max_exact, n, large)


def _bias_tiles(rel_bias, t):
    i = jnp.arange(t, dtype=jnp.int32)
    d = jnp.arange(-2, 3, dtype=jnp.int32)
    rel = d[:, None, None] * t + i[None, None, :] - i[None, :, None]
    onehot = (_t5_bucket(rel)[..., None] == jnp.arange(N_BUCKETS, dtype=jnp.int32)).astype(F32)
    tiles = jnp.einsum('dqkn,nh->hdqk', onehot, rel_bias.astype(F32), precision=lax.Precision.HIGHEST)
    return tiles * LOG2E


def _attn_kernel(lam_ref, far_ref, q_ref, k_ref, v_ref, bias_ref, g_ref, o_ref,
                 m_scr, acc_scr, sa_scr, sb_scr, mxa_scr, mxb_scr, *, t, sub, n_iter):
    head = pl.program_id(1)
    qi = pl.program_id(2)
    q = q_ref[0]
    lane = lax.broadcasted_iota(jnp.int32, q.shape, 1)
    zero = jnp.zeros_like(q)
    qs = (jnp.where(lane < HEAD_DIM, q, zero), jnp.where(lane >= HEAD_DIM, q, zero))
    nb = t // LANES
    n_near = min(n_iter, 2 if sub >= 2 else 3)
    first = jnp.maximum(qi - 1, 0) // sub

    m_scr[...] = jnp.full(m_scr.shape, -jnp.inf, F32)
    acc_scr[...] = jnp.zeros(acc_scr.shape, F32)

    def chunk_of(r):
        return (first + r) % n_iter

    def scores(r, s_scr, mx_scr, near):
        j = chunk_of(r)
        for mi in range(2):
            mx = None
            for c in range(sub):
                blk = j * sub + c
                kc = k_ref[0, pl.ds(pl.multiple_of(blk * t, t), t), :]
                s = lax.dot_general(qs[mi], kc, (((1,), (1,)), ((), ())), preferred_element_type=F32)
                if near:
                    s = s + bias_ref[0, jnp.clip(blk - qi, -2, 2) + 2]
                s_scr[mi, :, c * t:(c + 1) * t] = s
                for i in range(nb):
                    piece = s[:, i * LANES:(i + 1) * LANES]
                    mx = piece if mx is None else jnp.maximum(mx, piece)
            mx_scr[mi] = mx

    def accumulate(r, s_scr, mx_scr, near):
        j = chunk_of(r)
        vj = v_ref[0, pl.ds(pl.multiple_of(j * (sub * t), sub * t), sub * t), :]
        vj = jnp.concatenate([vj, jnp.ones_like(vj)], axis=1)
        shift = 0.0 if near else jnp.where(j * sub < qi, far_ref[2 * head], far_ref[2 * head + 1])
        for mi in range(2):
            m_prev = m_scr[mi]
            m_next = jnp.maximum(m_prev, jnp.max(mx_scr[mi], axis=1, keepdims=True) + shift)
            alpha = jnp.exp2(m_prev - m_next)
            m_scr[mi] = m_next
            m_raw = m_next - shift
            pv = None
            for i in range(sub * t // PV_KEYS):
                ps = []
                for c in range(PV_KEYS // LANES):
                    lo = i * PV_KEYS + c * LANES
                    p = jnp.exp2(s_scr[mi, :, lo:lo + LANES] - m_raw)
                    ps.append(p.astype(BF16))
                d = jnp.dot(jnp.concatenate(ps, axis=1), vj[i * PV_KEYS:(i + 1) * PV_KEYS, :],
                            preferred_element_type=F32)
                pv = d if pv is None else pv + d
            acc_scr[mi] = jnp.concatenate([alpha, alpha], axis=1) * acc_scr[mi] + pv

    def pair(i, first_near, second_near):
        scores(2 * i + 1, sb_scr, mxb_scr, second_near[0])
        accumulate(2 * i, sa_scr, mxa_scr, first_near)
        scores(2 * i + 2, sa_scr, mxa_scr, second_near[1])
        accumulate(2 * i + 1, sb_scr, mxb_scr, second_near[0])

    scores(0, sa_scr, mxa_scr, True)
    n_pairs = (n_iter - 1) // 2
    peeled = min(n_pairs, (n_near + 1) // 2)
    for i in range(peeled):
        pair(i, 2 * i < n_near, (2 * i + 1 < n_near, 2 * i + 2 < n_near))

    def body(i, carry):
        pair(i, False, (False, False))
        return carry

    lax.fori_loop(peeled, n_pairs, body, 0)
    if (n_iter - 1) % 2 == 1:
        scores(n_iter - 1, sb_scr, mxb_scr, n_iter - 1 < n_near)
        accumulate(n_iter - 2, sa_scr, mxa_scr, n_iter - 2 < n_near)
        accumulate(n_iter - 1, sb_scr, mxb_scr, n_iter - 1 < n_near)
    else:
        accumulate(n_iter - 1, sa_scr, mxa_scr, n_iter - 1 < n_near)

    o1 = acc_scr[0, :, :V_DIM] / acc_scr[0, :, V_DIM:]
    o2 = acc_scr[1, :, :V_DIM] / acc_scr[1, :, V_DIM:]
    o = o1 - lam_ref[0] * o2
    ms = jnp.mean(o * o, axis=-1, keepdims=True)
    o_ref[0] = (o * lax.rsqrt(ms + RMS_EPS) * g_ref[...]).astype(BF16)


def _attention(q, k, v, bias_tiles, lam, g_scaled, t):
    b, l, _ = q.shape
    nq = l // t
    sub = 2 if nq % 2 == 0 else 1
    kern = functools.partial(_attn_kernel, t=t, sub=sub, n_iter=nq // sub)
    far = jnp.stack([bias_tiles[:, 0, 0, 0], bias_tiles[:, 4, 0, 0]], axis=1).reshape(2 * N_HEADS)
    return pl.pallas_call(
        kern,
        grid=(b, N_HEADS, nq),
        in_specs=[pl.BlockSpec(memory_space=pltpu.SMEM),
                  pl.BlockSpec(memory_space=pltpu.SMEM),
                  pl.BlockSpec((1, t, V_DIM), lambda bi, h, qi: (bi, qi, h)),
                  pl.BlockSpec((1, l, V_DIM), lambda bi, h, qi: (bi, 0, h)),
                  pl.BlockSpec((1, l, V_DIM), lambda bi, h, qi: (bi, 0, h)),
                  pl.BlockSpec((1, 5, t, t), lambda bi, h, qi: (h, 0, 0, 0)),
                  pl.BlockSpec((1, V_DIM), lambda bi, h, qi: (0, 0))],
        out_specs=pl.BlockSpec((1, t, V_DIM), lambda bi, h, qi: (bi, qi, h)),
        out_shape=jax.ShapeDtypeStruct((b, l, W_ATT), BF16),
        scratch_shapes=[pltpu.VMEM((2, t, LANES), F32),
                        pltpu.VMEM((2, t, 2 * V_DIM), F32),
                        pltpu.VMEM((2, t, sub * t), F32),
                        pltpu.VMEM((2, t, sub * t), F32),
                        pltpu.VMEM((2, t, LANES), F32),
                        pltpu.VMEM((2, t, LANES), F32)],
        compiler_params=_cparams(("parallel", "parallel", "arbitrary")),
        name="diff_attention",
    )(lam, far, q, k, v, bias_tiles, g_scaled)


def _ssm_matrices(a_re, a_im, log_dt, b_re, b_im, c_re, c_im, d_skip):
    qn, g, p, hc = SSM_CHUNK, N_GROUPS, SSM_STATE, SSM_GROUP
    n = jnp.arange(qn + 1, dtype=F32)
    pw, bbar, cc = [], [], []
    for d in range(2):
        a = lax.complex(a_re[d].astype(F32), a_im[d].astype(F32))
        dt = jnp.exp(log_dt[d].astype(F32))[:, None]
        adt = a * dt
        a_bar = jnp.exp(adt)
        pw.append(jnp.exp(adt[None] * n[:, None, None]))
        bbar.append(((a_bar - 1.0) / a)[:, :, None]
                    * lax.complex(b_re[d].astype(F32), b_im[d].astype(F32)))
        cc.append(lax.complex(c_re[d].astype(F32), c_im[d].astype(F32)))

    hi = lax.Precision.HIGHEST
    kern = [jnp.einsum('gop,tgp,gpi->tgoi', cc[d], pw[d][:qn], bbar[d], precision=hi).real
            for d in range(2)]
    kf = jnp.transpose(kern[0], (1, 3, 2, 0))
    kb = jnp.transpose(kern[1], (1, 3, 2, 0))
    skip = jnp.eye(hc, dtype=F32)[None] * d_skip.astype(F32).reshape(g, hc)[:, :, None]
    taps = jnp.concatenate([kb[..., :0:-1], (kf[..., 0] + kb[..., 0] + skip)[..., None], kf[..., 1:]], axis=-1)
    m_mat = jnp.stack([taps[..., qn - 1 - s:2 * qn - 1 - s] for s in range(qn)], axis=2)
    m_mat = m_mat.reshape(g, SSM_ROW, SSM_ROW)

    zeros_p = jnp.zeros((g, SSM_ROW, LANES - p), F32)

    def pad_cols(x):
        return jnp.concatenate([x, zeros_p], axis=-1)

    pf = jnp.einsum('sgp,gpi->gisp', pw[0][:qn][::-1], bbar[0]).reshape(g, SSM_ROW, p)
    pb = jnp.einsum('sgp,gpi->gisp', pw[1][:qn], bbar[1]).reshape(g, SSM_ROW, p)
    p_mat = jnp.concatenate([pad_cols(pf.real), pad_cols(pf.imag),
                             pad_cols(pb.real), pad_cols(pb.imag)], axis=-1)

    wf = jnp.einsum('gop,tgp->gpot', cc[0], pw[0][1:qn + 1]).reshape(g, p, SSM_ROW)
    wb = jnp.einsum('gop,tgp->gpot', cc[1], pw[1][1:qn + 1][::-1]).reshape(g, p, SSM_ROW)
    zeros_r = jnp.zeros((g, LANES - p, SSM_ROW), F32)
    r_mat = jnp.concatenate([wf.real, zeros_r, -wf.imag, zeros_r,
                             wb.real, zeros_r, -wb.imag, zeros_r], axis=1)

    zeros_a = jnp.zeros((g, LANES - p), F32)

    def pad_vec(x):
        return jnp.concatenate([x, zeros_a], axis=-1)

    alpha = jnp.stack([pad_vec(pw[0][qn].real), pad_vec(pw[0][qn].imag),
                       pad_vec(pw[1][qn].real), pad_vec(pw[1][qn].imag)], axis=1)
    return p_mat.astype(BF16), m_mat.astype(BF16), r_mat.astype(BF16), alpha


def _s5_kernel(u_ref, p_ref, m_ref, r_ref, a_ref, y_ref, s_scr, x_scr, *, nc, bsz):
    u = jnp.concatenate([u_ref[h] for h in range(SSM_GROUP)], axis=-1).astype(BF16)
    s = jnp.dot(u, p_ref[0], preferred_element_type=F32)
    for part in range(4):
        s_scr[part] = s[:, part * LANES:(part + 1) * LANES]
    al = a_ref[0]
    afr = jnp.broadcast_to(al[0:1], (bsz, LANES))
    afi = jnp.broadcast_to(al[1:2], (bsz, LANES))
    abr = jnp.broadcast_to(al[2:3], (bsz, LANES))
    abi = jnp.broadcast_to(al[3:4], (bsz, LANES))
    fr = fi = br = bi = jnp.zeros((bsz, LANES), F32)
    for c in range(nc):
        rf = pl.ds(c, bsz, stride=nc)
        rb = pl.ds(nc - 1 - c, bsz, stride=nc)
        x_scr[0, rf, :] = fr
        x_scr[1, rf, :] = fi
        x_scr[2, rb, :] = br
        x_scr[3, rb, :] = bi
        sfr = s_scr[0, rf, :]
        sfi = s_scr[1, rf, :]
        sbr = s_scr[2, rb, :]
        sbi = s_scr[3, rb, :]
        fr, fi = afr * fr - afi * fi + sfr, afr * fi + afi * fr + sfi
        br, bi = abr * br - abi * bi + sbr, abr * bi + abi * br + sbi
    y = jnp.dot(u, m_ref[0], preferred_element_type=F32)
    x_in = jnp.concatenate([x_scr[part] for part in range(4)], axis=-1).astype(BF16)
    y = y + jnp.dot(x_in, r_ref[0], preferred_element_type=F32)
    for h in range(SSM_GROUP):
        y_ref[h] = y[:, h * SSM_CHUNK:(h + 1) * SSM_CHUNK]


def _s5(u_t, p_mat, m_mat, r_mat, alpha, nc, bsz):
    rows = u_t.shape[1]
    kern = functools.partial(_s5_kernel, nc=nc, bsz=bsz)
    seq = pl.BlockSpec((SSM_GROUP, rows, SSM_CHUNK), lambda i: (i, 0, 0))
    return pl.pallas_call(
        kern,
        grid=(N_GROUPS,),
        in_specs=[seq,
                  pl.BlockSpec((1, SSM_ROW, 4 * LANES), lambda i: (i, 0, 0)),
                  pl.BlockSpec((1, SSM_ROW, SSM_ROW), lambda i: (i, 0, 0)),
                  pl.BlockSpec((1, 4 * LANES, SSM_ROW), lambda i: (i, 0, 0)),
                  pl.BlockSpec((1, 4, LANES), lambda i: (i, 0, 0))],
        out_specs=seq,
        out_shape=jax.ShapeDtypeStruct(u_t.shape, F32),
        scratch_shapes=[pltpu.VMEM((4, rows, LANES), F32), pltpu.VMEM((4, rows, LANES), F32)],
        compiler_params=_cparams(("parallel",)),
        name="s5_scan",
    )(u_t, p_mat, m_mat, r_mat, alpha)


def _post_mix_kernel(x_ref, att_ref, y_ref, wglu_ref, bglu_ref, gs_ref, wout_ref, g2_ref,
                     wrh_ref, wrl_ref, br_ref, x2_ref, h2_ref, ids_ref, gates_ref):
    y = jnp.concatenate([y_ref[:, j, :].T for j in range(y_ref.shape[1])], axis=0)
    y = 0.5 * y * (1.0 + jnp.tanh(math.sqrt(2.0 / math.pi) * (y + 0.044715 * (y * y * y))))
    z = jnp.dot(y.astype(BF16), wglu_ref[...], preferred_element_type=F32) + bglu_ref[...]
    y = y * (1.0 / (1.0 + jnp.exp(-z)))
    ms = jnp.mean(y * y, axis=-1, keepdims=True)
    ssm = (y * lax.rsqrt(ms + RMS_EPS) * gs_ref[...]).astype(BF16)
    mix = jnp.dot(att_ref[...], wout_ref[0:W_ATT, :], preferred_element_type=F32)
    mix = mix + jnp.dot(ssm, wout_ref[W_ATT:, :], preferred_element_type=F32)
    x2 = x_ref[...] + mix
    x2_ref[...] = x2
    ms2 = jnp.mean(x2 * x2, axis=-1, keepdims=True)
    h2 = x2 * lax.rsqrt(ms2 + RMS_EPS) * g2_ref[...]
    h2_ref[...] = _pack_bf16_pair(h2[:, :D_MODEL // 2], h2[:, D_MODEL // 2:])
    h_hi = h2.astype(BF16)
    h_lo = (h2 - h_hi.astype(F32)).astype(BF16)
    logits = (jnp.dot(h_hi, wrh_ref[...], preferred_element_type=F32)
              + jnp.dot(h_lo, wrh_ref[...], preferred_element_type=F32)
              + jnp.dot(h_hi, wrl_ref[...], preferred_element_type=F32)) + br_ref[...]
    lane = lax.broadcasted_iota(jnp.int32, logits.shape, 1).astype(F32)
    neg = jnp.float32(-jnp.inf)
    cur = jnp.where(lane < N_EXPERTS, logits, neg)
    ids = jnp.zeros(logits.shape, F32)
    vals = jnp.zeros(logits.shape, F32)
    top = None
    den = None
    for kk in range(TOP_K):
        mx = jnp.max(cur, axis=1, keepdims=True)
        idx = jnp.min(jnp.where(cur == mx, lane, float(LANES)), axis=1, keepdims=True)
        if kk == 0:
            top = mx
        e = jnp.exp(mx - top)
        den = e if den is None else den + e
        ids = jnp.where(lane == kk, idx, ids)
        vals = jnp.where(lane == kk, e, vals)
        cur = jnp.where(lane == idx, neg, cur)
    ids_ref[...] = ids.astype(jnp.int32)
    gates_ref[...] = vals / den


def _post_mix(x2d, att, yssm, wglu, bglu, gs, wout, g2, wr_hi, wr_lo, br):
    t = x2d.shape[0]
    tm = min(ROW_TILE, t)
    row = lambda i: (i, 0)
    const = lambda i: (0, 0)
    return pl.pallas_call(
        _post_mix_kernel,
        grid=(t // tm,),
        in_specs=[pl.BlockSpec((tm, D_MODEL), row),
                  pl.BlockSpec((tm, W_ATT), row),
                  pl.BlockSpec((W_SSM, tm // SSM_CHUNK, SSM_CHUNK), lambda i: (0, i, 0)),
                  pl.BlockSpec((W_SSM, W_SSM), const),
                  pl.BlockSpec((1, W_SSM), const),
                  pl.BlockSpec((1, W_SSM), const),
                  pl.BlockSpec((D_MODEL, D_MODEL), const),
                  pl.BlockSpec((1, D_MODEL), const),
                  pl.BlockSpec((D_MODEL, LANES), const),
                  pl.BlockSpec((D_MODEL, LANES), const),
                  pl.BlockSpec((1, LANES), const)],
        out_specs=[pl.BlockSpec((tm, D_MODEL), row),
                   pl.BlockSpec((tm, D_MODEL // 2), row),
                   pl.BlockSpec((tm, LANES), row),
                   pl.BlockSpec((tm, LANES), row)],
        out_shape=[jax.ShapeDtypeStruct((t, D_MODEL), F32),
                   jax.ShapeDtypeStruct((t, D_MODEL // 2), PACKED),
                   jax.ShapeDtypeStruct((t, LANES), jnp.int32),
                   jax.ShapeDtypeStruct((t, LANES), F32)],
        compiler_params=_cparams(("parallel",)),
        name="post_mix",
    )(x2d, att, yssm, wglu, bglu, gs, wout, g2, wr_hi, wr_lo, br)


def _sc_gather_rows(x, idx):
    n = idx.shape[0]
    d = x.shape[1]
    row_bytes = d * x.dtype.itemsize
    step_rows = SC_STEP_BYTES // row_bytes
    steps = n // step_rows
    workers = SC_CORES * SC_SUBCORES
    assert steps % (2 * workers) == 0, (n, steps)
    per_worker = steps // workers
    idx_rows = jnp.pad(idx.reshape(steps, step_rows), ((0, 0), (0, LANES - step_rows)))
    mesh = plsc.VectorSubcoreMesh(core_axis_name="core", subcore_axis_name="subcore")

    cost = pl.CostEstimate(flops=0, transcendentals=0,
                           bytes_accessed=2 * n * row_bytes + idx_rows.size * idx_rows.dtype.itemsize)

    @pl.kernel(out_type=jax.ShapeDtypeStruct((n, d), x.dtype), mesh=mesh, cost_estimate=cost,
               name="sc_gather_rows",
               scratch_types=[pltpu.VMEM((2, 1, LANES), jnp.int32),
                              pltpu.VMEM((2, step_rows, d), x.dtype),
                              pltpu.SemaphoreType.DMA((2,)),
                              pltpu.SemaphoreType.DMA((2,))])
    def gather(x_hbm, i_hbm, o_hbm, idx_v, buf, sem_in, sem_out):
        base = (lax.axis_index("core") * SC_SUBCORES + lax.axis_index("subcore")) * per_worker

        def gather_copy(slot):
            return pltpu.make_async_copy(x_hbm.at[idx_v.at[slot, 0, pl.ds(0, step_rows)]], buf.at[slot],
                                         sem_in.at[slot])

        def out_copy(step, slot):
            return pltpu.make_async_copy(buf.at[slot], o_hbm.at[pl.ds(step * step_rows, step_rows), :],
                                         sem_out.at[slot])

        @pl.loop(0, per_worker // 2)
        def _(j):
            for slot in range(2):
                step = base + 2 * j + slot

                @pl.when(j > 0)
                def _():
                    out_copy(step, slot).wait()

                pltpu.sync_copy(i_hbm.at[pl.ds(step, 1), :], idx_v.at[slot])
                gather_copy(slot).start()
            for slot in range(2):
                gather_copy(slot).wait()
                out_copy(base + 2 * j + slot, slot).start()

        for slot in range(2):
            out_copy(base, slot).wait()

    return gather(x, idx_rows)


def _expert_kernel(be_ref, x_ref, w1_ref, b1_ref, w2_ref, b2_ref, y_ref, w1_bf, w2_bf):
    i = pl.program_id(0)

    @pl.when(jnp.logical_or(i == 0, be_ref[i] != be_ref[jnp.maximum(i - 1, 0)]))
    def _():
        w1_bf[...] = w1_ref[0].astype(BF16)
        w2_bf[...] = w2_ref[0].astype(BF16)

    x = jnp.concatenate(_unpack_bf16_pair(x_ref[...]), axis=1).astype(BF16)
    hdn = jnp.dot(x, w1_bf[...], preferred_element_type=F32) + b1_ref[0]
    gate = jnp.minimum(hdn[:, :D_FF], SWIGLU_LIMIT)
    lin = jnp.clip(hdn[:, D_FF:], -SWIGLU_LIMIT, SWIGLU_LIMIT)
    act = gate * (1.0 / (1.0 + jnp.exp(-SWIGLU_ALPHA * gate))) * (lin + 1.0)
    y = jnp.dot(act.astype(BF16), w2_bf[...], preferred_element_type=F32) + b2_ref[0]
    y_ref[...] = _pack_bf16_pair(y[:, :D_MODEL // 2], y[:, D_MODEL // 2:])


def _experts(block_e, x_pad, w1, b1, w2, b2):
    n_pad = x_pad.shape[0]
    rows = EXPERT_ROWS
    grid_spec = pltpu.PrefetchScalarGridSpec(
        num_scalar_prefetch=1,
        grid=(n_pad // rows,),
        in_specs=[pl.BlockSpec((rows, D_MODEL // 2), lambda i, be: (i, 0)),
                  pl.BlockSpec((1, D_MODEL, 2 * D_FF), lambda i, be: (be[i], 0, 0)),
                  pl.BlockSpec((1, 1, 2 * D_FF), lambda i, be: (be[i], 0, 0)),
                  pl.BlockSpec((1, D_FF, D_MODEL), lambda i, be: (be[i], 0, 0)),
                  pl.BlockSpec((1, 1, D_MODEL), lambda i, be: (be[i], 0, 0))],
        out_specs=pl.BlockSpec((rows, D_MODEL // 2), lambda i, be: (i, 0)),
        scratch_shapes=[pltpu.VMEM((D_MODEL, 2 * D_FF), BF16), pltpu.VMEM((D_FF, D_MODEL), BF16)],
    )
    return pl.pallas_call(
        _expert_kernel,
        grid_spec=grid_spec,
        out_shape=jax.ShapeDtypeStruct((n_pad, D_MODEL // 2), PACKED),
        compiler_params=_cparams(("arbitrary",)),
        name="moe_experts",
    )(block_e, x_pad, w1, b1, w2, b2)


def _combine_kernel(x2_ref, gates_ref, gf_ref, y_ref, o_ref):
    gates = gates_ref[...]
    lo = x2_ref[:, :D_MODEL // 2]
    hi = x2_ref[:, D_MODEL // 2:]
    for kk in range(TOP_K):
        y_lo, y_hi = _unpack_bf16_pair(y_ref[kk])
        lo = lo + gates[:, kk:kk + 1] * y_lo
        hi = hi + gates[:, kk:kk + 1] * y_hi
    y = jnp.concatenate([lo, hi], axis=1)
    ms = jnp.mean(y * y, axis=-1, keepdims=True)
    o_ref[...] = y * lax.rsqrt(ms + RMS_EPS) * gf_ref[...]


def _combine(x2, gates, gf, y_sel):
    t = x2.shape[0]
    rows = min(COMBINE_ROWS, t)
    row = lambda i: (i, 0)
    return pl.pallas_call(
        _combine_kernel,
        grid=(t // rows,),
        in_specs=[pl.BlockSpec((rows, D_MODEL), row),
                  pl.BlockSpec((rows, LANES), row),
                  pl.BlockSpec((1, D_MODEL), lambda i: (0, 0)),
                  pl.BlockSpec((TOP_K, rows, D_MODEL // 2), lambda i: (0, i, 0))],
        out_specs=pl.BlockSpec((rows, D_MODEL), row),
        out_shape=jax.ShapeDtypeStruct((t, D_MODEL), F32),
        compiler_params=_cparams(("parallel",)),
        name="moe_combine",
    )(x2, gates, gf, y_sel)


def _dispatch_plan(top_e, n_tok):
    n = n_tok * TOP_K
    blk = EXPERT_ROWS
    flat_e = top_e.reshape(n)
    order = jnp.argsort(flat_e).astype(jnp.int32)
    rank = jnp.argsort(order).astype(jnp.int32)
    st = order // TOP_K
    experts = jnp.arange(N_EXPERTS, dtype=jnp.int32)
    counts = jnp.sum((flat_e[:, None] == experts).astype(jnp.int32), axis=0)
    padded = (counts + blk - 1) // blk * blk
    start = jnp.cumsum(counts) - counts
    pend = jnp.cumsum(padded)
    pstart = pend - padded
    pos = (pstart - start)[flat_e] + rank
    n_blocks = n // blk + N_EXPERTS
    block_start = jnp.arange(n_blocks, dtype=jnp.int32) * blk
    block_e = jnp.minimum(jnp.sum((pend[None, :] <= block_start[:, None]).astype(jnp.int32), axis=1),
                          N_EXPERTS - 1)
    slot = jnp.arange(n_blocks * blk, dtype=jnp.int32)
    slot_e = jnp.repeat(block_e, blk)
    off = slot - pstart[slot_e]
    src = jnp.clip(start[slot_e] + off, 0, n - 1)
    tok_pad = jnp.where(off < counts[slot_e], st[src], 0)
    return tok_pad, pos, block_e


def _trunk_front(x, prm):
    bsz, l, _ = x.shape
    t = bsz * l
    x2d = x.reshape(t, D_MODEL)
    q, k, v, u_t = _in_proj(x2d, prm['norm1_g'], prm['w_in'])

    t_att = min(512, l)
    att = _attention(q.reshape(bsz, l, W_ATT), k.reshape(bsz, l, W_ATT), v.reshape(bsz, l, W_ATT),
                     prm['bias_tiles'][t_att], prm['lam'], prm['subln_g'], t_att)

    yssm = _s5(u_t, prm['ssm_p'], prm['ssm_m'], prm['ssm_r'], prm['ssm_alpha'], l // SSM_CHUNK, bsz)

    x2, h2, ids, gates = _post_mix(x2d, att.reshape(t, W_ATT), yssm, prm['w_glu'], prm['b_glu'],
                                   prm['ssm_norm_g'], prm['w_out'], prm['norm2_g'],
                                   prm['w_router_hi'], prm['w_router_lo'], prm['b_router'])

    tok_pad, pos, block_e = _dispatch_plan(ids[:, :TOP_K], t)
    return {'shape': x.shape, 'x2': x2, 'h2': h2, 'gates': gates,
            'tok_pad': tok_pad, 'pos': pos, 'block_e': block_e}


def _trunk_back(st, x_pad, prm):
    t = st['x2'].shape[0]
    y_pad = _experts(st['block_e'], x_pad, prm['w_moe1'], prm['b_moe1'], prm['w_moe2'], prm['b_moe2'])
    pos_by_k = st['pos'].reshape(t, TOP_K).T.reshape(TOP_K * t)
    y_sel = _sc_gather_rows(y_pad, pos_by_k).reshape(TOP_K, t, D_MODEL // 2)
    out = _combine(st['x2'], st['gates'], prm['normf_g'], y_sel)
    return out.reshape(st['shape'])


def _prepare(seq_lens, rel_bias, norm1_g, w_in, lambda_q1, lambda_k1, lambda_q2, lambda_k2, subln_g,
             ssm_A_re, ssm_A_im, ssm_log_dt, ssm_B_re, ssm_B_im, ssm_C_re, ssm_C_im, ssm_D,
             w_glu, b_glu, ssm_norm_g, w_out, norm2_g, w_router, b_router,
             w_moe1, b_moe1, w_moe2, b_moe2, normf_g):
    layer = 0
    lambda_init = 0.8 - 0.6 * math.exp(-0.3 * layer)
    lam = (jnp.exp(jnp.sum(lambda_q1[layer].astype(F32) * lambda_k1[layer].astype(F32)))
           - jnp.exp(jnp.sum(lambda_q2[layer].astype(F32) * lambda_k2[layer].astype(F32))) + lambda_init)
    p_mat, m_mat, r_mat, alpha = _ssm_matrices(
        ssm_A_re[layer], ssm_A_im[layer], ssm_log_dt[layer], ssm_B_re[layer], ssm_B_im[layer],
        ssm_C_re[layer], ssm_C_im[layer], ssm_D[layer])
    pad_e = LANES - N_EXPERTS
    w_r = jnp.pad(w_router[layer].astype(F32), ((0, 0), (0, pad_e)))
    w_r_hi = w_r.astype(BF16)
    return {
        'norm1_g': norm1_g[layer].reshape(1, D_MODEL).astype(F32),
        'w_in': w_in[layer].astype(BF16),
        'lam': lam.reshape(1).astype(F32),
        'subln_g': (subln_g[layer].astype(F32) * (1.0 - lambda_init)).reshape(1, V_DIM),
        'bias_tiles': {t: _bias_tiles(rel_bias, t) for t in sorted({min(512, l) for l in seq_lens})},
        'ssm_p': p_mat, 'ssm_m': m_mat, 'ssm_r': r_mat, 'ssm_alpha': alpha,
        'w_glu': w_glu[layer].astype(BF16),
        'b_glu': b_glu[layer].reshape(1, W_SSM).astype(F32),
        'ssm_norm_g': ssm_norm_g[layer].reshape(1, W_SSM).astype(F32),
        'w_out': w_out[layer].astype(BF16),
        'norm2_g': norm2_g[layer].reshape(1, D_MODEL).astype(F32),
        'w_router_hi': w_r_hi,
        'w_router_lo': (w_r - w_r_hi.astype(F32)).astype(BF16),
        'b_router': jnp.pad(b_router[layer].astype(F32), (0, pad_e)).reshape(1, LANES),
        'w_moe1': w_moe1[layer].astype(F32),
        'b_moe1': b_moe1[layer].reshape(N_EXPERTS, 1, 2 * D_FF).astype(F32),
        'w_moe2': w_moe2[layer].astype(F32),
        'b_moe2': b_moe2[layer].reshape(N_EXPERTS, 1, D_MODEL).astype(F32),
        'normf_g': normf_g.reshape(1, D_MODEL).astype(F32),
    }


def kernel(x_prompt, x_sample, rel_bias, norm1_g, w_in, lambda_q1, lambda_k1, lambda_q2, lambda_k2, subln_g, ssm_A_re, ssm_A_im, ssm_log_dt, ssm_B_re, ssm_B_im, ssm_C_re, ssm_C_im, ssm_D, w_glu, b_glu, ssm_norm_g, w_out, norm2_g, w_router, b_router, w_moe1, b_moe1, w_moe2, b_moe2, normf_g):
    prm = _prepare((x_prompt.shape[1], x_sample.shape[1]), rel_bias, norm1_g, w_in, lambda_q1,
                   lambda_k1, lambda_q2, lambda_k2, subln_g, ssm_A_re, ssm_A_im, ssm_log_dt,
                   ssm_B_re, ssm_B_im, ssm_C_re, ssm_C_im, ssm_D, w_glu, b_glu, ssm_norm_g, w_out,
                   norm2_g, w_router, b_router, w_moe1, b_moe1, w_moe2, b_moe2, normf_g)
    first = _trunk_front(x_prompt, prm)
    x_sample, tok_pad = lax.optimization_barrier((x_sample, first['tok_pad']))
    x_pad_first = _sc_gather_rows(first['h2'], tok_pad)
    second = _trunk_front(x_sample, prm)
    x_pad_second = _sc_gather_rows(second['h2'], second['tok_pad'])
    return (_trunk_back(first, x_pad_first, prm), _trunk_back(second, x_pad_second, prm))
```

```python
import functools
import math

import jax
import jax.numpy as jnp
from jax import lax
from jax.experimental import pallas as pl
from jax.experimental.pallas import tpu as pltpu
from jax.experimental.pallas import tpu_sc as plsc

F32 = jnp.float32
BF16 = jnp.bfloat16
PACKED = jnp.uint32

D_MODEL = 1024
W_ATT = 512
W_SSM = 512
HEAD_DIM = 64
N_HEADS = 4
V_DIM = 2 * HEAD_DIM
SSM_GROUP = 16
N_GROUPS = W_SSM // SSM_GROUP
SSM_STATE = 64
IN_WIDTH = 3 * W_ATT + W_SSM
N_BUCKETS = 32
MAX_DISTANCE = 128
N_EXPERTS = 32
TOP_K = 4
D_FF = D_MODEL
SWIGLU_ALPHA = 1.702
SWIGLU_LIMIT = 7.0
RMS_EPS = 1e-6
ATT_SCALE = HEAD_DIM ** -0.5
LOG2E = math.log2(math.e)

LANES = 128
PV_KEYS = 256
SSM_CHUNK = LANES
SSM_ROW = SSM_CHUNK * SSM_GROUP
ROW_TILE = 1024
EXPERT_ROWS = 512
SC_STEP_BYTES = 128 * 1024
SC_CORES = 2
SC_SUBCORES = 16
COMBINE_ROWS = 256
VMEM_LIMIT = 56 * 1024 * 1024


def _pack_bf16_pair(lo, hi):
    lo_bits = lax.bitcast_convert_type(lo.astype(BF16).astype(F32), PACKED)
    hi_bits = lax.bitcast_convert_type(hi.astype(BF16).astype(F32), PACKED)
    return (hi_bits & jnp.uint32(0xFFFF0000)) | (lo_bits >> 16)


def _unpack_bf16_pair(words):
    lo = lax.bitcast_convert_type(words << 16, F32)
    hi = lax.bitcast_convert_type(words & jnp.uint32(0xFFFF0000), F32)
    return lo, hi


def _cparams(sem):
    return pltpu.CompilerParams(dimension_semantics=sem, vmem_limit_bytes=VMEM_LIMIT)


def _in_proj_kernel(x_ref, g_ref, w_ref, q_ref, k_ref, v_ref, ut_ref):
    x = x_ref[...]
    ms = jnp.mean(x * x, axis=-1, keepdims=True)
    h = (x * lax.rsqrt(ms + RMS_EPS) * g_ref[...]).astype(BF16)
    proj = jnp.dot(h, w_ref[...], preferred_element_type=F32)
    q_ref[...] = (proj[:, 0:W_ATT] * (ATT_SCALE * LOG2E)).astype(BF16)
    k_ref[...] = proj[:, W_ATT:2 * W_ATT].astype(BF16)
    v_ref[...] = proj[:, 2 * W_ATT:3 * W_ATT].astype(BF16)
    ut = proj[:, 3 * W_ATT:].T
    for j in range(ut_ref.shape[1]):
        ut_ref[:, j, :] = ut[:, j * SSM_CHUNK:(j + 1) * SSM_CHUNK]


def _in_proj(x2d, g, w_bf16):
    t = x2d.shape[0]
    tm = min(ROW_TILE, t)
    out = jax.ShapeDtypeStruct((t, W_ATT), BF16)
    row = lambda i: (i, 0)
    return pl.pallas_call(
        _in_proj_kernel,
        grid=(t // tm,),
        in_specs=[pl.BlockSpec((tm, D_MODEL), row),
                  pl.BlockSpec((1, D_MODEL), lambda i: (0, 0)),
                  pl.BlockSpec((D_MODEL, IN_WIDTH), lambda i: (0, 0))],
        out_specs=[pl.BlockSpec((tm, W_ATT), row)] * 3
        + [pl.BlockSpec((W_SSM, tm // SSM_CHUNK, SSM_CHUNK), lambda i: (0, i, 0))],
        out_shape=[out] * 3 + [jax.ShapeDtypeStruct((W_SSM, t // SSM_CHUNK, SSM_CHUNK), F32)],
        compiler_params=_cparams(("parallel",)),
        name="in_proj",
    )(x2d, g, w_bf16)


def _t5_bucket(rel):
    half = N_BUCKETS // 2
    max_exact = half // 2
    ret = jnp.where(rel > 0, half, 0).astype(jnp.int32)
    n = jnp.abs(rel)
    nf = jnp.maximum(n, 1).astype(F32)
    large = max_exact + (jnp.log(nf / max_exact) / math.log(MAX_DISTANCE / max_exact)
                         * (half - max_exact)).astype(jnp.int32)
    large = jnp.minimum(large, half - 1)
    return ret + jnp.where(n < max_exact, n, large)


def _bias_tiles(rel_bias, t):
    i = jnp.arange(t, dtype=jnp.int32)
    d = jnp.arange(-2, 3, dtype=jnp.int32)
    rel = d[:, None, None] * t + i[None, None, :] - i[None, :, None]
    onehot = (_t5_bucket(rel)[..., None] == jnp.arange(N_BUCKETS, dtype=jnp.int32)).astype(F32)
    tiles = jnp.einsum('dqkn,nh->hdqk', onehot, rel_bias.astype(F32), precision=lax.Precision.HIGHEST)
    return tiles * LOG2E


def _attn_kernel(lam_ref, far_ref, q_ref, k_ref, v_ref, bias_ref, g_ref, o_ref,
                 m_scr, acc_scr, sa_scr, sb_scr, mxa_scr, mxb_scr, *, t, sub, n_iter):
    head = pl.program_id(1)
    qi = pl.program_id(2)
    q = q_ref[0]
    lane = lax.broadcasted_iota(jnp.int32, q.shape, 1)
    zero = jnp.zeros_like(q)
    qs = (jnp.where(lane < HEAD_DIM, q, zero), jnp.where(lane >= HEAD_DIM, q, zero))
    nb = t // LANES
    n_near = min(n_iter, 2 if sub >= 2 else 3)
    first = jnp.maximum(qi - 1, 0) // sub

    m_scr[...] = jnp.full(m_scr.shape, -jnp.inf, F32)
    acc_scr[...] = jnp.zeros(acc_scr.shape, F32)

    def chunk_of(r):
        return (first + r) % n_iter

    def scores(r, s_scr, mx_scr, near):
        j = chunk_of(r)
        for mi in range(2):
            mx = None
            for c in range(sub):
                blk = j * sub + c
                kc = k_ref[0, pl.ds(pl.multiple_of(blk * t, t), t), :]
                s = lax.dot_general(qs[mi], kc, (((1,), (1,)), ((), ())), preferred_element_type=F32)
                if near:
                    s = s + bias_ref[0, jnp.clip(blk - qi, -2, 2) + 2]
                s_scr[mi, :, c * t:(c + 1) * t] = s
                for i in range(nb):
                    piece = s[:, i * LANES:(i + 1) * LANES]
                    mx = piece if mx is None else jnp.maximum(mx, piece)
            mx_scr[mi] = mx

    def accumulate(r, s_scr, mx_scr, near):
        j = chunk_of(r)
        vj = v_ref[0, pl.ds(pl.multiple_of(j * (sub * t), sub * t), sub * t), :]
        vj = jnp.concatenate([vj, jnp.ones_like(vj)], axis=1)
        shift = 0.0 if near else jnp.where(j * sub < qi, far_ref[2 * head], far_ref[2 * head + 1])
        for mi in range(2):
            m_prev = m_scr[mi]
            m_next = jnp.maximum(m_prev, jnp.max(mx_scr[mi], axis=1, keepdims=True) + shift)
            alpha = jnp.exp2(m_prev - m_next)
            m_scr[mi] = m_next
            m_raw = m_next - shift
            pv = None
            for i in range(sub * t // PV_KEYS):
                ps = []
                for c in range(PV_KEYS // LANES):
                    lo = i * PV_KEYS + c * LANES
                    p = jnp.exp2(s_scr[mi, :, lo:lo + LANES] - m_raw)
                    ps.append(p.astype(BF16))
                d = jnp.dot(jnp.concatenate(ps, axis=1), vj[i * PV_KEYS:(i + 1) * PV_KEYS, :],
                            preferred_element_type=F32)
                pv = d if pv is None else pv + d
            acc_scr[mi] = jnp.concatenate([alpha, alpha], axis=1) * acc_scr[mi] + pv

    def pair(i, first_near, second_near):
        scores(2 * i + 1, sb_scr, mxb_scr, second_near[0])
        accumulate(2 * i, sa_scr, mxa_scr, first_near)
        scores(2 * i + 2, sa_scr, mxa_scr, second_near[1])
        accumulate(2 * i + 1, sb_scr, mxb_scr, second_near[0])

    scores(0, sa_scr, mxa_scr, True)
    n_pairs = (n_iter - 1) // 2
    peeled = min(n_pairs, (n_near + 1) // 2)
    for i in range(peeled):
        pair(i, 2 * i < n_near, (2 * i + 1 < n_near, 2 * i + 2 < n_near))

    def body(i, carry):
        pair(i, False, (False, False))
        return carry

    lax.fori_loop(peeled, n_pairs, body, 0)
    if (n_iter - 1) % 2 == 1:
        scores(n_iter - 1, sb_scr, mxb_scr, n_iter - 1 < n_near)
        accumulate(n_iter - 2, sa_scr, mxa_scr, n_iter - 2 < n_near)
        accumulate(n_iter - 1, sb_scr, mxb_scr, n_iter - 1 < n_near)
    else:
        accumulate(n_iter - 1, sa_scr, mxa_scr, n_iter - 1 < n_near)

    o1 = acc_scr[0, :, :V_DIM] / acc_scr[0, :, V_DIM:]
    o2 = acc_scr[1, :, :V_DIM] / acc_scr[1, :, V_DIM:]
    o = o1 - lam_ref[0] * o2
    ms = jnp.mean(o * o, axis=-1, keepdims=True)
    o_ref[0] = (o * lax.rsqrt(ms + RMS_EPS) * g_ref[...]).astype(BF16)


def _attention(q, k, v, bias_tiles, lam, g_scaled, t):
    b, l, _ = q.shape
    nq = l // t
    sub = 2 if nq % 2 == 0 else 1
    kern = functools.partial(_attn_kernel, t=t, sub=sub, n_iter=nq // sub)
    far = jnp.stack([bias_tiles[:, 0, 0, 0], bias_tiles[:, 4, 0, 0]], axis=1).reshape(2 * N_HEADS)
    return pl.pallas_call(
        kern,
        grid=(b, N_HEADS, nq),
        in_specs=[pl.BlockSpec(memory_space=pltpu.SMEM),
                  pl.BlockSpec(memory_space=pltpu.SMEM),
                  pl.BlockSpec((1, t, V_DIM), lambda bi, h, qi: (bi, qi, h)),
                  pl.BlockSpec((1, l, V_DIM), lambda bi, h, qi: (bi, 0, h)),
                  pl.BlockSpec((1, l, V_DIM), lambda bi, h, qi: (bi, 0, h)),
                  pl.BlockSpec((1, 5, t, t), lambda bi, h, qi: (h, 0, 0, 0)),
                  pl.BlockSpec((1, V_DIM), lambda bi, h, qi: (0, 0))],
        out_specs=pl.BlockSpec((1, t, V_DIM), lambda bi, h, qi: (bi, qi, h)),
        out_shape=jax.ShapeDtypeStruct((b, l, W_ATT), BF16),
        scratch_shapes=[pltpu.VMEM((2, t, LANES), F32),
                        pltpu.VMEM((2, t, 2 * V_DIM), F32),
                        pltpu.VMEM((2, t, sub * t), F32),
                        pltpu.VMEM((2, t, sub * t), F32),
                        pltpu.VMEM((2, t, LANES), F32),
                        pltpu.VMEM((2, t, LANES), F32)],
        compiler_params=_cparams(("parallel", "parallel", "arbitrary")),
        name="diff_attention",
    )(lam, far, q, k, v, bias_tiles, g_scaled)


def _ssm_matrices(a_re, a_im, log_dt, b_re, b_im, c_re, c_im, d_skip):
    qn, g, p, hc = SSM_CHUNK, N_GROUPS, SSM_STATE, SSM_GROUP
    n = jnp.arange(qn + 1, dtype=F32)
    pw, bbar, cc = [], [], []
    for d in range(2):
        a = lax.complex(a_re[d].astype(F32), a_im[d].astype(F32))
        dt = jnp.exp(log_dt[d].astype(F32))[:, None]
        adt = a * dt
        a_bar = jnp.exp(adt)
        pw.append(jnp.exp(adt[None] * n[:, None, None]))
        bbar.append(((a_bar - 1.0) / a)[:, :, None]
                    * lax.complex(b_re[d].astype(F32), b_im[d].astype(F32)))
        cc.append(lax.complex(c_re[d].astype(F32), c_im[d].astype(F32)))

    hi = lax.Precision.HIGHEST
    kern = [jnp.einsum('gop,tgp,gpi->tgoi', cc[d], pw[d][:qn], bbar[d], precision=hi).real
            for d in range(2)]
    kf = jnp.transpose(kern[0], (1, 3, 2, 0))
    kb = jnp.transpose(kern[1], (1, 3, 2, 0))
    skip = jnp.eye(hc, dtype=F32)[None] * d_skip.astype(F32).reshape(g, hc)[:, :, None]
    taps = jnp.concatenate([kb[..., :0:-1], (kf[..., 0] + kb[..., 0] + skip)[..., None], kf[..., 1:]], axis=-1)
    m_mat = jnp.stack([taps[..., qn - 1 - s:2 * qn - 1 - s] for s in range(qn)], axis=2)
    m_mat = m_mat.reshape(g, SSM_ROW, SSM_ROW)

    zeros_p = jnp.zeros((g, SSM_ROW, LANES - p), F32)

    def pad_cols(x):
        return jnp.concatenate([x, zeros_p], axis=-1)

    pf = jnp.einsum('sgp,gpi->gisp', pw[0][:qn][::-1], bbar[0]).reshape(g, SSM_ROW, p)
    pb = jnp.einsum('sgp,gpi->gisp', pw[1][:qn], bbar[1]).reshape(g, SSM_ROW, p)
    p_mat = jnp.concatenate([pad_cols(pf.real), pad_cols(pf.imag),
                             pad_cols(pb.real), pad_cols(pb.imag)], axis=-1)

    wf = jnp.einsum('gop,tgp->gpot', cc[0], pw[0][1:qn + 1]).reshape(g, p, SSM_ROW)
    wb = jnp.einsum('gop,tgp->gpot', cc[1], pw[1][1:qn + 1][::-1]).reshape(g, p, SSM_ROW)
    zeros_r = jnp.zeros((g, LANES - p, SSM_ROW), F32)
    r_mat = jnp.concatenate([wf.real, zeros_r, -wf.imag, zeros_r,
                             wb.real, zeros_r, -wb.imag, zeros_r], axis=1)

    zeros_a = jnp.zeros((g, LANES - p), F32)

    def pad_vec(x):
        return jnp.concatenate([x, zeros_a], axis=-1)

    alpha = jnp.stack([pad_vec(pw[0][qn].real), pad_vec(pw[0][qn].imag),
                       pad_vec(pw[1][qn].real), pad_vec(pw[1][qn].imag)], axis=1)
    return p_mat.astype(BF16), m_mat.astype(BF16), r_mat.astype(BF16), alpha


def _s5_kernel(u_ref, p_ref, m_ref, r_ref, a_ref, y_ref, s_scr, x_scr, *, nc, bsz):
    u = jnp.concatenate([u_ref[h] for h in range(SSM_GROUP)], axis=-1).astype(BF16)
    s = jnp.dot(u, p_ref[0], preferred_element_type=F32)
    for part in range(4):
        s_scr[part] = s[:, part * LANES:(part + 1) * LANES]
    al = a_ref[0]
    afr = jnp.broadcast_to(al[0:1], (bsz, LANES))
    afi = jnp.broadcast_to(al[1:2], (bsz, LANES))
    abr = jnp.broadcast_to(al[2:3], (bsz, LANES))
    abi = jnp.broadcast_to(al[3:4], (bsz, LANES))
    fr = fi = br = bi = jnp.zeros((bsz, LANES), F32)
    for c in range(nc):
        rf = pl.ds(c, bsz, stride=nc)
        rb = pl.ds(nc - 1 - c, bsz, stride=nc)
        x_scr[0, rf, :] = fr
        x_scr[1, rf, :] = fi
        x_scr[2, rb, :] = br
        x_scr[3, rb, :] = bi
        sfr = s_scr[0, rf, :]
        sfi = s_scr[1, rf, :]
        sbr = s_scr[2, rb, :]
        sbi = s_scr[3, rb, :]
        fr, fi = afr * fr - afi * fi + sfr, afr * fi + afi * fr + sfi
        br, bi = abr * br - abi * bi + sbr, abr * bi + abi * br + sbi
    y = jnp.dot(u, m_ref[0], preferred_element_type=F32)
    x_in = jnp.concatenate([x_scr[part] for part in range(4)], axis=-1).astype(BF16)
    y = y + jnp.dot(x_in, r_ref[0], preferred_element_type=F32)
    for h in range(SSM_GROUP):
        y_ref[h] = y[:, h * SSM_CHUNK:(h + 1) * SSM_CHUNK]


def _s5(u_t, p_mat, m_mat, r_mat, alpha, nc, bsz):
    rows = u_t.shape[1]
    kern = functools.partial(_s5_kernel, nc=nc, bsz=bsz)
    seq = pl.BlockSpec((SSM_GROUP, rows, SSM_CHUNK), lambda i: (i, 0, 0))
    return pl.pallas_call(
        kern,
        grid=(N_GROUPS,),
        in_specs=[seq,
                  pl.BlockSpec((1, SSM_ROW, 4 * LANES), lambda i: (i, 0, 0)),
                  pl.BlockSpec((1, SSM_ROW, SSM_ROW), lambda i: (i, 0, 0)),
                  pl.BlockSpec((1, 4 * LANES, SSM_ROW), lambda i: (i, 0, 0)),
                  pl.BlockSpec((1, 4, LANES), lambda i: (i, 0, 0))],
        out_specs=seq,
        out_shape=jax.ShapeDtypeStruct(u_t.shape, F32),
        scratch_shapes=[pltpu.VMEM((4, rows, LANES), F32), pltpu.VMEM((4, rows, LANES), F32)],
        compiler_params=_cparams(("parallel",)),
        name="s5_scan",
    )(u_t, p_mat, m_mat, r_mat, alpha)


def _post_mix_kernel(x_ref, att_ref, y_ref, wglu_ref, bglu_ref, gs_ref, wout_ref, g2_ref,
                     wrh_ref, wrl_ref, br_ref, x2_ref, h2_ref, ids_ref, gates_ref):
    y = jnp.concatenate([y_ref[:, j, :].T for j in range(y_ref.shape[1])], axis=0)
    y = 0.5 * y * (1.0 + jnp.tanh(math.sqrt(2.0 / math.pi) * (y + 0.044715 * (y * y * y))))
    z = jnp.dot(y.astype(BF16), wglu_ref[...], preferred_element_type=F32) + bglu_ref[...]
    y = y * (1.0 / (1.0 + jnp.exp(-z)))
    ms = jnp.mean(y * y, axis=-1, keepdims=True)
    ssm = (y * lax.rsqrt(ms + RMS_EPS) * gs_ref[...]).astype(BF16)
    mix = jnp.dot(att_ref[...], wout_ref[0:W_ATT, :], preferred_element_type=F32)
    mix = mix + jnp.dot(ssm, wout_ref[W_ATT:, :], preferred_element_type=F32)
    x2 = x_ref[...] + mix
    x2_ref[...] = x2
    ms2 = jnp.mean(x2 * x2, axis=-1, keepdims=True)
    h2 = x2 * lax.rsqrt(ms2 + RMS_EPS) * g2_ref[...]
    h2_ref[...] = _pack_bf16_pair(h2[:, :D_MODEL // 2], h2[:, D_MODEL // 2:])
    h_hi = h2.astype(BF16)
    h_lo = (h2 - h_hi.astype(F32)).astype(BF16)
    logits = (jnp.dot(h_hi, wrh_ref[...], preferred_element_type=F32)
              + jnp.dot(h_lo, wrh_ref[...], preferred_element_type=F32)
              + jnp.dot(h_hi, wrl_ref[...], preferred_element_type=F32)) + br_ref[...]
    lane = lax.broadcasted_iota(jnp.int32, logits.shape, 1).astype(F32)
    neg = jnp.float32(-jnp.inf)
    cur = jnp.where(lane < N_EXPERTS, logits, neg)
    ids = jnp.zeros(logits.shape, F32)
    vals = jnp.zeros(logits.shape, F32)
    top = None
    den = None
    for kk in range(TOP_K):
        mx = jnp.max(cur, axis=1, keepdims=True)
        idx = jnp.min(jnp.where(cur == mx, lane, float(LANES)), axis=1, keepdims=True)
        if kk == 0:
            top = mx
        e = jnp.exp(mx - top)
        den = e if den is None else den + e
        ids = jnp.where(lane == kk, idx, ids)
        vals = jnp.where(lane == kk, e, vals)
        cur = jnp.where(lane == idx, neg, cur)
    ids_ref[...] = ids.astype(jnp.int32)
    gates_ref[...] = vals / den


def _post_mix(x2d, att, yssm, wglu, bglu, gs, wout, g2, wr_hi, wr_lo, br):
    t = x2d.shape[0]
    tm = min(ROW_TILE, t)
    row = lambda i: (i, 0)
    const = lambda i: (0, 0)
    return pl.pallas_call(
        _post_mix_kernel,
        grid=(t // tm,),
        in_specs=[pl.BlockSpec((tm, D_MODEL), row),
                  pl.BlockSpec((tm, W_ATT), row),
                  pl.BlockSpec((W_SSM, tm // SSM_CHUNK, SSM_CHUNK), lambda i: (0, i, 0)),
                  pl.BlockSpec((W_SSM, W_SSM), const),
                  pl.BlockSpec((1, W_SSM), const),
                  pl.BlockSpec((1, W_SSM), const),
                  pl.BlockSpec((D_MODEL, D_MODEL), const),
                  pl.BlockSpec((1, D_MODEL), const),
                  pl.BlockSpec((D_MODEL, LANES), const),
                  pl.BlockSpec((D_MODEL, LANES), const),
                  pl.BlockSpec((1, LANES), const)],
        out_specs=[pl.BlockSpec((tm, D_MODEL), row),
                   pl.BlockSpec((tm, D_MODEL // 2), row),
                   pl.BlockSpec((tm, LANES), row),
                   pl.BlockSpec((tm, LANES), row)],
        out_shape=[jax.ShapeDtypeStruct((t, D_MODEL), F32),
                   jax.ShapeDtypeStruct((t, D_MODEL // 2), PACKED),
                   jax.ShapeDtypeStruct((t, LANES), jnp.int32),
                   jax.ShapeDtypeStruct((t, LANES), F32)],
        compiler_params=_cparams(("parallel",)),
        name="post_mix",
    )(x2d, att, yssm, wglu, bglu, gs, wout, g2, wr_hi, wr_lo, br)


def _sc_gather_rows(x, idx):
    n = idx.shape[0]
    d = x.shape[1]
    row_bytes = d * x.dtype.itemsize
    step_rows = SC_STEP_BYTES // row_bytes
    steps = n // step_rows
    workers = SC_CORES * SC_SUBCORES
    assert steps % (2 * workers) == 0, (n, steps)
    per_worker = steps // workers
    idx_rows = jnp.pad(idx.reshape(steps, step_rows), ((0, 0), (0, LANES - step_rows)))
    mesh = plsc.VectorSubcoreMesh(core_axis_name="core", subcore_axis_name="subcore")

    cost = pl.CostEstimate(flops=0, transcendentals=0,
                           bytes_accessed=2 * n * row_bytes + idx_rows.size * idx_rows.dtype.itemsize)

    @pl.kernel(out_type=jax.ShapeDtypeStruct((n, d), x.dtype), mesh=mesh, cost_estimate=cost,
               name="sc_gather_rows",
               scratch_types=[pltpu.VMEM((2, 1, LANES), jnp.int32),
                              pltpu.VMEM((2, step_rows, d), x.dtype),
                              pltpu.SemaphoreType.DMA((2,)),
                              pltpu.SemaphoreType.DMA((2,))])
    def gather(x_hbm, i_hbm, o_hbm, idx_v, buf, sem_in, sem_out):
        base = (lax.axis_index("core") * SC_SUBCORES + lax.axis_index("subcore")) * per_worker

        def gather_copy(slot):
            return pltpu.make_async_copy(x_hbm.at[idx_v.at[slot, 0, pl.ds(0, step_rows)]], buf.at[slot],
                                         sem_in.at[slot])

        def out_copy(step, slot):
            return pltpu.make_async_copy(buf.at[slot], o_hbm.at[pl.ds(step * step_rows, step_rows), :],
                                         sem_out.at[slot])

        @pl.loop(0, per_worker // 2)
        def _(j):
            for slot in range(2):
                step = base + 2 * j + slot

                @pl.when(j > 0)
                def _():
                    out_copy(step, slot).wait()

                pltpu.sync_copy(i_hbm.at[pl.ds(step, 1), :], idx_v.at[slot])
                gather_copy(slot).start()
            for slot in range(2):
                gather_copy(slot).wait()
                out_copy(base + 2 * j + slot, slot).start()

        for slot in range(2):
            out_copy(base, slot).wait()

    return gather(x, idx_rows)


def _expert_kernel(be_ref, x_ref, w1_ref, b1_ref, w2_ref, b2_ref, y_ref, w1_bf, w2_bf):
    i = pl.program_id(0)

    @pl.when(jnp.logical_or(i == 0, be_ref[i] != be_ref[jnp.maximum(i - 1, 0)]))
    def _():
        w1_bf[...] = w1_ref[0].astype(BF16)
        w2_bf[...] = w2_ref[0].astype(BF16)

    x = jnp.concatenate(_unpack_bf16_pair(x_ref[...]), axis=1).astype(BF16)
    hdn = jnp.dot(x, w1_bf[...], preferred_element_type=F32) + b1_ref[0]
    gate = jnp.minimum(hdn[:, :D_FF], SWIGLU_LIMIT)
    lin = jnp.clip(hdn[:, D_FF:], -SWIGLU_LIMIT, SWIGLU_LIMIT)
    act = gate * (1.0 / (1.0 + jnp.exp(-SWIGLU_ALPHA * gate))) * (lin + 1.0)
    y = jnp.dot(act.astype(BF16), w2_bf[...], preferred_element_type=F32) + b2_ref[0]
    y_ref[...] = _pack_bf16_pair(y[:, :D_MODEL // 2], y[:, D_MODEL // 2:])


def _experts(block_e, x_pad, w1, b1, w2, b2):
    n_pad = x_pad.shape[0]
    rows = EXPERT_ROWS
    grid_spec = pltpu.PrefetchScalarGridSpec(
        num_scalar_prefetch=1,
        grid=(n_pad // rows,),
        in_specs=[pl.BlockSpec((rows, D_MODEL // 2), lambda i, be: (i, 0)),
                  pl.BlockSpec((1, D_MODEL, 2 * D_FF), lambda i, be: (be[i], 0, 0)),
                  pl.BlockSpec((1, 1, 2 * D_FF), lambda i, be: (be[i], 0, 0)),
                  pl.BlockSpec((1, D_FF, D_MODEL), lambda i, be: (be[i], 0, 0)),
                  pl.BlockSpec((1, 1, D_MODEL), lambda i, be: (be[i], 0, 0))],
        out_specs=pl.BlockSpec((rows, D_MODEL // 2), lambda i, be: (i, 0)),
        scratch_shapes=[pltpu.VMEM((D_MODEL, 2 * D_FF), BF16), pltpu.VMEM((D_FF, D_MODEL), BF16)],
    )
    return pl.pallas_call(
        _expert_kernel,
        grid_spec=grid_spec,
        out_shape=jax.ShapeDtypeStruct((n_pad, D_MODEL // 2), PACKED),
        compiler_params=_cparams(("arbitrary",)),
        name="moe_experts",
    )(block_e, x_pad, w1, b1, w2, b2)


def _combine_kernel(x2_ref, gates_ref, gf_ref, y_ref, o_ref):
    gates = gates_ref[...]
    lo = x2_ref[:, :D_MODEL // 2]
    hi = x2_ref[:, D_MODEL // 2:]
    for kk in range(TOP_K):
        y_lo, y_hi = _unpack_bf16_pair(y_ref[kk])
        lo = lo + gates[:, kk:kk + 1] * y_lo
        hi = hi + gates[:, kk:kk + 1] * y_hi
    y = jnp.concatenate([lo, hi], axis=1)
    ms = jnp.mean(y * y, axis=-1, keepdims=True)
    o_ref[...] = y * lax.rsqrt(ms + RMS_EPS) * gf_ref[...]


def _combine(x2, gates, gf, y_sel):
    t = x2.shape[0]
    rows = min(COMBINE_ROWS, t)
    row = lambda i: (i, 0)
    return pl.pallas_call(
        _combine_kernel,
        grid=(t // rows,),
        in_specs=[pl.BlockSpec((rows, D_MODEL), row),
                  pl.BlockSpec((rows, LANES), row),
                  pl.BlockSpec((1, D_MODEL), lambda i: (0, 0)),
                  pl.BlockSpec((TOP_K, rows, D_MODEL // 2), lambda i: (0, i, 0))],
        out_specs=pl.BlockSpec((rows, D_MODEL), row),
        out_shape=jax.ShapeDtypeStruct((t, D_MODEL), F32),
        compiler_params=_cparams(("parallel",)),
        name="moe_combine",
    )(x2, gates, gf, y_sel)


def _dispatch_plan(top_e, n_tok):
    n = n_tok * TOP_K
    blk = EXPERT_ROWS
    flat_e = top_e.reshape(n)
    order = jnp.argsort(flat_e).astype(jnp.int32)
    rank = jnp.argsort(order).astype(jnp.int32)
    st = order // TOP_K
    experts = jnp.arange(N_EXPERTS, dtype=jnp.int32)
    counts = jnp.sum((flat_e[:, None] == experts).astype(jnp.int32), axis=0)
    padded = (counts + blk - 1) // blk * blk
    start = jnp.cumsum(counts) - counts
    pend = jnp.cumsum(padded)
    pstart = pend - padded
    pos = (pstart - start)[flat_e] + rank
    n_blocks = n // blk + N_EXPERTS
    block_start = jnp.arange(n_blocks, dtype=jnp.int32) * blk
    block_e = jnp.minimum(jnp.sum((pend[None, :] <= block_start[:, None]).astype(jnp.int32), axis=1),
                          N_EXPERTS - 1)
    slot = jnp.arange(n_blocks * blk, dtype=jnp.int32)
    slot_e = jnp.repeat(block_e, blk)
    off = slot - pstart[slot_e]
    src = jnp.clip(start[slot_e] + off, 0, n - 1)
    tok_pad = jnp.where(off < counts[slot_e], st[src], 0)
    return tok_pad, pos, block_e


def _trunk_front(x, prm, gathered=None):
    bsz, l, _ = x.shape
    t = bsz * l
    x2d = x.reshape(t, D_MODEL)
    q, k, v, u_t = _in_proj(x2d, prm['norm1_g'], prm['w_in'])

    t_att = min(512, l)
    att = _attention(q.reshape(bsz, l, W_ATT), k.reshape(bsz, l, W_ATT), v.reshape(bsz, l, W_ATT),
                     prm['bias_tiles'][t_att], prm['lam'], prm['subln_g'], t_att)

    yssm = _s5(u_t, prm['ssm_p'], prm['ssm_m'], prm['ssm_r'], prm['ssm_alpha'], l // SSM_CHUNK, bsz)
    if gathered is not None:
        yssm, gathered = lax.optimization_barrier((yssm, gathered))

    x2, h2, ids, gates = _post_mix(x2d, att.reshape(t, W_ATT), yssm, prm['w_glu'], prm['b_glu'],
                                   prm['ssm_norm_g'], prm['w_out'], prm['norm2_g'],
                                   prm['w_router_hi'], prm['w_router_lo'], prm['b_router'])

    tok_pad, pos, block_e = _dispatch_plan(ids[:, :TOP_K], t)
    return {'shape': x.shape, 'x2': x2, 'h2': h2, 'gates': gates,
            'tok_pad': tok_pad, 'pos': pos, 'block_e': block_e, 'gathered': gathered}


def _trunk_back(st, x_pad, prm):
    t = st['x2'].shape[0]
    y_pad = _experts(st['block_e'], x_pad, prm['w_moe1'], prm['b_moe1'], prm['w_moe2'], prm['b_moe2'])
    pos_by_k = st['pos'].reshape(t, TOP_K).T.reshape(TOP_K * t)
    y_sel = _sc_gather_rows(y_pad, pos_by_k).reshape(TOP_K, t, D_MODEL // 2)
    out = _combine(st['x2'], st['gates'], prm['normf_g'], y_sel)
    return out.reshape(st['shape'])


def _prepare(seq_lens, rel_bias, norm1_g, w_in, lambda_q1, lambda_k1, lambda_q2, lambda_k2, subln_g,
             ssm_A_re, ssm_A_im, ssm_log_dt, ssm_B_re, ssm_B_im, ssm_C_re, ssm_C_im, ssm_D,
             w_glu, b_glu, ssm_norm_g, w_out, norm2_g, w_router, b_router,
             w_moe1, b_moe1, w_moe2, b_moe2, normf_g):
    layer = 0
    lambda_init = 0.8 - 0.6 * math.exp(-0.3 * layer)
    lam = (jnp.exp(jnp.sum(lambda_q1[layer].astype(F32) * lambda_k1[layer].astype(F32)))
           - jnp.exp(jnp.sum(lambda_q2[layer].astype(F32) * lambda_k2[layer].astype(F32))) + lambda_init)
    p_mat, m_mat, r_mat, alpha = _ssm_matrices(
        ssm_A_re[layer], ssm_A_im[layer], ssm_log_dt[layer], ssm_B_re[layer], ssm_B_im[layer],
        ssm_C_re[layer], ssm_C_im[layer], ssm_D[layer])
    pad_e = LANES - N_EXPERTS
    w_r = jnp.pad(w_router[layer].astype(F32), ((0, 0), (0, pad_e)))
    w_r_hi = w_r.astype(BF16)
    return {
        'norm1_g': norm1_g[layer].reshape(1, D_MODEL).astype(F32),
        'w_in': w_in[layer].astype(BF16),
        'lam': lam.reshape(1).astype(F32),
        'subln_g': (subln_g[layer].astype(F32) * (1.0 - lambda_init)).reshape(1, V_DIM),
        'bias_tiles': {t: _bias_tiles(rel_bias, t) for t in sorted({min(512, l) for l in seq_lens})},
        'ssm_p': p_mat, 'ssm_m': m_mat, 'ssm_r': r_mat, 'ssm_alpha': alpha,
        'w_glu': w_glu[layer].astype(BF16),
        'b_glu': b_glu[layer].reshape(1, W_SSM).astype(F32),
        'ssm_norm_g': ssm_norm_g[layer].reshape(1, W_SSM).astype(F32),
        'w_out': w_out[layer].astype(BF16),
        'norm2_g': norm2_g[layer].reshape(1, D_MODEL).astype(F32),
        'w_router_hi': w_r_hi,
        'w_router_lo': (w_r - w_r_hi.astype(F32)).astype(BF16),
        'b_router': jnp.pad(b_router[layer].astype(F32), (0, pad_e)).reshape(1, LANES),
        'w_moe1': w_moe1[layer].astype(F32),
        'b_moe1': b_moe1[layer].reshape(N_EXPERTS, 1, 2 * D_FF).astype(F32),
        'w_moe2': w_moe2[layer].astype(F32),
        'b_moe2': b_moe2[layer].reshape(N_EXPERTS, 1, D_MODEL).astype(F32),
        'normf_g': normf_g.reshape(1, D_MODEL).astype(F32),
    }


def kernel(x_prompt, x_sample, rel_bias, norm1_g, w_in, lambda_q1, lambda_k1, lambda_q2, lambda_k2, subln_g, ssm_A_re, ssm_A_im, ssm_log_dt, ssm_B_re, ssm_B_im, ssm_C_re, ssm_C_im, ssm_D, w_glu, b_glu, ssm_norm_g, w_out, norm2_g, w_router, b_router, w_moe1, b_moe1, w_moe2, b_moe2, normf_g):
    prm = _prepare((x_prompt.shape[1], x_sample.shape[1]), rel_bias, norm1_g, w_in, lambda_q1,
                   lambda_k1, lambda_q2, lambda_k2, subln_g, ssm_A_re, ssm_A_im, ssm_log_dt,
                   ssm_B_re, ssm_B_im, ssm_C_re, ssm_C_im, ssm_D, w_glu, b_glu, ssm_norm_g, w_out,
                   norm2_g, w_router, b_router, w_moe1, b_moe1, w_moe2, b_moe2, normf_g)
    first = _trunk_front(x_prompt, prm)
    x_sample, tok_pad = lax.optimization_barrier((x_sample, first['tok_pad']))
    x_pad_first = _sc_gather_rows(first['h2'], tok_pad)
    second = _trunk_front(x_sample, prm, gathered=x_pad_first)
    x_pad_second = _sc_gather_rows(second['h2'], second['tok_pad'])
    return (_trunk_back(first, second['gathered'], prm), _trunk_back(second, x_pad_second, prm))
```

```python
import functools
import math

import jax
import jax.numpy as jnp
from jax import lax
from jax.experimental import pallas as pl
from jax.experimental.pallas import tpu as pltpu
from jax.experimental.pallas import tpu_sc as plsc

F32 = jnp.float32
BF16 = jnp.bfloat16
PACKED = jnp.uint32

D_MODEL = 1024
W_ATT = 512
W_SSM = 512
HEAD_DIM = 64
N_HEADS = 4
V_DIM = 2 * HEAD_DIM
SSM_GROUP = 16
N_GROUPS = W_SSM // SSM_GROUP
SSM_STATE = 64
IN_WIDTH = 3 * W_ATT + W_SSM
N_BUCKETS = 32
MAX_DISTANCE = 128
N_EXPERTS = 32
TOP_K = 4
D_FF = D_MODEL
SWIGLU_ALPHA = 1.702
SWIGLU_LIMIT = 7.0
RMS_EPS = 1e-6
ATT_SCALE = HEAD_DIM ** -0.5
LOG2E = math.log2(math.e)

LANES = 128
PV_KEYS = 256
ATT_TILE = 512
SSM_CHUNK = LANES
SSM_ROW = SSM_CHUNK * SSM_GROUP
ROW_TILE = 1024
EXPERT_ROWS = 512
SC_STEP_BYTES = 128 * 1024
SC_CORES = 2
SC_SUBCORES = 16
COMBINE_ROWS = 512
VMEM_LIMIT = 56 * 1024 * 1024


def _pack_bf16_pair(lo, hi):
    lo_bits = lax.bitcast_convert_type(lo.astype(BF16).astype(F32), PACKED)
    hi_bits = lax.bitcast_convert_type(hi.astype(BF16).astype(F32), PACKED)
    return (hi_bits & jnp.uint32(0xFFFF0000)) | (lo_bits >> 16)


def _unpack_bf16_pair(words):
    lo = lax.bitcast_convert_type(words << 16, F32)
    hi = lax.bitcast_convert_type(words & jnp.uint32(0xFFFF0000), F32)
    return lo, hi


def _cparams(sem):
    return pltpu.CompilerParams(dimension_semantics=sem, vmem_limit_bytes=VMEM_LIMIT)


def _in_proj_kernel(x_ref, g_ref, w_ref, q_ref, k_ref, v_ref, ut_ref):
    x = x_ref[...]
    ms = jnp.mean(x * x, axis=-1, keepdims=True)
    h = (x * lax.rsqrt(ms + RMS_EPS) * g_ref[...]).astype(BF16)
    proj = jnp.dot(h, w_ref[...], preferred_element_type=F32)
    q_ref[...] = (proj[:, 0:W_ATT] * (ATT_SCALE * LOG2E)).astype(BF16)
    k_ref[...] = proj[:, W_ATT:2 * W_ATT].astype(BF16)
    v_ref[...] = proj[:, 2 * W_ATT:3 * W_ATT].astype(BF16)
    ut = proj[:, 3 * W_ATT:].T
    for j in range(ut_ref.shape[1]):
        ut_ref[:, j, :] = ut[:, j * SSM_CHUNK:(j + 1) * SSM_CHUNK]


def _in_proj(x2d, g, w_bf16):
    t = x2d.shape[0]
    tm = min(ROW_TILE, t)
    out = jax.ShapeDtypeStruct((t, W_ATT), BF16)
    row = lambda i: (i, 0)
    return pl.pallas_call(
        _in_proj_kernel,
        grid=(t // tm,),
        in_specs=[pl.BlockSpec((tm, D_MODEL), row),
                  pl.BlockSpec((1, D_MODEL), lambda i: (0, 0)),
                  pl.BlockSpec((D_MODEL, IN_WIDTH), lambda i: (0, 0))],
        out_specs=[pl.BlockSpec((tm, W_ATT), row)] * 3
        + [pl.BlockSpec((W_SSM, tm // SSM_CHUNK, SSM_CHUNK), lambda i: (0, i, 0))],
        out_shape=[out] * 3 + [jax.ShapeDtypeStruct((W_SSM, t // SSM_CHUNK, SSM_CHUNK), F32)],
        compiler_params=_cparams(("parallel",)),
        name="in_proj",
    )(x2d, g, w_bf16)


def _t5_bucket(rel):
    half = N_BUCKETS // 2
    max_exact = half // 2
    ret = jnp.where(rel > 0, half, 0).astype(jnp.int32)
    n = jnp.abs(rel)
    nf = jnp.maximum(n, 1).astype(F32)
    large = max_exact + (jnp.log(nf / max_exact) / math.log(MAX_DISTANCE / max_exact)
                         * (half - max_exact)).astype(jnp.int32)
    large = jnp.minimum(large, half - 1)
    return ret + jnp.where(n < max_exact, n, large)


def _bias_tiles(rel_bias, t):
    i = jnp.arange(t, dtype=jnp.int32)
    d = jnp.arange(-2, 3, dtype=jnp.int32)
    rel = d[:, None, None] * t + i[None, None, :] - i[None, :, None]
    onehot = (_t5_bucket(rel)[..., None] == jnp.arange(N_BUCKETS, dtype=jnp.int32)).astype(F32)
    tiles = jnp.einsum('dqkn,nh->hdqk', onehot, rel_bias.astype(F32), precision=lax.Precision.HIGHEST)
    return tiles * LOG2E


def _attn_kernel(lam_ref, far_ref, q_ref, k_ref, v_ref, bias_ref, g_ref, o_ref,
                 m_scr, acc_scr, sa_scr, sb_scr, mxa_scr, mxb_scr, *, t, sub, n_iter):
    head = pl.program_id(1)
    qi = pl.program_id(2)
    q = q_ref[0]
    lane = lax.broadcasted_iota(jnp.int32, q.shape, 1)
    zero = jnp.zeros_like(q)
    qs = (jnp.where(lane < HEAD_DIM, q, zero), jnp.where(lane >= HEAD_DIM, q, zero))
    nb = t // LANES
    n_near = min(n_iter, 2 if sub >= 2 else 3)
    first = jnp.maximum(qi - 1, 0) // sub

    m_scr[...] = jnp.full(m_scr.shape, -jnp.inf, F32)
    acc_scr[...] = jnp.zeros(acc_scr.shape, F32)

    def chunk_of(r):
        return (first + r) % n_iter

    def scores(r, s_scr, mx_scr, near):
        j = chunk_of(r)
        for mi in range(2):
            mx = None
            for c in range(sub):
                blk = j * sub + c
                kc = k_ref[0, pl.ds(pl.multiple_of(blk * t, t), t), :]
                s = lax.dot_general(qs[mi], kc, (((1,), (1,)), ((), ())), preferred_element_type=F32)
                if near:
                    s = s + bias_ref[0, jnp.clip(blk - qi, -2, 2) + 2]
                s_scr[mi, :, c * t:(c + 1) * t] = s
                for i in range(nb):
                    piece = s[:, i * LANES:(i + 1) * LANES]
                    mx = piece if mx is None else jnp.maximum(mx, piece)
            mx_scr[mi] = mx

    def accumulate(r, s_scr, mx_scr, near):
        j = chunk_of(r)
        vj = v_ref[0, pl.ds(pl.multiple_of(j * (sub * t), sub * t), sub * t), :]
        vj = jnp.concatenate([vj, jnp.ones_like(vj)], axis=1)
        shift = 0.0 if near else jnp.where(j * sub < qi, far_ref[2 * head], far_ref[2 * head + 1])
        for mi in range(2):
            m_prev = m_scr[mi]
            m_next = jnp.maximum(m_prev, jnp.max(mx_scr[mi], axis=1, keepdims=True) + shift)
            alpha = jnp.exp2(m_prev - m_next)
            m_scr[mi] = m_next
            m_raw = m_next - shift
            pv = None
            for i in range(sub * t // PV_KEYS):
                ps = []
                for c in range(PV_KEYS // LANES):
                    lo = i * PV_KEYS + c * LANES
                    p = jnp.exp2(s_scr[mi, :, lo:lo + LANES] - m_raw)
                    ps.append(p.astype(BF16))
                d = jnp.dot(jnp.concatenate(ps, axis=1), vj[i * PV_KEYS:(i + 1) * PV_KEYS, :],
                            preferred_element_type=F32)
                pv = d if pv is None else pv + d
            acc_scr[mi] = jnp.concatenate([alpha, alpha], axis=1) * acc_scr[mi] + pv

    def pair(i, first_near, second_near):
        scores(2 * i + 1, sb_scr, mxb_scr, second_near[0])
        accumulate(2 * i, sa_scr, mxa_scr, first_near)
        scores(2 * i + 2, sa_scr, mxa_scr, second_near[1])
        accumulate(2 * i + 1, sb_scr, mxb_scr, second_near[0])

    scores(0, sa_scr, mxa_scr, True)
    n_pairs = (n_iter - 1) // 2
    peeled = min(n_pairs, (n_near + 1) // 2)
    for i in range(peeled):
        pair(i, 2 * i < n_near, (2 * i + 1 < n_near, 2 * i + 2 < n_near))

    def body(i, carry):
        pair(i, False, (False, False))
        return carry

    lax.fori_loop(peeled, n_pairs, body, 0)
    if (n_iter - 1) % 2 == 1:
        scores(n_iter - 1, sb_scr, mxb_scr, n_iter - 1 < n_near)
        accumulate(n_iter - 2, sa_scr, mxa_scr, n_iter - 2 < n_near)
        accumulate(n_iter - 1, sb_scr, mxb_scr, n_iter - 1 < n_near)
    else:
        accumulate(n_iter - 1, sa_scr, mxa_scr, n_iter - 1 < n_near)

    o1 = acc_scr[0, :, :V_DIM] / acc_scr[0, :, V_DIM:]
    o2 = acc_scr[1, :, :V_DIM] / acc_scr[1, :, V_DIM:]
    o = o1 - lam_ref[0] * o2
    ms = jnp.mean(o * o, axis=-1, keepdims=True)
    o_ref[0] = (o * lax.rsqrt(ms + RMS_EPS) * g_ref[...]).astype(BF16)


def _attention(q, k, v, bias_tiles, lam, g_scaled, t):
    b, l, _ = q.shape
    nq = l // t
    sub = 2 if nq % 2 == 0 else 1
    kern = functools.partial(_attn_kernel, t=t, sub=sub, n_iter=nq // sub)
    far = jnp.stack([bias_tiles[:, 0, 0, 0], bias_tiles[:, 4, 0, 0]], axis=1).reshape(2 * N_HEADS)
    return pl.pallas_call(
        kern,
        grid=(b, N_HEADS, nq),
        in_specs=[pl.BlockSpec(memory_space=pltpu.SMEM),
                  pl.BlockSpec(memory_space=pltpu.SMEM),
                  pl.BlockSpec((1, t, V_DIM), lambda bi, h, qi: (bi, qi, h)),
                  pl.BlockSpec((1, l, V_DIM), lambda bi, h, qi: (bi, 0, h)),
                  pl.BlockSpec((1, l, V_DIM), lambda bi, h, qi: (bi, 0, h)),
                  pl.BlockSpec((1, 5, t, t), lambda bi, h, qi: (h, 0, 0, 0)),
                  pl.BlockSpec((1, V_DIM), lambda bi, h, qi: (0, 0))],
        out_specs=pl.BlockSpec((1, t, V_DIM), lambda bi, h, qi: (bi, qi, h)),
        out_shape=jax.ShapeDtypeStruct((b, l, W_ATT), BF16),
        scratch_shapes=[pltpu.VMEM((2, t, LANES), F32),
                        pltpu.VMEM((2, t, 2 * V_DIM), F32),
                        pltpu.VMEM((2, t, sub * t), F32),
                        pltpu.VMEM((2, t, sub * t), F32),
                        pltpu.VMEM((2, t, LANES), F32),
                        pltpu.VMEM((2, t, LANES), F32)],
        compiler_params=_cparams(("parallel", "parallel", "arbitrary")),
        name="diff_attention",
    )(lam, far, q, k, v, bias_tiles, g_scaled)


def _ssm_matrices(a_re, a_im, log_dt, b_re, b_im, c_re, c_im, d_skip):
    qn, g, p, hc = SSM_CHUNK, N_GROUPS, SSM_STATE, SSM_GROUP
    n = jnp.arange(qn + 1, dtype=F32)
    pw, bbar, cc = [], [], []
    for d in range(2):
        a = lax.complex(a_re[d].astype(F32), a_im[d].astype(F32))
        dt = jnp.exp(log_dt[d].astype(F32))[:, None]
        adt = a * dt
        a_bar = jnp.exp(adt)
        pw.append(jnp.exp(adt[None] * n[:, None, None]))
        bbar.append(((a_bar - 1.0) / a)[:, :, None]
                    * lax.complex(b_re[d].astype(F32), b_im[d].astype(F32)))
        cc.append(lax.complex(c_re[d].astype(F32), c_im[d].astype(F32)))

    hi = lax.Precision.HIGHEST
    kern = [jnp.einsum('gop,tgp,gpi->tgoi', cc[d], pw[d][:qn], bbar[d], precision=hi).real
            for d in range(2)]
    kf = jnp.transpose(kern[0], (1, 3, 2, 0))
    kb = jnp.transpose(kern[1], (1, 3, 2, 0))
    skip = jnp.eye(hc, dtype=F32)[None] * d_skip.astype(F32).reshape(g, hc)[:, :, None]
    taps = jnp.concatenate([kb[..., :0:-1], (kf[..., 0] + kb[..., 0] + skip)[..., None], kf[..., 1:]], axis=-1)
    m_mat = jnp.stack([taps[..., qn - 1 - s:2 * qn - 1 - s] for s in range(qn)], axis=2)
    m_mat = m_mat.reshape(g, SSM_ROW, SSM_ROW)

    zeros_p = jnp.zeros((g, SSM_ROW, LANES - p), F32)

    def pad_cols(x):
        return jnp.concatenate([x, zeros_p], axis=-1)

    pf = jnp.einsum('sgp,gpi->gisp', pw[0][:qn][::-1], bbar[0]).reshape(g, SSM_ROW, p)
    pb = jnp.einsum('sgp,gpi->gisp', pw[1][:qn], bbar[1]).reshape(g, SSM_ROW, p)
    p_mat = jnp.concatenate([pad_cols(pf.real), pad_cols(pf.imag),
                             pad_cols(pb.real), pad_cols(pb.imag)], axis=-1)

    wf = jnp.einsum('gop,tgp->gpot', cc[0], pw[0][1:qn + 1]).reshape(g, p, SSM_ROW)
    wb = jnp.einsum('gop,tgp->gpot', cc[1], pw[1][1:qn + 1][::-1]).reshape(g, p, SSM_ROW)
    zeros_r = jnp.zeros((g, LANES - p, SSM_ROW), F32)
    r_mat = jnp.concatenate([wf.real, zeros_r, -wf.imag, zeros_r,
                             wb.real, zeros_r, -wb.imag, zeros_r], axis=1)

    zeros_a = jnp.zeros((g, LANES - p), F32)

    def pad_vec(x):
        return jnp.concatenate([x, zeros_a], axis=-1)

    alpha = jnp.stack([pad_vec(pw[0][qn].real), pad_vec(pw[0][qn].imag),
                       pad_vec(pw[1][qn].real), pad_vec(pw[1][qn].imag)], axis=1)
    return p_mat.astype(BF16), m_mat.astype(BF16), r_mat.astype(BF16), alpha


def _s5_kernel(u_ref, p_ref, m_ref, r_ref, a_ref, y_ref, s_scr, x_scr, *, nc, bsz):
    u = jnp.concatenate([u_ref[h] for h in range(SSM_GROUP)], axis=-1).astype(BF16)
    s = jnp.dot(u, p_ref[0], preferred_element_type=F32)
    for part in range(4):
        s_scr[part] = s[:, part * LANES:(part + 1) * LANES]
    al = a_ref[0]
    afr = jnp.broadcast_to(al[0:1], (bsz, LANES))
    afi = jnp.broadcast_to(al[1:2], (bsz, LANES))
    abr = jnp.broadcast_to(al[2:3], (bsz, LANES))
    abi = jnp.broadcast_to(al[3:4], (bsz, LANES))
    fr = fi = br = bi = jnp.zeros((bsz, LANES), F32)
    for c in range(nc):
        rf = pl.ds(c, bsz, stride=nc)
        rb = pl.ds(nc - 1 - c, bsz, stride=nc)
        x_scr[0, rf, :] = fr
        x_scr[1, rf, :] = fi
        x_scr[2, rb, :] = br
        x_scr[3, rb, :] = bi
        sfr = s_scr[0, rf, :]
        sfi = s_scr[1, rf, :]
        sbr = s_scr[2, rb, :]
        sbi = s_scr[3, rb, :]
        fr, fi = afr * fr - afi * fi + sfr, afr * fi + afi * fr + sfi
        br, bi = abr * br - abi * bi + sbr, abr * bi + abi * br + sbi
    y = jnp.dot(u, m_ref[0], preferred_element_type=F32)
    x_in = jnp.concatenate([x_scr[part] for part in range(4)], axis=-1).astype(BF16)
    y = y + jnp.dot(x_in, r_ref[0], preferred_element_type=F32)
    for h in range(SSM_GROUP):
        y_ref[h] = y[:, h * SSM_CHUNK:(h + 1) * SSM_CHUNK]


def _s5(u_t, p_mat, m_mat, r_mat, alpha, nc, bsz):
    rows = u_t.shape[1]
    kern = functools.partial(_s5_kernel, nc=nc, bsz=bsz)
    seq = pl.BlockSpec((SSM_GROUP, rows, SSM_CHUNK), lambda i: (i, 0, 0))
    return pl.pallas_call(
        kern,
        grid=(N_GROUPS,),
        in_specs=[seq,
                  pl.BlockSpec((1, SSM_ROW, 4 * LANES), lambda i: (i, 0, 0)),
                  pl.BlockSpec((1, SSM_ROW, SSM_ROW), lambda i: (i, 0, 0)),
                  pl.BlockSpec((1, 4 * LANES, SSM_ROW), lambda i: (i, 0, 0)),
                  pl.BlockSpec((1, 4, LANES), lambda i: (i, 0, 0))],
        out_specs=seq,
        out_shape=jax.ShapeDtypeStruct(u_t.shape, F32),
        scratch_shapes=[pltpu.VMEM((4, rows, LANES), F32), pltpu.VMEM((4, rows, LANES), F32)],
        compiler_params=_cparams(("parallel",)),
        name="s5_scan",
    )(u_t, p_mat, m_mat, r_mat, alpha)


def _post_mix_kernel(x_ref, att_ref, y_ref, wglu_ref, bglu_ref, gs_ref, wout_ref, g2_ref,
                     wrh_ref, wrl_ref, br_ref, x2_ref, h2_ref, ids_ref, gates_ref):
    y = jnp.concatenate([y_ref[:, j, :].T for j in range(y_ref.shape[1])], axis=0)
    y = 0.5 * y * (1.0 + jnp.tanh(math.sqrt(2.0 / math.pi) * (y + 0.044715 * (y * y * y))))
    z = jnp.dot(y.astype(BF16), wglu_ref[...], preferred_element_type=F32) + bglu_ref[...]
    y = y * (1.0 / (1.0 + jnp.exp(-z)))
    ms = jnp.mean(y * y, axis=-1, keepdims=True)
    ssm = (y * lax.rsqrt(ms + RMS_EPS) * gs_ref[...]).astype(BF16)
    mix = jnp.dot(att_ref[...], wout_ref[0:W_ATT, :], preferred_element_type=F32)
    mix = mix + jnp.dot(ssm, wout_ref[W_ATT:, :], preferred_element_type=F32)
    x2 = x_ref[...] + mix
    x2_ref[...] = x2
    ms2 = jnp.mean(x2 * x2, axis=-1, keepdims=True)
    h2 = x2 * lax.rsqrt(ms2 + RMS_EPS) * g2_ref[...]
    h2_ref[...] = _pack_bf16_pair(h2[:, :D_MODEL // 2], h2[:, D_MODEL // 2:])
    h_hi = h2.astype(BF16)
    h_lo = (h2 - h_hi.astype(F32)).astype(BF16)
    logits = (jnp.dot(h_hi, wrh_ref[...], preferred_element_type=F32)
              + jnp.dot(h_lo, wrh_ref[...], preferred_element_type=F32)
              + jnp.dot(h_hi, wrl_ref[...], preferred_element_type=F32)) + br_ref[...]
    lane = lax.broadcasted_iota(jnp.int32, logits.shape, 1).astype(F32)
    neg = jnp.float32(-jnp.inf)
    cur = jnp.where(lane < N_EXPERTS, logits, neg)
    ids = jnp.zeros(logits.shape, F32)
    vals = jnp.zeros(logits.shape, F32)
    top = None
    den = None
    for kk in range(TOP_K):
        mx = jnp.max(cur, axis=1, keepdims=True)
        idx = jnp.min(jnp.where(cur == mx, lane, float(LANES)), axis=1, keepdims=True)
        if kk == 0:
            top = mx
        e = jnp.exp(mx - top)
        den = e if den is None else den + e
        ids = jnp.where(lane == kk, idx, ids)
        vals = jnp.where(lane == kk, e, vals)
        cur = jnp.where(lane == idx, neg, cur)
    ids_ref[...] = ids.astype(jnp.int32)
    gates_ref[...] = vals / den


def _post_mix(x2d, att, yssm, wglu, bglu, gs, wout, g2, wr_hi, wr_lo, br):
    t = x2d.shape[0]
    tm = min(ROW_TILE, t)
    row = lambda i: (i, 0)
    const = lambda i: (0, 0)
    return pl.pallas_call(
        _post_mix_kernel,
        grid=(t // tm,),
        in_specs=[pl.BlockSpec((tm, D_MODEL), row),
                  pl.BlockSpec((tm, W_ATT), row),
                  pl.BlockSpec((W_SSM, tm // SSM_CHUNK, SSM_CHUNK), lambda i: (0, i, 0)),
                  pl.BlockSpec((W_SSM, W_SSM), const),
                  pl.BlockSpec((1, W_SSM), const),
                  pl.BlockSpec((1, W_SSM), const),
                  pl.BlockSpec((D_MODEL, D_MODEL), const),
                  pl.BlockSpec((1, D_MODEL), const),
                  pl.BlockSpec((D_MODEL, LANES), const),
                  pl.BlockSpec((D_MODEL, LANES), const),
                  pl.BlockSpec((1, LANES), const)],
        out_specs=[pl.BlockSpec((tm, D_MODEL), row),
                   pl.BlockSpec((tm, D_MODEL // 2), row),
                   pl.BlockSpec((tm, LANES), row),
                   pl.BlockSpec((tm, LANES), row)],
        out_shape=[jax.ShapeDtypeStruct((t, D_MODEL), F32),
                   jax.ShapeDtypeStruct((t, D_MODEL // 2), PACKED),
                   jax.ShapeDtypeStruct((t, LANES), jnp.int32),
                   jax.ShapeDtypeStruct((t, LANES), F32)],
        compiler_params=_cparams(("parallel",)),
        name="post_mix",
    )(x2d, att, yssm, wglu, bglu, gs, wout, g2, wr_hi, wr_lo, br)


def _sc_gather_rows(x, idx):
    n = idx.shape[0]
    d = x.shape[1]
    row_bytes = d * x.dtype.itemsize
    step_rows = SC_STEP_BYTES // row_bytes
    steps = n // step_rows
    workers = SC_CORES * SC_SUBCORES
    assert steps % (2 * workers) == 0, (n, steps)
    per_worker = steps // workers
    idx_rows = jnp.pad(idx.reshape(steps, step_rows), ((0, 0), (0, LANES - step_rows)))
    mesh = plsc.VectorSubcoreMesh(core_axis_name="core", subcore_axis_name="subcore")

    cost = pl.CostEstimate(flops=0, transcendentals=0,
                           bytes_accessed=2 * n * row_bytes + idx_rows.size * idx_rows.dtype.itemsize)

    @pl.kernel(out_type=jax.ShapeDtypeStruct((n, d), x.dtype), mesh=mesh, cost_estimate=cost,
               name="sc_gather_rows",
               scratch_types=[pltpu.VMEM((2, 1, LANES), jnp.int32),
                              pltpu.VMEM((2, step_rows, d), x.dtype),
                              pltpu.SemaphoreType.DMA((2,)),
                              pltpu.SemaphoreType.DMA((2,))])
    def gather(x_hbm, i_hbm, o_hbm, idx_v, buf, sem_in, sem_out):
        base = (lax.axis_index("core") * SC_SUBCORES + lax.axis_index("subcore")) * per_worker

        def gather_copy(slot):
            return pltpu.make_async_copy(x_hbm.at[idx_v.at[slot, 0, pl.ds(0, step_rows)]], buf.at[slot],
                                         sem_in.at[slot])

        def out_copy(step, slot):
            return pltpu.make_async_copy(buf.at[slot], o_hbm.at[pl.ds(step * step_rows, step_rows), :],
                                         sem_out.at[slot])

        @pl.loop(0, per_worker // 2)
        def _(j):
            for slot in range(2):
                step = base + 2 * j + slot

                @pl.when(j > 0)
                def _():
                    out_copy(step, slot).wait()

                pltpu.sync_copy(i_hbm.at[pl.ds(step, 1), :], idx_v.at[slot])
                gather_copy(slot).start()
            for slot in range(2):
                gather_copy(slot).wait()
                out_copy(base + 2 * j + slot, slot).start()

        for slot in range(2):
            out_copy(base, slot).wait()

    return gather(x, idx_rows)


def _expert_kernel(be_ref, x_ref, w1_ref, b1_ref, w2_ref, b2_ref, y_ref, w1_bf, w2_bf):
    i = pl.program_id(0)

    @pl.when(jnp.logical_or(i == 0, be_ref[i] != be_ref[jnp.maximum(i - 1, 0)]))
    def _():
        w1_bf[...] = w1_ref[0].astype(BF16)
        w2_bf[...] = w2_ref[0].astype(BF16)

    x = jnp.concatenate(_unpack_bf16_pair(x_ref[...]), axis=1).astype(BF16)
    hdn = jnp.dot(x, w1_bf[...], preferred_element_type=F32) + b1_ref[0]
    gate = jnp.minimum(hdn[:, :D_FF], SWIGLU_LIMIT)
    lin = jnp.clip(hdn[:, D_FF:], -SWIGLU_LIMIT, SWIGLU_LIMIT)
    act = gate * (1.0 / (1.0 + jnp.exp(-SWIGLU_ALPHA * gate))) * (lin + 1.0)
    y = jnp.dot(act.astype(BF16), w2_bf[...], preferred_element_type=F32) + b2_ref[0]
    y_ref[...] = _pack_bf16_pair(y[:, :D_MODEL // 2], y[:, D_MODEL // 2:])


def _experts(block_e, x_pad, w1, b1, w2, b2):
    n_pad = x_pad.shape[0]
    rows = EXPERT_ROWS
    grid_spec = pltpu.PrefetchScalarGridSpec(
        num_scalar_prefetch=1,
        grid=(n_pad // rows,),
        in_specs=[pl.BlockSpec((rows, D_MODEL // 2), lambda i, be: (i, 0)),
                  pl.BlockSpec((1, D_MODEL, 2 * D_FF), lambda i, be: (be[i], 0, 0)),
                  pl.BlockSpec((1, 1, 2 * D_FF), lambda i, be: (be[i], 0, 0)),
                  pl.BlockSpec((1, D_FF, D_MODEL), lambda i, be: (be[i], 0, 0)),
                  pl.BlockSpec((1, 1, D_MODEL), lambda i, be: (be[i], 0, 0))],
        out_specs=pl.BlockSpec((rows, D_MODEL // 2), lambda i, be: (i, 0)),
        scratch_shapes=[pltpu.VMEM((D_MODEL, 2 * D_FF), BF16), pltpu.VMEM((D_FF, D_MODEL), BF16)],
    )
    return pl.pallas_call(
        _expert_kernel,
        grid_spec=grid_spec,
        out_shape=jax.ShapeDtypeStruct((n_pad, D_MODEL // 2), PACKED),
        compiler_params=_cparams(("arbitrary",)),
        name="moe_experts",
    )(block_e, x_pad, w1, b1, w2, b2)


def _combine_kernel(x2_ref, gates_ref, gf_ref, y_ref, o_ref):
    gates = gates_ref[...]
    lo = x2_ref[:, :D_MODEL // 2]
    hi = x2_ref[:, D_MODEL // 2:]
    for kk in range(TOP_K):
        y_lo, y_hi = _unpack_bf16_pair(y_ref[kk])
        lo = lo + gates[:, kk:kk + 1] * y_lo
        hi = hi + gates[:, kk:kk + 1] * y_hi
    y = jnp.concatenate([lo, hi], axis=1)
    ms = jnp.mean(y * y, axis=-1, keepdims=True)
    o_ref[...] = y * lax.rsqrt(ms + RMS_EPS) * gf_ref[...]


def _combine(x2, gates, gf, y_sel):
    t = x2.shape[0]
    rows = min(COMBINE_ROWS, t)
    row = lambda i: (i, 0)
    return pl.pallas_call(
        _combine_kernel,
        grid=(t // rows,),
        in_specs=[pl.BlockSpec((rows, D_MODEL), row),
                  pl.BlockSpec((rows, LANES), row),
                  pl.BlockSpec((1, D_MODEL), lambda i: (0, 0)),
                  pl.BlockSpec((TOP_K, rows, D_MODEL // 2), lambda i: (0, i, 0))],
        out_specs=pl.BlockSpec((rows, D_MODEL), row),
        out_shape=jax.ShapeDtypeStruct((t, D_MODEL), F32),
        compiler_params=_cparams(("parallel",)),
        name="moe_combine",
    )(x2, gates, gf, y_sel)


def _dispatch_plan(top_e, n_tok):
    n = n_tok * TOP_K
    blk = EXPERT_ROWS
    flat_e = top_e.reshape(n)
    order = jnp.argsort(flat_e).astype(jnp.int32)
    rank = jnp.argsort(order).astype(jnp.int32)
    st = order // TOP_K
    experts = jnp.arange(N_EXPERTS, dtype=jnp.int32)
    counts = jnp.sum((flat_e[:, None] == experts).astype(jnp.int32), axis=0)
    padded = (counts + blk - 1) // blk * blk
    start = jnp.cumsum(counts) - counts
    pend = jnp.cumsum(padded)
    pstart = pend - padded
    pos = (pstart - start)[flat_e] + rank
    n_blocks = n // blk + N_EXPERTS
    block_start = jnp.arange(n_blocks, dtype=jnp.int32) * blk
    block_e = jnp.minimum(jnp.sum((pend[None, :] <= block_start[:, None]).astype(jnp.int32), axis=1),
                          N_EXPERTS - 1)
    slot = jnp.arange(n_blocks * blk, dtype=jnp.int32)
    slot_e = jnp.repeat(block_e, blk)
    off = slot - pstart[slot_e]
    src = jnp.clip(start[slot_e] + off, 0, n - 1)
    tok_pad = jnp.where(off < counts[slot_e], st[src], 0)
    return tok_pad, pos, block_e


def _trunk_front(x, prm, gathered=None):
    bsz, l, _ = x.shape
    t = bsz * l
    x2d = x.reshape(t, D_MODEL)
    q, k, v, u_t = _in_proj(x2d, prm['norm1_g'], prm['w_in'])

    t_att = min(ATT_TILE, l)
    att = _attention(q.reshape(bsz, l, W_ATT), k.reshape(bsz, l, W_ATT), v.reshape(bsz, l, W_ATT),
                     prm['bias_tiles'][t_att], prm['lam'], prm['subln_g'], t_att)

    yssm = _s5(u_t, prm['ssm_p'], prm['ssm_m'], prm['ssm_r'], prm['ssm_alpha'], l // SSM_CHUNK, bsz)
    if gathered is not None:
        yssm, gathered = lax.optimization_barrier((yssm, gathered))

    x2, h2, ids, gates = _post_mix(x2d, att.reshape(t, W_ATT), yssm, prm['w_glu'], prm['b_glu'],
                                   prm['ssm_norm_g'], prm['w_out'], prm['norm2_g'],
                                   prm['w_router_hi'], prm['w_router_lo'], prm['b_router'])

    tok_pad, pos, block_e = _dispatch_plan(ids[:, :TOP_K], t)
    return {'shape': x.shape, 'x2': x2, 'h2': h2, 'gates': gates,
            'tok_pad': tok_pad, 'pos': pos, 'block_e': block_e, 'gathered': gathered}


def _trunk_back(st, x_pad, prm):
    t = st['x2'].shape[0]
    y_pad = _experts(st['block_e'], x_pad, prm['w_moe1'], prm['b_moe1'], prm['w_moe2'], prm['b_moe2'])
    pos_by_k = st['pos'].reshape(t, TOP_K).T.reshape(TOP_K * t)
    y_sel = _sc_gather_rows(y_pad, pos_by_k).reshape(TOP_K, t, D_MODEL // 2)
    out = _combine(st['x2'], st['gates'], prm['normf_g'], y_sel)
    return out.reshape(st['shape'])


def _prepare(seq_lens, rel_bias, norm1_g, w_in, lambda_q1, lambda_k1, lambda_q2, lambda_k2, subln_g,
             ssm_A_re, ssm_A_im, ssm_log_dt, ssm_B_re, ssm_B_im, ssm_C_re, ssm_C_im, ssm_D,
             w_glu, b_glu, ssm_norm_g, w_out, norm2_g, w_router, b_router,
             w_moe1, b_moe1, w_moe2, b_moe2, normf_g):
    layer = 0
    lambda_init = 0.8 - 0.6 * math.exp(-0.3 * layer)
    lam = (jnp.exp(jnp.sum(lambda_q1[layer].astype(F32) * lambda_k1[layer].astype(F32)))
           - jnp.exp(jnp.sum(lambda_q2[layer].astype(F32) * lambda_k2[layer].astype(F32))) + lambda_init)
    p_mat, m_mat, r_mat, alpha = _ssm_matrices(
        ssm_A_re[layer], ssm_A_im[layer], ssm_log_dt[layer], ssm_B_re[layer], ssm_B_im[layer],
        ssm_C_re[layer], ssm_C_im[layer], ssm_D[layer])
    pad_e = LANES - N_EXPERTS
    w_r = jnp.pad(w_router[layer].astype(F32), ((0, 0), (0, pad_e)))
    w_r_hi = w_r.astype(BF16)
    return {
        'norm1_g': norm1_g[layer].reshape(1, D_MODEL).astype(F32),
        'w_in': w_in[layer].astype(BF16),
        'lam': lam.reshape(1).astype(F32),
        'subln_g': (subln_g[layer].astype(F32) * (1.0 - lambda_init)).reshape(1, V_DIM),
        'bias_tiles': {t: _bias_tiles(rel_bias, t) for t in sorted({min(ATT_TILE, l) for l in seq_lens})},
        'ssm_p': p_mat, 'ssm_m': m_mat, 'ssm_r': r_mat, 'ssm_alpha': alpha,
        'w_glu': w_glu[layer].astype(BF16),
        'b_glu': b_glu[layer].reshape(1, W_SSM).astype(F32),
        'ssm_norm_g': ssm_norm_g[layer].reshape(1, W_SSM).astype(F32),
        'w_out': w_out[layer].astype(BF16),
        'norm2_g': norm2_g[layer].reshape(1, D_MODEL).astype(F32),
        'w_router_hi': w_r_hi,
        'w_router_lo': (w_r - w_r_hi.astype(F32)).astype(BF16),
        'b_router': jnp.pad(b_router[layer].astype(F32), (0, pad_e)).reshape(1, LANES),
        'w_moe1': w_moe1[layer].astype(F32),
        'b_moe1': b_moe1[layer].reshape(N_EXPERTS, 1, 2 * D_FF).astype(F32),
        'w_moe2': w_moe2[layer].astype(F32),
        'b_moe2': b_moe2[layer].reshape(N_EXPERTS, 1, D_MODEL).astype(F32),
        'normf_g': normf_g.reshape(1, D_MODEL).astype(F32),
    }


def kernel(x_prompt, x_sample, rel_bias, norm1_g, w_in, lambda_q1, lambda_k1, lambda_q2, lambda_k2, subln_g, ssm_A_re, ssm_A_im, ssm_log_dt, ssm_B_re, ssm_B_im, ssm_C_re, ssm_C_im, ssm_D, w_glu, b_glu, ssm_norm_g, w_out, norm2_g, w_router, b_router, w_moe1, b_moe1, w_moe2, b_moe2, normf_g):
    prm = _prepare((x_prompt.shape[1], x_sample.shape[1]), rel_bias, norm1_g, w_in, lambda_q1,
                   lambda_k1, lambda_q2, lambda_k2, subln_g, ssm_A_re, ssm_A_im, ssm_log_dt,
                   ssm_B_re, ssm_B_im, ssm_C_re, ssm_C_im, ssm_D, w_glu, b_glu, ssm_norm_g, w_out,
                   norm2_g, w_router, b_router, w_moe1, b_moe1, w_moe2, b_moe2, normf_g)
    first = _trunk_front(x_prompt, prm)
    x_sample, tok_pad = lax.optimization_barrier((x_sample, first['tok_pad']))
    x_pad_first = _sc_gather_rows(first['h2'], tok_pad)
    second = _trunk_front(x_sample, prm, gathered=x_pad_first)
    x_pad_second = _sc_gather_rows(second['h2'], second['tok_pad'])
    return (_trunk_back(first, second['gathered'], prm), _trunk_back(second, x_pad_second, prm))
```

```python
import functools
import math

import jax
import jax.numpy as jnp
from jax import lax
from jax.experimental import pallas as pl
from jax.experimental.pallas import tpu as pltpu
from jax.experimental.pallas import tpu_sc as plsc

F32 = jnp.float32
BF16 = jnp.bfloat16
PACKED = jnp.uint32

D_MODEL = 1024
W_ATT = 512
W_SSM = 512
HEAD_DIM = 64
N_HEADS = 4
V_DIM = 2 * HEAD_DIM
SSM_GROUP = 16
N_GROUPS = W_SSM // SSM_GROUP
SSM_STATE = 64
IN_WIDTH = 3 * W_ATT + W_SSM
N_BUCKETS = 32
MAX_DISTANCE = 128
N_EXPERTS = 32
TOP_K = 4
D_FF = D_MODEL
SWIGLU_ALPHA = 1.702
SWIGLU_LIMIT = 7.0
RMS_EPS = 1e-6
ATT_SCALE = HEAD_DIM ** -0.5
LOG2E = math.log2(math.e)

LANES = 128
PV_KEYS = 256
ATT_TILE = 512
SSM_CHUNK = LANES
SSM_ROW = SSM_CHUNK * SSM_GROUP
ROW_TILE = 1024
EXPERT_ROWS = 512
SC_STEP_BYTES = 128 * 1024
SC_CORES = 2
SC_SUBCORES = 16
COMBINE_ROWS = 512
VMEM_LIMIT = 56 * 1024 * 1024


def _pack_bf16_pair(lo, hi):
    lo_bits = lax.bitcast_convert_type(lo.astype(BF16).astype(F32), PACKED)
    hi_bits = lax.bitcast_convert_type(hi.astype(BF16).astype(F32), PACKED)
    return (hi_bits & jnp.uint32(0xFFFF0000)) | (lo_bits >> 16)


def _unpack_bf16_pair(words):
    lo = lax.bitcast_convert_type(words << 16, F32)
    hi = lax.bitcast_convert_type(words & jnp.uint32(0xFFFF0000), F32)
    return lo, hi


def _cparams(sem):
    return pltpu.CompilerParams(dimension_semantics=sem, vmem_limit_bytes=VMEM_LIMIT)


def _in_proj_kernel(x_ref, g_ref, w_ref, q_ref, k_ref, v_ref, ut_ref):
    x = x_ref[...]
    ms = jnp.mean(x * x, axis=-1, keepdims=True)
    h = (x * lax.rsqrt(ms + RMS_EPS) * g_ref[...]).astype(BF16)
    proj = jnp.dot(h, w_ref[...], preferred_element_type=F32)
    q_ref[...] = (proj[:, 0:W_ATT] * (ATT_SCALE * LOG2E)).astype(BF16)
    k_ref[...] = proj[:, W_ATT:2 * W_ATT].astype(BF16)
    v_ref[...] = proj[:, 2 * W_ATT:3 * W_ATT].astype(BF16)
    ut = proj[:, 3 * W_ATT:].T
    for j in range(ut_ref.shape[1]):
        ut_ref[:, j, :] = ut[:, j * SSM_CHUNK:(j + 1) * SSM_CHUNK]


def _in_proj(x2d, g, w_bf16):
    t = x2d.shape[0]
    tm = min(ROW_TILE, t)
    out = jax.ShapeDtypeStruct((t, W_ATT), BF16)
    row = lambda i: (i, 0)
    return pl.pallas_call(
        _in_proj_kernel,
        grid=(t // tm,),
        in_specs=[pl.BlockSpec((tm, D_MODEL), row),
                  pl.BlockSpec((1, D_MODEL), lambda i: (0, 0)),
                  pl.BlockSpec((D_MODEL, IN_WIDTH), lambda i: (0, 0))],
        out_specs=[pl.BlockSpec((tm, W_ATT), row)] * 3
        + [pl.BlockSpec((W_SSM, tm // SSM_CHUNK, SSM_CHUNK), lambda i: (0, i, 0))],
        out_shape=[out] * 3 + [jax.ShapeDtypeStruct((W_SSM, t // SSM_CHUNK, SSM_CHUNK), F32)],
        compiler_params=_cparams(("parallel",)),
        name="in_proj",
    )(x2d, g, w_bf16)


def _t5_bucket(rel):
    half = N_BUCKETS // 2
    max_exact = half // 2
    ret = jnp.where(rel > 0, half, 0).astype(jnp.int32)
    n = jnp.abs(rel)
    nf = jnp.maximum(n, 1).astype(F32)
    large = max_exact + (jnp.log(nf / max_exact) / math.log(MAX_DISTANCE / max_exact)
                         * (half - max_exact)).astype(jnp.int32)
    large = jnp.minimum(large, half - 1)
    return ret + jnp.where(n < max_exact, n, large)


def _bias_tiles(rel_bias, t):
    i = jnp.arange(t, dtype=jnp.int32)
    d = jnp.arange(-2, 3, dtype=jnp.int32)
    rel = d[:, None, None] * t + i[None, None, :] - i[None, :, None]
    onehot = (_t5_bucket(rel)[..., None] == jnp.arange(N_BUCKETS, dtype=jnp.int32)).astype(F32)
    tiles = jnp.einsum('dqkn,nh->hdqk', onehot, rel_bias.astype(F32), precision=lax.Precision.HIGHEST)
    return tiles * LOG2E


def _attn_kernel(lam_ref, far_ref, q_ref, k_ref, v_ref, bias_ref, g_ref, o_ref,
                 m_scr, acc_scr, sa_scr, sb_scr, mxa_scr, mxb_scr, *, t, sub, n_iter):
    head = pl.program_id(1)
    qi = pl.program_id(2)
    q = q_ref[0]
    lane = lax.broadcasted_iota(jnp.int32, q.shape, 1)
    zero = jnp.zeros_like(q)
    qs = (jnp.where(lane < HEAD_DIM, q, zero), jnp.where(lane >= HEAD_DIM, q, zero))
    nb = t // LANES
    n_near = min(n_iter, 2 if sub >= 2 else 3)
    first = jnp.maximum(qi - 1, 0) // sub

    m_scr[...] = jnp.full(m_scr.shape, -jnp.inf, F32)
    acc_scr[...] = jnp.zeros(acc_scr.shape, F32)

    def chunk_of(r):
        return (first + r) % n_iter

    def scores(r, s_scr, mx_scr, near):
        j = chunk_of(r)
        for mi in range(2):
            mx = None
            for c in range(sub):
                blk = j * sub + c
                kc = k_ref[0, pl.ds(pl.multiple_of(blk * t, t), t), :]
                s = lax.dot_general(qs[mi], kc, (((1,), (1,)), ((), ())), preferred_element_type=F32)
                if near:
                    s = s + bias_ref[0, jnp.clip(blk - qi, -2, 2) + 2]
                s_scr[mi, :, c * t:(c + 1) * t] = s
                for i in range(nb):
                    piece = s[:, i * LANES:(i + 1) * LANES]
                    mx = piece if mx is None else jnp.maximum(mx, piece)
            mx_scr[mi] = mx

    def accumulate(r, s_scr, mx_scr, near):
        j = chunk_of(r)
        vj = v_ref[0, pl.ds(pl.multiple_of(j * (sub * t), sub * t), sub * t), :]
        vj = jnp.concatenate([vj, jnp.ones_like(vj)], axis=1)
        shift = 0.0 if near else jnp.where(j * sub < qi, far_ref[2 * head], far_ref[2 * head + 1])
        for mi in range(2):
            m_prev = m_scr[mi]
            m_next = jnp.maximum(m_prev, jnp.max(mx_scr[mi], axis=1, keepdims=True) + shift)
            alpha = jnp.exp2(m_prev - m_next)
            m_scr[mi] = m_next
            m_raw = m_next - shift
            pv = None
            for i in range(sub * t // PV_KEYS):
                ps = []
                for c in range(PV_KEYS // LANES):
                    lo = i * PV_KEYS + c * LANES
                    p = jnp.exp2(s_scr[mi, :, lo:lo + LANES] - m_raw)
                    ps.append(p.astype(BF16))
                d = jnp.dot(jnp.concatenate(ps, axis=1), vj[i * PV_KEYS:(i + 1) * PV_KEYS, :],
                            preferred_element_type=F32)
                pv = d if pv is None else pv + d
            acc_scr[mi] = jnp.concatenate([alpha, alpha], axis=1) * acc_scr[mi] + pv

    def pair(i, first_near, second_near):
        scores(2 * i + 1, sb_scr, mxb_scr, second_near[0])
        accumulate(2 * i, sa_scr, mxa_scr, first_near)
        scores(2 * i + 2, sa_scr, mxa_scr, second_near[1])
        accumulate(2 * i + 1, sb_scr, mxb_scr, second_near[0])

    scores(0, sa_scr, mxa_scr, True)
    n_pairs = (n_iter - 1) // 2
    peeled = min(n_pairs, (n_near + 1) // 2)
    for i in range(peeled):
        pair(i, 2 * i < n_near, (2 * i + 1 < n_near, 2 * i + 2 < n_near))

    def body(i, carry):
        pair(i, False, (False, False))
        return carry

    lax.fori_loop(peeled, n_pairs, body, 0)
    if (n_iter - 1) % 2 == 1:
        scores(n_iter - 1, sb_scr, mxb_scr, n_iter - 1 < n_near)
        accumulate(n_iter - 2, sa_scr, mxa_scr, n_iter - 2 < n_near)
        accumulate(n_iter - 1, sb_scr, mxb_scr, n_iter - 1 < n_near)
    else:
        accumulate(n_iter - 1, sa_scr, mxa_scr, n_iter - 1 < n_near)

    o1 = acc_scr[0, :, :V_DIM] / acc_scr[0, :, V_DIM:]
    o2 = acc_scr[1, :, :V_DIM] / acc_scr[1, :, V_DIM:]
    o = o1 - lam_ref[0] * o2
    ms = jnp.mean(o * o, axis=-1, keepdims=True)
    o_ref[0] = (o * lax.rsqrt(ms + RMS_EPS) * g_ref[...]).astype(BF16)


def _attention(q, k, v, bias_tiles, lam, g_scaled, t):
    b, l, _ = q.shape
    nq = l // t
    sub = 2 if nq % 2 == 0 else 1
    kern = functools.partial(_attn_kernel, t=t, sub=sub, n_iter=nq // sub)
    far = jnp.stack([bias_tiles[:, 0, 0, 0], bias_tiles[:, 4, 0, 0]], axis=1).reshape(2 * N_HEADS)
    return pl.pallas_call(
        kern,
        grid=(b, N_HEADS, nq),
        in_specs=[pl.BlockSpec(memory_space=pltpu.SMEM),
                  pl.BlockSpec(memory_space=pltpu.SMEM),
                  pl.BlockSpec((1, t, V_DIM), lambda bi, h, qi: (bi, qi, h)),
                  pl.BlockSpec((1, l, V_DIM), lambda bi, h, qi: (bi, 0, h)),
                  pl.BlockSpec((1, l, V_DIM), lambda bi, h, qi: (bi, 0, h)),
                  pl.BlockSpec((1, 5, t, t), lambda bi, h, qi: (h, 0, 0, 0)),
                  pl.BlockSpec((1, V_DIM), lambda bi, h, qi: (0, 0))],
        out_specs=pl.BlockSpec((1, t, V_DIM), lambda bi, h, qi: (bi, qi, h)),
        out_shape=jax.ShapeDtypeStruct((b, l, W_ATT), BF16),
        scratch_shapes=[pltpu.VMEM((2, t, LANES), F32),
                        pltpu.VMEM((2, t, 2 * V_DIM), F32),
                        pltpu.VMEM((2, t, sub * t), F32),
                        pltpu.VMEM((2, t, sub * t), F32),
                        pltpu.VMEM((2, t, LANES), F32),
                        pltpu.VMEM((2, t, LANES), F32)],
        compiler_params=_cparams(("parallel", "parallel", "arbitrary")),
        name="diff_attention",
    )(lam, far, q, k, v, bias_tiles, g_scaled)


def _ssm_matrices(a_re, a_im, log_dt, b_re, b_im, c_re, c_im, d_skip):
    qn, g, p, hc = SSM_CHUNK, N_GROUPS, SSM_STATE, SSM_GROUP
    n = jnp.arange(qn + 1, dtype=F32)
    pw, bbar, cc = [], [], []
    for d in range(2):
        a = lax.complex(a_re[d].astype(F32), a_im[d].astype(F32))
        dt = jnp.exp(log_dt[d].astype(F32))[:, None]
        adt = a * dt
        a_bar = jnp.exp(adt)
        pw.append(jnp.exp(adt[None] * n[:, None, None]))
        bbar.append(((a_bar - 1.0) / a)[:, :, None]
                    * lax.complex(b_re[d].astype(F32), b_im[d].astype(F32)))
        cc.append(lax.complex(c_re[d].astype(F32), c_im[d].astype(F32)))

    hi = lax.Precision.HIGHEST
    kern = [jnp.einsum('gop,tgp,gpi->tgoi', cc[d], pw[d][:qn], bbar[d], precision=hi).real
            for d in range(2)]
    kf = jnp.transpose(kern[0], (1, 3, 2, 0))
    kb = jnp.transpose(kern[1], (1, 3, 2, 0))
    skip = jnp.eye(hc, dtype=F32)[None] * d_skip.astype(F32).reshape(g, hc)[:, :, None]
    taps = jnp.concatenate([kb[..., :0:-1], (kf[..., 0] + kb[..., 0] + skip)[..., None], kf[..., 1:]], axis=-1)
    m_mat = jnp.stack([taps[..., qn - 1 - s:2 * qn - 1 - s] for s in range(qn)], axis=2)
    m_mat = m_mat.reshape(g, SSM_ROW, SSM_ROW)

    zeros_p = jnp.zeros((g, SSM_ROW, LANES - p), F32)

    def pad_cols(x):
        return jnp.concatenate([x, zeros_p], axis=-1)

    pf = jnp.einsum('sgp,gpi->gisp', pw[0][:qn][::-1], bbar[0]).reshape(g, SSM_ROW, p)
    pb = jnp.einsum('sgp,gpi->gisp', pw[1][:qn], bbar[1]).reshape(g, SSM_ROW, p)
    p_mat = jnp.concatenate([pad_cols(pf.real), pad_cols(pf.imag),
                             pad_cols(pb.real), pad_cols(pb.imag)], axis=-1)

    wf = jnp.einsum('gop,tgp->gpot', cc[0], pw[0][1:qn + 1]).reshape(g, p, SSM_ROW)
    wb = jnp.einsum('gop,tgp->gpot', cc[1], pw[1][1:qn + 1][::-1]).reshape(g, p, SSM_ROW)
    zeros_r = jnp.zeros((g, LANES - p, SSM_ROW), F32)
    r_mat = jnp.concatenate([wf.real, zeros_r, -wf.imag, zeros_r,
                             wb.real, zeros_r, -wb.imag, zeros_r], axis=1)

    zeros_a = jnp.zeros((g, LANES - p), F32)

    def pad_vec(x):
        return jnp.concatenate([x, zeros_a], axis=-1)

    alpha = jnp.stack([pad_vec(pw[0][qn].real), pad_vec(pw[0][qn].imag),
                       pad_vec(pw[1][qn].real), pad_vec(pw[1][qn].imag)], axis=1)
    return p_mat.astype(BF16), m_mat.astype(BF16), r_mat.astype(BF16), alpha


def _s5_kernel(u_ref, p_ref, m_ref, r_ref, a_ref, y_ref, s_scr, x_scr, *, nc, bsz):
    u = jnp.concatenate([u_ref[h] for h in range(SSM_GROUP)], axis=-1).astype(BF16)
    s = jnp.dot(u, p_ref[0], preferred_element_type=F32)
    for part in range(4):
        s_scr[part] = s[:, part * LANES:(part + 1) * LANES]
    al = a_ref[0]
    afr = jnp.broadcast_to(al[0:1], (bsz, LANES))
    afi = jnp.broadcast_to(al[1:2], (bsz, LANES))
    abr = jnp.broadcast_to(al[2:3], (bsz, LANES))
    abi = jnp.broadcast_to(al[3:4], (bsz, LANES))
    fr = fi = br = bi = jnp.zeros((bsz, LANES), F32)
    for c in range(nc):
        rf = pl.ds(c, bsz, stride=nc)
        rb = pl.ds(nc - 1 - c, bsz, stride=nc)
        x_scr[0, rf, :] = fr
        x_scr[1, rf, :] = fi
        x_scr[2, rb, :] = br
        x_scr[3, rb, :] = bi
        sfr = s_scr[0, rf, :]
        sfi = s_scr[1, rf, :]
        sbr = s_scr[2, rb, :]
        sbi = s_scr[3, rb, :]
        fr, fi = afr * fr - afi * fi + sfr, afr * fi + afi * fr + sfi
        br, bi = abr * br - abi * bi + sbr, abr * bi + abi * br + sbi
    y = jnp.dot(u, m_ref[0], preferred_element_type=F32)
    x_in = jnp.concatenate([x_scr[part] for part in range(4)], axis=-1).astype(BF16)
    y = y + jnp.dot(x_in, r_ref[0], preferred_element_type=F32)
    for h in range(SSM_GROUP):
        y_ref[h] = y[:, h * SSM_CHUNK:(h + 1) * SSM_CHUNK]


def _s5(u_t, p_mat, m_mat, r_mat, alpha, nc, bsz):
    rows = u_t.shape[1]
    kern = functools.partial(_s5_kernel, nc=nc, bsz=bsz)
    seq = pl.BlockSpec((SSM_GROUP, rows, SSM_CHUNK), lambda i: (i, 0, 0))
    return pl.pallas_call(
        kern,
        grid=(N_GROUPS,),
        in_specs=[seq,
                  pl.BlockSpec((1, SSM_ROW, 4 * LANES), lambda i: (i, 0, 0)),
                  pl.BlockSpec((1, SSM_ROW, SSM_ROW), lambda i: (i, 0, 0)),
                  pl.BlockSpec((1, 4 * LANES, SSM_ROW), lambda i: (i, 0, 0)),
                  pl.BlockSpec((1, 4, LANES), lambda i: (i, 0, 0))],
        out_specs=seq,
        out_shape=jax.ShapeDtypeStruct(u_t.shape, F32),
        scratch_shapes=[pltpu.VMEM((4, rows, LANES), F32), pltpu.VMEM((4, rows, LANES), F32)],
        compiler_params=_cparams(("parallel",)),
        name="s5_scan",
    )(u_t, p_mat, m_mat, r_mat, alpha)


def _post_mix_kernel(x_ref, att_ref, y_ref, wglu_ref, bglu_ref, gs_ref, wout_ref, g2_ref,
                     wrh_ref, wrl_ref, br_ref, x2_ref, h2_ref, ids_ref, gates_ref):
    y = jnp.concatenate([y_ref[:, j, :].T for j in range(y_ref.shape[1])], axis=0)
    y = 0.5 * y * (1.0 + jnp.tanh(math.sqrt(2.0 / math.pi) * (y + 0.044715 * (y * y * y))))
    z = jnp.dot(y.astype(BF16), wglu_ref[...], preferred_element_type=F32) + bglu_ref[...]
    y = y * (1.0 / (1.0 + jnp.exp(-z)))
    ms = jnp.mean(y * y, axis=-1, keepdims=True)
    ssm = (y * lax.rsqrt(ms + RMS_EPS) * gs_ref[...]).astype(BF16)
    mix = jnp.dot(att_ref[...], wout_ref[0:W_ATT, :], preferred_element_type=F32)
    mix = mix + jnp.dot(ssm, wout_ref[W_ATT:, :], preferred_element_type=F32)
    x2 = x_ref[...] + mix
    x2_ref[...] = x2
    ms2 = jnp.mean(x2 * x2, axis=-1, keepdims=True)
    h2 = x2 * lax.rsqrt(ms2 + RMS_EPS) * g2_ref[...]
    h2_ref[...] = _pack_bf16_pair(h2[:, :D_MODEL // 2], h2[:, D_MODEL // 2:])
    h_hi = h2.astype(BF16)
    h_lo = (h2 - h_hi.astype(F32)).astype(BF16)
    logits = (jnp.dot(h_hi, wrh_ref[...], preferred_element_type=F32)
              + jnp.dot(h_lo, wrh_ref[...], preferred_element_type=F32)
              + jnp.dot(h_hi, wrl_ref[...], preferred_element_type=F32)) + br_ref[...]
    lane = lax.broadcasted_iota(jnp.int32, logits.shape, 1).astype(F32)
    neg = jnp.float32(-jnp.inf)
    cur = jnp.where(lane < N_EXPERTS, logits, neg)
    ids = jnp.zeros(logits.shape, F32)
    vals = jnp.zeros(logits.shape, F32)
    top = None
    den = None
    for kk in range(TOP_K):
        mx = jnp.max(cur, axis=1, keepdims=True)
        idx = jnp.min(jnp.where(cur == mx, lane, float(LANES)), axis=1, keepdims=True)
        if kk == 0:
            top = mx
        e = jnp.exp(mx - top)
        den = e if den is None else den + e
        ids = jnp.where(lane == kk, idx, ids)
        vals = jnp.where(lane == kk, e, vals)
        cur = jnp.where(lane == idx, neg, cur)
    ids_ref[...] = ids.astype(jnp.int32)
    gates_ref[...] = vals / den


def _post_mix(x2d, att, yssm, wglu, bglu, gs, wout, g2, wr_hi, wr_lo, br):
    t = x2d.shape[0]
    tm = min(ROW_TILE, t)
    row = lambda i: (i, 0)
    const = lambda i: (0, 0)
    return pl.pallas_call(
        _post_mix_kernel,
        grid=(t // tm,),
        in_specs=[pl.BlockSpec((tm, D_MODEL), row),
                  pl.BlockSpec((tm, W_ATT), row),
                  pl.BlockSpec((W_SSM, tm // SSM_CHUNK, SSM_CHUNK), lambda i: (0, i, 0)),
                  pl.BlockSpec((W_SSM, W_SSM), const),
                  pl.BlockSpec((1, W_SSM), const),
                  pl.BlockSpec((1, W_SSM), const),
                  pl.BlockSpec((D_MODEL, D_MODEL), const),
                  pl.BlockSpec((1, D_MODEL), const),
                  pl.BlockSpec((D_MODEL, LANES), const),
                  pl.BlockSpec((D_MODEL, LANES), const),
                  pl.BlockSpec((1, LANES), const)],
        out_specs=[pl.BlockSpec((tm, D_MODEL), row),
                   pl.BlockSpec((tm, D_MODEL // 2), row),
                   pl.BlockSpec((tm, LANES), row),
                   pl.BlockSpec((tm, LANES), row)],
        out_shape=[jax.ShapeDtypeStruct((t, D_MODEL), F32),
                   jax.ShapeDtypeStruct((t, D_MODEL // 2), PACKED),
                   jax.ShapeDtypeStruct((t, LANES), jnp.int32),
                   jax.ShapeDtypeStruct((t, LANES), F32)],
        compiler_params=_cparams(("parallel",)),
        name="post_mix",
    )(x2d, att, yssm, wglu, bglu, gs, wout, g2, wr_hi, wr_lo, br)


def _sc_gather_rows(x, idx):
    n = idx.shape[0]
    d = x.shape[1]
    row_bytes = d * x.dtype.itemsize
    step_rows = SC_STEP_BYTES // row_bytes
    steps = n // step_rows
    workers = SC_CORES * SC_SUBCORES
    assert steps % (2 * workers) == 0, (n, steps)
    per_worker = steps // workers
    idx_rows = jnp.pad(idx.reshape(steps, step_rows), ((0, 0), (0, LANES - step_rows)))
    mesh = plsc.VectorSubcoreMesh(core_axis_name="core", subcore_axis_name="subcore")

    cost = pl.CostEstimate(flops=0, transcendentals=0,
                           bytes_accessed=2 * n * row_bytes + idx_rows.size * idx_rows.dtype.itemsize)

    @pl.kernel(out_type=jax.ShapeDtypeStruct((n, d), x.dtype), mesh=mesh, cost_estimate=cost,
               name="sc_gather_rows",
               scratch_types=[pltpu.VMEM((2, 1, LANES), jnp.int32),
                              pltpu.VMEM((2, step_rows, d), x.dtype),
                              pltpu.SemaphoreType.DMA((2,)),
                              pltpu.SemaphoreType.DMA((2,))])
    def gather(x_hbm, i_hbm, o_hbm, idx_v, buf, sem_in, sem_out):
        base = (lax.axis_index("core") * SC_SUBCORES + lax.axis_index("subcore")) * per_worker

        def gather_copy(slot):
            return pltpu.make_async_copy(x_hbm.at[idx_v.at[slot, 0, pl.ds(0, step_rows)]], buf.at[slot],
                                         sem_in.at[slot])

        def out_copy(step, slot):
            return pltpu.make_async_copy(buf.at[slot], o_hbm.at[pl.ds(step * step_rows, step_rows), :],
                                         sem_out.at[slot])

        @pl.loop(0, per_worker // 2)
        def _(j):
            for slot in range(2):
                step = base + 2 * j + slot

                @pl.when(j > 0)
                def _():
                    out_copy(step, slot).wait()

                pltpu.sync_copy(i_hbm.at[pl.ds(step, 1), :], idx_v.at[slot])
                gather_copy(slot).start()
            for slot in range(2):
                gather_copy(slot).wait()
                out_copy(base + 2 * j + slot, slot).start()

        for slot in range(2):
            out_copy(base, slot).wait()

    return gather(x, idx_rows)


def _expert_kernel(be_ref, x_ref, w1_ref, b1_ref, w2_ref, b2_ref, y_ref, w1_bf, w2_bf):
    i = pl.program_id(0)

    @pl.when(jnp.logical_or(i == 0, be_ref[i] != be_ref[jnp.maximum(i - 1, 0)]))
    def _():
        w1_bf[...] = w1_ref[0].astype(BF16)
        w2_bf[...] = w2_ref[0].astype(BF16)

    x = jnp.concatenate(_unpack_bf16_pair(x_ref[...]), axis=1).astype(BF16)
    hdn = jnp.dot(x, w1_bf[...], preferred_element_type=F32) + b1_ref[0]
    gate = jnp.minimum(hdn[:, :D_FF], SWIGLU_LIMIT)
    lin = jnp.clip(hdn[:, D_FF:], -SWIGLU_LIMIT, SWIGLU_LIMIT)
    act = gate * (1.0 / (1.0 + jnp.exp(-SWIGLU_ALPHA * gate))) * (lin + 1.0)
    y = jnp.dot(act.astype(BF16), w2_bf[...], preferred_element_type=F32) + b2_ref[0]
    y_ref[...] = _pack_bf16_pair(y[:, :D_MODEL // 2], y[:, D_MODEL // 2:])


def _experts(block_e, x_pad, w1, b1, w2, b2):
    n_pad = x_pad.shape[0]
    rows = EXPERT_ROWS
    grid_spec = pltpu.PrefetchScalarGridSpec(
        num_scalar_prefetch=1,
        grid=(n_pad // rows,),
        in_specs=[pl.BlockSpec((rows, D_MODEL // 2), lambda i, be: (i, 0)),
                  pl.BlockSpec((1, D_MODEL, 2 * D_FF), lambda i, be: (be[i], 0, 0)),
                  pl.BlockSpec((1, 1, 2 * D_FF), lambda i, be: (be[i], 0, 0)),
                  pl.BlockSpec((1, D_FF, D_MODEL), lambda i, be: (be[i], 0, 0)),
                  pl.BlockSpec((1, 1, D_MODEL), lambda i, be: (be[i], 0, 0))],
        out_specs=pl.BlockSpec((rows, D_MODEL // 2), lambda i, be: (i, 0)),
        scratch_shapes=[pltpu.VMEM((D_MODEL, 2 * D_FF), BF16), pltpu.VMEM((D_FF, D_MODEL), BF16)],
    )
    return pl.pallas_call(
        _expert_kernel,
        grid_spec=grid_spec,
        out_shape=jax.ShapeDtypeStruct((n_pad, D_MODEL // 2), PACKED),
        compiler_params=_cparams(("arbitrary",)),
        name="moe_experts",
    )(block_e, x_pad, w1, b1, w2, b2)


def _combine_kernel(x2_ref, gates_ref, gf_ref, y_ref, o_ref):
    gates = gates_ref[...]
    lo = x2_ref[:, :D_MODEL // 2]
    hi = x2_ref[:, D_MODEL // 2:]
    for kk in range(TOP_K):
        y_lo, y_hi = _unpack_bf16_pair(y_ref[kk])
        lo = lo + gates[:, kk:kk + 1] * y_lo
        hi = hi + gates[:, kk:kk + 1] * y_hi
    y = jnp.concatenate([lo, hi], axis=1)
    ms = jnp.mean(y * y, axis=-1, keepdims=True)
    o_ref[...] = y * lax.rsqrt(ms + RMS_EPS) * gf_ref[...]


def _combine(x2, gates, gf, y_sel):
    t = x2.shape[0]
    rows = min(COMBINE_ROWS, t)
    row = lambda i: (i, 0)
    return pl.pallas_call(
        _combine_kernel,
        grid=(t // rows,),
        in_specs=[pl.BlockSpec((rows, D_MODEL), row),
                  pl.BlockSpec((rows, LANES), row),
                  pl.BlockSpec((1, D_MODEL), lambda i: (0, 0)),
                  pl.BlockSpec((TOP_K, rows, D_MODEL // 2), lambda i: (0, i, 0))],
        out_specs=pl.BlockSpec((rows, D_MODEL), row),
        out_shape=jax.ShapeDtypeStruct((t, D_MODEL), F32),
        compiler_params=_cparams(("parallel",)),
        name="moe_combine",
    )(x2, gates, gf, y_sel)


def _dispatch_plan(top_e, n_tok):
    n = n_tok * TOP_K
    blk = EXPERT_ROWS
    flat_e = top_e.reshape(n)
    order = jnp.argsort(flat_e).astype(jnp.int32)
    rank = jnp.argsort(order).astype(jnp.int32)
    st = order // TOP_K
    experts = jnp.arange(N_EXPERTS, dtype=jnp.int32)
    counts = jnp.sum((flat_e[:, None] == experts).astype(jnp.int32), axis=0)
    padded = (counts + blk - 1) // blk * blk
    start = jnp.cumsum(counts) - counts
    pend = jnp.cumsum(padded)
    pstart = pend - padded
    pos = (pstart - start)[flat_e] + rank
    n_blocks = n // blk + N_EXPERTS
    block_start = jnp.arange(n_blocks, dtype=jnp.int32) * blk
    block_e = jnp.minimum(jnp.sum((pend[None, :] <= block_start[:, None]).astype(jnp.int32), axis=1),
                          N_EXPERTS - 1)
    slot = jnp.arange(n_blocks * blk, dtype=jnp.int32)
    slot_e = jnp.repeat(block_e, blk)
    off = slot - pstart[slot_e]
    src = jnp.clip(start[slot_e] + off, 0, n - 1)
    tok_pad = jnp.where(off < counts[slot_e], st[src], 0)
    return tok_pad, pos, block_e


def _trunk_front(x, prm, pending=None):
    bsz, l, _ = x.shape
    t = bsz * l
    x2d = x.reshape(t, D_MODEL)
    q, k, v, u_t = _in_proj(x2d, prm['norm1_g'], prm['w_in'])
    yssm = _s5(u_t, prm['ssm_p'], prm['ssm_m'], prm['ssm_r'], prm['ssm_alpha'], l // SSM_CHUNK, bsz)
    gathered = None
    if pending is not None:
        rows, idx = pending
        yssm, q, idx = lax.optimization_barrier((yssm, q, idx))
        gathered = _sc_gather_rows(rows, idx)

    t_att = min(ATT_TILE, l)
    att = _attention(q.reshape(bsz, l, W_ATT), k.reshape(bsz, l, W_ATT), v.reshape(bsz, l, W_ATT),
                     prm['bias_tiles'][t_att], prm['lam'], prm['subln_g'], t_att)
    if gathered is not None:
        att, gathered = lax.optimization_barrier((att, gathered))

    x2, h2, ids, gates = _post_mix(x2d, att.reshape(t, W_ATT), yssm, prm['w_glu'], prm['b_glu'],
                                   prm['ssm_norm_g'], prm['w_out'], prm['norm2_g'],
                                   prm['w_router_hi'], prm['w_router_lo'], prm['b_router'])

    tok_pad, pos, block_e = _dispatch_plan(ids[:, :TOP_K], t)
    return {'shape': x.shape, 'x2': x2, 'h2': h2, 'gates': gates,
            'tok_pad': tok_pad, 'pos': pos, 'block_e': block_e, 'gathered': gathered}


def _trunk_back(st, x_pad, prm):
    t = st['x2'].shape[0]
    y_pad = _experts(st['block_e'], x_pad, prm['w_moe1'], prm['b_moe1'], prm['w_moe2'], prm['b_moe2'])
    pos_by_k = st['pos'].reshape(t, TOP_K).T.reshape(TOP_K * t)
    y_sel = _sc_gather_rows(y_pad, pos_by_k).reshape(TOP_K, t, D_MODEL // 2)
    out = _combine(st['x2'], st['gates'], prm['normf_g'], y_sel)
    return out.reshape(st['shape'])


def _prepare(seq_lens, rel_bias, norm1_g, w_in, lambda_q1, lambda_k1, lambda_q2, lambda_k2, subln_g,
             ssm_A_re, ssm_A_im, ssm_log_dt, ssm_B_re, ssm_B_im, ssm_C_re, ssm_C_im, ssm_D,
             w_glu, b_glu, ssm_norm_g, w_out, norm2_g, w_router, b_router,
             w_moe1, b_moe1, w_moe2, b_moe2, normf_g):
    layer = 0
    lambda_init = 0.8 - 0.6 * math.exp(-0.3 * layer)
    lam = (jnp.exp(jnp.sum(lambda_q1[layer].astype(F32) * lambda_k1[layer].astype(F32)))
           - jnp.exp(jnp.sum(lambda_q2[layer].astype(F32) * lambda_k2[layer].astype(F32))) + lambda_init)
    p_mat, m_mat, r_mat, alpha = _ssm_matrices(
        ssm_A_re[layer], ssm_A_im[layer], ssm_log_dt[layer], ssm_B_re[layer], ssm_B_im[layer],
        ssm_C_re[layer], ssm_C_im[layer], ssm_D[layer])
    pad_e = LANES - N_EXPERTS
    w_r = jnp.pad(w_router[layer].astype(F32), ((0, 0), (0, pad_e)))
    w_r_hi = w_r.astype(BF16)
    return {
        'norm1_g': norm1_g[layer].reshape(1, D_MODEL).astype(F32),
        'w_in': w_in[layer].astype(BF16),
        'lam': lam.reshape(1).astype(F32),
        'subln_g': (subln_g[layer].astype(F32) * (1.0 - lambda_init)).reshape(1, V_DIM),
        'bias_tiles': {t: _bias_tiles(rel_bias, t) for t in sorted({min(ATT_TILE, l) for l in seq_lens})},
        'ssm_p': p_mat, 'ssm_m': m_mat, 'ssm_r': r_mat, 'ssm_alpha': alpha,
        'w_glu': w_glu[layer].astype(BF16),
        'b_glu': b_glu[layer].reshape(1, W_SSM).astype(F32),
        'ssm_norm_g': ssm_norm_g[layer].reshape(1, W_SSM).astype(F32),
        'w_out': w_out[layer].astype(BF16),
        'norm2_g': norm2_g[layer].reshape(1, D_MODEL).astype(F32),
        'w_router_hi': w_r_hi,
        'w_router_lo': (w_r - w_r_hi.astype(F32)).astype(BF16),
        'b_router': jnp.pad(b_router[layer].astype(F32), (0, pad_e)).reshape(1, LANES),
        'w_moe1': w_moe1[layer].astype(F32),
        'b_moe1': b_moe1[layer].reshape(N_EXPERTS, 1, 2 * D_FF).astype(F32),
        'w_moe2': w_moe2[layer].astype(F32),
        'b_moe2': b_moe2[layer].reshape(N_EXPERTS, 1, D_MODEL).astype(F32),
        'normf_g': normf_g.reshape(1, D_MODEL).astype(F32),
    }


def kernel(x_prompt, x_sample, rel_bias, norm1_g, w_in, lambda_q1, lambda_k1, lambda_q2, lambda_k2, subln_g, ssm_A_re, ssm_A_im, ssm_log_dt, ssm_B_re, ssm_B_im, ssm_C_re, ssm_C_im, ssm_D, w_glu, b_glu, ssm_norm_g, w_out, norm2_g, w_router, b_router, w_moe1, b_moe1, w_moe2, b_moe2, normf_g):
    prm = _prepare((x_prompt.shape[1], x_sample.shape[1]), rel_bias, norm1_g, w_in, lambda_q1,
                   lambda_k1, lambda_q2, lambda_k2, subln_g, ssm_A_re, ssm_A_im, ssm_log_dt,
                   ssm_B_re, ssm_B_im, ssm_C_re, ssm_C_im, ssm_D, w_glu, b_glu, ssm_norm_g, w_out,
                   norm2_g, w_router, b_router, w_moe1, b_moe1, w_moe2, b_moe2, normf_g)
    first = _trunk_front(x_prompt, prm)
    x_sample, tok_pad = lax.optimization_barrier((x_sample, first['tok_pad']))
    second = _trunk_front(x_sample, prm, pending=(first['h2'], tok_pad))
    x_pad_second = _sc_gather_rows(second['h2'], second['tok_pad'])
    return (_trunk_back(first, second['gathered'], prm), _trunk_back(second, x_pad_second, prm))
```

```python
import functools
import math

import jax
import jax.numpy as jnp
from jax import lax
from jax.experimental import pallas as pl
from jax.experimental.pallas import tpu as pltpu
from jax.experimental.pallas import tpu_sc as plsc

F32 = jnp.float32
BF16 = jnp.bfloat16
PACKED = jnp.uint32

D_MODEL = 1024
W_ATT = 512
W_SSM = 512
HEAD_DIM = 64
N_HEADS = 4
V_DIM = 2 * HEAD_DIM
SSM_GROUP = 16
N_GROUPS = W_SSM // SSM_GROUP
SSM_STATE = 64
IN_WIDTH = 3 * W_ATT + W_SSM
N_BUCKETS = 32
MAX_DISTANCE = 128
N_EXPERTS = 32
TOP_K = 4
D_FF = D_MODEL
SWIGLU_ALPHA = 1.702
SWIGLU_LIMIT = 7.0
RMS_EPS = 1e-6
ATT_SCALE = HEAD_DIM ** -0.5
LOG2E = math.log2(math.e)

LANES = 128
PV_KEYS = 512
ATT_TILE = 512
SSM_CHUNK = LANES
SSM_ROW = SSM_CHUNK * SSM_GROUP
ROW_TILE = 1024
EXPERT_ROWS = 512
SC_STEP_BYTES = 128 * 1024
SC_CORES = 2
SC_SUBCORES = 16
COMBINE_ROWS = 512
VMEM_LIMIT = 56 * 1024 * 1024


def _pack_bf16_pair(lo, hi):
    lo_bits = lax.bitcast_convert_type(lo.astype(BF16).astype(F32), PACKED)
    hi_bits = lax.bitcast_convert_type(hi.astype(BF16).astype(F32), PACKED)
    return (hi_bits & jnp.uint32(0xFFFF0000)) | (lo_bits >> 16)


def _unpack_bf16_pair(words):
    lo = lax.bitcast_convert_type(words << 16, F32)
    hi = lax.bitcast_convert_type(words & jnp.uint32(0xFFFF0000), F32)
    return lo, hi


def _cparams(sem):
    return pltpu.CompilerParams(dimension_semantics=sem, vmem_limit_bytes=VMEM_LIMIT)


def _in_proj_kernel(x_ref, g_ref, w_ref, q_ref, k_ref, v_ref, ut_ref):
    x = x_ref[...]
    ms = jnp.mean(x * x, axis=-1, keepdims=True)
    h = (x * lax.rsqrt(ms + RMS_EPS) * g_ref[...]).astype(BF16)
    proj = jnp.dot(h, w_ref[...], preferred_element_type=F32)
    q_ref[...] = (proj[:, 0:W_ATT] * (ATT_SCALE * LOG2E)).astype(BF16)
    k_ref[...] = proj[:, W_ATT:2 * W_ATT].astype(BF16)
    v_ref[...] = proj[:, 2 * W_ATT:3 * W_ATT].astype(BF16)
    ut = proj[:, 3 * W_ATT:].T
    for j in range(ut_ref.shape[1]):
        ut_ref[:, j, :] = ut[:, j * SSM_CHUNK:(j + 1) * SSM_CHUNK]


def _in_proj(x2d, g, w_bf16):
    t = x2d.shape[0]
    tm = min(ROW_TILE, t)
    out = jax.ShapeDtypeStruct((t, W_ATT), BF16)
    row = lambda i: (i, 0)
    return pl.pallas_call(
        _in_proj_kernel,
        grid=(t // tm,),
        in_specs=[pl.BlockSpec((tm, D_MODEL), row),
                  pl.BlockSpec((1, D_MODEL), lambda i: (0, 0)),
                  pl.BlockSpec((D_MODEL, IN_WIDTH), lambda i: (0, 0))],
        out_specs=[pl.BlockSpec((tm, W_ATT), row)] * 3
        + [pl.BlockSpec((W_SSM, tm // SSM_CHUNK, SSM_CHUNK), lambda i: (0, i, 0))],
        out_shape=[out] * 3 + [jax.ShapeDtypeStruct((W_SSM, t // SSM_CHUNK, SSM_CHUNK), F32)],
        compiler_params=_cparams(("parallel",)),
        name="in_proj",
    )(x2d, g, w_bf16)


def _t5_bucket(rel):
    half = N_BUCKETS // 2
    max_exact = half // 2
    ret = jnp.where(rel > 0, half, 0).astype(jnp.int32)
    n = jnp.abs(rel)
    nf = jnp.maximum(n, 1).astype(F32)
    large = max_exact + (jnp.log(nf / max_exact) / math.log(MAX_DISTANCE / max_exact)
                         * (half - max_exact)).astype(jnp.int32)
    large = jnp.minimum(large, half - 1)
    return ret + jnp.where(n < max_exact, n, large)


def _bias_tiles(rel_bias, t):
    i = jnp.arange(t, dtype=jnp.int32)
    d = jnp.arange(-2, 3, dtype=jnp.int32)
    rel = d[:, None, None] * t + i[None, None, :] - i[None, :, None]
    onehot = (_t5_bucket(rel)[..., None] == jnp.arange(N_BUCKETS, dtype=jnp.int32)).astype(F32)
    tiles = jnp.einsum('dqkn,nh->hdqk', onehot, rel_bias.astype(F32), precision=lax.Precision.HIGHEST)
    return tiles * LOG2E


def _attn_kernel(lam_ref, far_ref, q_ref, k_ref, v_ref, bias_ref, g_ref, o_ref,
                 m_scr, acc_scr, sa_scr, sb_scr, mxa_scr, mxb_scr, *, t, sub, n_iter):
    head = pl.program_id(1)
    qi = pl.program_id(2)
    q = q_ref[0]
    lane = lax.broadcasted_iota(jnp.int32, q.shape, 1)
    zero = jnp.zeros_like(q)
    qs = (jnp.where(lane < HEAD_DIM, q, zero), jnp.where(lane >= HEAD_DIM, q, zero))
    nb = t // LANES
    n_near = min(n_iter, 2 if sub >= 2 else 3)
    first = jnp.maximum(qi - 1, 0) // sub

    m_scr[...] = jnp.full(m_scr.shape, -jnp.inf, F32)
    acc_scr[...] = jnp.zeros(acc_scr.shape, F32)

    def chunk_of(r):
        return (first + r) % n_iter

    def scores(r, s_scr, mx_scr, near):
        j = chunk_of(r)
        for mi in range(2):
            mx = None
            for c in range(sub):
                blk = j * sub + c
                kc = k_ref[0, pl.ds(pl.multiple_of(blk * t, t), t), :]
                s = lax.dot_general(qs[mi], kc, (((1,), (1,)), ((), ())), preferred_element_type=F32)
                if near:
                    s = s + bias_ref[0, jnp.clip(blk - qi, -2, 2) + 2]
                s_scr[mi, :, c * t:(c + 1) * t] = s
                for i in range(nb):
                    piece = s[:, i * LANES:(i + 1) * LANES]
                    mx = piece if mx is None else jnp.maximum(mx, piece)
            mx_scr[mi] = mx

    def accumulate(r, s_scr, mx_scr, near):
        j = chunk_of(r)
        vj = v_ref[0, pl.ds(pl.multiple_of(j * (sub * t), sub * t), sub * t), :]
        vj = jnp.concatenate([vj, jnp.ones_like(vj)], axis=1)
        shift = 0.0 if near else jnp.where(j * sub < qi, far_ref[2 * head], far_ref[2 * head + 1])
        for mi in range(2):
            m_prev = m_scr[mi]
            m_next = jnp.maximum(m_prev, jnp.max(mx_scr[mi], axis=1, keepdims=True) + shift)
            alpha = jnp.exp2(m_prev - m_next)
            m_scr[mi] = m_next
            m_raw = m_next - shift
            pv = None
            for i in range(sub * t // PV_KEYS):
                ps = []
                for c in range(PV_KEYS // LANES):
                    lo = i * PV_KEYS + c * LANES
                    p = jnp.exp2(s_scr[mi, :, lo:lo + LANES] - m_raw)
                    ps.append(p.astype(BF16))
                d = jnp.dot(jnp.concatenate(ps, axis=1), vj[i * PV_KEYS:(i + 1) * PV_KEYS, :],
                            preferred_element_type=F32)
                pv = d if pv is None else pv + d
            acc_scr[mi] = jnp.concatenate([alpha, alpha], axis=1) * acc_scr[mi] + pv

    def pair(i, first_near, second_near):
        scores(2 * i + 1, sb_scr, mxb_scr, second_near[0])
        accumulate(2 * i, sa_scr, mxa_scr, first_near)
        scores(2 * i + 2, sa_scr, mxa_scr, second_near[1])
        accumulate(2 * i + 1, sb_scr, mxb_scr, second_near[0])

    scores(0, sa_scr, mxa_scr, True)
    n_pairs = (n_iter - 1) // 2
    peeled = min(n_pairs, (n_near + 1) // 2)
    for i in range(peeled):
        pair(i, 2 * i < n_near, (2 * i + 1 < n_near, 2 * i + 2 < n_near))

    def body(i, carry):
        pair(i, False, (False, False))
        return carry

    lax.fori_loop(peeled, n_pairs, body, 0)
    if (n_iter - 1) % 2 == 1:
        scores(n_iter - 1, sb_scr, mxb_scr, n_iter - 1 < n_near)
        accumulate(n_iter - 2, sa_scr, mxa_scr, n_iter - 2 < n_near)
        accumulate(n_iter - 1, sb_scr, mxb_scr, n_iter - 1 < n_near)
    else:
        accumulate(n_iter - 1, sa_scr, mxa_scr, n_iter - 1 < n_near)

    o1 = acc_scr[0, :, :V_DIM] / acc_scr[0, :, V_DIM:]
    o2 = acc_scr[1, :, :V_DIM] / acc_scr[1, :, V_DIM:]
    o = o1 - lam_ref[0] * o2
    ms = jnp.mean(o * o, axis=-1, keepdims=True)
    o_ref[0] = (o * lax.rsqrt(ms + RMS_EPS) * g_ref[...]).astype(BF16)


def _attention(q, k, v, bias_tiles, lam, g_scaled, t):
    b, l, _ = q.shape
    nq = l // t
    sub = 2 if nq % 2 == 0 else 1
    kern = functools.partial(_attn_kernel, t=t, sub=sub, n_iter=nq // sub)
    far = jnp.stack([bias_tiles[:, 0, 0, 0], bias_tiles[:, 4, 0, 0]], axis=1).reshape(2 * N_HEADS)
    return pl.pallas_call(
        kern,
        grid=(b, N_HEADS, nq),
        in_specs=[pl.BlockSpec(memory_space=pltpu.SMEM),
                  pl.BlockSpec(memory_space=pltpu.SMEM),
                  pl.BlockSpec((1, t, V_DIM), lambda bi, h, qi: (bi, qi, h)),
                  pl.BlockSpec((1, l, V_DIM), lambda bi, h, qi: (bi, 0, h)),
                  pl.BlockSpec((1, l, V_DIM), lambda bi, h, qi: (bi, 0, h)),
                  pl.BlockSpec((1, 5, t, t), lambda bi, h, qi: (h, 0, 0, 0)),
                  pl.BlockSpec((1, V_DIM), lambda bi, h, qi: (0, 0))],
        out_specs=pl.BlockSpec((1, t, V_DIM), lambda bi, h, qi: (bi, qi, h)),
        out_shape=jax.ShapeDtypeStruct((b, l, W_ATT), BF16),
        scratch_shapes=[pltpu.VMEM((2, t, LANES), F32),
                        pltpu.VMEM((2, t, 2 * V_DIM), F32),
                        pltpu.VMEM((2, t, sub * t), F32),
                        pltpu.VMEM((2, t, sub * t), F32),
                        pltpu.VMEM((2, t, LANES), F32),
                        pltpu.VMEM((2, t, LANES), F32)],
        compiler_params=_cparams(("parallel", "parallel", "arbitrary")),
        name="diff_attention",
    )(lam, far, q, k, v, bias_tiles, g_scaled)


def _ssm_matrices(a_re, a_im, log_dt, b_re, b_im, c_re, c_im, d_skip):
    qn, g, p, hc = SSM_CHUNK, N_GROUPS, SSM_STATE, SSM_GROUP
    n = jnp.arange(qn + 1, dtype=F32)
    pw, bbar, cc = [], [], []
    for d in range(2):
        a = lax.complex(a_re[d].astype(F32), a_im[d].astype(F32))
        dt = jnp.exp(log_dt[d].astype(F32))[:, None]
        adt = a * dt
        a_bar = jnp.exp(adt)
        pw.append(jnp.exp(adt[None] * n[:, None, None]))
        bbar.append(((a_bar - 1.0) / a)[:, :, None]
                    * lax.complex(b_re[d].astype(F32), b_im[d].astype(F32)))
        cc.append(lax.complex(c_re[d].astype(F32), c_im[d].astype(F32)))

    hi = lax.Precision.HIGHEST
    kern = [jnp.einsum('gop,tgp,gpi->tgoi', cc[d], pw[d][:qn], bbar[d], precision=hi).real
            for d in range(2)]
    kf = jnp.transpose(kern[0], (1, 3, 2, 0))
    kb = jnp.transpose(kern[1], (1, 3, 2, 0))
    skip = jnp.eye(hc, dtype=F32)[None] * d_skip.astype(F32).reshape(g, hc)[:, :, None]
    taps = jnp.concatenate([kb[..., :0:-1], (kf[..., 0] + kb[..., 0] + skip)[..., None], kf[..., 1:]], axis=-1)
    m_mat = jnp.stack([taps[..., qn - 1 - s:2 * qn - 1 - s] for s in range(qn)], axis=2)
    m_mat = m_mat.reshape(g, SSM_ROW, SSM_ROW)

    zeros_p = jnp.zeros((g, SSM_ROW, LANES - p), F32)

    def pad_cols(x):
        return jnp.concatenate([x, zeros_p], axis=-1)

    pf = jnp.einsum('sgp,gpi->gisp', pw[0][:qn][::-1], bbar[0]).reshape(g, SSM_ROW, p)
    pb = jnp.einsum('sgp,gpi->gisp', pw[1][:qn], bbar[1]).reshape(g, SSM_ROW, p)
    p_mat = jnp.concatenate([pad_cols(pf.real), pad_cols(pf.imag),
                             pad_cols(pb.real), pad_cols(pb.imag)], axis=-1)

    wf = jnp.einsum('gop,tgp->gpot', cc[0], pw[0][1:qn + 1]).reshape(g, p, SSM_ROW)
    wb = jnp.einsum('gop,tgp->gpot', cc[1], pw[1][1:qn + 1][::-1]).reshape(g, p, SSM_ROW)
    zeros_r = jnp.zeros((g, LANES - p, SSM_ROW), F32)
    r_mat = jnp.concatenate([wf.real, zeros_r, -wf.imag, zeros_r,
                             wb.real, zeros_r, -wb.imag, zeros_r], axis=1)

    zeros_a = jnp.zeros((g, LANES - p), F32)

    def pad_vec(x):
        return jnp.concatenate([x, zeros_a], axis=-1)

    alpha = jnp.stack([pad_vec(pw[0][qn].real), pad_vec(pw[0][qn].imag),
                       pad_vec(pw[1][qn].real), pad_vec(pw[1][qn].imag)], axis=1)
    return p_mat.astype(BF16), m_mat.astype(BF16), r_mat.astype(BF16), alpha


def _s5_kernel(u_ref, p_ref, m_ref, r_ref, a_ref, y_ref, s_scr, x_scr, *, nc, bsz):
    u = jnp.concatenate([u_ref[h] for h in range(SSM_GROUP)], axis=-1).astype(BF16)
    s = jnp.dot(u, p_ref[0], preferred_element_type=F32)
    for part in range(4):
        s_scr[part] = s[:, part * LANES:(part + 1) * LANES]
    al = a_ref[0]
    afr = jnp.broadcast_to(al[0:1], (bsz, LANES))
    afi = jnp.broadcast_to(al[1:2], (bsz, LANES))
    abr = jnp.broadcast_to(al[2:3], (bsz, LANES))
    abi = jnp.broadcast_to(al[3:4], (bsz, LANES))
    fr = fi = br = bi = jnp.zeros((bsz, LANES), F32)
    for c in range(nc):
        rf = pl.ds(c, bsz, stride=nc)
        rb = pl.ds(nc - 1 - c, bsz, stride=nc)
        x_scr[0, rf, :] = fr
        x_scr[1, rf, :] = fi
        x_scr[2, rb, :] = br
        x_scr[3, rb, :] = bi
        sfr = s_scr[0, rf, :]
        sfi = s_scr[1, rf, :]
        sbr = s_scr[2, rb, :]
        sbi = s_scr[3, rb, :]
        fr, fi = afr * fr - afi * fi + sfr, afr * fi + afi * fr + sfi
        br, bi = abr * br - abi * bi + sbr, abr * bi + abi * br + sbi
    y = jnp.dot(u, m_ref[0], preferred_element_type=F32)
    x_in = jnp.concatenate([x_scr[part] for part in range(4)], axis=-1).astype(BF16)
    y = y + jnp.dot(x_in, r_ref[0], preferred_element_type=F32)
    for h in range(SSM_GROUP):
        y_ref[h] = y[:, h * SSM_CHUNK:(h + 1) * SSM_CHUNK]


def _s5(u_t, p_mat, m_mat, r_mat, alpha, nc, bsz):
    rows = u_t.shape[1]
    kern = functools.partial(_s5_kernel, nc=nc, bsz=bsz)
    seq = pl.BlockSpec((SSM_GROUP, rows, SSM_CHUNK), lambda i: (i, 0, 0))
    return pl.pallas_call(
        kern,
        grid=(N_GROUPS,),
        in_specs=[seq,
                  pl.BlockSpec((1, SSM_ROW, 4 * LANES), lambda i: (i, 0, 0)),
                  pl.BlockSpec((1, SSM_ROW, SSM_ROW), lambda i: (i, 0, 0)),
                  pl.BlockSpec((1, 4 * LANES, SSM_ROW), lambda i: (i, 0, 0)),
                  pl.BlockSpec((1, 4, LANES), lambda i: (i, 0, 0))],
        out_specs=seq,
        out_shape=jax.ShapeDtypeStruct(u_t.shape, F32),
        scratch_shapes=[pltpu.VMEM((4, rows, LANES), F32), pltpu.VMEM((4, rows, LANES), F32)],
        compiler_params=_cparams(("parallel",)),
        name="s5_scan",
    )(u_t, p_mat, m_mat, r_mat, alpha)


def _post_mix_kernel(x_ref, att_ref, y_ref, wglu_ref, bglu_ref, gs_ref, wout_ref, g2_ref,
                     wrh_ref, wrl_ref, br_ref, x2_ref, h2_ref, ids_ref, gates_ref):
    y = jnp.concatenate([y_ref[:, j, :].T for j in range(y_ref.shape[1])], axis=0)
    y = 0.5 * y * (1.0 + jnp.tanh(math.sqrt(2.0 / math.pi) * (y + 0.044715 * (y * y * y))))
    z = jnp.dot(y.astype(BF16), wglu_ref[...], preferred_element_type=F32) + bglu_ref[...]
    y = y * (1.0 / (1.0 + jnp.exp(-z)))
    ms = jnp.mean(y * y, axis=-1, keepdims=True)
    ssm = (y * lax.rsqrt(ms + RMS_EPS) * gs_ref[...]).astype(BF16)
    mix = jnp.dot(att_ref[...], wout_ref[0:W_ATT, :], preferred_element_type=F32)
    mix = mix + jnp.dot(ssm, wout_ref[W_ATT:, :], preferred_element_type=F32)
    x2 = x_ref[...] + mix
    x2_ref[...] = x2
    ms2 = jnp.mean(x2 * x2, axis=-1, keepdims=True)
    h2 = x2 * lax.rsqrt(ms2 + RMS_EPS) * g2_ref[...]
    h2_ref[...] = _pack_bf16_pair(h2[:, :D_MODEL // 2], h2[:, D_MODEL // 2:])
    h_hi = h2.astype(BF16)
    h_lo = (h2 - h_hi.astype(F32)).astype(BF16)
    logits = (jnp.dot(h_hi, wrh_ref[...], preferred_element_type=F32)
              + jnp.dot(h_lo, wrh_ref[...], preferred_element_type=F32)
              + jnp.dot(h_hi, wrl_ref[...], preferred_element_type=F32)) + br_ref[...]
    lane = lax.broadcasted_iota(jnp.int32, logits.shape, 1).astype(F32)
    neg = jnp.float32(-jnp.inf)
    cur = jnp.where(lane < N_EXPERTS, logits, neg)
    ids = jnp.zeros(logits.shape, F32)
    vals = jnp.zeros(logits.shape, F32)
    top = None
    den = None
    for kk in range(TOP_K):
        mx = jnp.max(cur, axis=1, keepdims=True)
        idx = jnp.min(jnp.where(cur == mx, lane, float(LANES)), axis=1, keepdims=True)
        if kk == 0:
            top = mx
        e = jnp.exp(mx - top)
        den = e if den is None else den + e
        ids = jnp.where(lane == kk, idx, ids)
        vals = jnp.where(lane == kk, e, vals)
        cur = jnp.where(lane == idx, neg, cur)
    ids_ref[...] = ids.astype(jnp.int32)
    gates_ref[...] = vals / den


def _post_mix(x2d, att, yssm, wglu, bglu, gs, wout, g2, wr_hi, wr_lo, br):
    t = x2d.shape[0]
    tm = min(ROW_TILE, t)
    row = lambda i: (i, 0)
    const = lambda i: (0, 0)
    return pl.pallas_call(
        _post_mix_kernel,
        grid=(t // tm,),
        in_specs=[pl.BlockSpec((tm, D_MODEL), row),
                  pl.BlockSpec((tm, W_ATT), row),
                  pl.BlockSpec((W_SSM, tm // SSM_CHUNK, SSM_CHUNK), lambda i: (0, i, 0)),
                  pl.BlockSpec((W_SSM, W_SSM), const),
                  pl.BlockSpec((1, W_SSM), const),
                  pl.BlockSpec((1, W_SSM), const),
                  pl.BlockSpec((D_MODEL, D_MODEL), const),
                  pl.BlockSpec((1, D_MODEL), const),
                  pl.BlockSpec((D_MODEL, LANES), const),
                  pl.BlockSpec((D_MODEL, LANES), const),
                  pl.BlockSpec((1, LANES), const)],
        out_specs=[pl.BlockSpec((tm, D_MODEL), row),
                   pl.BlockSpec((tm, D_MODEL // 2), row),
                   pl.BlockSpec((tm, LANES), row),
                   pl.BlockSpec((tm, LANES), row)],
        out_shape=[jax.ShapeDtypeStruct((t, D_MODEL), F32),
                   jax.ShapeDtypeStruct((t, D_MODEL // 2), PACKED),
                   jax.ShapeDtypeStruct((t, LANES), jnp.int32),
                   jax.ShapeDtypeStruct((t, LANES), F32)],
        compiler_params=_cparams(("parallel",)),
        name="post_mix",
    )(x2d, att, yssm, wglu, bglu, gs, wout, g2, wr_hi, wr_lo, br)


def _sc_gather_rows(x, idx):
    n = idx.shape[0]
    d = x.shape[1]
    row_bytes = d * x.dtype.itemsize
    step_rows = SC_STEP_BYTES // row_bytes
    steps = n // step_rows
    workers = SC_CORES * SC_SUBCORES
    assert steps % (2 * workers) == 0, (n, steps)
    per_worker = steps // workers
    idx_rows = jnp.pad(idx.reshape(steps, step_rows), ((0, 0), (0, LANES - step_rows)))
    mesh = plsc.VectorSubcoreMesh(core_axis_name="core", subcore_axis_name="subcore")

    cost = pl.CostEstimate(flops=0, transcendentals=0,
                           bytes_accessed=2 * n * row_bytes + idx_rows.size * idx_rows.dtype.itemsize)

    @pl.kernel(out_type=jax.ShapeDtypeStruct((n, d), x.dtype), mesh=mesh, cost_estimate=cost,
               name="sc_gather_rows",
               scratch_types=[pltpu.VMEM((2, 1, LANES), jnp.int32),
                              pltpu.VMEM((2, step_rows, d), x.dtype),
                              pltpu.SemaphoreType.DMA((2,)),
                              pltpu.SemaphoreType.DMA((2,))])
    def gather(x_hbm, i_hbm, o_hbm, idx_v, buf, sem_in, sem_out):
        base = (lax.axis_index("core") * SC_SUBCORES + lax.axis_index("subcore")) * per_worker

        def gather_copy(slot):
            return pltpu.make_async_copy(x_hbm.at[idx_v.at[slot, 0, pl.ds(0, step_rows)]], buf.at[slot],
                                         sem_in.at[slot])

        def out_copy(step, slot):
            return pltpu.make_async_copy(buf.at[slot], o_hbm.at[pl.ds(step * step_rows, step_rows), :],
                                         sem_out.at[slot])

        @pl.loop(0, per_worker // 2)
        def _(j):
            for slot in range(2):
                step = base + 2 * j + slot

                @pl.when(j > 0)
                def _():
                    out_copy(step, slot).wait()

                pltpu.sync_copy(i_hbm.at[pl.ds(step, 1), :], idx_v.at[slot])
                gather_copy(slot).start()
            for slot in range(2):
                gather_copy(slot).wait()
                out_copy(base + 2 * j + slot, slot).start()

        for slot in range(2):
            out_copy(base, slot).wait()

    return gather(x, idx_rows)


def _expert_kernel(be_ref, x_ref, w1_ref, b1_ref, w2_ref, b2_ref, y_ref, w1_bf, w2_bf):
    i = pl.program_id(0)

    @pl.when(jnp.logical_or(i == 0, be_ref[i] != be_ref[jnp.maximum(i - 1, 0)]))
    def _():
        w1_bf[...] = w1_ref[0].astype(BF16)
        w2_bf[...] = w2_ref[0].astype(BF16)

    x = jnp.concatenate(_unpack_bf16_pair(x_ref[...]), axis=1).astype(BF16)
    hdn = jnp.dot(x, w1_bf[...], preferred_element_type=F32) + b1_ref[0]
    gate = jnp.minimum(hdn[:, :D_FF], SWIGLU_LIMIT)
    lin = jnp.clip(hdn[:, D_FF:], -SWIGLU_LIMIT, SWIGLU_LIMIT)
    act = gate * (1.0 / (1.0 + jnp.exp(-SWIGLU_ALPHA * gate))) * (lin + 1.0)
    y = jnp.dot(act.astype(BF16), w2_bf[...], preferred_element_type=F32) + b2_ref[0]
    y_ref[...] = _pack_bf16_pair(y[:, :D_MODEL // 2], y[:, D_MODEL // 2:])


def _experts(block_e, x_pad, w1, b1, w2, b2):
    n_pad = x_pad.shape[0]
    rows = EXPERT_ROWS
    grid_spec = pltpu.PrefetchScalarGridSpec(
        num_scalar_prefetch=1,
        grid=(n_pad // rows,),
        in_specs=[pl.BlockSpec((rows, D_MODEL // 2), lambda i, be: (i, 0)),
                  pl.BlockSpec((1, D_MODEL, 2 * D_FF), lambda i, be: (be[i], 0, 0)),
                  pl.BlockSpec((1, 1, 2 * D_FF), lambda i, be: (be[i], 0, 0)),
                  pl.BlockSpec((1, D_FF, D_MODEL), lambda i, be: (be[i], 0, 0)),
                  pl.BlockSpec((1, 1, D_MODEL), lambda i, be: (be[i], 0, 0))],
        out_specs=pl.BlockSpec((rows, D_MODEL // 2), lambda i, be: (i, 0)),
        scratch_shapes=[pltpu.VMEM((D_MODEL, 2 * D_FF), BF16), pltpu.VMEM((D_FF, D_MODEL), BF16)],
    )
    return pl.pallas_call(
        _expert_kernel,
        grid_spec=grid_spec,
        out_shape=jax.ShapeDtypeStruct((n_pad, D_MODEL // 2), PACKED),
        compiler_params=_cparams(("arbitrary",)),
        name="moe_experts",
    )(block_e, x_pad, w1, b1, w2, b2)


def _combine_kernel(x2_ref, gates_ref, gf_ref, y_ref, o_ref):
    gates = gates_ref[...]
    lo = x2_ref[:, :D_MODEL // 2]
    hi = x2_ref[:, D_MODEL // 2:]
    for kk in range(TOP_K):
        y_lo, y_hi = _unpack_bf16_pair(y_ref[kk])
        lo = lo + gates[:, kk:kk + 1] * y_lo
        hi = hi + gates[:, kk:kk + 1] * y_hi
    y = jnp.concatenate([lo, hi], axis=1)
    ms = jnp.mean(y * y, axis=-1, keepdims=True)
    o_ref[...] = y * lax.rsqrt(ms + RMS_EPS) * gf_ref[...]


def _combine(x2, gates, gf, y_sel):
    t = x2.shape[0]
    rows = min(COMBINE_ROWS, t)
    row = lambda i: (i, 0)
    return pl.pallas_call(
        _combine_kernel,
        grid=(t // rows,),
        in_specs=[pl.BlockSpec((rows, D_MODEL), row),
                  pl.BlockSpec((rows, LANES), row),
                  pl.BlockSpec((1, D_MODEL), lambda i: (0, 0)),
                  pl.BlockSpec((TOP_K, rows, D_MODEL // 2), lambda i: (0, i, 0))],
        out_specs=pl.BlockSpec((rows, D_MODEL), row),
        out_shape=jax.ShapeDtypeStruct((t, D_MODEL), F32),
        compiler_params=_cparams(("parallel",)),
        name="moe_combine",
    )(x2, gates, gf, y_sel)


def _dispatch_plan(top_e, n_tok):
    n = n_tok * TOP_K
    blk = EXPERT_ROWS
    flat_e = top_e.reshape(n)
    order = jnp.argsort(flat_e).astype(jnp.int32)
    rank = jnp.argsort(order).astype(jnp.int32)
    st = order // TOP_K
    experts = jnp.arange(N_EXPERTS, dtype=jnp.int32)
    counts = jnp.sum((flat_e[:, None] == experts).astype(jnp.int32), axis=0)
    padded = (counts + blk - 1) // blk * blk
    start = jnp.cumsum(counts) - counts
    pend = jnp.cumsum(padded)
    pstart = pend - padded
    pos = (pstart - start)[flat_e] + rank
    n_blocks = n // blk + N_EXPERTS
    block_start = jnp.arange(n_blocks, dtype=jnp.int32) * blk
    block_e = jnp.minimum(jnp.sum((pend[None, :] <= block_start[:, None]).astype(jnp.int32), axis=1),
                          N_EXPERTS - 1)
    slot = jnp.arange(n_blocks * blk, dtype=jnp.int32)
    slot_e = jnp.repeat(block_e, blk)
    off = slot - pstart[slot_e]
    src = jnp.clip(start[slot_e] + off, 0, n - 1)
    tok_pad = jnp.where(off < counts[slot_e], st[src], 0)
    return tok_pad, pos, block_e


def _trunk_front(x, prm, pending=None):
    bsz, l, _ = x.shape
    t = bsz * l
    x2d = x.reshape(t, D_MODEL)
    q, k, v, u_t = _in_proj(x2d, prm['norm1_g'], prm['w_in'])
    yssm = _s5(u_t, prm['ssm_p'], prm['ssm_m'], prm['ssm_r'], prm['ssm_alpha'], l // SSM_CHUNK, bsz)
    gathered = None
    if pending is not None:
        rows, idx = pending
        yssm, q, idx = lax.optimization_barrier((yssm, q, idx))
        gathered = _sc_gather_rows(rows, idx)

    t_att = min(ATT_TILE, l)
    att = _attention(q.reshape(bsz, l, W_ATT), k.reshape(bsz, l, W_ATT), v.reshape(bsz, l, W_ATT),
                     prm['bias_tiles'][t_att], prm['lam'], prm['subln_g'], t_att)
    if gathered is not None:
        att, gathered = lax.optimization_barrier((att, gathered))

    x2, h2, ids, gates = _post_mix(x2d, att.reshape(t, W_ATT), yssm, prm['w_glu'], prm['b_glu'],
                                   prm['ssm_norm_g'], prm['w_out'], prm['norm2_g'],
                                   prm['w_router_hi'], prm['w_router_lo'], prm['b_router'])

    tok_pad, pos, block_e = _dispatch_plan(ids[:, :TOP_K], t)
    return {'shape': x.shape, 'x2': x2, 'h2': h2, 'gates': gates,
            'tok_pad': tok_pad, 'pos': pos, 'block_e': block_e, 'gathered': gathered}


def _trunk_back(st, x_pad, prm):
    t = st['x2'].shape[0]
    y_pad = _experts(st['block_e'], x_pad, prm['w_moe1'], prm['b_moe1'], prm['w_moe2'], prm['b_moe2'])
    pos_by_k = st['pos'].reshape(t, TOP_K).T.reshape(TOP_K * t)
    y_sel = _sc_gather_rows(y_pad, pos_by_k).reshape(TOP_K, t, D_MODEL // 2)
    out = _combine(st['x2'], st['gates'], prm['normf_g'], y_sel)
    return out.reshape(st['shape'])


def _prepare(seq_lens, rel_bias, norm1_g, w_in, lambda_q1, lambda_k1, lambda_q2, lambda_k2, subln_g,
             ssm_A_re, ssm_A_im, ssm_log_dt, ssm_B_re, ssm_B_im, ssm_C_re, ssm_C_im, ssm_D,
             w_glu, b_glu, ssm_norm_g, w_out, norm2_g, w_router, b_router,
             w_moe1, b_moe1, w_moe2, b_moe2, normf_g):
    layer = 0
    lambda_init = 0.8 - 0.6 * math.exp(-0.3 * layer)
    lam = (jnp.exp(jnp.sum(lambda_q1[layer].astype(F32) * lambda_k1[layer].astype(F32)))
           - jnp.exp(jnp.sum(lambda_q2[layer].astype(F32) * lambda_k2[layer].astype(F32))) + lambda_init)
    p_mat, m_mat, r_mat, alpha = _ssm_matrices(
        ssm_A_re[layer], ssm_A_im[layer], ssm_log_dt[layer], ssm_B_re[layer], ssm_B_im[layer],
        ssm_C_re[layer], ssm_C_im[layer], ssm_D[layer])
    pad_e = LANES - N_EXPERTS
    w_r = jnp.pad(w_router[layer].astype(F32), ((0, 0), (0, pad_e)))
    w_r_hi = w_r.astype(BF16)
    return {
        'norm1_g': norm1_g[layer].reshape(1, D_MODEL).astype(F32),
        'w_in': w_in[layer].astype(BF16),
        'lam': lam.reshape(1).astype(F32),
        'subln_g': (subln_g[layer].astype(F32) * (1.0 - lambda_init)).reshape(1, V_DIM),
        'bias_tiles': {t: _bias_tiles(rel_bias, t) for t in sorted({min(ATT_TILE, l) for l in seq_lens})},
        'ssm_p': p_mat, 'ssm_m': m_mat, 'ssm_r': r_mat, 'ssm_alpha': alpha,
        'w_glu': w_glu[layer].astype(BF16),
        'b_glu': b_glu[layer].reshape(1, W_SSM).astype(F32),
        'ssm_norm_g': ssm_norm_g[layer].reshape(1, W_SSM).astype(F32),
        'w_out': w_out[layer].astype(BF16),
        'norm2_g': norm2_g[layer].reshape(1, D_MODEL).astype(F32),
        'w_router_hi': w_r_hi,
        'w_router_lo': (w_r - w_r_hi.astype(F32)).astype(BF16),
        'b_router': jnp.pad(b_router[layer].astype(F32), (0, pad_e)).reshape(1, LANES),
        'w_moe1': w_moe1[layer].astype(F32),
        'b_moe1': b_moe1[layer].reshape(N_EXPERTS, 1, 2 * D_FF).astype(F32),
        'w_moe2': w_moe2[layer].astype(F32),
        'b_moe2': b_moe2[layer].reshape(N_EXPERTS, 1, D_MODEL).astype(F32),
        'normf_g': normf_g.reshape(1, D_MODEL).astype(F32),
    }


def kernel(x_prompt, x_sample, rel_bias, norm1_g, w_in, lambda_q1, lambda_k1, lambda_q2, lambda_k2, subln_g, ssm_A_re, ssm_A_im, ssm_log_dt, ssm_B_re, ssm_B_im, ssm_C_re, ssm_C_im, ssm_D, w_glu, b_glu, ssm_norm_g, w_out, norm2_g, w_router, b_router, w_moe1, b_moe1, w_moe2, b_moe2, normf_g):
    prm = _prepare((x_prompt.shape[1], x_sample.shape[1]), rel_bias, norm1_g, w_in, lambda_q1,
                   lambda_k1, lambda_q2, lambda_k2, subln_g, ssm_A_re, ssm_A_im, ssm_log_dt,
                   ssm_B_re, ssm_B_im, ssm_C_re, ssm_C_im, ssm_D, w_glu, b_glu, ssm_norm_g, w_out,
                   norm2_g, w_router, b_router, w_moe1, b_moe1, w_moe2, b_moe2, normf_g)
    first = _trunk_front(x_prompt, prm)
    x_sample, tok_pad = lax.optimization_barrier((x_sample, first['tok_pad']))
    second = _trunk_front(x_sample, prm, pending=(first['h2'], tok_pad))
    x_pad_second = _sc_gather_rows(second['h2'], second['tok_pad'])
    return (_trunk_back(first, second['gathered'], prm), _trunk_back(second, x_pad_second, prm))
```
